```python
import jax, jax.numpy as jnp
from jax import lax
import numpy as np

D_MODEL = 1024
BATCH = 8
SEQ = 4096
DEPTH = 1

MLA_HEADS = 8
MLA_NOPE = 64
MLA_ROPE = 32
MLA_QK = MLA_NOPE + MLA_ROPE
MLA_V = 64
Q_LORA = 256
KV_LORA = 128
ROPE_THETA = 10000.0
Q_BLOCK = 128
RWKV_HEADS = 8
RWKV_HEAD = 64
RWKV_WIDTH = RWKV_HEADS * RWKV_HEAD
DECAY_LORA = 64
AAA_LORA = 64
GATE_LORA = 128
RWKV_SLAB = 3 * RWKV_WIDTH + DECAY_LORA + AAA_LORA + GATE_LORA
GN_EPS = 64e-5
IN_SIZES = (Q_LORA, KV_LORA, MLA_ROPE, RWKV_SLAB, D_MODEL, D_MODEL)
D_IN = sum(IN_SIZES)
N_EXPERTS = 32
TOP_K = 4
D_FF = D_MODEL
SWIGLU_LIMIT = 7.0
SWIGLU_ALPHA = 1.702
EXPERT_BLOCK = 128
NORM_EPS = 1e-6

kernel_name = 'hybrid_mla_rwkv7_moe_adaln'


def _split(x, sizes):
    return jnp.split(x, np.cumsum(sizes)[:-1].tolist(), axis=-1)


def rms_norm(x, w):
    xf = x.astype(jnp.float32)
    y = xf * lax.rsqrt(jnp.mean(xf * xf, axis=-1, keepdims=True) + NORM_EPS)
    return (y * w.astype(jnp.float32)).astype(x.dtype)


def modulate(x, norm_w, shift, scale):
    return rms_norm(x, norm_w) * (1.0 + scale[:, None, :]) + shift[:, None, :]


def apply_rope(x, positions):
    half = x.shape[-1] // 2
    inv_freq = ROPE_THETA ** (-jnp.arange(half, dtype=jnp.float32) / half)
    ang = positions.astype(jnp.float32)[:, :, None] * inv_freq
    cos = jnp.cos(ang)[:, :, None, :]
    sin = jnp.sin(ang)[:, :, None, :]
    xf = x.astype(jnp.float32)
    x1, x2 = xf[..., :half], xf[..., half:]
    return jnp.concatenate([x1 * cos - x2 * sin, x2 * cos + x1 * sin], axis=-1).astype(x.dtype)


def causal_attention(q, k, v):
    B, S, H, Dq = q.shape
    scale = Dq ** -0.5
    outs = []
    for i in range(S // Q_BLOCK):
        lo, hi = i * Q_BLOCK, (i + 1) * Q_BLOCK
        s = jnp.einsum('bqhd,bkhd->bhqk', q[:, lo:hi], k[:, :hi],
                       preferred_element_type=jnp.float32) * scale
        q_pos = lo + jnp.arange(Q_BLOCK)
        k_pos = jnp.arange(hi)
        s = jnp.where(k_pos[None, :] <= q_pos[:, None], s, -jnp.inf)
        p = jax.nn.softmax(s, axis=-1).astype(v.dtype)
        outs.append(jnp.einsum('bhqk,bkhd->bqhd', p, v[:, :hi]))
    return jnp.concatenate(outs, axis=1)


def mla_branch(c_q, c_kv, k_rope, positions, q_a_norm_w, w_q_up, kv_a_norm_w, w_kv_up,
               q_norm_w, k_norm_w, w_o_mla):
    B, S, _ = c_q.shape
    q = (rms_norm(c_q, q_a_norm_w) @ w_q_up).reshape(B, S, MLA_HEADS, MLA_QK)
    kv = (rms_norm(c_kv, kv_a_norm_w) @ w_kv_up).reshape(B, S, MLA_HEADS, MLA_NOPE + MLA_V)
    k_nope, v = kv[..., :MLA_NOPE], kv[..., MLA_NOPE:]
    k_rope = jnp.broadcast_to(k_rope[:, :, None, :], (B, S, MLA_HEADS, MLA_ROPE)).astype(k_nope.dtype)
    k = jnp.concatenate([k_nope, k_rope], axis=-1)
    q = rms_norm(q, q_norm_w)
    k = rms_norm(k, k_norm_w)
    q = jnp.concatenate([q[..., :MLA_NOPE], apply_rope(q[..., MLA_NOPE:], positions)], axis=-1)
    k = jnp.concatenate([k[..., :MLA_NOPE], apply_rope(k[..., MLA_NOPE:], positions)], axis=-1)
    o = causal_attention(q, k, v)
    return o.reshape(B, S, MLA_HEADS * MLA_V) @ w_o_mla


def wkv7_scan(r, decay, k, v, a_vec, b_vec):
    B, S, H, N = r.shape

    def step(state, inp):
        r_t, w_t, k_t, v_t, a_t, b_t = inp
        sa = jnp.einsum('bhvk,bhk->bhv', state, a_t)
        state = (state * w_t[:, :, None, :] + sa[..., None] * b_t[:, :, None, :]
                 + v_t[..., None] * k_t[:, :, None, :])
        y = jnp.einsum('bhvk,bhk->bhv', state, r_t)
        return state, y

    xs = tuple(jnp.moveaxis(t.astype(jnp.float32), 1, 0) for t in (r, decay, k, v, a_vec, b_vec))
    s0 = jnp.zeros((B, H, N, N), jnp.float32)
    _, ys = lax.scan(step, s0, xs)
    return jnp.moveaxis(ys, 0, 1)


def rwkv7_branch(slab, rwkv_mu, w0, w2, a0, a2, g2, k_k, k_a, r_k, ln_w, ln_b, w_o):
    B, S, _ = slab.shape
    p = slab.astype(jnp.float32)
    prev = jnp.pad(p[:, :-1], ((0, 0), (1, 0), (0, 0)))
    p = p + (prev - p) * rwkv_mu
    r, k, v, w_lo, a_lo, g_lo = _split(
        p, (RWKV_WIDTH, RWKV_WIDTH, RWKV_WIDTH, DECAY_LORA, AAA_LORA, GATE_LORA))
    log_w = -jax.nn.softplus(-(w0 + jnp.tanh(w_lo) @ w2)) - 0.5
    decay = jnp.exp(-jnp.exp(log_w.astype(jnp.float32)))
    a = jax.nn.sigmoid(a0 + a_lo @ a2)
    g = jax.nn.sigmoid(g_lo) @ g2

    def heads(t):
        return t.reshape(B, S, RWKV_HEADS, RWKV_HEAD).astype(jnp.float32)

    kk = heads(k * k_k)
    kk = kk / jnp.maximum(jnp.sqrt(jnp.sum(kk * kk, axis=-1, keepdims=True)), 1e-12)
    k = k * (1.0 + (a - 1.0) * k_a)
    rh, kh, vh, ah = heads(r), heads(k), heads(v), heads(a)
    y = wkv7_scan(rh, heads(decay), kh, vh, -kk, kk * ah)
    mu = jnp.mean(y, axis=-1, keepdims=True)
    var = jnp.mean(jnp.square(y - mu), axis=-1, keepdims=True)
    y = ((y - mu) * lax.rsqrt(var + GN_EPS)).reshape(B, S, RWKV_WIDTH) * ln_w + ln_b
    bonus = jnp.sum(rh * kh * r_k, axis=-1, keepdims=True) * vh
    y = (y + bonus.reshape(B, S, RWKV_WIDTH)) * g
    return y @ w_o


def hybrid_mixer(h, positions, w_in, q_a_norm_w, w_q_up, kv_a_norm_w, w_kv_up, q_norm_w, k_norm_w,
                 w_o_mla, rwkv_mu, rwkv_w0, rwkv_w2, rwkv_a0, rwkv_a2, rwkv_g2, rwkv_k_k, rwkv_k_a,
                 rwkv_r_k, rwkv_ln_w, rwkv_ln_b, rwkv_w_o, w_out):
    c_q, c_kv, k_rope, slab, gate_a, gate_b = _split(h @ w_in, IN_SIZES)
    o_a = mla_branch(c_q, c_kv, k_rope, positions, q_a_norm_w, w_q_up, kv_a_norm_w, w_kv_up,
                     q_norm_w, k_norm_w, w_o_mla)
    o_b = rwkv7_branch(slab, rwkv_mu, rwkv_w0, rwkv_w2, rwkv_a0, rwkv_a2, rwkv_g2, rwkv_k_k,
                       rwkv_k_a, rwkv_r_k, rwkv_ln_w, rwkv_ln_b, rwkv_w_o)
    merged = (jax.nn.sigmoid(gate_a.astype(jnp.float32)) * o_a.astype(jnp.float32)
              + jax.nn.sigmoid(gate_b.astype(jnp.float32)) * o_b.astype(jnp.float32))
    return merged.astype(h.dtype) @ w_out


def moe_ffn(h, w_router, b_router, w_gate_up, b_gate_up, w_down, b_down):
    B, S, D = h.shape
    T = B * S
    n_assign = T * TOP_K
    hf = h.reshape(T, D)
    logits = jnp.dot(hf, w_router, preferred_element_type=jnp.float32) + b_router.astype(jnp.float32)
    top_v, top_e = lax.top_k(logits, TOP_K)
    top_w = jax.nn.softmax(top_v, axis=-1)
    flat_e = top_e.reshape(-1)
    flat_tok = jnp.arange(n_assign, dtype=jnp.int32) // TOP_K
    order = jnp.argsort(flat_e)
    sorted_e = flat_e[order]
    counts = jnp.bincount(flat_e, length=N_EXPERTS)
    padded = (counts + EXPERT_BLOCK - 1) // EXPERT_BLOCK * EXPERT_BLOCK
    pad_end = jnp.cumsum(padded)
    pad_start = pad_end - padded
    start = jnp.cumsum(counts) - counts
    dest = pad_start[sorted_e] + jnp.arange(n_assign) - start[sorted_e]
    n_rows = n_assign + N_EXPERTS * EXPERT_BLOCK
    n_blocks = n_rows // EXPERT_BLOCK
    row_tok = jnp.zeros((n_rows,), jnp.int32).at[dest].set(flat_tok[order])
    row_w = jnp.zeros((n_rows,), jnp.float32).at[dest].set(top_w.reshape(-1)[order])
    blk_e = jnp.minimum(jnp.searchsorted(pad_end, jnp.arange(n_blocks) * EXPERT_BLOCK, side='right'),
                        N_EXPERTS - 1)
    xs = hf[row_tok].reshape(n_blocks, EXPERT_BLOCK, D)

    def expert_block(args):
        xb, e = args
        gu = xb @ w_gate_up[e] + b_gate_up[e]
        gate = jnp.minimum(gu[:, 0::2], SWIGLU_LIMIT)
        up = jnp.clip(gu[:, 1::2], -SWIGLU_LIMIT, SWIGLU_LIMIT)
        act = (up + 1.0) * gate * jax.nn.sigmoid(SWIGLU_ALPHA * gate)
        return act @ w_down[e] + b_down[e]

    ys = lax.map(expert_block, (xs, blk_e)).reshape(n_rows, D)
    out = jnp.zeros((T, D), jnp.float32).at[row_tok].add(ys.astype(jnp.float32) * row_w[:, None])
    return out.reshape(B, S, D).astype(h.dtype)


def setup_inputs(seed: int = 0) -> dict:
    key = jax.random.key(seed)
    ks = iter(jax.random.split(key, 40))
    L, D = DEPTH, D_MODEL

    def nrm(shape, scale):
        return jax.random.normal(next(ks), shape, jnp.float32) * scale

    def gain(shape):
        return 1.0 + nrm(shape, 0.05)

    x = nrm((BATCH, SEQ, D), 1.0)
    c = nrm((BATCH, D), 1.0)
    offsets = jax.random.randint(next(ks), (BATCH, 1), 0, 1024, dtype=jnp.int32)
    positions = offsets + jnp.arange(SEQ, dtype=jnp.int32)[None, :]
    return dict(
        x=x, c=c, positions=positions,
        ada_w=nrm((L, D, 6 * D), 0.5 * D ** -0.5), ada_b=nrm((L, 6 * D), 0.1),
        norm_mix_w=gain((L, D)), norm_ffn_w=gain((L, D)),
        w_in=nrm((L, D, D_IN), D ** -0.5),
        q_a_norm_w=gain((L, Q_LORA)),
        w_q_up=nrm((L, Q_LORA, MLA_HEADS * MLA_QK), Q_LORA ** -0.5),
        kv_a_norm_w=gain((L, KV_LORA)),
        w_kv_up=nrm((L, KV_LORA, MLA_HEADS * (MLA_NOPE + MLA_V)), KV_LORA ** -0.5),
        q_norm_w=gain((L, MLA_QK)), k_norm_w=gain((L, MLA_QK)),
        w_o_mla=nrm((L, MLA_HEADS * MLA_V, D), (MLA_HEADS * MLA_V) ** -0.5),
        rwkv_mu=jax.random.uniform(next(ks), (L, RWKV_SLAB), jnp.float32, 0.0, 1.0),
        rwkv_w0=jax.random.uniform(next(ks), (L, RWKV_WIDTH), jnp.float32, -6.0, 1.0),
        rwkv_w2=nrm((L, DECAY_LORA, RWKV_WIDTH), 0.5 * DECAY_LORA ** -0.5),
        rwkv_a0=nrm((L, RWKV_WIDTH), 0.5),
        rwkv_a2=nrm((L, AAA_LORA, RWKV_WIDTH), 0.5 * AAA_LORA ** -0.5),
        rwkv_g2=nrm((L, GATE_LORA, RWKV_WIDTH), GATE_LORA ** -0.5),
        rwkv_k_k=0.85 + nrm((L, RWKV_WIDTH), 0.05),
        rwkv_k_a=1.0 + nrm((L, RWKV_WIDTH), 0.05),
        rwkv_r_k=nrm((L, RWKV_HEADS, RWKV_HEAD), 0.1),
        rwkv_ln_w=gain((L, RWKV_WIDTH)), rwkv_ln_b=nrm((L, RWKV_WIDTH), 0.02),
        rwkv_w_o=nrm((L, RWKV_WIDTH, D), RWKV_WIDTH ** -0.5),
        w_out=nrm((L, D, D), D ** -0.5),
        w_router=nrm((L, D, N_EXPERTS), D ** -0.5), b_router=nrm((L, N_EXPERTS), 0.01),
        w_gate_up=nrm((L, N_EXPERTS, D, 2 * D_FF), D ** -0.5),
        b_gate_up=nrm((L, N_EXPERTS, 2 * D_FF), 0.01),
        w_down=nrm((L, N_EXPERTS, D_FF, D), D_FF ** -0.5),
        b_down=nrm((L, N_EXPERTS, D), 0.01),
    )


def reference(x, c, positions, ada_w, ada_b, norm_mix_w, norm_ffn_w, w_in, q_a_norm_w, w_q_up,
              kv_a_norm_w, w_kv_up, q_norm_w, k_norm_w, w_o_mla, rwkv_mu, rwkv_w0, rwkv_w2,
              rwkv_a0, rwkv_a2, rwkv_g2, rwkv_k_k, rwkv_k_a, rwkv_r_k, rwkv_ln_w, rwkv_ln_b,
              rwkv_w_o, w_out, w_router, b_router, w_gate_up, b_gate_up, w_down, b_down):
    cond = jax.nn.silu(c)
    for l in range(DEPTH):
        mod = cond @ ada_w[l] + ada_b[l]
        shift1, scale1, gate1, shift2, scale2, gate2 = jnp.split(mod, 6, axis=-1)
        h = modulate(x, norm_mix_w[l], shift1, scale1)
        mix = hybrid_mixer(h, positions, w_in[l], q_a_norm_w[l], w_q_up[l], kv_a_norm_w[l],
                           w_kv_up[l], q_norm_w[l], k_norm_w[l], w_o_mla[l], rwkv_mu[l],
                           rwkv_w0[l], rwkv_w2[l], rwkv_a0[l], rwkv_a2[l], rwkv_g2[l], rwkv_k_k[l],
                           rwkv_k_a[l], rwkv_r_k[l], rwkv_ln_w[l], rwkv_ln_b[l], rwkv_w_o[l], w_out[l])
        x = x + (gate1[:, None, :] * mix).astype(x.dtype)
        h = modulate(x, norm_ffn_w[l], shift2, scale2)
        ffn = moe_ffn(h, w_router[l], b_router[l], w_gate_up[l], b_gate_up[l], w_down[l], b_down[l])
        x = x + (gate2[:, None, :] * ffn).astype(x.dtype)
    return x
```

```python
import functools

import numpy as np
import jax
import jax.numpy as jnp
from jax import lax
from jax.experimental import pallas as pl
from jax.experimental.pallas import tpu as pltpu

F32 = jnp.float32
BF16 = jnp.bfloat16

D_MODEL = 1024
MLA_HEADS = 8
MLA_NOPE = 64
MLA_ROPE = 32
MLA_QK = MLA_NOPE + MLA_ROPE
MLA_V = 64
Q_LORA = 256
KV_LORA = 128
ROPE_THETA = 10000.0
RWKV_HEADS = 8
RWKV_HEAD = 64
RWKV_WIDTH = RWKV_HEADS * RWKV_HEAD
DECAY_LORA = 64
AAA_LORA = 64
GATE_LORA = 128
RWKV_SLAB = 3 * RWKV_WIDTH + DECAY_LORA + AAA_LORA + GATE_LORA
GN_EPS = 64e-5
N_EXPERTS = 32
TOP_K = 4
D_FF = D_MODEL
SWIGLU_LIMIT = 7.0
SWIGLU_ALPHA = 1.702
NORM_EPS = 1e-6

LANES = 128
HEAD_PAD = 128
MLA_COLS = 640
WKV_CHUNK = 64
MOE_BLOCK = 256
VMEM_LIMIT = 56 * 1024 * 1024


def _dot(a, b):
    return jnp.dot(a, b, preferred_element_type=F32)


def _dot_nt(a, b):
    return lax.dot_general(a, b, (((1,), (1,)), ((), ())), preferred_element_type=F32)


def _split(x):
    hi = x.astype(BF16)
    lo = (x - hi.astype(F32)).astype(BF16)
    return hi, lo


def _mm3(a, b):
    ah, al = _split(a)
    bh, bl = _split(b)
    return _dot(ah, bh) + (_dot(ah, bl) + _dot(al, bh))


def _mm3_nt(a, b):
    ah, al = _split(a)
    bh, bl = _split(b)
    return _dot_nt(ah, bh) + (_dot_nt(ah, bl) + _dot_nt(al, bh))


def _mm_exact_rhs(a, b_bf16):
    ah, al = _split(a)
    return _dot(ah, b_bf16) + _dot(al, b_bf16)


def _params(*sem):
    return pltpu.CompilerParams(dimension_semantics=sem, vmem_limit_bytes=VMEM_LIMIT)


def _ada_kernel(c_ref, w_ref, b_ref, o_ref):
    c = c_ref[...]
    cond = c * jax.nn.sigmoid(c)
    o_ref[...] = _mm3(cond, w_ref[...]) + b_ref[...]


def _ada(c, w, b):
    bsz, d = c.shape
    n = w.shape[1]
    tn = 1024
    return pl.pallas_call(
        _ada_kernel,
        grid=(n // tn,),
        in_specs=[pl.BlockSpec((bsz, d), lambda j: (0, 0)),
                  pl.BlockSpec((d, tn), lambda j: (0, j)),
                  pl.BlockSpec((1, tn), lambda j: (0, j))],
        out_specs=pl.BlockSpec((bsz, tn), lambda j: (0, j)),
        out_shape=jax.ShapeDtypeStruct((bsz, n), F32),
        compiler_params=_params("parallel"),
        name="ada",
    )(c, w, b.reshape(1, n))


def _inproj_kernel(x_ref, mod_ref, nw_ref, w_ref, mla_ref, slab_ref, gates_ref):
    x = x_ref[0]
    ms = jnp.mean(x * x, axis=-1, keepdims=True)
    y = x * lax.rsqrt(ms + NORM_EPS) * nw_ref[...]
    h = y * (1.0 + mod_ref[0, 1:2, :]) + mod_ref[0, 0:1, :]
    hb = h.astype(BF16)
    mla_ref[0] = _dot(hb, w_ref[:, 0:MLA_COLS])
    slab_ref[0] = _dot(hb, w_ref[:, MLA_COLS:MLA_COLS + RWKV_SLAB])
    gates_ref[0] = _dot(hb, w_ref[:, MLA_COLS + RWKV_SLAB:]).astype(BF16)


def _inproj(x, mod3, norm_w, w_in_p, tm):
    bsz, s, d = x.shape
    ncol = w_in_p.shape[1]
    return pl.pallas_call(
        _inproj_kernel,
        grid=(bsz, s // tm),
        in_specs=[pl.BlockSpec((1, tm, d), lambda b, i: (b, i, 0)),
                  pl.BlockSpec((1, 6, d), lambda b, i: (b, 0, 0)),
                  pl.BlockSpec((1, d), lambda b, i: (0, 0)),
                  pl.BlockSpec((d, ncol), lambda b, i: (0, 0))],
        out_specs=[pl.BlockSpec((1, tm, MLA_COLS), lambda b, i: (b, i, 0)),
                   pl.BlockSpec((1, tm, RWKV_SLAB), lambda b, i: (b, i, 0)),
                   pl.BlockSpec((1, tm, 2 * d), lambda b, i: (b, i, 0))],
        out_shape=[jax.ShapeDtypeStruct((bsz, s, MLA_COLS), F32),
                   jax.ShapeDtypeStruct((bsz, s, RWKV_SLAB), F32),
                   jax.ShapeDtypeStruct((bsz, s, 2 * d), BF16)],
        compiler_params=_params("parallel", "parallel"),
        name="inproj",
    )(x, mod3, norm_w.reshape(1, d), w_in_p)


def _mla_prep_kernel(mla_ref, pos_ref, invf_ref, qan_ref, kvn_ref, wq_ref, wkv_ref,
                     qg_ref, qgs_ref, kg_ref, kgs_ref, q_out, k_out, v_out):
    m = mla_ref[0]
    cq = m[:, 0:Q_LORA]
    ckv = m[:, Q_LORA:Q_LORA + KV_LORA]
    kr = m[:, 384:512]
    krs = m[:, 512:640]
    ql = cq * lax.rsqrt(jnp.mean(cq * cq, axis=-1, keepdims=True) + NORM_EPS) * qan_ref[...]
    kvl = ckv * lax.rsqrt(jnp.mean(ckv * ckv, axis=-1, keepdims=True) + NORM_EPS) * kvn_ref[...]
    qall = _dot(ql.astype(BF16), wq_ref[...])
    kvall = _dot(kvl.astype(BF16), wkv_ref[...])
    ang = pos_ref[0] * invf_ref[...]
    cosf = jnp.cos(ang)
    sinf = jnp.sin(ang)
    scale = MLA_QK ** -0.5
    hw = MLA_HEADS * HEAD_PAD
    for h in range(MLA_HEADS):
        lo, hi = h * HEAD_PAD, (h + 1) * HEAD_PAD
        qh = qall[:, lo:hi]
        qs = qall[:, hw + lo:hw + hi]
        rs = lax.rsqrt(jnp.sum(qh * qh, axis=-1, keepdims=True) * (1.0 / MLA_QK) + NORM_EPS)
        qo = (qh * rs * qg_ref[...]) * cosf + (qs * rs * qgs_ref[...]) * sinf
        q_out[0, h] = (qo * scale).astype(BF16)
        kh = kvall[:, lo:hi] + kr
        rs = lax.rsqrt(jnp.sum(kh * kh, axis=-1, keepdims=True) * (1.0 / MLA_QK) + NORM_EPS)
        ko = (kh * rs * kg_ref[...]) * cosf + (krs * rs * kgs_ref[...]) * sinf
        k_out[0, h] = ko.astype(BF16)
        v_out[0, h] = kvall[:, hw + lo:hw + hi].astype(BF16)


def _mla_prep(mla_in, pos_b, invf, qan, kvn, wq, wkv, qg, qgs, kg, kgs, tm):
    bsz, s, _ = mla_in.shape
    hw = MLA_HEADS * HEAD_PAD
    row = lambda n: pl.BlockSpec((1, n), lambda b, i: (0, 0))
    out_spec = pl.BlockSpec((1, MLA_HEADS, tm, HEAD_PAD), lambda b, i: (b, 0, i, 0))
    out_sds = jax.ShapeDtypeStruct((bsz, MLA_HEADS, s, HEAD_PAD), BF16)
    return pl.pallas_call(
        _mla_prep_kernel,
        grid=(bsz, s // tm),
        in_specs=[pl.BlockSpec((1, tm, MLA_COLS), lambda b, i: (b, i, 0)),
                  pl.BlockSpec((1, tm, LANES), lambda b, i: (b, i, 0)),
                  row(LANES), row(Q_LORA), row(KV_LORA),
                  pl.BlockSpec((Q_LORA, 2 * hw), lambda b, i: (0, 0)),
                  pl.BlockSpec((KV_LORA, 2 * hw), lambda b, i: (0, 0)),
                  row(LANES), row(LANES), row(LANES), row(LANES)],
        out_specs=[out_spec, out_spec, out_spec],
        out_shape=[out_sds, out_sds, out_sds],
        compiler_params=_params("parallel", "parallel"),
        name="mla_prep",
    )(mla_in, pos_b, invf, qan, kvn, wq, wkv, qg, qgs, kg, kgs)


def _attn_kernel(q_ref, k_ref, v_ref, o_ref, *, tq):
    qi = pl.program_id(2)
    q = q_ref[0, 0]

    def step(kb, vb, carry, mask):
        m, l, acc = carry
        s = _dot_nt(q, kb)
        if mask is not None:
            s = jnp.where(mask, s, -jnp.inf)
        m_new = jnp.maximum(m, jnp.max(s, axis=-1, keepdims=True))
        p = jnp.exp(s - m_new)
        alpha = jnp.exp(m - m_new)
        l_new = alpha * l + jnp.sum(p, axis=-1, keepdims=True)
        acc_new = alpha * acc + _dot(p.astype(BF16), vb)
        return m_new, l_new, acc_new

    def body(j, carry):
        off = pl.multiple_of(j * tq, tq)
        return step(k_ref[0, 0, pl.ds(off, tq), :], v_ref[0, 0, pl.ds(off, tq), :], carry, None)

    init = (jnp.full((tq, 1), -jnp.inf, F32), jnp.zeros((tq, 1), F32), jnp.zeros((tq, HEAD_PAD), F32))
    carry = lax.fori_loop(0, qi, body, init)
    off = pl.multiple_of(qi * tq, tq)
    rows = lax.broadcasted_iota(jnp.int32, (tq, tq), 0)
    cols = lax.broadcasted_iota(jnp.int32, (tq, tq), 1)
    _, l, acc = step(k_ref[0, 0, pl.ds(off, tq), :], v_ref[0, 0, pl.ds(off, tq), :], carry, cols <= rows)
    o_ref[0, 0] = (acc / l).astype(BF16)


def _attention(q, k, v, tq):
    bsz, nh, s, dh = q.shape
    return pl.pallas_call(
        functools.partial(_attn_kernel, tq=tq),
        grid=(bsz, nh, s // tq),
        in_specs=[pl.BlockSpec((1, 1, tq, dh), lambda b, h, i: (b, h, i, 0)),
                  pl.BlockSpec((1, 1, s, dh), lambda b, h, i: (b, h, 0, 0)),
                  pl.BlockSpec((1, 1, s, dh), lambda b, h, i: (b, h, 0, 0))],
        out_specs=pl.BlockSpec((1, 1, tq, dh), lambda b, h, i: (b, h, i, 0)),
        out_shape=jax.ShapeDtypeStruct((bsz, nh, s, dh), BF16),
        compiler_params=_params("parallel", "parallel", "parallel"),
        name="attn",
    )(q, k, v)


def _softplus(x):
    return jnp.maximum(x, 0.0) + jnp.log(1.0 + jnp.exp(-jnp.abs(x)))


def _rwkv_prep_kernel(slab_ref, mu_ref, wwa_ref, w0_ref, a0_ref, g2_ref, kk_ref, ka_ref, bd_ref,
                      r_out, lw_out, k_out, v_out, a_out, b_out, g_out, last_ref):
    i = pl.program_id(1)
    tm = slab_ref.shape[1]
    w = RWKV_WIDTH

    @pl.when(i == 0)
    def _():
        last_ref[...] = jnp.zeros_like(last_ref)

    slab = slab_ref[0]
    rolled = pltpu.roll(slab, 1, 0)
    rowi = lax.broadcasted_iota(jnp.int32, slab.shape, 0)
    prev = jnp.where(rowi == 0, last_ref[0:1, :], rolled)
    last_ref[0:1, :] = slab[tm - 1:tm, :]
    p = slab + (prev - slab) * mu_ref[...]
    r = p[:, 0:w]
    k = p[:, w:2 * w]
    v = p[:, 2 * w:3 * w]
    wa = p[:, 3 * w:3 * w + LANES]
    g_lo = p[:, 3 * w + LANES:3 * w + 2 * LANES]
    lane = lax.broadcasted_iota(jnp.int32, wa.shape, 1)
    wa = jnp.where(lane < DECAY_LORA, jnp.tanh(wa), wa)
    wa_o = _mm3(wa, wwa_ref[...])
    log_w = -_softplus(-(w0_ref[...] + wa_o[:, 0:w])) - 0.5
    lw = -jnp.exp(log_w)
    a = jax.nn.sigmoid(a0_ref[...] + wa_o[:, w:2 * w])
    g = _mm3(jax.nn.sigmoid(g_lo), g2_ref[...])
    kk = k * kk_ref[...]
    ss = _mm_exact_rhs(kk * kk, bd_ref[...])
    kk = kk / jnp.maximum(jnp.sqrt(ss), 1e-12)
    r_out[0] = r
    lw_out[0] = lw
    k_out[0] = k * (1.0 + (a - 1.0) * ka_ref[...])
    v_out[0] = v
    a_out[0] = -kk
    b_out[0] = kk * a
    g_out[0] = g


def _rwkv_prep(slab, mu, wwa, w0, a0, g2, k_k, k_a, bd, tm):
    bsz, s, _ = slab.shape
    w = RWKV_WIDTH
    row = lambda n: pl.BlockSpec((1, n), lambda b, i: (0, 0))
    full = lambda a: pl.BlockSpec(a.shape, lambda b, i: (0, 0))
    out_spec = pl.BlockSpec((1, tm, w), lambda b, i: (b, i, 0))
    out_sds = jax.ShapeDtypeStruct((bsz, s, w), F32)
    return pl.pallas_call(
        _rwkv_prep_kernel,
        grid=(bsz, s // tm),
        in_specs=[pl.BlockSpec((1, tm, RWKV_SLAB), lambda b, i: (b, i, 0)),
                  row(RWKV_SLAB), full(wwa), row(w), row(w), full(g2), row(w), row(w), full(bd)],
        out_specs=[out_spec] * 7,
        out_shape=[out_sds] * 7,
        scratch_shapes=[pltpu.VMEM((8, RWKV_SLAB), F32)],
        compiler_params=_params("parallel", "arbitrary"),
        name="rwkv_prep",
    )(slab, mu, wwa, w0, a0, g2, k_k, k_a, bd)


def _wkv_kernel(r_ref, lw_ref, k_ref, v_ref, a_ref, b_ref, y_ref, s_ref):
    c = WKV_CHUNK
    n = pl.program_id(1)

    @pl.when(n == 0)
    def _():
        s_ref[...] = jnp.zeros_like(s_ref)

    row = lax.broadcasted_iota(jnp.int32, (LANES, LANES), 0)
    col = lax.broadcasted_iota(jnp.int32, (LANES, LANES), 1)
    lower_strict = row > col
    lower_incl = row >= col
    eye = row == col
    same16 = (row >> 4) == (col >> 4)
    same32 = (row >> 5) == (col >> 5)
    ident = jnp.where(eye, 1.0, 0.0).astype(F32)
    tr = lax.broadcasted_iota(jnp.int32, (c, c), 0)
    tc = lax.broadcasted_iota(jnp.int32, (c, c), 1)
    tri = jnp.where(tr >= tc, 1.0, 0.0).astype(BF16)
    first = lax.broadcasted_iota(jnp.int32, (c, LANES), 1) < RWKV_HEAD

    def stack(x):
        return jnp.concatenate([jnp.where(first, x, 0.0), jnp.where(first, 0.0, x)], axis=0)

    for p in range(RWKV_HEADS // 2):
        sl = slice(p * LANES, (p + 1) * LANES)
        r = r_ref[0, :, sl]
        lw = lw_ref[0, :, sl]
        k = k_ref[0, :, sl]
        v = v_ref[0, :, sl]
        a = a_ref[0, :, sl]
        b = b_ref[0, :, sl]
        lw_hi = lw.astype(BF16)
        rem = lw - lw_hi.astype(F32)
        lw_mid = rem.astype(BF16)
        lw_lo = (rem - lw_mid.astype(F32)).astype(BF16)
        cum = _dot(tri, lw_hi) + (_dot(tri, lw_mid) + _dot(tri, lw_lo))
        cum_x = cum - lw
        cum_c = cum[c - 1:c, :]
        e_in = jnp.exp(cum)
        e_neg = jnp.exp(-cum)
        e_end = jnp.exp(cum_c - cum)
        a_s = stack(a * jnp.exp(cum_x))
        r_s = stack(r * e_in)
        b_s = stack(b * e_neg)
        k_s = stack(k * e_neg)
        b_e = stack(b * e_end)
        k_e = stack(k * e_end)
        v_s = stack(v)

        tt = _mm3_nt(jnp.concatenate([a_s, r_s], axis=0), jnp.concatenate([b_s, k_s], axis=0))
        d_ab = jnp.where(lower_strict, tt[0:LANES, 0:LANES], 0.0)
        e_ak = jnp.where(lower_strict, tt[0:LANES, LANES:], 0.0)
        f_rb = jnp.where(lower_incl, tt[LANES:, 0:LANES], 0.0)
        f_rk = jnp.where(lower_incl, tt[LANES:, LANES:], 0.0)

        d16 = jnp.where(same16, d_ab, 0.0)
        d32 = jnp.where(same32, d_ab, 0.0)
        x2 = _mm3(d16, d16)
        x4 = _mm3(x2, x2)
        x8 = _mm3(x4, x4)
        t = ident + d16
        t = t + _mm3(t, x2)
        t = t + _mm3(t, x4)
        t = t + _mm3(t, x8)
        t = t + _mm3(_mm3(t, d32 - d16), t)
        t = t + _mm3(_mm3(t, d_ab - d32), t)

        z = _mm3(e_ak, v_s)
        au = _mm3(t, jnp.concatenate([a_s, z], axis=1))
        g = _mm3(f_rb, au)
        r1 = r_s + g[:, 0:LANES]
        y0 = g[:, LANES:] + _mm3(f_rk, v_s)
        hmat = _mm3(b_e.T, au)
        w_c = jnp.exp(cum_c)
        m_mat = jnp.where(eye, w_c, 0.0) + hmat[:, 0:LANES]
        n_mat = hmat[:, LANES:] + _mm3(k_e.T, v_s)

        s0 = s_ref[p]
        ys = _mm3(r1, s0) + y0
        s_ref[p] = _mm3(m_mat, s0) + n_mat
        y_ref[0, :, sl] = ys[0:c, :] + ys[c:, :]


def _wkv(r, lw, k, v, a, b):
    bsz, s, w = r.shape
    spec = pl.BlockSpec((1, WKV_CHUNK, w), lambda bi, n: (bi, n, 0))
    return pl.pallas_call(
        _wkv_kernel,
        grid=(bsz, s // WKV_CHUNK),
        in_specs=[spec] * 6,
        out_specs=spec,
        out_shape=jax.ShapeDtypeStruct((bsz, s, w), F32),
        scratch_shapes=[pltpu.VMEM((RWKV_HEADS // 2, LANES, LANES), F32)],
        compiler_params=_params("parallel", "arbitrary"),
        name="wkv",
    )(r, lw, k, v, a, b)


def _post_kernel(x_ref, mod_ref, y_ref, r_ref, k_ref, v_ref, g_ref, o_ref, gates_ref,
                 lnw_ref, lnb_ref, rk_ref, bd_ref, wor_ref, woa_ref, wout_ref, nfw_ref,
                 wrh_ref, wrl_ref, br_ref,
                 x1_ref, h2_ref, route_ref, cnt_ref, run_ref):
    first_step = jnp.logical_and(pl.program_id(0) == 0, pl.program_id(1) == 0)

    @pl.when(first_step)
    def _():
        run_ref[...] = jnp.zeros_like(run_ref)

    tm = x_ref.shape[1]
    d = x_ref.shape[2]
    bd = bd_ref[...]
    inv_n = 1.0 / RWKV_HEAD
    y = y_ref[0]
    mu = _mm_exact_rhs(y, bd) * inv_n
    dlt = y - mu
    var = _mm_exact_rhs(dlt * dlt, bd) * inv_n
    yn = dlt * lax.rsqrt(var + GN_EPS) * lnw_ref[...] + lnb_ref[...]
    v = v_ref[0]
    bonus = _mm_exact_rhs(r_ref[0] * k_ref[0] * rk_ref[...], bd) * v
    z = (yn + bonus) * g_ref[0]
    o_b = _dot(z.astype(BF16), wor_ref[...])
    o_a = _dot(o_ref[0, 0], woa_ref[0])
    for h in range(1, MLA_HEADS):
        o_a = o_a + _dot(o_ref[0, h], woa_ref[h])
    ga = jax.nn.sigmoid(gates_ref[0, :, 0:d].astype(F32))
    gb = jax.nn.sigmoid(gates_ref[0, :, d:2 * d].astype(F32))
    merged = ga * o_a + gb * o_b
    mix = _dot(merged.astype(BF16), wout_ref[...])
    x1 = x_ref[0] + mod_ref[0, 2:3, :] * mix
    x1_ref[0] = x1
    ms = jnp.mean(x1 * x1, axis=-1, keepdims=True)
    h2 = x1 * lax.rsqrt(ms + NORM_EPS) * nfw_ref[...]
    h2 = h2 * (1.0 + mod_ref[0, 4:5, :]) + mod_ref[0, 3:4, :]
    h2_ref[0] = h2

    hh, hl = _split(h2)
    logits = _dot(hh, wrh_ref[...]) + (_dot(hh, wrl_ref[...]) + _dot(hl, wrh_ref[...])) + br_ref[...]
    lane = lax.broadcasted_iota(jnp.int32, (tm, LANES), 1)
    lanef = lane.astype(F32)
    cur = jnp.where(lane < N_EXPERTS, logits, -jnp.inf)
    vals, idxs, hots = [], [], []
    for _ in range(TOP_K):
        mx = jnp.max(cur, axis=-1, keepdims=True)
        idx = jnp.min(jnp.where(cur == mx, lanef, float(LANES)), axis=-1, keepdims=True)
        hot = lanef == idx
        cur = jnp.where(hot, -jnp.inf, cur)
        vals.append(mx)
        idxs.append(idx)
        hots.append(hot)
    exps = [jnp.exp(vv - vals[0]) for vv in vals]
    den = exps[0] + exps[1] + exps[2] + exps[3]
    sel = jnp.zeros((tm, LANES), F32)
    for hot in hots:
        sel = sel + jnp.where(hot, 1.0, 0.0)
    ri = lax.broadcasted_iota(jnp.int32, (tm, tm), 0)
    ci = lax.broadcasted_iota(jnp.int32, (tm, tm), 1)
    below = jnp.where(ri > ci, 1.0, 0.0).astype(BF16)
    run = run_ref[0:1, :]
    before = _dot(below, sel.astype(BF16)) + run
    route = jnp.zeros((tm, LANES), F32)
    for j in range(TOP_K):
        rank = jnp.sum(jnp.where(hots[j], before, 0.0), axis=-1, keepdims=True)
        route = jnp.where(lane == j, idxs[j], route)
        route = jnp.where(lane == TOP_K + j, exps[j] / den, route)
        route = jnp.where(lane == 2 * TOP_K + j, rank, route)
    route_ref[0] = route
    run = run + jnp.sum(sel, axis=0, keepdims=True)
    run_ref[0:1, :] = run
    cnt_ref[...] = jnp.broadcast_to(run, cnt_ref.shape)


def _post(x, mod3, y, r, k, v, g, o, gates, lnw, lnb, rk, bd, wor, woa, wout, nfw, wrh, wrl, br, tm):
    bsz, s, d = x.shape
    w = RWKV_WIDTH
    tok = lambda n: pl.BlockSpec((1, tm, n), lambda b, i: (b, i, 0))
    row = lambda n: pl.BlockSpec((1, n), lambda b, i: (0, 0))
    full = lambda a: pl.BlockSpec(a.shape, lambda b, i: (0,) * a.ndim)
    return pl.pallas_call(
        _post_kernel,
        grid=(bsz, s // tm),
        in_specs=[tok(d), pl.BlockSpec((1, 6, d), lambda b, i: (b, 0, 0)),
                  tok(w), tok(w), tok(w), tok(w), tok(w),
                  pl.BlockSpec((1, MLA_HEADS, tm, HEAD_PAD), lambda b, i: (b, 0, i, 0)),
                  tok(2 * d),
                  row(w), row(w), row(w), full(bd), full(wor), full(woa), full(wout), row(d),
                  full(wrh), full(wrl), row(LANES)],
        out_specs=[tok(d), tok(d), tok(LANES), pl.BlockSpec((8, LANES), lambda b, i: (0, 0))],
        out_shape=[jax.ShapeDtypeStruct((bsz, s, d), F32),
                   jax.ShapeDtypeStruct((bsz, s, d), F32),
                   jax.ShapeDtypeStruct((bsz, s, LANES), F32),
                   jax.ShapeDtypeStruct((8, LANES), F32)],
        scratch_shapes=[pltpu.VMEM((8, LANES), F32)],
        compiler_params=_params("arbitrary", "arbitrary"),
        name="post",
    )(x, mod3, y, r, k, v, g, o, gates, lnw, lnb, rk, bd, wor, woa, wout, nfw, wrh, wrl, br)


def _dispatch_kernel(dest_ref, h_ref, xs_in_ref, xs_ref, sem):
    del xs_in_ref
    tm = h_ref.shape[0]
    base = pl.program_id(0) * tm * TOP_K

    def copy(r, j):
        dst = dest_ref[base + r * TOP_K + j]
        return pltpu.make_async_copy(h_ref.at[pl.ds(r, 1), :], xs_ref.at[pl.ds(dst, 1), :], sem)

    def start(r, carry):
        for j in range(TOP_K):
            copy(r, j).start()
        return carry

    def wait(r, carry):
        for j in range(TOP_K):
            copy(r, j).wait()
        return carry

    lax.fori_loop(0, tm, start, 0)
    lax.fori_loop(0, tm, wait, 0)


def _dispatch(dest, h2, xs_init, tm):
    t, d = h2.shape
    return pl.pallas_call(
        _dispatch_kernel,
        grid_spec=pltpu.PrefetchScalarGridSpec(
            num_scalar_prefetch=1,
            grid=(t // tm,),
            in_specs=[pl.BlockSpec((tm, d), lambda i, dest: (i, 0)),
                      pl.BlockSpec(memory_space=pl.ANY)],
            out_specs=pl.BlockSpec(memory_space=pl.ANY),
            scratch_shapes=[pltpu.SemaphoreType.DMA(())]),
        out_shape=jax.ShapeDtypeStruct(xs_init.shape, F32),
        input_output_aliases={2: 0},
        compiler_params=_params("arbitrary"),
        name="dispatch",
    )(dest, h2, xs_init)


def _moe_kernel(blk_e_ref, n_used_ref, xs_ref, wg_ref, wu_ref, bg_ref, bu_ref, wd_ref, bd_ref, ys_ref):
    del blk_e_ref
    i = pl.program_id(0)

    @pl.when(i < n_used_ref[0])
    def _():
        x = xs_ref[...].astype(BF16)
        gate = _dot(x, wg_ref[0]) + bg_ref[0]
        up = _dot(x, wu_ref[0]) + bu_ref[0]
        gate = jnp.minimum(gate, SWIGLU_LIMIT)
        up = jnp.clip(up, -SWIGLU_LIMIT, SWIGLU_LIMIT)
        act = (up + 1.0) * gate * jax.nn.sigmoid(SWIGLU_ALPHA * gate)
        ys_ref[...] = _dot(act.astype(BF16), wd_ref[0]) + bd_ref[0]

    @pl.when(i >= n_used_ref[0])
    def _():
        ys_ref[...] = jnp.zeros_like(ys_ref)


def _moe(blk_e, n_used, xs, wg, wu, bg, bu, wd, bd):
    n_rows, d = xs.shape
    ff = wg.shape[2]
    bm = MOE_BLOCK
    wspec = lambda k, n: pl.BlockSpec((1, k, n), lambda i, be, nu: (be[i], 0, 0))
    return pl.pallas_call(
        _moe_kernel,
        grid_spec=pltpu.PrefetchScalarGridSpec(
            num_scalar_prefetch=2,
            grid=(n_rows // bm,),
            in_specs=[pl.BlockSpec((bm, d), lambda i, be, nu: (i, 0)),
                      wspec(d, ff), wspec(d, ff), wspec(1, ff), wspec(1, ff), wspec(ff, d), wspec(1, d)],
            out_specs=pl.BlockSpec((bm, d), lambda i, be, nu: (i, 0))),
        out_shape=jax.ShapeDtypeStruct((n_rows, d), F32),
        compiler_params=_params("arbitrary"),
        name="moe",
    )(blk_e, n_used, xs, wg, wu, bg, bu, wd, bd)


def _combine_kernel(dest_ref, x1_ref, route_ref, mod_ref, ys_ref, o_ref, buf_ref, sem):
    tm = x1_ref.shape[1]
    s = pl.num_programs(1) * tm
    base = (pl.program_id(0) * s + pl.program_id(1) * tm) * TOP_K

    def copy(r, j):
        src = dest_ref[base + r * TOP_K + j]
        return pltpu.make_async_copy(ys_ref.at[pl.ds(src, 1), :], buf_ref.at[j, pl.ds(r, 1), :], sem)

    def start(r, carry):
        for j in range(TOP_K):
            copy(r, j).start()
        return carry

    def wait(r, carry):
        for j in range(TOP_K):
            copy(r, j).wait()
        return carry

    lax.fori_loop(0, tm, start, 0)
    lax.fori_loop(0, tm, wait, 0)
    route = route_ref[0]
    acc = route[:, TOP_K:TOP_K + 1] * buf_ref[0]
    for j in range(1, TOP_K):
        acc = acc + route[:, TOP_K + j:TOP_K + j + 1] * buf_ref[j]
    o_ref[0] = x1_ref[0] + mod_ref[0, 5:6, :] * acc


def _combine(dest, x1, route, mod3, ys, tm):
    bsz, s, d = x1.shape
    return pl.pallas_call(
        _combine_kernel,
        grid_spec=pltpu.PrefetchScalarGridSpec(
            num_scalar_prefetch=1,
            grid=(bsz, s // tm),
            in_specs=[pl.BlockSpec((1, tm, d), lambda b, i, dest: (b, i, 0)),
                      pl.BlockSpec((1, tm, LANES), lambda b, i, dest: (b, i, 0)),
                      pl.BlockSpec((1, 6, d), lambda b, i, dest: (b, 0, 0)),
                      pl.BlockSpec(memory_space=pl.ANY)],
            out_specs=pl.BlockSpec((1, tm, d), lambda b, i, dest: (b, i, 0)),
            scratch_shapes=[pltpu.VMEM((TOP_K, tm, d), F32), pltpu.SemaphoreType.DMA(())]),
        out_shape=jax.ShapeDtypeStruct((bsz, s, d), F32),
        compiler_params=_params("arbitrary", "arbitrary"),
        name="combine",
    )(dest, x1, route, mod3, ys)


def _pad_cols(a, n):
    return jnp.pad(a, ((0, 0), (0, n - a.shape[1])))


def _head_blocks(cols_main, cols_rot=None):
    k = cols_main.shape[0]
    out = jnp.zeros((k, MLA_HEADS, HEAD_PAD), F32)
    out = out.at[:, :, :cols_main.shape[2]].set(cols_main)
    return out.reshape(k, MLA_HEADS * HEAD_PAD)


def _layer(x, cond_mod, positions, w_in, q_a_norm_w, w_q_up, kv_a_norm_w, w_kv_up, q_norm_w, k_norm_w,
           w_o_mla, rwkv_mu, rwkv_w0, rwkv_w2, rwkv_a0, rwkv_a2, rwkv_g2, rwkv_k_k, rwkv_k_a, rwkv_r_k,
           rwkv_ln_w, rwkv_ln_b, rwkv_w_o, w_out, norm_mix_w, norm_ffn_w, w_router, b_router,
           w_gate_up, b_gate_up, w_down, b_down):
    bsz, s, d = x.shape
    t = bsz * s
    half = MLA_ROPE // 2
    nope, qk = MLA_NOPE, MLA_QK
    mod3 = cond_mod.reshape(bsz, 6, d)

    o_q, o_kv, o_kr = 0, Q_LORA, Q_LORA + KV_LORA
    o_slab = o_kr + MLA_ROPE
    o_gate = o_slab + RWKV_SLAB
    kr_w = w_in[:, o_kr:o_slab]
    zeros = lambda n: jnp.zeros((d, n), F32)
    kr_blk = jnp.concatenate([zeros(nope), kr_w, zeros(HEAD_PAD - qk)], axis=1)
    kr_rot = jnp.concatenate([zeros(nope), -kr_w[:, half:], kr_w[:, :half], zeros(HEAD_PAD - qk)], axis=1)
    w_in_p = jnp.concatenate([w_in[:, o_q:o_kr], kr_blk, kr_rot, w_in[:, o_slab:]], axis=1).astype(BF16)

    tm = min(256, s)
    mla_in, slab, gates = _inproj(x, mod3, norm_mix_w, w_in_p, tm)

    wq3 = w_q_up.reshape(Q_LORA, MLA_HEADS, qk)
    wq_rot = jnp.concatenate([jnp.zeros((Q_LORA, MLA_HEADS, nope), F32), -wq3[:, :, nope + half:],
                              wq3[:, :, nope:nope + half]], axis=2)
    wq = jnp.concatenate([_head_blocks(wq3), _head_blocks(wq_rot)], axis=1).astype(BF16)
    wkv3 = w_kv_up.reshape(KV_LORA, MLA_HEADS, nope + MLA_V)
    wkv = jnp.concatenate([_head_blocks(wkv3[:, :, :nope]), _head_blocks(wkv3[:, :, nope:])], axis=1).astype(BF16)

    def gains(wn):
        main = jnp.pad(wn, (0, HEAD_PAD - qk)).reshape(1, HEAD_PAD)
        rot = jnp.concatenate([jnp.zeros((nope,), F32), wn[nope + half:], wn[nope:nope + half],
                               jnp.zeros((HEAD_PAD - qk,), F32)]).reshape(1, HEAD_PAD)
        return main, rot

    qg, qgs = gains(q_norm_w)
    kg, kgs = gains(k_norm_w)
    inv_freq = ROPE_THETA ** (-jnp.arange(half, dtype=F32) / half)
    invf = jnp.concatenate([jnp.zeros((nope,), F32), inv_freq, inv_freq,
                            jnp.zeros((HEAD_PAD - qk,), F32)]).reshape(1, HEAD_PAD)
    pos_b = jnp.broadcast_to(positions.astype(F32)[:, :, None], (bsz, s, LANES))
    q, k, v = _mla_prep(mla_in, pos_b, invf, q_a_norm_w.reshape(1, -1), kv_a_norm_w.reshape(1, -1),
                        wq, wkv, qg, qgs, kg, kgs, tm)
    o = _attention(q, k, v, min(256, s))

    w = RWKV_WIDTH
    wwa = jnp.zeros((LANES, 2 * w), F32)
    wwa = wwa.at[:DECAY_LORA, :w].set(rwkv_w2).at[DECAY_LORA:, w:].set(rwkv_a2)
    hid = np.arange(w) // RWKV_HEAD
    bd = jnp.asarray(hid[:, None] == hid[None, :], BF16)
    r_, lw_, k_, v_, a_, b_, g_ = _rwkv_prep(
        slab, rwkv_mu.reshape(1, -1), wwa, rwkv_w0.reshape(1, -1), rwkv_a0.reshape(1, -1), rwkv_g2,
        rwkv_k_k.reshape(1, -1), rwkv_k_a.reshape(1, -1), bd, tm)
    y = _wkv(r_, lw_, k_, v_, a_, b_)

    woa = jnp.zeros((MLA_HEADS, HEAD_PAD, d), F32).at[:, :MLA_V, :].set(
        w_o_mla.reshape(MLA_HEADS, MLA_V, d)).astype(BF16)
    wr = _pad_cols(w_router, LANES)
    wrh = wr.astype(BF16)
    wrl = (wr - wrh.astype(F32)).astype(BF16)
    br = jnp.pad(b_router, (0, LANES - N_EXPERTS)).reshape(1, LANES)
    x1, h2, route, counts = _post(
        x, mod3, y, r_, k_, v_, g_, o, gates, rwkv_ln_w.reshape(1, -1), rwkv_ln_b.reshape(1, -1),
        rwkv_r_k.reshape(1, -1), bd, rwkv_w_o.astype(BF16), woa, w_out.astype(BF16),
        norm_ffn_w.reshape(1, -1), wrh, wrl, br, tm)

    bm = MOE_BLOCK
    n_rows = t * TOP_K + N_EXPERTS * bm
    n_blocks = n_rows // bm
    cnt = counts[0, :N_EXPERTS].astype(jnp.int32)
    padded = (cnt + bm - 1) // bm * bm
    pad_end = jnp.cumsum(padded)
    pad_start = pad_end - padded
    route2 = route.reshape(t, LANES)
    top_e = route2[:, 0:TOP_K].astype(jnp.int32)
    rank = route2[:, 2 * TOP_K:3 * TOP_K].astype(jnp.int32)
    dest = (pad_start[top_e] + rank).reshape(-1)
    blk_e = jnp.minimum(jnp.searchsorted(pad_end, jnp.arange(n_blocks, dtype=jnp.int32) * bm, side='right'),
                        N_EXPERTS - 1).astype(jnp.int32)
    n_used = (pad_end[-1:] // bm).astype(jnp.int32)

    xs = _dispatch(dest, h2.reshape(t, d), jnp.zeros((n_rows, d), F32), min(256, t))
    wgu = w_gate_up.reshape(N_EXPERTS, d, D_FF, 2)
    bgu = b_gate_up.reshape(N_EXPERTS, 1, D_FF, 2)
    ys = _moe(blk_e, n_used, xs, wgu[..., 0].astype(BF16), wgu[..., 1].astype(BF16),
              bgu[..., 0], bgu[..., 1], w_down.astype(BF16), b_down.reshape(N_EXPERTS, 1, d))
    return _combine(dest, x1, route, mod3, ys, min(128, s))


def kernel(x, c, positions, ada_w, ada_b, norm_mix_w, norm_ffn_w, w_in, q_a_norm_w, w_q_up, kv_a_norm_w, w_kv_up, q_norm_w, k_norm_w, w_o_mla, rwkv_mu, rwkv_w0, rwkv_w2, rwkv_a0, rwkv_a2, rwkv_g2, rwkv_k_k, rwkv_k_a, rwkv_r_k, rwkv_ln_w, rwkv_ln_b, rwkv_w_o, w_out, w_router, b_router, w_gate_up, b_gate_up, w_down, b_down):
    depth = ada_w.shape[0]
    for l in range(depth):
        mod = _ada(c, ada_w[l], ada_b[l])
        x = _layer(x, mod, positions, w_in[l], q_a_norm_w[l], w_q_up[l], kv_a_norm_w[l], w_kv_up[l],
                   q_norm_w[l], k_norm_w[l], w_o_mla[l], rwkv_mu[l], rwkv_w0[l], rwkv_w2[l], rwkv_a0[l],
                   rwkv_a2[l], rwkv_g2[l], rwkv_k_k[l], rwkv_k_a[l], rwkv_r_k[l], rwkv_ln_w[l],
                   rwkv_ln_b[l], rwkv_w_o[l], w_out[l], norm_mix_w[l], norm_ffn_w[l], w_router[l],
                   b_router[l], w_gate_up[l], b_gate_up[l], w_down[l], b_down[l])
    return x
```

```python
import functools

import numpy as np
import jax
import jax.numpy as jnp
from jax import lax
from jax.experimental import pallas as pl
from jax.experimental.pallas import tpu as pltpu

F32 = jnp.float32
BF16 = jnp.bfloat16

D_MODEL = 1024
MLA_HEADS = 8
MLA_NOPE = 64
MLA_ROPE = 32
MLA_QK = MLA_NOPE + MLA_ROPE
MLA_V = 64
Q_LORA = 256
KV_LORA = 128
ROPE_THETA = 10000.0
RWKV_HEADS = 8
RWKV_HEAD = 64
RWKV_WIDTH = RWKV_HEADS * RWKV_HEAD
DECAY_LORA = 64
AAA_LORA = 64
GATE_LORA = 128
RWKV_SLAB = 3 * RWKV_WIDTH + DECAY_LORA + AAA_LORA + GATE_LORA
GN_EPS = 64e-5
N_EXPERTS = 32
TOP_K = 4
D_FF = D_MODEL
SWIGLU_LIMIT = 7.0
SWIGLU_ALPHA = 1.702
NORM_EPS = 1e-6
LOG2_E = 1.4426950408889634

LANES = 128
HEAD_PAD = 128
MLA_COLS = 640
WKV_CHUNK = 64
WKV_BATCH = 2
ATTN_BLOCK = 512
MOE_BLOCK = 256
DMA_UNROLL = 8
VMEM_LIMIT = 56 * 1024 * 1024


def _dot(a, b):
    return jnp.dot(a, b, preferred_element_type=F32)


def _dot_nt(a, b):
    return lax.dot_general(a, b, (((1,), (1,)), ((), ())), preferred_element_type=F32)


def _split(x):
    hi = x.astype(BF16)
    lo = (x - hi.astype(F32)).astype(BF16)
    return hi, lo


def _mm3(a, b):
    ah, al = _split(a)
    bh, bl = _split(b)
    return _dot(ah, bh) + (_dot(ah, bl) + _dot(al, bh))


def _mm3_nt(a, b):
    ah, al = _split(a)
    bh, bl = _split(b)
    return _dot_nt(ah, bh) + (_dot_nt(ah, bl) + _dot_nt(al, bh))


def _mm_exact_rhs(a, b_bf16):
    ah, al = _split(a)
    return _dot(ah, b_bf16) + _dot(al, b_bf16)


def _params(*sem):
    return pltpu.CompilerParams(dimension_semantics=sem, vmem_limit_bytes=VMEM_LIMIT)


def _ada_kernel(c_ref, w_ref, b_ref, o_ref):
    c = c_ref[...]
    cond = c * jax.nn.sigmoid(c)
    o_ref[...] = _mm3(cond, w_ref[...]) + b_ref[...]


def _ada(c, w, b):
    bsz, d = c.shape
    n = w.shape[1]
    tn = 1024
    return pl.pallas_call(
        _ada_kernel,
        grid=(n // tn,),
        in_specs=[pl.BlockSpec((bsz, d), lambda j: (0, 0)),
                  pl.BlockSpec((d, tn), lambda j: (0, j)),
                  pl.BlockSpec((1, tn), lambda j: (0, j))],
        out_specs=pl.BlockSpec((bsz, tn), lambda j: (0, j)),
        out_shape=jax.ShapeDtypeStruct((bsz, n), F32),
        compiler_params=_params("parallel"),
        name="ada",
    )(c, w, b.reshape(1, n))


def _inproj_kernel(x_ref, mod_ref, nw_ref, w_ref, mla_ref, slab_ref, gates_ref):
    x = x_ref[0]
    ms = jnp.mean(x * x, axis=-1, keepdims=True)
    y = x * lax.rsqrt(ms + NORM_EPS) * nw_ref[...]
    h = y * (1.0 + mod_ref[0, 1:2, :]) + mod_ref[0, 0:1, :]
    hb = h.astype(BF16)
    mla_ref[0] = _dot(hb, w_ref[:, 0:MLA_COLS])
    slab_ref[0] = _dot(hb, w_ref[:, MLA_COLS:MLA_COLS + RWKV_SLAB])
    gates_ref[0] = _dot(hb, w_ref[:, MLA_COLS + RWKV_SLAB:]).astype(BF16)


def _inproj(x, mod3, norm_w, w_in_p, tm):
    bsz, s, d = x.shape
    ncol = w_in_p.shape[1]
    return pl.pallas_call(
        _inproj_kernel,
        grid=(bsz, s // tm),
        in_specs=[pl.BlockSpec((1, tm, d), lambda b, i: (b, i, 0)),
                  pl.BlockSpec((1, 6, d), lambda b, i: (b, 0, 0)),
                  pl.BlockSpec((1, d), lambda b, i: (0, 0)),
                  pl.BlockSpec((d, ncol), lambda b, i: (0, 0))],
        out_specs=[pl.BlockSpec((1, tm, MLA_COLS), lambda b, i: (b, i, 0)),
                   pl.BlockSpec((1, tm, RWKV_SLAB), lambda b, i: (b, i, 0)),
                   pl.BlockSpec((1, tm, 2 * d), lambda b, i: (b, i, 0))],
        out_shape=[jax.ShapeDtypeStruct((bsz, s, MLA_COLS), F32),
                   jax.ShapeDtypeStruct((bsz, s, RWKV_SLAB), F32),
                   jax.ShapeDtypeStruct((bsz, s, 2 * d), BF16)],
        compiler_params=_params("parallel", "parallel"),
        name="inproj",
    )(x, mod3, norm_w.reshape(1, d), w_in_p)


def _mla_prep_kernel(mla_ref, pos_ref, invf_ref, qan_ref, kvn_ref, wq_ref, wkv_ref,
                     qg_ref, qgs_ref, kg_ref, kgs_ref, q_out, k_out, v_out):
    m = mla_ref[0]
    cq = m[:, 0:Q_LORA]
    ckv = m[:, Q_LORA:Q_LORA + KV_LORA]
    kr = m[:, 384:512]
    krs = m[:, 512:640]
    ql = cq * lax.rsqrt(jnp.mean(cq * cq, axis=-1, keepdims=True) + NORM_EPS) * qan_ref[...]
    kvl = ckv * lax.rsqrt(jnp.mean(ckv * ckv, axis=-1, keepdims=True) + NORM_EPS) * kvn_ref[...]
    qall = _dot(ql.astype(BF16), wq_ref[...])
    kvall = _dot(kvl.astype(BF16), wkv_ref[...])
    ang = pos_ref[0] * invf_ref[...]
    cosf = jnp.cos(ang)
    sinf = jnp.sin(ang)
    scale = MLA_QK ** -0.5 * LOG2_E
    hw = MLA_HEADS * HEAD_PAD
    for h in range(MLA_HEADS):
        lo, hi = h * HEAD_PAD, (h + 1) * HEAD_PAD
        qh = qall[:, lo:hi]
        qs = qall[:, hw + lo:hw + hi]
        rs = lax.rsqrt(jnp.sum(qh * qh, axis=-1, keepdims=True) * (1.0 / MLA_QK) + NORM_EPS)
        qo = (qh * rs * qg_ref[...]) * cosf + (qs * rs * qgs_ref[...]) * sinf
        q_out[0, h] = (qo * scale).astype(BF16)
        kh = kvall[:, lo:hi] + kr
        rs = lax.rsqrt(jnp.sum(kh * kh, axis=-1, keepdims=True) * (1.0 / MLA_QK) + NORM_EPS)
        ko = (kh * rs * kg_ref[...]) * cosf + (krs * rs * kgs_ref[...]) * sinf
        k_out[0, h] = ko.astype(BF16)
        v_out[0, h] = kvall[:, hw + lo:hw + hi].astype(BF16)


def _mla_prep(mla_in, pos_b, invf, qan, kvn, wq, wkv, qg, qgs, kg, kgs, tm):
    bsz, s, _ = mla_in.shape
    hw = MLA_HEADS * HEAD_PAD
    row = lambda n: pl.BlockSpec((1, n), lambda b, i: (0, 0))
    out_spec = pl.BlockSpec((1, MLA_HEADS, tm, HEAD_PAD), lambda b, i: (b, 0, i, 0))
    out_sds = jax.ShapeDtypeStruct((bsz, MLA_HEADS, s, HEAD_PAD), BF16)
    return pl.pallas_call(
        _mla_prep_kernel,
        grid=(bsz, s // tm),
        in_specs=[pl.BlockSpec((1, tm, MLA_COLS), lambda b, i: (b, i, 0)),
                  pl.BlockSpec((1, tm, LANES), lambda b, i: (b, i, 0)),
                  row(LANES), row(Q_LORA), row(KV_LORA),
                  pl.BlockSpec((Q_LORA, 2 * hw), lambda b, i: (0, 0)),
                  pl.BlockSpec((KV_LORA, 2 * hw), lambda b, i: (0, 0)),
                  row(LANES), row(LANES), row(LANES), row(LANES)],
        out_specs=[out_spec, out_spec, out_spec],
        out_shape=[out_sds, out_sds, out_sds],
        compiler_params=_params("parallel", "parallel"),
        name="mla_prep",
    )(mla_in, pos_b, invf, qan, kvn, wq, wkv, qg, qgs, kg, kgs)


def _attn_kernel(q_ref, k_ref, vt_ref, o_ref, m_ref, l_ref, acc_ref, *, tq):
    qi = pl.program_id(2)
    q = q_ref[0, 0]
    m_ref[...] = jnp.full_like(m_ref, -jnp.inf)
    l_ref[...] = jnp.zeros_like(l_ref)
    acc_ref[...] = jnp.zeros_like(acc_ref)

    def step(j, mask):
        off = pl.multiple_of(j * tq, tq)
        st = _dot_nt(k_ref[0, 0, pl.ds(off, tq), :], q)
        if mask is not None:
            st = jnp.where(mask, st, -jnp.inf)
        m = m_ref[...]
        m_new = jnp.maximum(m, jnp.max(st, axis=0, keepdims=True))
        p = jnp.exp2(st - m_new)
        alpha = jnp.exp2(m - m_new)
        m_ref[...] = m_new
        l_ref[...] = alpha * l_ref[...] + jnp.sum(p, axis=0, keepdims=True)
        acc_ref[...] = alpha * acc_ref[...] + _dot(vt_ref[0, 0, j], p.astype(BF16))

    def body(j, carry):
        step(j, None)
        return carry

    lax.fori_loop(0, qi, body, 0)
    keys = lax.broadcasted_iota(jnp.int32, (tq, tq), 0)
    queries = lax.broadcasted_iota(jnp.int32, (tq, tq), 1)
    step(qi, keys <= queries)
    o_ref[0, 0] = (acc_ref[...] / l_ref[...]).T.astype(BF16)


def _attention(q, k, vt, tq):
    bsz, nh, s, dh = q.shape
    return pl.pallas_call(
        functools.partial(_attn_kernel, tq=tq),
        grid=(bsz, nh, s // tq),
        in_specs=[pl.BlockSpec((1, 1, tq, dh), lambda b, h, i: (b, h, i, 0)),
                  pl.BlockSpec((1, 1, s, dh), lambda b, h, i: (b, h, 0, 0)),
                  pl.BlockSpec((1, 1, s // tq, dh, tq), lambda b, h, i: (b, h, 0, 0, 0))],
        out_specs=pl.BlockSpec((1, 1, tq, dh), lambda b, h, i: (b, h, i, 0)),
        out_shape=jax.ShapeDtypeStruct((bsz, nh, s, dh), BF16),
        scratch_shapes=[pltpu.VMEM((1, tq), F32), pltpu.VMEM((1, tq), F32), pltpu.VMEM((dh, tq), F32)],
        compiler_params=_params("parallel", "parallel", "parallel"),
        name="attn",
    )(q, k, vt)


def _softplus(x):
    return jnp.maximum(x, 0.0) + jnp.log(1.0 + jnp.exp(-jnp.abs(x)))


def _rwkv_prep_kernel(slab_ref, mu_ref, wwa_ref, w0_ref, a0_ref, g2_ref, kk_ref, ka_ref, bd_ref,
                      r_out, lw_out, k_out, v_out, a_out, b_out, g_out, last_ref):
    i = pl.program_id(1)
    tm = slab_ref.shape[1]
    w = RWKV_WIDTH

    @pl.when(i == 0)
    def _():
        last_ref[...] = jnp.zeros_like(last_ref)

    slab = slab_ref[0]
    rolled = pltpu.roll(slab, 1, 0)
    rowi = lax.broadcasted_iota(jnp.int32, slab.shape, 0)
    prev = jnp.where(rowi == 0, last_ref[0:1, :], rolled)
    last_ref[0:1, :] = slab[tm - 1:tm, :]
    p = slab + (prev - slab) * mu_ref[...]
    r = p[:, 0:w]
    k = p[:, w:2 * w]
    v = p[:, 2 * w:3 * w]
    wa = p[:, 3 * w:3 * w + LANES]
    g_lo = p[:, 3 * w + LANES:3 * w + 2 * LANES]
    lane = lax.broadcasted_iota(jnp.int32, wa.shape, 1)
    wa = jnp.where(lane < DECAY_LORA, jnp.tanh(wa), wa)
    wa_o = _mm3(wa, wwa_ref[...])
    log_w = -_softplus(-(w0_ref[...] + wa_o[:, 0:w])) - 0.5
    lw = -jnp.exp(log_w)
    a = jax.nn.sigmoid(a0_ref[...] + wa_o[:, w:2 * w])
    g = _mm3(jax.nn.sigmoid(g_lo), g2_ref[...])
    kk = k * kk_ref[...]
    ss = _mm_exact_rhs(kk * kk, bd_ref[...])
    kk = kk / jnp.maximum(jnp.sqrt(ss), 1e-12)
    r_out[0] = r
    lw_out[0] = lw
    k_out[0] = k * (1.0 + (a - 1.0) * ka_ref[...])
    v_out[0] = v
    a_out[0] = -kk
    b_out[0] = kk * a
    g_out[0] = g


def _rwkv_prep(slab, mu, wwa, w0, a0, g2, k_k, k_a, bd, tm):
    bsz, s, _ = slab.shape
    w = RWKV_WIDTH
    row = lambda n: pl.BlockSpec((1, n), lambda b, i: (0, 0))
    full = lambda a: pl.BlockSpec(a.shape, lambda b, i: (0, 0))
    out_spec = pl.BlockSpec((1, tm, w), lambda b, i: (b, i, 0))
    out_sds = jax.ShapeDtypeStruct((bsz, s, w), F32)
    return pl.pallas_call(
        _rwkv_prep_kernel,
        grid=(bsz, s // tm),
        in_specs=[pl.BlockSpec((1, tm, RWKV_SLAB), lambda b, i: (b, i, 0)),
                  row(RWKV_SLAB), full(wwa), row(w), row(w), full(g2), row(w), row(w), full(bd)],
        out_specs=[out_spec] * 7,
        out_shape=[out_sds] * 7,
        scratch_shapes=[pltpu.VMEM((8, RWKV_SLAB), F32)],
        compiler_params=_params("parallel", "arbitrary"),
        name="rwkv_prep",
    )(slab, mu, wwa, w0, a0, g2, k_k, k_a, bd)


def _wkv_kernel(r_ref, lw_ref, k_ref, v_ref, a_ref, b_ref, y_ref, s_ref):
    c = WKV_CHUNK
    n = pl.program_id(1)

    @pl.when(n == 0)
    def _():
        s_ref[...] = jnp.zeros_like(s_ref)

    row = lax.broadcasted_iota(jnp.int32, (LANES, LANES), 0)
    col = lax.broadcasted_iota(jnp.int32, (LANES, LANES), 1)
    lower_strict = row > col
    lower_incl = row >= col
    eye = row == col
    same16 = (row >> 4) == (col >> 4)
    same32 = (row >> 5) == (col >> 5)
    ident = jnp.where(eye, 1.0, 0.0).astype(F32)
    tr = lax.broadcasted_iota(jnp.int32, (c, c), 0)
    tc = lax.broadcasted_iota(jnp.int32, (c, c), 1)
    tri = jnp.where(tr >= tc, 1.0, 0.0).astype(BF16)
    first = lax.broadcasted_iota(jnp.int32, (c, LANES), 1) < RWKV_HEAD

    def stack(x):
        return jnp.concatenate([jnp.where(first, x, 0.0), jnp.where(first, 0.0, x)], axis=0)

    def mm(x, y):
        return _dot(x.astype(BF16), y.astype(BF16))

    nb = r_ref.shape[0]
    probs = [(bi, p) for bi in range(nb) for p in range(RWKV_HEADS // 2)]
    each = lambda f, *xs: [f(*args) for args in zip(*xs)]
    load = lambda ref: [ref[bi, :, p * LANES:(p + 1) * LANES] for bi, p in probs]
    r, lw, k, v, a, b = (load(ref) for ref in (r_ref, lw_ref, k_ref, v_ref, a_ref, b_ref))

    def cumsum(x):
        hi = x.astype(BF16)
        rem = x - hi.astype(F32)
        mid = rem.astype(BF16)
        lo = (rem - mid.astype(F32)).astype(BF16)
        return _dot(tri, hi) + (_dot(tri, mid) + _dot(tri, lo))

    cum = each(cumsum, lw)
    cum_c = [x[c - 1:c, :] for x in cum]
    e_in = each(jnp.exp, cum)
    e_neg = each(lambda x: jnp.exp(-x), cum)
    e_end = each(lambda x, xc: jnp.exp(xc - x), cum, cum_c)
    a_s = each(lambda x, cu, l: stack(x * jnp.exp(cu - l)).astype(BF16), a, cum, lw)
    r_s = each(lambda x, e: stack(x * e), r, e_in)
    b_s = each(lambda x, e: stack(x * e).astype(BF16), b, e_neg)
    k_s = each(lambda x, e: stack(x * e).astype(BF16), k, e_neg)
    b_e = each(lambda x, e: stack(x * e).T.astype(BF16), b, e_end)
    k_e = each(lambda x, e: stack(x * e).T.astype(BF16), k, e_end)
    v_s = each(lambda x: stack(x).astype(BF16), v)

    tt = each(lambda x1, x2, y1, y2: _dot_nt(jnp.concatenate([x1, x2.astype(BF16)], axis=0),
                                             jnp.concatenate([y1, y2], axis=0)), a_s, r_s, b_s, k_s)
    d_ab = [jnp.where(lower_strict, x[0:LANES, 0:LANES], 0.0) for x in tt]
    e_ak = [jnp.where(lower_strict, x[0:LANES, LANES:], 0.0).astype(BF16) for x in tt]
    f_rb = [jnp.where(lower_incl, x[LANES:, 0:LANES], 0.0).astype(BF16) for x in tt]
    f_rk = [jnp.where(lower_incl, x[LANES:, LANES:], 0.0).astype(BF16) for x in tt]

    d16 = [jnp.where(same16, x, 0.0) for x in d_ab]
    d32 = [jnp.where(same32, x, 0.0) for x in d_ab]
    z = each(_dot, e_ak, v_s)
    x2 = each(mm, d16, d16)
    x4 = each(mm, x2, x2)
    x8 = each(mm, x4, x4)
    t = [ident + x for x in d16]
    t = each(lambda t_, x: t_ + mm(t_, x), t, x2)
    t = each(lambda t_, x: t_ + mm(t_, x), t, x4)
    t = each(lambda t_, x: t_ + mm(t_, x), t, x8)
    t = each(lambda t_, hi, lo: t_ + mm(mm(t_, hi - lo), t_), t, d32, d16)
    t = each(lambda t_, hi, lo: t_ + mm(mm(t_, hi - lo), t_), t, d_ab, d32)

    au = each(lambda t_, x, zz: mm(t_, jnp.concatenate([x, zz.astype(BF16)], axis=1)).astype(BF16),
              t, a_s, z)
    g = each(_dot, f_rb, au)
    y0 = each(lambda gg, f, vv: gg[:, LANES:] + _dot(f, vv), g, f_rk, v_s)
    r1 = each(lambda x, gg: x + gg[:, 0:LANES], r_s, g)
    hmat = each(_dot, b_e, au)
    m_mat = each(lambda h, xc: jnp.where(eye, jnp.exp(xc), 0.0) + h[:, 0:LANES], hmat, cum_c)
    n_mat = each(lambda h, ke, vv: h[:, LANES:] + _dot(ke, vv), hmat, k_e, v_s)

    for i, (bi, p) in enumerate(probs):
        s0 = s_ref[i]
        ys = _mm3(r1[i], s0) + y0[i]
        s_ref[i] = _mm3(m_mat[i], s0) + n_mat[i]
        y_ref[bi, :, p * LANES:(p + 1) * LANES] = ys[0:c, :] + ys[c:, :]


def _wkv(r, lw, k, v, a, b, nb):
    bsz, s, w = r.shape
    spec = pl.BlockSpec((nb, WKV_CHUNK, w), lambda bi, n: (bi, n, 0))
    return pl.pallas_call(
        _wkv_kernel,
        grid=(bsz // nb, s // WKV_CHUNK),
        in_specs=[spec] * 6,
        out_specs=spec,
        out_shape=jax.ShapeDtypeStruct((bsz, s, w), F32),
        scratch_shapes=[pltpu.VMEM((nb * RWKV_HEADS // 2, LANES, LANES), F32)],
        compiler_params=_params("parallel", "arbitrary"),
        name="wkv",
    )(r, lw, k, v, a, b)


def _post_kernel(x_ref, mod_ref, y_ref, r_ref, k_ref, v_ref, g_ref, o_ref, gates_ref,
                 lnw_ref, lnb_ref, rk_ref, bd_ref, wor_ref, woa_ref, wout_ref, nfw_ref,
                 wrh_ref, wrl_ref, br_ref,
                 x1_ref, h2_ref, route_ref, cnt_ref, run_ref):
    first_step = jnp.logical_and(pl.program_id(0) == 0, pl.program_id(1) == 0)

    @pl.when(first_step)
    def _():
        run_ref[...] = jnp.zeros_like(run_ref)

    tm = x_ref.shape[1]
    d = x_ref.shape[2]
    bd = bd_ref[...]
    inv_n = 1.0 / RWKV_HEAD
    y = y_ref[0]
    mu = _mm_exact_rhs(y, bd) * inv_n
    dlt = y - mu
    var = _mm_exact_rhs(dlt * dlt, bd) * inv_n
    yn = dlt * lax.rsqrt(var + GN_EPS) * lnw_ref[...] + lnb_ref[...]
    v = v_ref[0]
    bonus = _mm_exact_rhs(r_ref[0] * k_ref[0] * rk_ref[...], bd) * v
    z = (yn + bonus) * g_ref[0]
    o_b = _dot(z.astype(BF16), wor_ref[...])
    o_a = _dot(o_ref[0, 0], woa_ref[0])
    for h in range(1, MLA_HEADS):
        o_a = o_a + _dot(o_ref[0, h], woa_ref[h])
    ga = jax.nn.sigmoid(gates_ref[0, :, 0:d].astype(F32))
    gb = jax.nn.sigmoid(gates_ref[0, :, d:2 * d].astype(F32))
    merged = ga * o_a + gb * o_b
    mix = _dot(merged.astype(BF16), wout_ref[...])
    x1 = x_ref[0] + mod_ref[0, 2:3, :] * mix
    x1_ref[0] = x1
    ms = jnp.mean(x1 * x1, axis=-1, keepdims=True)
    h2 = x1 * lax.rsqrt(ms + NORM_EPS) * nfw_ref[...]
    h2 = h2 * (1.0 + mod_ref[0, 4:5, :]) + mod_ref[0, 3:4, :]
    h2_ref[0] = h2

    hh, hl = _split(h2)
    logits = _dot(hh, wrh_ref[...]) + (_dot(hh, wrl_ref[...]) + _dot(hl, wrh_ref[...])) + br_ref[...]
    lane = lax.broadcasted_iota(jnp.int32, (tm, LANES), 1)
    lanef = lane.astype(F32)
    cur = jnp.where(lane < N_EXPERTS, logits, -jnp.inf)
    vals, idxs, hots = [], [], []
    for _ in range(TOP_K):
        mx = jnp.max(cur, axis=-1, keepdims=True)
        idx = jnp.min(jnp.where(cur == mx, lanef, float(LANES)), axis=-1, keepdims=True)
        hot = lanef == idx
        cur = jnp.where(hot, -jnp.inf, cur)
        vals.append(mx)
        idxs.append(idx)
        hots.append(hot)
    exps = [jnp.exp(vv - vals[0]) for vv in vals]
    den = exps[0] + exps[1] + exps[2] + exps[3]
    sel = jnp.zeros((tm, LANES), F32)
    for hot in hots:
        sel = sel + jnp.where(hot, 1.0, 0.0)
    ri = lax.broadcasted_iota(jnp.int32, (tm, tm), 0)
    ci = lax.broadcasted_iota(jnp.int32, (tm, tm), 1)
    below = jnp.where(ri > ci, 1.0, 0.0).astype(BF16)
    run = run_ref[0:1, :]
    before = _dot(below, sel.astype(BF16)) + run
    route = jnp.zeros((tm, LANES), F32)
    for j in range(TOP_K):
        rank = jnp.sum(jnp.where(hots[j], before, 0.0), axis=-1, keepdims=True)
        route = jnp.where(lane == j, idxs[j], route)
        route = jnp.where(lane == TOP_K + j, exps[j] / den, route)
        route = jnp.where(lane == 2 * TOP_K + j, rank, route)
    route_ref[0] = route
    run = run + jnp.sum(sel, axis=0, keepdims=True)
    run_ref[0:1, :] = run
    cnt_ref[...] = jnp.broadcast_to(run, cnt_ref.shape)


def _post(x, mod3, y, r, k, v, g, o, gates, lnw, lnb, rk, bd, wor, woa, wout, nfw, wrh, wrl, br, tm):
    bsz, s, d = x.shape
    w = RWKV_WIDTH
    tok = lambda n: pl.BlockSpec((1, tm, n), lambda b, i: (b, i, 0))
    row = lambda n: pl.BlockSpec((1, n), lambda b, i: (0, 0))
    full = lambda a: pl.BlockSpec(a.shape, lambda b, i: (0,) * a.ndim)
    return pl.pallas_call(
        _post_kernel,
        grid=(bsz, s // tm),
        in_specs=[tok(d), pl.BlockSpec((1, 6, d), lambda b, i: (b, 0, 0)),
                  tok(w), tok(w), tok(w), tok(w), tok(w),
                  pl.BlockSpec((1, MLA_HEADS, tm, HEAD_PAD), lambda b, i: (b, 0, i, 0)),
                  tok(2 * d),
                  row(w), row(w), row(w), full(bd), full(wor), full(woa), full(wout), row(d),
                  full(wrh), full(wrl), row(LANES)],
        out_specs=[tok(d), tok(d), tok(LANES), pl.BlockSpec((8, LANES), lambda b, i: (0, 0))],
        out_shape=[jax.ShapeDtypeStruct((bsz, s, d), F32),
                   jax.ShapeDtypeStruct((bsz, s, d), F32),
                   jax.ShapeDtypeStruct((bsz, s, LANES), F32),
                   jax.ShapeDtypeStruct((8, LANES), F32)],
        scratch_shapes=[pltpu.VMEM((8, LANES), F32)],
        compiler_params=_params("arbitrary", "arbitrary"),
        name="post",
    )(x, mod3, y, r, k, v, g, o, gates, lnw, lnb, rk, bd, wor, woa, wout, nfw, wrh, wrl, br)


def _dispatch_kernel(dest_ref, h_ref, xs_in_ref, xs_ref, sem):
    del xs_in_ref
    tm = h_ref.shape[0]
    base = pl.program_id(0) * tm * TOP_K

    def start(g, carry):
        for u in range(DMA_UNROLL):
            r = g * DMA_UNROLL + u
            for j in range(TOP_K):
                dst = dest_ref[base + r * TOP_K + j]
                pltpu.make_async_copy(h_ref.at[pl.ds(r, 1), :], xs_ref.at[pl.ds(dst, 1), :], sem).start()
        return carry

    lax.fori_loop(0, tm // DMA_UNROLL, start, 0)
    for _ in range(TOP_K):
        pltpu.make_async_copy(h_ref, xs_ref.at[pl.ds(0, tm), :], sem).wait()


def _dispatch(dest, h2, xs_init, tm):
    t, d = h2.shape
    return pl.pallas_call(
        _dispatch_kernel,
        grid_spec=pltpu.PrefetchScalarGridSpec(
            num_scalar_prefetch=1,
            grid=(t // tm,),
            in_specs=[pl.BlockSpec((tm, d), lambda i, dest: (i, 0)),
                      pl.BlockSpec(memory_space=pl.ANY)],
            out_specs=pl.BlockSpec(memory_space=pl.ANY),
            scratch_shapes=[pltpu.SemaphoreType.DMA(())]),
        out_shape=jax.ShapeDtypeStruct(xs_init.shape, F32),
        input_output_aliases={2: 0},
        compiler_params=_params("arbitrary"),
        name="dispatch",
    )(dest, h2, xs_init)


def _moe_kernel(blk_e_ref, n_used_ref, xs_ref, wg_ref, wu_ref, bg_ref, bu_ref, wd_ref, bd_ref, ys_ref):
    del blk_e_ref
    i = pl.program_id(0)

    @pl.when(i < n_used_ref[0])
    def _():
        x = xs_ref[...].astype(BF16)
        gate = _dot(x, wg_ref[0]) + bg_ref[0]
        up = _dot(x, wu_ref[0]) + bu_ref[0]
        gate = jnp.minimum(gate, SWIGLU_LIMIT)
        up = jnp.clip(up, -SWIGLU_LIMIT, SWIGLU_LIMIT)
        act = (up + 1.0) * gate * jax.nn.sigmoid(SWIGLU_ALPHA * gate)
        ys_ref[...] = _dot(act.astype(BF16), wd_ref[0]) + bd_ref[0]

    @pl.when(i >= n_used_ref[0])
    def _():
        ys_ref[...] = jnp.zeros_like(ys_ref)


def _moe(blk_e, n_used, xs, wg, wu, bg, bu, wd, bd):
    n_rows, d = xs.shape
    ff = wg.shape[2]
    bm = MOE_BLOCK
    wspec = lambda k, n: pl.BlockSpec((1, k, n), lambda i, be, nu: (be[i], 0, 0))
    return pl.pallas_call(
        _moe_kernel,
        grid_spec=pltpu.PrefetchScalarGridSpec(
            num_scalar_prefetch=2,
            grid=(n_rows // bm,),
            in_specs=[pl.BlockSpec((bm, d), lambda i, be, nu: (i, 0)),
                      wspec(d, ff), wspec(d, ff), wspec(1, ff), wspec(1, ff), wspec(ff, d), wspec(1, d)],
            out_specs=pl.BlockSpec((bm, d), lambda i, be, nu: (i, 0))),
        out_shape=jax.ShapeDtypeStruct((n_rows, d), F32),
        compiler_params=_params("arbitrary"),
        name="moe",
    )(blk_e, n_used, xs, wg, wu, bg, bu, wd, bd)


def _combine_kernel(dest_ref, x1_ref, route_ref, mod_ref, ys_ref, o_ref, buf_ref, sem):
    tm = x1_ref.shape[1]
    s = pl.num_programs(1) * tm
    base = (pl.program_id(0) * s + pl.program_id(1) * tm) * TOP_K

    def start(g, carry):
        for u in range(DMA_UNROLL):
            r = g * DMA_UNROLL + u
            for j in range(TOP_K):
                src = dest_ref[base + r * TOP_K + j]
                pltpu.make_async_copy(ys_ref.at[pl.ds(src, 1), :], buf_ref.at[j, pl.ds(r, 1), :], sem).start()
        return carry

    lax.fori_loop(0, tm // DMA_UNROLL, start, 0)
    for j in range(TOP_K):
        pltpu.make_async_copy(ys_ref.at[pl.ds(0, tm), :], buf_ref.at[j], sem).wait()
    route = route_ref[0]
    acc = route[:, TOP_K:TOP_K + 1] * buf_ref[0]
    for j in range(1, TOP_K):
        acc = acc + route[:, TOP_K + j:TOP_K + j + 1] * buf_ref[j]
    o_ref[0] = x1_ref[0] + mod_ref[0, 5:6, :] * acc


def _combine(dest, x1, route, mod3, ys, tm):
    bsz, s, d = x1.shape
    return pl.pallas_call(
        _combine_kernel,
        grid_spec=pltpu.PrefetchScalarGridSpec(
            num_scalar_prefetch=1,
            grid=(bsz, s // tm),
            in_specs=[pl.BlockSpec((1, tm, d), lambda b, i, dest: (b, i, 0)),
                      pl.BlockSpec((1, tm, LANES), lambda b, i, dest: (b, i, 0)),
                      pl.BlockSpec((1, 6, d), lambda b, i, dest: (b, 0, 0)),
                      pl.BlockSpec(memory_space=pl.ANY)],
            out_specs=pl.BlockSpec((1, tm, d), lambda b, i, dest: (b, i, 0)),
            scratch_shapes=[pltpu.VMEM((TOP_K, tm, d), F32), pltpu.SemaphoreType.DMA(())]),
        out_shape=jax.ShapeDtypeStruct((bsz, s, d), F32),
        compiler_params=_params("arbitrary", "arbitrary"),
        name="combine",
    )(dest, x1, route, mod3, ys)


def _pad_cols(a, n):
    return jnp.pad(a, ((0, 0), (0, n - a.shape[1])))


def _head_blocks(cols_main, cols_rot=None):
    k = cols_main.shape[0]
    out = jnp.zeros((k, MLA_HEADS, HEAD_PAD), F32)
    out = out.at[:, :, :cols_main.shape[2]].set(cols_main)
    return out.reshape(k, MLA_HEADS * HEAD_PAD)


def _layer(x, cond_mod, positions, w_in, q_a_norm_w, w_q_up, kv_a_norm_w, w_kv_up, q_norm_w, k_norm_w,
           w_o_mla, rwkv_mu, rwkv_w0, rwkv_w2, rwkv_a0, rwkv_a2, rwkv_g2, rwkv_k_k, rwkv_k_a, rwkv_r_k,
           rwkv_ln_w, rwkv_ln_b, rwkv_w_o, w_out, norm_mix_w, norm_ffn_w, w_router, b_router,
           w_gate_up, b_gate_up, w_down, b_down):
    bsz, s, d = x.shape
    t = bsz * s
    half = MLA_ROPE // 2
    nope, qk = MLA_NOPE, MLA_QK
    mod3 = cond_mod.reshape(bsz, 6, d)

    o_q, o_kv, o_kr = 0, Q_LORA, Q_LORA + KV_LORA
    o_slab = o_kr + MLA_ROPE
    o_gate = o_slab + RWKV_SLAB
    kr_w = w_in[:, o_kr:o_slab]
    zeros = lambda n: jnp.zeros((d, n), F32)
    kr_blk = jnp.concatenate([zeros(nope), kr_w, zeros(HEAD_PAD - qk)], axis=1)
    kr_rot = jnp.concatenate([zeros(nope), -kr_w[:, half:], kr_w[:, :half], zeros(HEAD_PAD - qk)], axis=1)
    w_in_p = jnp.concatenate([w_in[:, o_q:o_kr], kr_blk, kr_rot, w_in[:, o_slab:]], axis=1).astype(BF16)

    tm = min(256, s)
    mla_in, slab, gates = _inproj(x, mod3, norm_mix_w, w_in_p, tm)

    wq3 = w_q_up.reshape(Q_LORA, MLA_HEADS, qk)
    wq_rot = jnp.concatenate([jnp.zeros((Q_LORA, MLA_HEADS, nope), F32), -wq3[:, :, nope + half:],
                              wq3[:, :, nope:nope + half]], axis=2)
    wq = jnp.concatenate([_head_blocks(wq3), _head_blocks(wq_rot)], axis=1).astype(BF16)
    wkv3 = w_kv_up.reshape(KV_LORA, MLA_HEADS, nope + MLA_V)
    wkv = jnp.concatenate([_head_blocks(wkv3[:, :, :nope]), _head_blocks(wkv3[:, :, nope:])], axis=1).astype(BF16)

    def gains(wn):
        main = jnp.pad(wn, (0, HEAD_PAD - qk)).reshape(1, HEAD_PAD)
        rot = jnp.concatenate([jnp.zeros((nope,), F32), wn[nope + half:], wn[nope:nope + half],
                               jnp.zeros((HEAD_PAD - qk,), F32)]).reshape(1, HEAD_PAD)
        return main, rot

    qg, qgs = gains(q_norm_w)
    kg, kgs = gains(k_norm_w)
    inv_freq = ROPE_THETA ** (-jnp.arange(half, dtype=F32) / half)
    invf = jnp.concatenate([jnp.zeros((nope,), F32), inv_freq, inv_freq,
                            jnp.zeros((HEAD_PAD - qk,), F32)]).reshape(1, HEAD_PAD)
    pos_b = jnp.broadcast_to(positions.astype(F32)[:, :, None], (bsz, s, LANES))
    q, k, v = _mla_prep(mla_in, pos_b, invf, q_a_norm_w.reshape(1, -1), kv_a_norm_w.reshape(1, -1),
                        wq, wkv, qg, qgs, kg, kgs, tm)
    tq = min(ATTN_BLOCK, s)
    vt = v.reshape(bsz, MLA_HEADS, s // tq, tq, HEAD_PAD).transpose(0, 1, 2, 4, 3)
    o = _attention(q, k, vt, tq)

    w = RWKV_WIDTH
    wwa = jnp.zeros((LANES, 2 * w), F32)
    wwa = wwa.at[:DECAY_LORA, :w].set(rwkv_w2).at[DECAY_LORA:, w:].set(rwkv_a2)
    hid = np.arange(w) // RWKV_HEAD
    bd = jnp.asarray(hid[:, None] == hid[None, :], BF16)
    r_, lw_, k_, v_, a_, b_, g_ = _rwkv_prep(
        slab, rwkv_mu.reshape(1, -1), wwa, rwkv_w0.reshape(1, -1), rwkv_a0.reshape(1, -1), rwkv_g2,
        rwkv_k_k.reshape(1, -1), rwkv_k_a.reshape(1, -1), bd, tm)
    y = _wkv(r_, lw_, k_, v_, a_, b_, WKV_BATCH if bsz % WKV_BATCH == 0 else 1)

    woa = jnp.zeros((MLA_HEADS, HEAD_PAD, d), F32).at[:, :MLA_V, :].set(
        w_o_mla.reshape(MLA_HEADS, MLA_V, d)).astype(BF16)
    wr = _pad_cols(w_router, LANES)
    wrh = wr.astype(BF16)
    wrl = (wr - wrh.astype(F32)).astype(BF16)
    br = jnp.pad(b_router, (0, LANES - N_EXPERTS)).reshape(1, LANES)
    x1, h2, route, counts = _post(
        x, mod3, y, r_, k_, v_, g_, o, gates, rwkv_ln_w.reshape(1, -1), rwkv_ln_b.reshape(1, -1),
        rwkv_r_k.reshape(1, -1), bd, rwkv_w_o.astype(BF16), woa, w_out.astype(BF16),
        norm_ffn_w.reshape(1, -1), wrh, wrl, br, tm)

    bm = MOE_BLOCK
    n_rows = t * TOP_K + N_EXPERTS * bm
    n_blocks = n_rows // bm
    cnt = counts[0, :N_EXPERTS].astype(jnp.int32)
    padded = (cnt + bm - 1) // bm * bm
    pad_end = jnp.cumsum(padded)
    pad_start = pad_end - padded
    route2 = route.reshape(t, LANES)
    top_e = route2[:, 0:TOP_K].astype(jnp.int32)
    rank = route2[:, 2 * TOP_K:3 * TOP_K].astype(jnp.int32)
    dest = (pad_start[top_e] + rank).reshape(-1)
    blk_start = jnp.arange(n_blocks, dtype=jnp.int32) * bm
    blk_e = jnp.minimum(jnp.sum((pad_end[None, :] <= blk_start[:, None]).astype(jnp.int32), axis=1),
                        N_EXPERTS - 1)
    n_used = (pad_end[-1:] // bm).astype(jnp.int32)

    xs = _dispatch(dest, h2.reshape(t, d), jnp.zeros((n_rows, d), F32), min(256, t))
    wgu = w_gate_up.reshape(N_EXPERTS, d, D_FF, 2)
    bgu = b_gate_up.reshape(N_EXPERTS, 1, D_FF, 2)
    ys = _moe(blk_e, n_used, xs, wgu[..., 0].astype(BF16), wgu[..., 1].astype(BF16),
              bgu[..., 0], bgu[..., 1], w_down.astype(BF16), b_down.reshape(N_EXPERTS, 1, d))
    return _combine(dest, x1, route, mod3, ys, min(128, s))


def kernel(x, c, positions, ada_w, ada_b, norm_mix_w, norm_ffn_w, w_in, q_a_norm_w, w_q_up, kv_a_norm_w, w_kv_up, q_norm_w, k_norm_w, w_o_mla, rwkv_mu, rwkv_w0, rwkv_w2, rwkv_a0, rwkv_a2, rwkv_g2, rwkv_k_k, rwkv_k_a, rwkv_r_k, rwkv_ln_w, rwkv_ln_b, rwkv_w_o, w_out, w_router, b_router, w_gate_up, b_gate_up, w_down, b_down):
    depth = ada_w.shape[0]
    for l in range(depth):
        mod = _ada(c, ada_w[l], ada_b[l])
        x = _layer(x, mod, positions, w_in[l], q_a_norm_w[l], w_q_up[l], kv_a_norm_w[l], w_kv_up[l],
                   q_norm_w[l], k_norm_w[l], w_o_mla[l], rwkv_mu[l], rwkv_w0[l], rwkv_w2[l], rwkv_a0[l],
                   rwkv_a2[l], rwkv_g2[l], rwkv_k_k[l], rwkv_k_a[l], rwkv_r_k[l], rwkv_ln_w[l],
                   rwkv_ln_b[l], rwkv_w_o[l], w_out[l], norm_mix_w[l], norm_ffn_w[l], w_router[l],
                   b_router[l], w_gate_up[l], b_gate_up[l], w_down[l], b_down[l])
    return x
```

```python
import functools

import numpy as np
import jax
import jax.numpy as jnp
from jax import lax
from jax.experimental import pallas as pl
from jax.experimental.pallas import tpu as pltpu

F32 = jnp.float32
BF16 = jnp.bfloat16

D_MODEL = 1024
MLA_HEADS = 8
MLA_NOPE = 64
MLA_ROPE = 32
MLA_QK = MLA_NOPE + MLA_ROPE
MLA_V = 64
Q_LORA = 256
KV_LORA = 128
ROPE_THETA = 10000.0
RWKV_HEADS = 8
RWKV_HEAD = 64
RWKV_WIDTH = RWKV_HEADS * RWKV_HEAD
DECAY_LORA = 64
AAA_LORA = 64
GATE_LORA = 128
RWKV_SLAB = 3 * RWKV_WIDTH + DECAY_LORA + AAA_LORA + GATE_LORA
GN_EPS = 64e-5
N_EXPERTS = 32
TOP_K = 4
D_FF = D_MODEL
SWIGLU_LIMIT = 7.0
SWIGLU_ALPHA = 1.702
NORM_EPS = 1e-6
LOG2_E = 1.4426950408889634

LANES = 128
HEAD_PAD = 128
MLA_COLS = 640
WKV_CHUNK = 64
WKV_BATCH = 2
ATTN_BLOCK = 512
ATTN_HEADS = 4
MOE_BLOCK = 256
DMA_UNROLL = 8
VMEM_LIMIT = 56 * 1024 * 1024


def _dot(a, b):
    return jnp.dot(a, b, preferred_element_type=F32)


def _dot_nt(a, b):
    return lax.dot_general(a, b, (((1,), (1,)), ((), ())), preferred_element_type=F32)


def _split(x):
    hi = x.astype(BF16)
    lo = (x - hi.astype(F32)).astype(BF16)
    return hi, lo


def _mm3(a, b):
    ah, al = _split(a)
    bh, bl = _split(b)
    return _dot(ah, bh) + (_dot(ah, bl) + _dot(al, bh))


def _mm3_nt(a, b):
    ah, al = _split(a)
    bh, bl = _split(b)
    return _dot_nt(ah, bh) + (_dot_nt(ah, bl) + _dot_nt(al, bh))


def _mm_exact_rhs(a, b_bf16):
    ah, al = _split(a)
    return _dot(ah, b_bf16) + _dot(al, b_bf16)


def _params(*sem):
    return pltpu.CompilerParams(dimension_semantics=sem, vmem_limit_bytes=VMEM_LIMIT)


def _ada_kernel(c_ref, w_ref, b_ref, o_ref):
    c = c_ref[...]
    cond = c * jax.nn.sigmoid(c)
    o_ref[...] = _mm3(cond, w_ref[...]) + b_ref[...]


def _ada(c, w, b):
    bsz, d = c.shape
    n = w.shape[1]
    tn = 1024
    return pl.pallas_call(
        _ada_kernel,
        grid=(n // tn,),
        in_specs=[pl.BlockSpec((bsz, d), lambda j: (0, 0)),
                  pl.BlockSpec((d, tn), lambda j: (0, j)),
                  pl.BlockSpec((1, tn), lambda j: (0, j))],
        out_specs=pl.BlockSpec((bsz, tn), lambda j: (0, j)),
        out_shape=jax.ShapeDtypeStruct((bsz, n), F32),
        compiler_params=_params("parallel"),
        name="ada",
    )(c, w, b.reshape(1, n))


def _inproj_kernel(x_ref, mod_ref, nw_ref, w_ref, mla_ref, slab_ref, gates_ref):
    x = x_ref[0]
    ms = jnp.mean(x * x, axis=-1, keepdims=True)
    y = x * lax.rsqrt(ms + NORM_EPS) * nw_ref[...]
    h = y * (1.0 + mod_ref[0, 1:2, :]) + mod_ref[0, 0:1, :]
    hb = h.astype(BF16)
    mla_ref[0] = _dot(hb, w_ref[:, 0:MLA_COLS])
    slab_ref[0] = _dot(hb, w_ref[:, MLA_COLS:MLA_COLS + RWKV_SLAB])
    gates_ref[0] = _dot(hb, w_ref[:, MLA_COLS + RWKV_SLAB:]).astype(BF16)


def _inproj(x, mod3, norm_w, w_in_p, tm):
    bsz, s, d = x.shape
    ncol = w_in_p.shape[1]
    return pl.pallas_call(
        _inproj_kernel,
        grid=(bsz, s // tm),
        in_specs=[pl.BlockSpec((1, tm, d), lambda b, i: (b, i, 0)),
                  pl.BlockSpec((1, 6, d), lambda b, i: (b, 0, 0)),
                  pl.BlockSpec((1, d), lambda b, i: (0, 0)),
                  pl.BlockSpec((d, ncol), lambda b, i: (0, 0))],
        out_specs=[pl.BlockSpec((1, tm, MLA_COLS), lambda b, i: (b, i, 0)),
                   pl.BlockSpec((1, tm, RWKV_SLAB), lambda b, i: (b, i, 0)),
                   pl.BlockSpec((1, tm, 2 * d), lambda b, i: (b, i, 0))],
        out_shape=[jax.ShapeDtypeStruct((bsz, s, MLA_COLS), F32),
                   jax.ShapeDtypeStruct((bsz, s, RWKV_SLAB), F32),
                   jax.ShapeDtypeStruct((bsz, s, 2 * d), BF16)],
        compiler_params=_params("parallel", "parallel"),
        name="inproj",
    )(x, mod3, norm_w.reshape(1, d), w_in_p)


def _mla_prep_kernel(mla_ref, pos_ref, invf_ref, qan_ref, kvn_ref, wq_ref, wkv_ref,
                     qg_ref, qgs_ref, kg_ref, kgs_ref, q_out, k_out, vt_out):
    m = mla_ref[0]
    cq = m[:, 0:Q_LORA]
    ckv = m[:, Q_LORA:Q_LORA + KV_LORA]
    kr = m[:, 384:512]
    krs = m[:, 512:640]
    ql = cq * lax.rsqrt(jnp.mean(cq * cq, axis=-1, keepdims=True) + NORM_EPS) * qan_ref[...]
    kvl = ckv * lax.rsqrt(jnp.mean(ckv * ckv, axis=-1, keepdims=True) + NORM_EPS) * kvn_ref[...]
    qall = _dot(ql.astype(BF16), wq_ref[...])
    kvall = _dot(kvl.astype(BF16), wkv_ref[...])
    ang = pos_ref[0] * invf_ref[...]
    cosf = jnp.cos(ang)
    sinf = jnp.sin(ang)
    scale = MLA_QK ** -0.5 * LOG2_E
    hw = MLA_HEADS * HEAD_PAD
    for h in range(MLA_HEADS):
        lo, hi = h * HEAD_PAD, (h + 1) * HEAD_PAD
        qh = qall[:, lo:hi]
        qs = qall[:, hw + lo:hw + hi]
        rs = lax.rsqrt(jnp.sum(qh * qh, axis=-1, keepdims=True) * (1.0 / MLA_QK) + NORM_EPS)
        qo = (qh * rs * qg_ref[...]) * cosf + (qs * rs * qgs_ref[...]) * sinf
        q_out[0, h] = (qo * scale).astype(BF16)
        kh = kvall[:, lo:hi] + kr
        rs = lax.rsqrt(jnp.sum(kh * kh, axis=-1, keepdims=True) * (1.0 / MLA_QK) + NORM_EPS)
        ko = (kh * rs * kg_ref[...]) * cosf + (krs * rs * kgs_ref[...]) * sinf
        k_out[0, h] = ko.astype(BF16)
        vt_out[0, h, 0] = kvall[:, hw + lo:hw + hi].T[0:MLA_V, :].astype(BF16)


def _mla_prep(mla_in, pos_b, invf, qan, kvn, wq, wkv, qg, qgs, kg, kgs, tm, tq):
    bsz, s, _ = mla_in.shape
    hw = MLA_HEADS * HEAD_PAD
    per = tq // tm
    row = lambda n: pl.BlockSpec((1, n), lambda b, i: (0, 0))
    out_spec = pl.BlockSpec((1, MLA_HEADS, tm, HEAD_PAD), lambda b, i: (b, 0, i, 0))
    out_sds = jax.ShapeDtypeStruct((bsz, MLA_HEADS, s, HEAD_PAD), BF16)
    vt_spec = pl.BlockSpec((1, MLA_HEADS, 1, MLA_V, tm), lambda b, i: (b, 0, i // per, 0, i % per))
    vt_sds = jax.ShapeDtypeStruct((bsz, MLA_HEADS, s // tq, MLA_V, tq), BF16)
    return pl.pallas_call(
        _mla_prep_kernel,
        grid=(bsz, s // tm),
        in_specs=[pl.BlockSpec((1, tm, MLA_COLS), lambda b, i: (b, i, 0)),
                  pl.BlockSpec((1, tm, LANES), lambda b, i: (b, i, 0)),
                  row(LANES), row(Q_LORA), row(KV_LORA),
                  pl.BlockSpec((Q_LORA, 2 * hw), lambda b, i: (0, 0)),
                  pl.BlockSpec((KV_LORA, 2 * hw), lambda b, i: (0, 0)),
                  row(LANES), row(LANES), row(LANES), row(LANES)],
        out_specs=[out_spec, out_spec, vt_spec],
        out_shape=[out_sds, out_sds, vt_sds],
        compiler_params=_params("parallel", "parallel"),
        name="mla_prep",
    )(mla_in, pos_b, invf, qan, kvn, wq, wkv, qg, qgs, kg, kgs)


def _attn_kernel(q_ref, k_ref, vt_ref, o_ref, m_ref, l_ref, acc_ref, *, tq):
    qi = pl.program_id(2)
    heads = range(q_ref.shape[1])
    qs = [q_ref[0, h] for h in heads]
    m_ref[...] = jnp.full_like(m_ref, -jnp.inf)
    l_ref[...] = jnp.zeros_like(l_ref)
    acc_ref[...] = jnp.zeros_like(acc_ref)

    def step(j, mask):
        off = pl.multiple_of(j * tq, tq)
        sts = [_dot_nt(k_ref[0, h, pl.ds(off, tq), :], qs[h]) for h in heads]
        for h in heads:
            st = sts[h] if mask is None else jnp.where(mask, sts[h], -jnp.inf)
            m = m_ref[h]
            m_new = jnp.maximum(m, jnp.max(st, axis=0, keepdims=True))
            p = jnp.exp2(st - m_new)
            alpha = jnp.exp2(m - m_new)
            m_ref[h] = m_new
            l_ref[h] = alpha * l_ref[h] + jnp.sum(p, axis=0, keepdims=True)
            acc_ref[h] = alpha * acc_ref[h] + _dot(vt_ref[0, h, j], p.astype(BF16))

    def body(j, carry):
        step(j, None)
        return carry

    lax.fori_loop(0, qi, body, 0)
    keys = lax.broadcasted_iota(jnp.int32, (tq, tq), 0)
    queries = lax.broadcasted_iota(jnp.int32, (tq, tq), 1)
    step(qi, keys <= queries)
    for h in heads[::2]:
        pair = jnp.concatenate([acc_ref[h] / l_ref[h], acc_ref[h + 1] / l_ref[h + 1]], axis=0)
        o_ref[0, :, h * MLA_V:(h + 2) * MLA_V] = pair.T.astype(BF16)


def _attention(q, k, vt, tq):
    bsz, nh, s, dh = q.shape
    dv = vt.shape[3]
    hp = ATTN_HEADS
    return pl.pallas_call(
        functools.partial(_attn_kernel, tq=tq),
        grid=(bsz, nh // hp, s // tq),
        in_specs=[pl.BlockSpec((1, hp, tq, dh), lambda b, h, i: (b, h, i, 0)),
                  pl.BlockSpec((1, hp, s, dh), lambda b, h, i: (b, h, 0, 0)),
                  pl.BlockSpec((1, hp, s // tq, dv, tq), lambda b, h, i: (b, h, 0, 0, 0))],
        out_specs=pl.BlockSpec((1, tq, hp * dv), lambda b, h, i: (b, i, h)),
        out_shape=jax.ShapeDtypeStruct((bsz, s, nh * dv), BF16),
        scratch_shapes=[pltpu.VMEM((hp, 1, tq), F32), pltpu.VMEM((hp, 1, tq), F32),
                        pltpu.VMEM((hp, dv, tq), F32)],
        compiler_params=_params("parallel", "parallel", "parallel"),
        name="attn",
    )(q, k, vt)


def _softplus(x):
    return jnp.maximum(x, 0.0) + jnp.log(1.0 + jnp.exp(-jnp.abs(x)))


def _rwkv_prep_kernel(slab_ref, mu_ref, wwa_ref, w0_ref, a0_ref, g2_ref, kk_ref, ka_ref, bd_ref,
                      r_out, lw_out, k_out, v_out, a_out, b_out, g_out, last_ref):
    i = pl.program_id(1)
    tm = slab_ref.shape[1]
    w = RWKV_WIDTH

    @pl.when(i == 0)
    def _():
        last_ref[...] = jnp.zeros_like(last_ref)

    slab = slab_ref[0]
    rolled = pltpu.roll(slab, 1, 0)
    rowi = lax.broadcasted_iota(jnp.int32, slab.shape, 0)
    prev = jnp.where(rowi == 0, last_ref[0:1, :], rolled)
    last_ref[0:1, :] = slab[tm - 1:tm, :]
    p = slab + (prev - slab) * mu_ref[...]
    r = p[:, 0:w]
    k = p[:, w:2 * w]
    v = p[:, 2 * w:3 * w]
    wa = p[:, 3 * w:3 * w + LANES]
    g_lo = p[:, 3 * w + LANES:3 * w + 2 * LANES]
    lane = lax.broadcasted_iota(jnp.int32, wa.shape, 1)
    wa = jnp.where(lane < DECAY_LORA, jnp.tanh(wa), wa)
    wa_o = _mm3(wa, wwa_ref[...])
    log_w = -_softplus(-(w0_ref[...] + wa_o[:, 0:w])) - 0.5
    lw = -jnp.exp(log_w)
    a = jax.nn.sigmoid(a0_ref[...] + wa_o[:, w:2 * w])
    g = _mm3(jax.nn.sigmoid(g_lo), g2_ref[...])
    kk = k * kk_ref[...]
    ss = _mm_exact_rhs(kk * kk, bd_ref[...])
    kk = kk / jnp.maximum(jnp.sqrt(ss), 1e-12)
    r_out[0] = r
    lw_out[0] = lw
    k_out[0] = k * (1.0 + (a - 1.0) * ka_ref[...])
    v_out[0] = v
    a_out[0] = -kk
    b_out[0] = kk * a
    g_out[0] = g


def _rwkv_prep(slab, mu, wwa, w0, a0, g2, k_k, k_a, bd, tm):
    bsz, s, _ = slab.shape
    w = RWKV_WIDTH
    row = lambda n: pl.BlockSpec((1, n), lambda b, i: (0, 0))
    full = lambda a: pl.BlockSpec(a.shape, lambda b, i: (0, 0))
    out_spec = pl.BlockSpec((1, tm, w), lambda b, i: (b, i, 0))
    out_sds = jax.ShapeDtypeStruct((bsz, s, w), F32)
    return pl.pallas_call(
        _rwkv_prep_kernel,
        grid=(bsz, s // tm),
        in_specs=[pl.BlockSpec((1, tm, RWKV_SLAB), lambda b, i: (b, i, 0)),
                  row(RWKV_SLAB), full(wwa), row(w), row(w), full(g2), row(w), row(w), full(bd)],
        out_specs=[out_spec] * 7,
        out_shape=[out_sds] * 7,
        scratch_shapes=[pltpu.VMEM((8, RWKV_SLAB), F32)],
        compiler_params=_params("parallel", "arbitrary"),
        name="rwkv_prep",
    )(slab, mu, wwa, w0, a0, g2, k_k, k_a, bd)


def _wkv_kernel(r_ref, lw_ref, k_ref, v_ref, a_ref, b_ref, y_ref, s_ref):
    c = WKV_CHUNK
    n = pl.program_id(1)

    @pl.when(n == 0)
    def _():
        s_ref[...] = jnp.zeros_like(s_ref)

    row = lax.broadcasted_iota(jnp.int32, (LANES, LANES), 0)
    col = lax.broadcasted_iota(jnp.int32, (LANES, LANES), 1)
    lower_strict = row > col
    lower_incl = row >= col
    eye = row == col
    same16 = (row >> 4) == (col >> 4)
    same32 = (row >> 5) == (col >> 5)
    ident = jnp.where(eye, 1.0, 0.0).astype(F32)
    tr = lax.broadcasted_iota(jnp.int32, (c, c), 0)
    tc = lax.broadcasted_iota(jnp.int32, (c, c), 1)
    tri = jnp.where(tr >= tc, 1.0, 0.0).astype(BF16)
    first = lax.broadcasted_iota(jnp.int32, (c, LANES), 1) < RWKV_HEAD

    def stack(x):
        return jnp.concatenate([jnp.where(first, x, 0.0), jnp.where(first, 0.0, x)], axis=0)

    def mm(x, y):
        return _dot(x.astype(BF16), y.astype(BF16))

    nb = r_ref.shape[0]
    probs = [(bi, p) for bi in range(nb) for p in range(RWKV_HEADS // 2)]
    each = lambda f, *xs: [f(*args) for args in zip(*xs)]
    load = lambda ref: [ref[bi, :, p * LANES:(p + 1) * LANES] for bi, p in probs]
    r, lw, k, v, a, b = (load(ref) for ref in (r_ref, lw_ref, k_ref, v_ref, a_ref, b_ref))

    def cumsum(x):
        hi = x.astype(BF16)
        rem = x - hi.astype(F32)
        mid = rem.astype(BF16)
        lo = (rem - mid.astype(F32)).astype(BF16)
        return _dot(tri, hi) + (_dot(tri, mid) + _dot(tri, lo))

    cum = each(cumsum, lw)
    cum_c = [x[c - 1:c, :] for x in cum]
    e_in = each(jnp.exp, cum)
    e_neg = each(lambda x: jnp.exp(-x), cum)
    e_end = each(lambda x, xc: jnp.exp(xc - x), cum, cum_c)
    a_s = each(lambda x, cu, l: stack(x * jnp.exp(cu - l)).astype(BF16), a, cum, lw)
    r_s = each(lambda x, e: stack(x * e), r, e_in)
    b_s = each(lambda x, e: stack(x * e).astype(BF16), b, e_neg)
    k_s = each(lambda x, e: stack(x * e).astype(BF16), k, e_neg)
    b_e = each(lambda x, e: stack(x * e).T.astype(BF16), b, e_end)
    k_e = each(lambda x, e: stack(x * e).T.astype(BF16), k, e_end)
    v_s = each(lambda x: stack(x).astype(BF16), v)

    tt = each(lambda x1, x2, y1, y2: _dot_nt(jnp.concatenate([x1, x2.astype(BF16)], axis=0),
                                             jnp.concatenate([y1, y2], axis=0)), a_s, r_s, b_s, k_s)
    d_ab = [jnp.where(lower_strict, x[0:LANES, 0:LANES], 0.0) for x in tt]
    e_ak = [jnp.where(lower_strict, x[0:LANES, LANES:], 0.0).astype(BF16) for x in tt]
    f_rb = [jnp.where(lower_incl, x[LANES:, 0:LANES], 0.0).astype(BF16) for x in tt]
    f_rk = [jnp.where(lower_incl, x[LANES:, LANES:], 0.0).astype(BF16) for x in tt]

    d16 = [jnp.where(same16, x, 0.0) for x in d_ab]
    d32 = [jnp.where(same32, x, 0.0) for x in d_ab]
    z = each(_dot, e_ak, v_s)
    x2 = each(mm, d16, d16)
    x4 = each(mm, x2, x2)
    x8 = each(mm, x4, x4)
    t = [ident + x for x in d16]
    t = each(lambda t_, x: t_ + mm(t_, x), t, x2)
    t = each(lambda t_, x: t_ + mm(t_, x), t, x4)
    t = each(lambda t_, x: t_ + mm(t_, x), t, x8)
    t = each(lambda t_, hi, lo: t_ + mm(mm(t_, hi - lo), t_), t, d32, d16)
    t = each(lambda t_, hi, lo: t_ + mm(mm(t_, hi - lo), t_), t, d_ab, d32)

    au = each(lambda t_, x, zz: mm(t_, jnp.concatenate([x, zz.astype(BF16)], axis=1)).astype(BF16),
              t, a_s, z)
    g = each(_dot, f_rb, au)
    y0 = each(lambda gg, f, vv: gg[:, LANES:] + _dot(f, vv), g, f_rk, v_s)
    r1 = each(lambda x, gg: x + gg[:, 0:LANES], r_s, g)
    hmat = each(_dot, b_e, au)
    m_mat = each(lambda h, xc: jnp.where(eye, jnp.exp(xc), 0.0) + h[:, 0:LANES], hmat, cum_c)
    n_mat = each(lambda h, ke, vv: h[:, LANES:] + _dot(ke, vv), hmat, k_e, v_s)

    for i, (bi, p) in enumerate(probs):
        s0 = s_ref[i]
        ys = _mm3(r1[i], s0) + y0[i]
        s_ref[i] = _mm3(m_mat[i], s0) + n_mat[i]
        y_ref[bi, :, p * LANES:(p + 1) * LANES] = ys[0:c, :] + ys[c:, :]


def _wkv(r, lw, k, v, a, b, nb):
    bsz, s, w = r.shape
    spec = pl.BlockSpec((nb, WKV_CHUNK, w), lambda bi, n: (bi, n, 0))
    return pl.pallas_call(
        _wkv_kernel,
        grid=(bsz // nb, s // WKV_CHUNK),
        in_specs=[spec] * 6,
        out_specs=spec,
        out_shape=jax.ShapeDtypeStruct((bsz, s, w), F32),
        scratch_shapes=[pltpu.VMEM((nb * RWKV_HEADS // 2, LANES, LANES), F32)],
        compiler_params=_params("parallel", "arbitrary"),
        name="wkv",
    )(r, lw, k, v, a, b)


def _post_kernel(x_ref, mod_ref, y_ref, r_ref, k_ref, v_ref, g_ref, o_ref, gates_ref,
                 lnw_ref, lnb_ref, rk_ref, bd_ref, wor_ref, woa_ref, wout_ref, nfw_ref,
                 wrh_ref, wrl_ref, br_ref,
                 x1_ref, h2_ref, route_ref, cnt_ref, run_ref):
    first_step = jnp.logical_and(pl.program_id(0) == 0, pl.program_id(1) == 0)

    @pl.when(first_step)
    def _():
        run_ref[...] = jnp.zeros_like(run_ref)

    tm = x_ref.shape[1]
    d = x_ref.shape[2]
    bd = bd_ref[...]
    inv_n = 1.0 / RWKV_HEAD
    y = y_ref[0]
    seg = lambda t: _dot(t.astype(BF16), bd)
    mu = seg(y) * inv_n
    dlt = y - mu
    var = seg(dlt * dlt) * inv_n
    yn = dlt * lax.rsqrt(var + GN_EPS) * lnw_ref[...] + lnb_ref[...]
    v = v_ref[0]
    bonus = seg(r_ref[0] * k_ref[0] * rk_ref[...]) * v
    z = (yn + bonus) * g_ref[0]
    o_b = _dot(z.astype(BF16), wor_ref[...])
    o_a = _dot(o_ref[0], woa_ref[...])
    ga = jax.nn.sigmoid(gates_ref[0, :, 0:d].astype(F32))
    gb = jax.nn.sigmoid(gates_ref[0, :, d:2 * d].astype(F32))
    merged = ga * o_a + gb * o_b
    mix = _dot(merged.astype(BF16), wout_ref[...])
    x1 = x_ref[0] + mod_ref[0, 2:3, :] * mix
    x1_ref[0] = x1
    ms = jnp.mean(x1 * x1, axis=-1, keepdims=True)
    h2 = x1 * lax.rsqrt(ms + NORM_EPS) * nfw_ref[...]
    h2 = h2 * (1.0 + mod_ref[0, 4:5, :]) + mod_ref[0, 3:4, :]
    h2_ref[0] = h2

    hh, hl = _split(h2)
    logits = _dot(hh, wrh_ref[...]) + (_dot(hh, wrl_ref[...]) + _dot(hl, wrh_ref[...])) + br_ref[...]
    lane = lax.broadcasted_iota(jnp.int32, (tm, LANES), 1)
    lanef = lane.astype(F32)
    cur = jnp.where(lane < N_EXPERTS, logits, -jnp.inf)
    vals, idxs, hots = [], [], []
    for _ in range(TOP_K):
        mx = jnp.max(cur, axis=-1, keepdims=True)
        idx = jnp.min(jnp.where(cur == mx, lanef, float(LANES)), axis=-1, keepdims=True)
        hot = lanef == idx
        cur = jnp.where(hot, -jnp.inf, cur)
        vals.append(mx)
        idxs.append(idx)
        hots.append(hot)
    exps = [jnp.exp(vv - vals[0]) for vv in vals]
    den = exps[0] + exps[1] + exps[2] + exps[3]
    sel = jnp.zeros((tm, LANES), F32)
    for hot in hots:
        sel = sel + jnp.where(hot, 1.0, 0.0)
    ri = lax.broadcasted_iota(jnp.int32, (tm, tm), 0)
    ci = lax.broadcasted_iota(jnp.int32, (tm, tm), 1)
    below = jnp.where(ri > ci, 1.0, 0.0).astype(BF16)
    run = run_ref[0:1, :]
    before = _dot(below, sel.astype(BF16)) + run
    route = jnp.zeros((tm, LANES), F32)
    for j in range(TOP_K):
        rank = jnp.sum(jnp.where(hots[j], before, 0.0), axis=-1, keepdims=True)
        route = jnp.where(lane == j, idxs[j], route)
        route = jnp.where(lane == TOP_K + j, exps[j] / den, route)
        route = jnp.where(lane == 2 * TOP_K + j, rank, route)
    route_ref[0] = route
    run = run + jnp.sum(sel, axis=0, keepdims=True)
    run_ref[0:1, :] = run
    cnt_ref[...] = jnp.broadcast_to(run, cnt_ref.shape)


def _post(x, mod3, y, r, k, v, g, o, gates, lnw, lnb, rk, bd, wor, woa, wout, nfw, wrh, wrl, br, tm):
    bsz, s, d = x.shape
    w = RWKV_WIDTH
    tok = lambda n: pl.BlockSpec((1, tm, n), lambda b, i: (b, i, 0))
    row = lambda n: pl.BlockSpec((1, n), lambda b, i: (0, 0))
    full = lambda a: pl.BlockSpec(a.shape, lambda b, i: (0,) * a.ndim)
    return pl.pallas_call(
        _post_kernel,
        grid=(bsz, s // tm),
        in_specs=[tok(d), pl.BlockSpec((1, 6, d), lambda b, i: (b, 0, 0)),
                  tok(w), tok(w), tok(w), tok(w), tok(w),
                  tok(MLA_HEADS * MLA_V), tok(2 * d),
                  row(w), row(w), row(w), full(bd), full(wor), full(woa), full(wout), row(d),
                  full(wrh), full(wrl), row(LANES)],
        out_specs=[tok(d), tok(d), tok(LANES), pl.BlockSpec((8, LANES), lambda b, i: (0, 0))],
        out_shape=[jax.ShapeDtypeStruct((bsz, s, d), F32),
                   jax.ShapeDtypeStruct((bsz, s, d), F32),
                   jax.ShapeDtypeStruct((bsz, s, LANES), F32),
                   jax.ShapeDtypeStruct((8, LANES), F32)],
        scratch_shapes=[pltpu.VMEM((8, LANES), F32)],
        compiler_params=_params("arbitrary", "arbitrary"),
        name="post",
    )(x, mod3, y, r, k, v, g, o, gates, lnw, lnb, rk, bd, wor, woa, wout, nfw, wrh, wrl, br)


def _dispatch_kernel(dest_ref, h_ref, xs_in_ref, xs_ref, sem):
    del xs_in_ref
    tm = h_ref.shape[0]
    base = pl.program_id(0) * tm * TOP_K

    def start(g, carry):
        for u in range(DMA_UNROLL):
            r = g * DMA_UNROLL + u
            for j in range(TOP_K):
                dst = dest_ref[base + r * TOP_K + j]
                pltpu.make_async_copy(h_ref.at[pl.ds(r, 1), :], xs_ref.at[pl.ds(dst, 1), :], sem).start()
        return carry

    lax.fori_loop(0, tm // DMA_UNROLL, start, 0)
    for _ in range(TOP_K):
        pltpu.make_async_copy(h_ref, xs_ref.at[pl.ds(0, tm), :], sem).wait()


def _dispatch(dest, h2, xs_init, tm):
    t, d = h2.shape
    return pl.pallas_call(
        _dispatch_kernel,
        grid_spec=pltpu.PrefetchScalarGridSpec(
            num_scalar_prefetch=1,
            grid=(t // tm,),
            in_specs=[pl.BlockSpec((tm, d), lambda i, dest: (i, 0)),
                      pl.BlockSpec(memory_space=pl.ANY)],
            out_specs=pl.BlockSpec(memory_space=pl.ANY),
            scratch_shapes=[pltpu.SemaphoreType.DMA(())]),
        out_shape=jax.ShapeDtypeStruct(xs_init.shape, F32),
        input_output_aliases={2: 0},
        compiler_params=_params("arbitrary"),
        name="dispatch",
    )(dest, h2, xs_init)


def _moe_kernel(blk_e_ref, n_used_ref, xs_ref, wg_ref, wu_ref, bg_ref, bu_ref, wd_ref, bd_ref, ys_ref):
    del blk_e_ref
    i = pl.program_id(0)

    @pl.when(i < n_used_ref[0])
    def _():
        x = xs_ref[...].astype(BF16)
        gate = _dot(x, wg_ref[0]) + bg_ref[0]
        up = _dot(x, wu_ref[0]) + bu_ref[0]
        gate = jnp.minimum(gate, SWIGLU_LIMIT)
        up = jnp.clip(up, -SWIGLU_LIMIT, SWIGLU_LIMIT)
        act = (up + 1.0) * gate * jax.nn.sigmoid(SWIGLU_ALPHA * gate)
        ys_ref[...] = _dot(act.astype(BF16), wd_ref[0]) + bd_ref[0]

    @pl.when(i >= n_used_ref[0])
    def _():
        ys_ref[...] = jnp.zeros_like(ys_ref)


def _moe(blk_e, n_used, xs, wg, wu, bg, bu, wd, bd):
    n_rows, d = xs.shape
    ff = wg.shape[2]
    bm = MOE_BLOCK
    wspec = lambda k, n: pl.BlockSpec((1, k, n), lambda i, be, nu: (be[i], 0, 0))
    return pl.pallas_call(
        _moe_kernel,
        grid_spec=pltpu.PrefetchScalarGridSpec(
            num_scalar_prefetch=2,
            grid=(n_rows // bm,),
            in_specs=[pl.BlockSpec((bm, d), lambda i, be, nu: (i, 0)),
                      wspec(d, ff), wspec(d, ff), wspec(1, ff), wspec(1, ff), wspec(ff, d), wspec(1, d)],
            out_specs=pl.BlockSpec((bm, d), lambda i, be, nu: (i, 0))),
        out_shape=jax.ShapeDtypeStruct((n_rows, d), F32),
        compiler_params=_params("arbitrary"),
        name="moe",
    )(blk_e, n_used, xs, wg, wu, bg, bu, wd, bd)


def _combine_kernel(dest_ref, x1_ref, route_ref, mod_ref, ys_ref, o_ref, buf_ref, sem):
    tm = x1_ref.shape[1]
    s = pl.num_programs(1) * tm
    base = (pl.program_id(0) * s + pl.program_id(1) * tm) * TOP_K

    def start(g, carry):
        for u in range(DMA_UNROLL):
            r = g * DMA_UNROLL + u
            for j in range(TOP_K):
                src = dest_ref[base + r * TOP_K + j]
                pltpu.make_async_copy(ys_ref.at[pl.ds(src, 1), :], buf_ref.at[j, pl.ds(r, 1), :], sem).start()
        return carry

    lax.fori_loop(0, tm // DMA_UNROLL, start, 0)
    for j in range(TOP_K):
        pltpu.make_async_copy(ys_ref.at[pl.ds(0, tm), :], buf_ref.at[j], sem).wait()
    route = route_ref[0]
    acc = route[:, TOP_K:TOP_K + 1] * buf_ref[0]
    for j in range(1, TOP_K):
        acc = acc + route[:, TOP_K + j:TOP_K + j + 1] * buf_ref[j]
    o_ref[0] = x1_ref[0] + mod_ref[0, 5:6, :] * acc


def _combine(dest, x1, route, mod3, ys, tm):
    bsz, s, d = x1.shape
    return pl.pallas_call(
        _combine_kernel,
        grid_spec=pltpu.PrefetchScalarGridSpec(
            num_scalar_prefetch=1,
            grid=(bsz, s // tm),
            in_specs=[pl.BlockSpec((1, tm, d), lambda b, i, dest: (b, i, 0)),
                      pl.BlockSpec((1, tm, LANES), lambda b, i, dest: (b, i, 0)),
                      pl.BlockSpec((1, 6, d), lambda b, i, dest: (b, 0, 0)),
                      pl.BlockSpec(memory_space=pl.ANY)],
            out_specs=pl.BlockSpec((1, tm, d), lambda b, i, dest: (b, i, 0)),
            scratch_shapes=[pltpu.VMEM((TOP_K, tm, d), F32), pltpu.SemaphoreType.DMA(())]),
        out_shape=jax.ShapeDtypeStruct((bsz, s, d), F32),
        compiler_params=_params("arbitrary", "arbitrary"),
        name="combine",
    )(dest, x1, route, mod3, ys)


def _pad_cols(a, n):
    return jnp.pad(a, ((0, 0), (0, n - a.shape[1])))


def _head_blocks(cols_main, cols_rot=None):
    k = cols_main.shape[0]
    out = jnp.zeros((k, MLA_HEADS, HEAD_PAD), F32)
    out = out.at[:, :, :cols_main.shape[2]].set(cols_main)
    return out.reshape(k, MLA_HEADS * HEAD_PAD)


def _layer(x, cond_mod, positions, w_in, q_a_norm_w, w_q_up, kv_a_norm_w, w_kv_up, q_norm_w, k_norm_w,
           w_o_mla, rwkv_mu, rwkv_w0, rwkv_w2, rwkv_a0, rwkv_a2, rwkv_g2, rwkv_k_k, rwkv_k_a, rwkv_r_k,
           rwkv_ln_w, rwkv_ln_b, rwkv_w_o, w_out, norm_mix_w, norm_ffn_w, w_router, b_router,
           w_gate_up, b_gate_up, w_down, b_down):
    bsz, s, d = x.shape
    t = bsz * s
    half = MLA_ROPE // 2
    nope, qk = MLA_NOPE, MLA_QK
    mod3 = cond_mod.reshape(bsz, 6, d)

    o_q, o_kv, o_kr = 0, Q_LORA, Q_LORA + KV_LORA
    o_slab = o_kr + MLA_ROPE
    o_gate = o_slab + RWKV_SLAB
    kr_w = w_in[:, o_kr:o_slab]
    zeros = lambda n: jnp.zeros((d, n), F32)
    kr_blk = jnp.concatenate([zeros(nope), kr_w, zeros(HEAD_PAD - qk)], axis=1)
    kr_rot = jnp.concatenate([zeros(nope), -kr_w[:, half:], kr_w[:, :half], zeros(HEAD_PAD - qk)], axis=1)
    w_in_p = jnp.concatenate([w_in[:, o_q:o_kr], kr_blk, kr_rot, w_in[:, o_slab:]], axis=1).astype(BF16)

    tm = min(256, s)
    mla_in, slab, gates = _inproj(x, mod3, norm_mix_w, w_in_p, tm)

    wq3 = w_q_up.reshape(Q_LORA, MLA_HEADS, qk)
    wq_rot = jnp.concatenate([jnp.zeros((Q_LORA, MLA_HEADS, nope), F32), -wq3[:, :, nope + half:],
                              wq3[:, :, nope:nope + half]], axis=2)
    wq = jnp.concatenate([_head_blocks(wq3), _head_blocks(wq_rot)], axis=1).astype(BF16)
    wkv3 = w_kv_up.reshape(KV_LORA, MLA_HEADS, nope + MLA_V)
    wkv = jnp.concatenate([_head_blocks(wkv3[:, :, :nope]), _head_blocks(wkv3[:, :, nope:])], axis=1).astype(BF16)

    def gains(wn):
        main = jnp.pad(wn, (0, HEAD_PAD - qk)).reshape(1, HEAD_PAD)
        rot = jnp.concatenate([jnp.zeros((nope,), F32), wn[nope + half:], wn[nope:nope + half],
                               jnp.zeros((HEAD_PAD - qk,), F32)]).reshape(1, HEAD_PAD)
        return main, rot

    qg, qgs = gains(q_norm_w)
    kg, kgs = gains(k_norm_w)
    inv_freq = ROPE_THETA ** (-jnp.arange(half, dtype=F32) / half)
    invf = jnp.concatenate([jnp.zeros((nope,), F32), inv_freq, inv_freq,
                            jnp.zeros((HEAD_PAD - qk,), F32)]).reshape(1, HEAD_PAD)
    pos_b = jnp.broadcast_to(positions.astype(F32)[:, :, None], (bsz, s, LANES))
    tq = min(ATTN_BLOCK, s)
    q, k, vt = _mla_prep(mla_in, pos_b, invf, q_a_norm_w.reshape(1, -1), kv_a_norm_w.reshape(1, -1),
                         wq, wkv, qg, qgs, kg, kgs, tm, tq)
    o = _attention(q, k, vt, tq)

    w = RWKV_WIDTH
    wwa = jnp.zeros((LANES, 2 * w), F32)
    wwa = wwa.at[:DECAY_LORA, :w].set(rwkv_w2).at[DECAY_LORA:, w:].set(rwkv_a2)
    hid = np.arange(w) // RWKV_HEAD
    bd = jnp.asarray(hid[:, None] == hid[None, :], BF16)
    r_, lw_, k_, v_, a_, b_, g_ = _rwkv_prep(
        slab, rwkv_mu.reshape(1, -1), wwa, rwkv_w0.reshape(1, -1), rwkv_a0.reshape(1, -1), rwkv_g2,
        rwkv_k_k.reshape(1, -1), rwkv_k_a.reshape(1, -1), bd, tm)
    y = _wkv(r_, lw_, k_, v_, a_, b_, WKV_BATCH if bsz % WKV_BATCH == 0 else 1)

    woa = w_o_mla.astype(BF16)
    wr =_pad_cols(w_router, LANES)
    wrh = wr.astype(BF16)
    wrl = (wr - wrh.astype(F32)).astype(BF16)
    br = jnp.pad(b_router, (0, LANES - N_EXPERTS)).reshape(1, LANES)
    x1, h2, route, counts = _post(
        x, mod3, y, r_, k_, v_, g_, o, gates, rwkv_ln_w.reshape(1, -1), rwkv_ln_b.reshape(1, -1),
        rwkv_r_k.reshape(1, -1), bd, rwkv_w_o.astype(BF16), woa, w_out.astype(BF16),
        norm_ffn_w.reshape(1, -1), wrh, wrl, br, tm)

    bm = MOE_BLOCK
    n_rows = t * TOP_K + N_EXPERTS * bm
    n_blocks = n_rows // bm
    cnt = counts[0, :N_EXPERTS].astype(jnp.int32)
    padded = (cnt + bm - 1) // bm * bm
    pad_end = jnp.cumsum(padded)
    pad_start = pad_end - padded
    route2 = route.reshape(t, LANES)
    top_e = route2[:, 0:TOP_K].astype(jnp.int32)
    rank = route2[:, 2 * TOP_K:3 * TOP_K].astype(jnp.int32)
    dest = (pad_start[top_e] + rank).reshape(-1)
    blk_start = jnp.arange(n_blocks, dtype=jnp.int32) * bm
    blk_e = jnp.minimum(jnp.sum((pad_end[None, :] <= blk_start[:, None]).astype(jnp.int32), axis=1),
                        N_EXPERTS - 1)
    n_used = (pad_end[-1:] // bm).astype(jnp.int32)

    xs = _dispatch(dest, h2.reshape(t, d), jnp.zeros((n_rows, d), F32), min(256, t))
    wgu = w_gate_up.reshape(N_EXPERTS, d, D_FF, 2)
    bgu = b_gate_up.reshape(N_EXPERTS, 1, D_FF, 2)
    ys = _moe(blk_e, n_used, xs, wgu[..., 0].astype(BF16), wgu[..., 1].astype(BF16),
              bgu[..., 0], bgu[..., 1], w_down.astype(BF16), b_down.reshape(N_EXPERTS, 1, d))
    return _combine(dest, x1, route, mod3, ys, min(128, s))


def kernel(x, c, positions, ada_w, ada_b, norm_mix_w, norm_ffn_w, w_in, q_a_norm_w, w_q_up, kv_a_norm_w, w_kv_up, q_norm_w, k_norm_w, w_o_mla, rwkv_mu, rwkv_w0, rwkv_w2, rwkv_a0, rwkv_a2, rwkv_g2, rwkv_k_k, rwkv_k_a, rwkv_r_k, rwkv_ln_w, rwkv_ln_b, rwkv_w_o, w_out, w_router, b_router, w_gate_up, b_gate_up, w_down, b_down):
    depth = ada_w.shape[0]
    for l in range(depth):
        mod = _ada(c, ada_w[l], ada_b[l])
        x = _layer(x, mod, positions, w_in[l], q_a_norm_w[l], w_q_up[l], kv_a_norm_w[l], w_kv_up[l],
                   q_norm_w[l], k_norm_w[l], w_o_mla[l], rwkv_mu[l], rwkv_w0[l], rwkv_w2[l], rwkv_a0[l],
                   rwkv_a2[l], rwkv_g2[l], rwkv_k_k[l], rwkv_k_a[l], rwkv_r_k[l], rwkv_ln_w[l],
                   rwkv_ln_b[l], rwkv_w_o[l], w_out[l], norm_mix_w[l], norm_ffn_w[l], w_router[l],
                   b_router[l], w_gate_up[l], b_gate_up[l], w_down[l], b_down[l])
    return x
```

```python
import functools

import numpy as np
import jax
import jax.numpy as jnp
from jax import lax
from jax.experimental import pallas as pl
from jax.experimental.pallas import tpu as pltpu

F32 = jnp.float32
BF16 = jnp.bfloat16

D_MODEL = 1024
MLA_HEADS = 8
MLA_NOPE = 64
MLA_ROPE = 32
MLA_QK = MLA_NOPE + MLA_ROPE
MLA_V = 64
Q_LORA = 256
KV_LORA = 128
ROPE_THETA = 10000.0
RWKV_HEADS = 8
RWKV_HEAD = 64
RWKV_WIDTH = RWKV_HEADS * RWKV_HEAD
DECAY_LORA = 64
AAA_LORA = 64
GATE_LORA = 128
RWKV_SLAB = 3 * RWKV_WIDTH + DECAY_LORA + AAA_LORA + GATE_LORA
GN_EPS = 64e-5
N_EXPERTS = 32
TOP_K = 4
D_FF = D_MODEL
SWIGLU_LIMIT = 7.0
SWIGLU_ALPHA = 1.702
NORM_EPS = 1e-6
LOG2_E = 1.4426950408889634

LANES = 128
HEAD_PAD = 128
MLA_COLS = 640
WKV_CHUNK = 64
WKV_BATCH = 2
ATTN_BLOCK = 512
ATTN_HEADS = 4
MOE_BLOCK = 256
VMEM_LIMIT = 56 * 1024 * 1024


def _dot(a, b):
    return jnp.dot(a, b, preferred_element_type=F32)


def _dot_nt(a, b):
    return lax.dot_general(a, b, (((1,), (1,)), ((), ())), preferred_element_type=F32)


def _split(x):
    hi = x.astype(BF16)
    lo = (x - hi.astype(F32)).astype(BF16)
    return hi, lo


def _mm3(a, b):
    ah, al = _split(a)
    bh, bl = _split(b)
    return _dot(ah, bh) + (_dot(ah, bl) + _dot(al, bh))


def _mm3_nt(a, b):
    ah, al = _split(a)
    bh, bl = _split(b)
    return _dot_nt(ah, bh) + (_dot_nt(ah, bl) + _dot_nt(al, bh))


def _mm_exact_rhs(a, b_bf16):
    ah, al = _split(a)
    return _dot(ah, b_bf16) + _dot(al, b_bf16)


def _params(*sem):
    return pltpu.CompilerParams(dimension_semantics=sem, vmem_limit_bytes=VMEM_LIMIT)


def _ada_kernel(c_ref, w_ref, b_ref, o_ref):
    c = c_ref[...]
    cond = c * jax.nn.sigmoid(c)
    o_ref[...] = _mm3(cond, w_ref[...]) + b_ref[...]


def _ada(c, w, b):
    bsz, d = c.shape
    n = w.shape[1]
    tn = 1024
    return pl.pallas_call(
        _ada_kernel,
        grid=(n // tn,),
        in_specs=[pl.BlockSpec((bsz, d), lambda j: (0, 0)),
                  pl.BlockSpec((d, tn), lambda j: (0, j)),
                  pl.BlockSpec((1, tn), lambda j: (0, j))],
        out_specs=pl.BlockSpec((bsz, tn), lambda j: (0, j)),
        out_shape=jax.ShapeDtypeStruct((bsz, n), F32),
        compiler_params=_params("parallel"),
        name="ada",
    )(c, w, b.reshape(1, n))


def _inproj_kernel(x_ref, mod_ref, nw_ref, w_ref, mla_ref, slab_ref, gates_ref):
    x = x_ref[0]
    ms = jnp.mean(x * x, axis=-1, keepdims=True)
    y = x * lax.rsqrt(ms + NORM_EPS) * nw_ref[...]
    h = y * (1.0 + mod_ref[0, 1:2, :]) + mod_ref[0, 0:1, :]
    hb = h.astype(BF16)
    mla_ref[0] = _dot(hb, w_ref[:, 0:MLA_COLS])
    slab_ref[0] = _dot(hb, w_ref[:, MLA_COLS:MLA_COLS + RWKV_SLAB])
    gates_ref[0] = _dot(hb, w_ref[:, MLA_COLS + RWKV_SLAB:]).astype(BF16)


def _inproj(x, mod3, norm_w, w_in_p, tm):
    bsz, s, d = x.shape
    ncol = w_in_p.shape[1]
    return pl.pallas_call(
        _inproj_kernel,
        grid=(bsz, s // tm),
        in_specs=[pl.BlockSpec((1, tm, d), lambda b, i: (b, i, 0)),
                  pl.BlockSpec((1, 6, d), lambda b, i: (b, 0, 0)),
                  pl.BlockSpec((1, d), lambda b, i: (0, 0)),
                  pl.BlockSpec((d, ncol), lambda b, i: (0, 0))],
        out_specs=[pl.BlockSpec((1, tm, MLA_COLS), lambda b, i: (b, i, 0)),
                   pl.BlockSpec((1, tm, RWKV_SLAB), lambda b, i: (b, i, 0)),
                   pl.BlockSpec((1, tm, 2 * d), lambda b, i: (b, i, 0))],
        out_shape=[jax.ShapeDtypeStruct((bsz, s, MLA_COLS), F32),
                   jax.ShapeDtypeStruct((bsz, s, RWKV_SLAB), F32),
                   jax.ShapeDtypeStruct((bsz, s, 2 * d), BF16)],
        compiler_params=_params("parallel", "parallel"),
        name="inproj",
    )(x, mod3, norm_w.reshape(1, d), w_in_p)


def _mla_prep_kernel(mla_ref, pos_ref, invf_ref, qan_ref, kvn_ref, wq_ref, wkv_ref,
                     qg_ref, qgs_ref, kg_ref, kgs_ref, q_out, k_out, vt_out):
    m = mla_ref[0]
    cq = m[:, 0:Q_LORA]
    ckv = m[:, Q_LORA:Q_LORA + KV_LORA]
    kr = m[:, 384:512]
    krs = m[:, 512:640]
    ql = cq * lax.rsqrt(jnp.mean(cq * cq, axis=-1, keepdims=True) + NORM_EPS) * qan_ref[...]
    kvl = ckv * lax.rsqrt(jnp.mean(ckv * ckv, axis=-1, keepdims=True) + NORM_EPS) * kvn_ref[...]
    qall = _dot(ql.astype(BF16), wq_ref[...])
    kvall = _dot(kvl.astype(BF16), wkv_ref[...])
    ang = pos_ref[0] * invf_ref[...]
    cosf = jnp.cos(ang)
    sinf = jnp.sin(ang)
    scale = MLA_QK ** -0.5 * LOG2_E
    hw = MLA_HEADS * HEAD_PAD
    for h in range(MLA_HEADS):
        lo, hi = h * HEAD_PAD, (h + 1) * HEAD_PAD
        qh = qall[:, lo:hi]
        qs = qall[:, hw + lo:hw + hi]
        rs = lax.rsqrt(jnp.sum(qh * qh, axis=-1, keepdims=True) * (1.0 / MLA_QK) + NORM_EPS)
        qo = (qh * rs * qg_ref[...]) * cosf + (qs * rs * qgs_ref[...]) * sinf
        q_out[0, h] = (qo * scale).astype(BF16)
        kh = kvall[:, lo:hi] + kr
        rs = lax.rsqrt(jnp.sum(kh * kh, axis=-1, keepdims=True) * (1.0 / MLA_QK) + NORM_EPS)
        ko = (kh * rs * kg_ref[...]) * cosf + (krs * rs * kgs_ref[...]) * sinf
        k_out[0, h] = ko.astype(BF16)
        vt_out[0, h, 0] = kvall[:, hw + lo:hw + hi].T[0:MLA_V, :].astype(BF16)


def _mla_prep(mla_in, pos_b, invf, qan, kvn, wq, wkv, qg, qgs, kg, kgs, tm, tq):
    bsz, s, _ = mla_in.shape
    hw = MLA_HEADS * HEAD_PAD
    per = tq // tm
    row = lambda n: pl.BlockSpec((1, n), lambda b, i: (0, 0))
    out_spec = pl.BlockSpec((1, MLA_HEADS, tm, HEAD_PAD), lambda b, i: (b, 0, i, 0))
    out_sds = jax.ShapeDtypeStruct((bsz, MLA_HEADS, s, HEAD_PAD), BF16)
    vt_spec = pl.BlockSpec((1, MLA_HEADS, 1, MLA_V, tm), lambda b, i: (b, 0, i // per, 0, i % per))
    vt_sds = jax.ShapeDtypeStruct((bsz, MLA_HEADS, s // tq, MLA_V, tq), BF16)
    return pl.pallas_call(
        _mla_prep_kernel,
        grid=(bsz, s // tm),
        in_specs=[pl.BlockSpec((1, tm, MLA_COLS), lambda b, i: (b, i, 0)),
                  pl.BlockSpec((1, tm, LANES), lambda b, i: (b, i, 0)),
                  row(LANES), row(Q_LORA), row(KV_LORA),
                  pl.BlockSpec((Q_LORA, 2 * hw), lambda b, i: (0, 0)),
                  pl.BlockSpec((KV_LORA, 2 * hw), lambda b, i: (0, 0)),
                  row(LANES), row(LANES), row(LANES), row(LANES)],
        out_specs=[out_spec, out_spec, vt_spec],
        out_shape=[out_sds, out_sds, vt_sds],
        compiler_params=_params("parallel", "parallel"),
        name="mla_prep",
    )(mla_in, pos_b, invf, qan, kvn, wq, wkv, qg, qgs, kg, kgs)


def _attn_kernel(q_ref, k_ref, vt_ref, o_ref, m_ref, l_ref, acc_ref, *, tq):
    qi = pl.program_id(2)
    heads = range(q_ref.shape[1])
    qs = [q_ref[0, h] for h in heads]
    m_ref[...] = jnp.full_like(m_ref, -jnp.inf)
    l_ref[...] = jnp.zeros_like(l_ref)
    acc_ref[...] = jnp.zeros_like(acc_ref)

    def step(j, mask):
        off = pl.multiple_of(j * tq, tq)
        sts = [_dot_nt(k_ref[0, h, pl.ds(off, tq), :], qs[h]) for h in heads]
        for h in heads:
            st = sts[h] if mask is None else jnp.where(mask, sts[h], -jnp.inf)
            m = m_ref[h]
            m_new = jnp.maximum(m, jnp.max(st, axis=0, keepdims=True))
            p = jnp.exp2(st - m_new)
            alpha = jnp.exp2(m - m_new)
            m_ref[h] = m_new
            l_ref[h] = alpha * l_ref[h] + jnp.sum(p, axis=0, keepdims=True)
            acc_ref[h] = alpha * acc_ref[h] + _dot(vt_ref[0, h, j], p.astype(BF16))

    def body(j, carry):
        step(j, None)
        return carry

    lax.fori_loop(0, qi, body, 0)
    keys = lax.broadcasted_iota(jnp.int32, (tq, tq), 0)
    queries = lax.broadcasted_iota(jnp.int32, (tq, tq), 1)
    step(qi, keys <= queries)
    for h in heads[::2]:
        pair = jnp.concatenate([acc_ref[h] / l_ref[h], acc_ref[h + 1] / l_ref[h + 1]], axis=0)
        o_ref[0, :, h * MLA_V:(h + 2) * MLA_V] = pair.T.astype(BF16)


def _attention(q, k, vt, tq):
    bsz, nh, s, dh = q.shape
    dv = vt.shape[3]
    hp = ATTN_HEADS
    return pl.pallas_call(
        functools.partial(_attn_kernel, tq=tq),
        grid=(bsz, nh // hp, s // tq),
        in_specs=[pl.BlockSpec((1, hp, tq, dh), lambda b, h, i: (b, h, i, 0)),
                  pl.BlockSpec((1, hp, s, dh), lambda b, h, i: (b, h, 0, 0)),
                  pl.BlockSpec((1, hp, s // tq, dv, tq), lambda b, h, i: (b, h, 0, 0, 0))],
        out_specs=pl.BlockSpec((1, tq, hp * dv), lambda b, h, i: (b, i, h)),
        out_shape=jax.ShapeDtypeStruct((bsz, s, nh * dv), BF16),
        scratch_shapes=[pltpu.VMEM((hp, 1, tq), F32), pltpu.VMEM((hp, 1, tq), F32),
                        pltpu.VMEM((hp, dv, tq), F32)],
        compiler_params=_params("parallel", "parallel", "parallel"),
        name="attn",
    )(q, k, vt)


def _softplus(x):
    return jnp.maximum(x, 0.0) + jnp.log(1.0 + jnp.exp(-jnp.abs(x)))


def _rwkv_prep_kernel(slab_ref, mu_ref, wwa_ref, w0_ref, a0_ref, g2_ref, kk_ref, ka_ref, bd_ref,
                      r_out, lw_out, k_out, v_out, a_out, b_out, g_out, last_ref):
    i = pl.program_id(1)
    tm = slab_ref.shape[1]
    w = RWKV_WIDTH

    @pl.when(i == 0)
    def _():
        last_ref[...] = jnp.zeros_like(last_ref)

    slab = slab_ref[0]
    rolled = pltpu.roll(slab, 1, 0)
    rowi = lax.broadcasted_iota(jnp.int32, slab.shape, 0)
    prev = jnp.where(rowi == 0, last_ref[0:1, :], rolled)
    last_ref[0:1, :] = slab[tm - 1:tm, :]
    p = slab + (prev - slab) * mu_ref[...]
    r = p[:, 0:w]
    k = p[:, w:2 * w]
    v = p[:, 2 * w:3 * w]
    wa = p[:, 3 * w:3 * w + LANES]
    g_lo = p[:, 3 * w + LANES:3 * w + 2 * LANES]
    lane = lax.broadcasted_iota(jnp.int32, wa.shape, 1)
    wa = jnp.where(lane < DECAY_LORA, jnp.tanh(wa), wa)
    wa_o = _mm3(wa, wwa_ref[...])
    log_w = -_softplus(-(w0_ref[...] + wa_o[:, 0:w])) - 0.5
    lw = -jnp.exp(log_w)
    a = jax.nn.sigmoid(a0_ref[...] + wa_o[:, w:2 * w])
    g = _mm3(jax.nn.sigmoid(g_lo), g2_ref[...])
    kk = k * kk_ref[...]
    ss = _mm_exact_rhs(kk * kk, bd_ref[...])
    kk = kk / jnp.maximum(jnp.sqrt(ss), 1e-12)
    r_out[0] = r
    lw_out[0] = lw
    k_out[0] = k * (1.0 + (a - 1.0) * ka_ref[...])
    v_out[0] = v
    a_out[0] = -kk
    b_out[0] = kk * a
    g_out[0] = g


def _rwkv_prep(slab, mu, wwa, w0, a0, g2, k_k, k_a, bd, tm):
    bsz, s, _ = slab.shape
    w = RWKV_WIDTH
    row = lambda n: pl.BlockSpec((1, n), lambda b, i: (0, 0))
    full = lambda a: pl.BlockSpec(a.shape, lambda b, i: (0, 0))
    out_spec = pl.BlockSpec((1, tm, w), lambda b, i: (b, i, 0))
    out_sds = jax.ShapeDtypeStruct((bsz, s, w), F32)
    return pl.pallas_call(
        _rwkv_prep_kernel,
        grid=(bsz, s // tm),
        in_specs=[pl.BlockSpec((1, tm, RWKV_SLAB), lambda b, i: (b, i, 0)),
                  row(RWKV_SLAB), full(wwa), row(w), row(w), full(g2), row(w), row(w), full(bd)],
        out_specs=[out_spec] * 7,
        out_shape=[out_sds] * 7,
        scratch_shapes=[pltpu.VMEM((8, RWKV_SLAB), F32)],
        compiler_params=_params("parallel", "arbitrary"),
        name="rwkv_prep",
    )(slab, mu, wwa, w0, a0, g2, k_k, k_a, bd)


def _wkv_kernel(r_ref, lw_ref, k_ref, v_ref, a_ref, b_ref, y_ref, s_ref):
    c = WKV_CHUNK
    n = pl.program_id(1)

    @pl.when(n == 0)
    def _():
        s_ref[...] = jnp.zeros_like(s_ref)

    row = lax.broadcasted_iota(jnp.int32, (LANES, LANES), 0)
    col = lax.broadcasted_iota(jnp.int32, (LANES, LANES), 1)
    lower_strict = row > col
    lower_incl = row >= col
    eye = row == col
    same16 = (row >> 4) == (col >> 4)
    same32 = (row >> 5) == (col >> 5)
    ident = jnp.where(eye, 1.0, 0.0).astype(F32)
    tr = lax.broadcasted_iota(jnp.int32, (c, c), 0)
    tc = lax.broadcasted_iota(jnp.int32, (c, c), 1)
    tri = jnp.where(tr >= tc, 1.0, 0.0).astype(BF16)
    first = lax.broadcasted_iota(jnp.int32, (c, LANES), 1) < RWKV_HEAD

    def stack(x):
        return jnp.concatenate([jnp.where(first, x, 0.0), jnp.where(first, 0.0, x)], axis=0)

    def mm(x, y):
        return _dot(x.astype(BF16), y.astype(BF16))

    nb = r_ref.shape[0]
    probs = [(bi, p) for bi in range(nb) for p in range(RWKV_HEADS // 2)]
    each = lambda f, *xs: [f(*args) for args in zip(*xs)]
    load = lambda ref: [ref[bi, :, p * LANES:(p + 1) * LANES] for bi, p in probs]
    r, lw, k, v, a, b = (load(ref) for ref in (r_ref, lw_ref, k_ref, v_ref, a_ref, b_ref))

    def cumsum(x):
        hi = x.astype(BF16)
        rem = x - hi.astype(F32)
        mid = rem.astype(BF16)
        lo = (rem - mid.astype(F32)).astype(BF16)
        return _dot(tri, hi) + (_dot(tri, mid) + _dot(tri, lo))

    cum = each(cumsum, lw)
    cum_c = [x[c - 1:c, :] for x in cum]
    e_in = each(jnp.exp, cum)
    e_neg = each(lambda x: jnp.exp(-x), cum)
    e_end = each(lambda x, xc: jnp.exp(xc - x), cum, cum_c)
    a_s = each(lambda x, cu, l: stack(x * jnp.exp(cu - l)).astype(BF16), a, cum, lw)
    r_s = each(lambda x, e: stack(x * e), r, e_in)
    b_s = each(lambda x, e: stack(x * e).astype(BF16), b, e_neg)
    k_s = each(lambda x, e: stack(x * e).astype(BF16), k, e_neg)
    b_e = each(lambda x, e: stack(x * e).T.astype(BF16), b, e_end)
    k_e = each(lambda x, e: stack(x * e).T.astype(BF16), k, e_end)
    v_s = each(lambda x: stack(x).astype(BF16), v)

    tt = each(lambda x1, x2, y1, y2: _dot_nt(jnp.concatenate([x1, x2.astype(BF16)], axis=0),
                                             jnp.concatenate([y1, y2], axis=0)), a_s, r_s, b_s, k_s)
    d_ab = [jnp.where(lower_strict, x[0:LANES, 0:LANES], 0.0) for x in tt]
    e_ak = [jnp.where(lower_strict, x[0:LANES, LANES:], 0.0).astype(BF16) for x in tt]
    f_rb = [jnp.where(lower_incl, x[LANES:, 0:LANES], 0.0).astype(BF16) for x in tt]
    f_rk = [jnp.where(lower_incl, x[LANES:, LANES:], 0.0).astype(BF16) for x in tt]

    d16 = [jnp.where(same16, x, 0.0) for x in d_ab]
    d32 = [jnp.where(same32, x, 0.0) for x in d_ab]
    z = each(_dot, e_ak, v_s)
    x2 = each(mm, d16, d16)
    x4 = each(mm, x2, x2)
    x8 = each(mm, x4, x4)
    t = [ident + x for x in d16]
    t = each(lambda t_, x: t_ + mm(t_, x), t, x2)
    t = each(lambda t_, x: t_ + mm(t_, x), t, x4)
    t = each(lambda t_, x: t_ + mm(t_, x), t, x8)
    t = each(lambda t_, hi, lo: t_ + mm(mm(t_, hi - lo), t_), t, d32, d16)
    t = each(lambda t_, hi, lo: t_ + mm(mm(t_, hi - lo), t_), t, d_ab, d32)

    au = each(lambda t_, x, zz: mm(t_, jnp.concatenate([x, zz.astype(BF16)], axis=1)).astype(BF16),
              t, a_s, z)
    g = each(_dot, f_rb, au)
    y0 = each(lambda gg, f, vv: gg[:, LANES:] + _dot(f, vv), g, f_rk, v_s)
    r1 = each(lambda x, gg: x + gg[:, 0:LANES], r_s, g)
    hmat = each(_dot, b_e, au)
    m_mat = each(lambda h, xc: jnp.where(eye, jnp.exp(xc), 0.0) + h[:, 0:LANES], hmat, cum_c)
    n_mat = each(lambda h, ke, vv: h[:, LANES:] + _dot(ke, vv), hmat, k_e, v_s)

    for i, (bi, p) in enumerate(probs):
        s0 = s_ref[i]
        ys = _mm3(r1[i], s0) + y0[i]
        s_ref[i] = _mm3(m_mat[i], s0) + n_mat[i]
        y_ref[bi, :, p * LANES:(p + 1) * LANES] = ys[0:c, :] + ys[c:, :]


def _wkv(r, lw, k, v, a, b, nb):
    bsz, s, w = r.shape
    spec = pl.BlockSpec((nb, WKV_CHUNK, w), lambda bi, n: (bi, n, 0))
    return pl.pallas_call(
        _wkv_kernel,
        grid=(bsz // nb, s // WKV_CHUNK),
        in_specs=[spec] * 6,
        out_specs=spec,
        out_shape=jax.ShapeDtypeStruct((bsz, s, w), F32),
        scratch_shapes=[pltpu.VMEM((nb * RWKV_HEADS // 2, LANES, LANES), F32)],
        compiler_params=_params("parallel", "arbitrary"),
        name="wkv",
    )(r, lw, k, v, a, b)


def _post_kernel(x_ref, mod_ref, y_ref, r_ref, k_ref, v_ref, g_ref, o_ref, gates_ref,
                 lnw_ref, lnb_ref, rk_ref, bd_ref, wor_ref, woa_ref, wout_ref, nfw_ref,
                 wrh_ref, wrl_ref, br_ref,
                 x1_ref, h2_ref, route_ref, cnt_ref):
    tm = x_ref.shape[1]
    d = x_ref.shape[2]
    bd = bd_ref[...]
    inv_n = 1.0 / RWKV_HEAD
    y = y_ref[0]
    seg = lambda t: _dot(t.astype(BF16), bd)
    mu = seg(y) * inv_n
    dlt = y - mu
    var = seg(dlt * dlt) * inv_n
    yn = dlt * lax.rsqrt(var + GN_EPS) * lnw_ref[...] + lnb_ref[...]
    v = v_ref[0]
    bonus = seg(r_ref[0] * k_ref[0] * rk_ref[...]) * v
    z = (yn + bonus) * g_ref[0]
    o_b = _dot(z.astype(BF16), wor_ref[...])
    o_a = _dot(o_ref[0], woa_ref[...])
    ga = jax.nn.sigmoid(gates_ref[0, :, 0:d].astype(F32))
    gb = jax.nn.sigmoid(gates_ref[0, :, d:2 * d].astype(F32))
    merged = ga * o_a + gb * o_b
    mix = _dot(merged.astype(BF16), wout_ref[...])
    x1 = x_ref[0] + mod_ref[0, 2:3, :] * mix
    x1_ref[0] = x1
    ms = jnp.mean(x1 * x1, axis=-1, keepdims=True)
    h2 = x1 * lax.rsqrt(ms + NORM_EPS) * nfw_ref[...]
    h2 = h2 * (1.0 + mod_ref[0, 4:5, :]) + mod_ref[0, 3:4, :]
    h2_ref[0] = h2

    hh, hl = _split(h2)
    logits = _dot(hh, wrh_ref[...]) + (_dot(hh, wrl_ref[...]) + _dot(hl, wrh_ref[...])) + br_ref[...]
    lane = lax.broadcasted_iota(jnp.int32, (tm, LANES), 1)
    lanef = lane.astype(F32)
    cur = jnp.where(lane < N_EXPERTS, logits, -jnp.inf)
    vals, idxs, hots = [], [], []
    for _ in range(TOP_K):
        mx = jnp.max(cur, axis=-1, keepdims=True)
        idx = jnp.min(jnp.where(cur == mx, lanef, float(LANES)), axis=-1, keepdims=True)
        hot = lanef == idx
        cur = jnp.where(hot, -jnp.inf, cur)
        vals.append(mx)
        idxs.append(idx)
        hots.append(hot)
    exps = [jnp.exp(vv - vals[0]) for vv in vals]
    den = exps[0] + exps[1] + exps[2] + exps[3]
    sel = jnp.zeros((tm, LANES), F32)
    for hot in hots:
        sel = sel + jnp.where(hot, 1.0, 0.0)
    ri = lax.broadcasted_iota(jnp.int32, (tm, tm), 0)
    ci = lax.broadcasted_iota(jnp.int32, (tm, tm), 1)
    below = jnp.where(ri > ci, 1.0, 0.0).astype(BF16)
    before = _dot(below, sel.astype(BF16))
    counts = jnp.broadcast_to(jnp.sum(sel, axis=0, keepdims=True), (8, LANES))
    er = lax.broadcasted_iota(jnp.int32, (LANES, LANES), 0)
    ec = lax.broadcasted_iota(jnp.int32, (LANES, LANES), 1)
    lower = jnp.where(er < ec, 1.0, 0.0).astype(BF16)
    start = _dot(counts.astype(BF16), lower)[0:1, :]
    route = jnp.zeros((tm, LANES), F32)
    for j in range(TOP_K):
        pos = jnp.sum(jnp.where(hots[j], before + start, 0.0), axis=-1, keepdims=True)
        route = jnp.where(lane == j, pos, route)
        route = jnp.where(lane == TOP_K + j, exps[j] / den, route)
    route_ref[0] = route
    cnt_ref[0, 0] = counts


def _post(x, mod3, y, r, k, v, g, o, gates, lnw, lnb, rk, bd, wor, woa, wout, nfw, wrh, wrl, br, tm):
    bsz, s, d = x.shape
    w = RWKV_WIDTH
    tok = lambda n: pl.BlockSpec((1, tm, n), lambda b, i: (b, i, 0))
    row = lambda n: pl.BlockSpec((1, n), lambda b, i: (0, 0))
    full = lambda a: pl.BlockSpec(a.shape, lambda b, i: (0,) * a.ndim)
    return pl.pallas_call(
        _post_kernel,
        grid=(bsz, s // tm),
        in_specs=[tok(d), pl.BlockSpec((1, 6, d), lambda b, i: (b, 0, 0)),
                  tok(w), tok(w), tok(w), tok(w), tok(w),
                  tok(MLA_HEADS * MLA_V), tok(2 * d),
                  row(w), row(w), row(w), full(bd), full(wor), full(woa), full(wout), row(d),
                  full(wrh), full(wrl), row(LANES)],
        out_specs=[tok(d), tok(d), tok(LANES), pl.BlockSpec((1, 1, 8, LANES), lambda b, i: (b, i, 0, 0))],
        out_shape=[jax.ShapeDtypeStruct((bsz, s, d), F32),
                   jax.ShapeDtypeStruct((bsz, s, d), F32),
                   jax.ShapeDtypeStruct((bsz, s, LANES), F32),
                   jax.ShapeDtypeStruct((bsz, s // tm, 8, LANES), F32)],
        compiler_params=_params("parallel", "parallel"),
        name="post",
    )(x, mod3, y, r, k, v, g, o, gates, lnw, lnb, rk, bd, wor, woa, wout, nfw, wrh, wrl, br)


def _load_rows(ref, n):
    nc = ref.shape[0] // n
    return jnp.concatenate([ref[pl.ds(c, n, stride=nc), :] for c in range(nc)], axis=1)


def _store_rows(ref, val):
    nc = val.shape[1] // LANES
    for c in range(nc):
        ref[pl.ds(c, val.shape[0], stride=nc), :] = val[:, c * LANES:(c + 1) * LANES]


def _run_copies(tile, cnt_ref, off_ref, dst_ref, tm, make_copy):
    def per_expert(e, carry):
        idx = tile * N_EXPERTS + e
        o = off_ref[idx]
        d0 = dst_ref[idx]
        _pieces(cnt_ref[idx], tm, lambda done, size: make_copy(o + done, d0 + done, size).start())
        return carry

    lax.fori_loop(0, N_EXPERTS, per_expert, 0)


def _pieces(count, limit, fn):
    for b in range(limit.bit_length() - 1, -1, -1):
        size = 1 << b
        done = count & ~(2 * size - 1)

        @pl.when((count & size) != 0)
        def _():
            fn(done, size)


def _zero_fill(gap_ref, xs_ref, zero_ref, sem, nc):
    bm = MOE_BLOCK
    zero_ref[...] = jnp.zeros_like(zero_ref)

    def piece(dst, size):
        return pltpu.make_async_copy(zero_ref.at[pl.ds(0, size * nc), :],
                                     xs_ref.at[pl.ds(pl.multiple_of(dst * nc, nc), size * nc), :], sem)

    def sweep(act):
        def per_expert(e, carry):
            g0 = gap_ref[e]
            _pieces(gap_ref[N_EXPERTS + e], bm, lambda done, size: act(piece(g0 + done, size)))
            return carry

        def tail(blk, carry):
            act(piece(gap_ref[2 * N_EXPERTS] + blk * bm, bm))
            return carry

        lax.fori_loop(0, N_EXPERTS, per_expert, 0)
        lax.fori_loop(0, gap_ref[2 * N_EXPERTS + 1], tail, 0)

    sweep(lambda cp: cp.start())
    sweep(lambda cp: cp.wait())


def _dispatch_kernel(cnt_ref, off_ref, dst_ref, gap_ref, h_ref, route_ref, xs_ref, sorted_ref, zero_ref,
                     sem, zero_sem):
    tm = h_ref.shape[0]

    @pl.when(pl.program_id(0) == 0)
    def _():
        _zero_fill(gap_ref, xs_ref, zero_ref, zero_sem, h_ref.shape[1] // LANES)

    pos_t = route_ref[...].T
    slot = lax.broadcasted_iota(jnp.int32, (TOP_K * tm, tm), 0).astype(F32)
    perm = jnp.where(slot == pos_t[0:1, :], 1.0, 0.0)
    for j in range(1, TOP_K):
        perm = perm + jnp.where(slot == pos_t[j:j + 1, :], 1.0, 0.0)
    _store_rows(sorted_ref, _dot(perm.astype(BF16), h_ref[...].astype(BF16)))
    nc = h_ref.shape[1] // LANES

    def make_copy(src, dst, size):
        return pltpu.make_async_copy(sorted_ref.at[pl.ds(pl.multiple_of(src * nc, nc), size * nc), :],
                                     xs_ref.at[pl.ds(pl.multiple_of(dst * nc, nc), size * nc), :], sem)

    _run_copies(pl.program_id(0), cnt_ref, off_ref, dst_ref, tm, make_copy)
    pltpu.make_async_copy(sorted_ref, xs_ref.at[pl.ds(0, TOP_K * tm * nc), :], sem).wait()


def _dispatch(cnt_tab, off_tab, dst_tab, gap_tab, h2, route, n_rows, tm):
    t, d = h2.shape
    nc = d // LANES
    return pl.pallas_call(
        _dispatch_kernel,
        grid_spec=pltpu.PrefetchScalarGridSpec(
            num_scalar_prefetch=4,
            grid=(t // tm,),
            in_specs=[pl.BlockSpec((tm, d), lambda i, *_: (i, 0)),
                      pl.BlockSpec((tm, LANES), lambda i, *_: (i, 0))],
            out_specs=pl.BlockSpec(memory_space=pl.ANY),
            scratch_shapes=[pltpu.VMEM((TOP_K * tm * nc, LANES), F32), pltpu.VMEM((MOE_BLOCK * nc, LANES), F32),
                            pltpu.SemaphoreType.DMA(()), pltpu.SemaphoreType.DMA(())]),
        out_shape=jax.ShapeDtypeStruct((n_rows * nc, LANES), F32),
        compiler_params=_params("arbitrary"),
        name="dispatch",
    )(cnt_tab, off_tab, dst_tab, gap_tab, h2, route)


def _moe_kernel(blk_e_ref, valid_ref, xs_ref, wg_ref, wu_ref, bg_ref, bu_ref, wd_ref, bd_ref, ys_ref):
    del blk_e_ref
    i = pl.program_id(0)
    valid = valid_ref[i]

    @pl.when(valid > 0)
    def _():
        x = _load_rows(xs_ref, MOE_BLOCK).astype(BF16)
        gate = _dot(x, wg_ref[0]) + bg_ref[0]
        up = _dot(x, wu_ref[0]) + bu_ref[0]
        gate = jnp.minimum(gate, SWIGLU_LIMIT)
        up = jnp.clip(up, -SWIGLU_LIMIT, SWIGLU_LIMIT)
        act = (up + 1.0) * gate * jax.nn.sigmoid(SWIGLU_ALPHA * gate)
        _store_rows(ys_ref, _dot(act.astype(BF16), wd_ref[0]) + bd_ref[0])

    @pl.when(valid == 0)
    def _():
        ys_ref[...] = jnp.zeros_like(ys_ref)


def _moe(blk_e, blk_valid, xs, wg, wu, bg, bu, wd, bd):
    d, ff = wg.shape[1], wg.shape[2]
    nc = d // LANES
    n_rows = xs.shape[0] // nc
    bm = MOE_BLOCK
    wspec = lambda k, n: pl.BlockSpec((1, k, n), lambda i, be, nu: (be[i], 0, 0))
    return pl.pallas_call(
        _moe_kernel,
        grid_spec=pltpu.PrefetchScalarGridSpec(
            num_scalar_prefetch=2,
            grid=(n_rows // bm,),
            in_specs=[pl.BlockSpec((bm * nc, LANES), lambda i, be, nu: (i, 0)),
                      wspec(d, ff), wspec(d, ff), wspec(1, ff), wspec(1, ff), wspec(ff, d), wspec(1, d)],
            out_specs=pl.BlockSpec((bm * nc, LANES), lambda i, be, nu: (i, 0))),
        out_shape=jax.ShapeDtypeStruct((n_rows * nc, LANES), F32),
        compiler_params=_params("arbitrary"),
        name="moe",
    )(blk_e, blk_valid, xs, wg, wu, bg, bu, wd, bd)


def _combine_kernel(cnt_ref, off_ref, dst_ref, x1_ref, route_ref, mod_ref, ys_ref, o_ref, rows_ref, sem):
    tm = x1_ref.shape[1]
    tile = pl.program_id(0) * pl.num_programs(1) + pl.program_id(1)

    nc = x1_ref.shape[2] // LANES

    def make_copy(dst, src, size):
        return pltpu.make_async_copy(ys_ref.at[pl.ds(pl.multiple_of(src * nc, nc), size * nc), :],
                                     rows_ref.at[pl.ds(pl.multiple_of(dst * nc, nc), size * nc), :], sem)

    _run_copies(tile, cnt_ref, off_ref, dst_ref, tm, make_copy)
    route = route_ref[0]
    slot = lax.broadcasted_iota(jnp.int32, (tm, TOP_K * tm), 1).astype(F32)
    mix = jnp.where(slot == route[:, 0:1], route[:, TOP_K:TOP_K + 1], 0.0)
    for j in range(1, TOP_K):
        mix = mix + jnp.where(slot == route[:, j:j + 1], route[:, TOP_K + j:TOP_K + j + 1], 0.0)
    pltpu.make_async_copy(ys_ref.at[pl.ds(0, TOP_K * tm * nc), :], rows_ref, sem).wait()
    acc = _dot(mix.astype(BF16), _load_rows(rows_ref, TOP_K * tm).astype(BF16))
    o_ref[0] = x1_ref[0] + mod_ref[0, 5:6, :] * acc


def _combine(cnt_tab, off_tab, dst_tab, x1, route, mod3, ys, tm):
    bsz, s, d = x1.shape
    return pl.pallas_call(
        _combine_kernel,
        grid_spec=pltpu.PrefetchScalarGridSpec(
            num_scalar_prefetch=3,
            grid=(bsz, s // tm),
            in_specs=[pl.BlockSpec((1, tm, d), lambda b, i, *_: (b, i, 0)),
                      pl.BlockSpec((1, tm, LANES), lambda b, i, *_: (b, i, 0)),
                      pl.BlockSpec((1, 6, d), lambda b, i, *_: (b, 0, 0)),
                      pl.BlockSpec(memory_space=pl.ANY)],
            out_specs=pl.BlockSpec((1, tm, d), lambda b, i, *_: (b, i, 0)),
            scratch_shapes=[pltpu.VMEM((TOP_K * tm * d // LANES, LANES), F32), pltpu.SemaphoreType.DMA(())]),
        out_shape=jax.ShapeDtypeStruct((bsz, s, d), F32),
        compiler_params=_params("arbitrary", "arbitrary"),
        name="combine",
    )(cnt_tab, off_tab, dst_tab, x1, route, mod3, ys)


def _pad_cols(a, n):
    return jnp.pad(a, ((0, 0), (0, n - a.shape[1])))


def _head_blocks(cols_main, cols_rot=None):
    k = cols_main.shape[0]
    out = jnp.zeros((k, MLA_HEADS, HEAD_PAD), F32)
    out = out.at[:, :, :cols_main.shape[2]].set(cols_main)
    return out.reshape(k, MLA_HEADS * HEAD_PAD)


def _layer(x, cond_mod, positions, w_in, q_a_norm_w, w_q_up, kv_a_norm_w, w_kv_up, q_norm_w, k_norm_w,
           w_o_mla, rwkv_mu, rwkv_w0, rwkv_w2, rwkv_a0, rwkv_a2, rwkv_g2, rwkv_k_k, rwkv_k_a, rwkv_r_k,
           rwkv_ln_w, rwkv_ln_b, rwkv_w_o, w_out, norm_mix_w, norm_ffn_w, w_router, b_router,
           w_gate_up, b_gate_up, w_down, b_down):
    bsz, s, d = x.shape
    t = bsz * s
    half = MLA_ROPE // 2
    nope, qk = MLA_NOPE, MLA_QK
    mod3 = cond_mod.reshape(bsz, 6, d)

    o_q, o_kv, o_kr = 0, Q_LORA, Q_LORA + KV_LORA
    o_slab = o_kr + MLA_ROPE
    o_gate = o_slab + RWKV_SLAB
    kr_w = w_in[:, o_kr:o_slab]
    zeros = lambda n: jnp.zeros((d, n), F32)
    kr_blk = jnp.concatenate([zeros(nope), kr_w, zeros(HEAD_PAD - qk)], axis=1)
    kr_rot = jnp.concatenate([zeros(nope), -kr_w[:, half:], kr_w[:, :half], zeros(HEAD_PAD - qk)], axis=1)
    w_in_p = jnp.concatenate([w_in[:, o_q:o_kr], kr_blk, kr_rot, w_in[:, o_slab:]], axis=1).astype(BF16)

    tm = min(256, s)
    mla_in, slab, gates = _inproj(x, mod3, norm_mix_w, w_in_p, tm)

    wq3 = w_q_up.reshape(Q_LORA, MLA_HEADS, qk)
    wq_rot = jnp.concatenate([jnp.zeros((Q_LORA, MLA_HEADS, nope), F32), -wq3[:, :, nope + half:],
                              wq3[:, :, nope:nope + half]], axis=2)
    wq = jnp.concatenate([_head_blocks(wq3), _head_blocks(wq_rot)], axis=1).astype(BF16)
    wkv3 = w_kv_up.reshape(KV_LORA, MLA_HEADS, nope + MLA_V)
    wkv = jnp.concatenate([_head_blocks(wkv3[:, :, :nope]), _head_blocks(wkv3[:, :, nope:])], axis=1).astype(BF16)

    def gains(wn):
        main = jnp.pad(wn, (0, HEAD_PAD - qk)).reshape(1, HEAD_PAD)
        rot = jnp.concatenate([jnp.zeros((nope,), F32), wn[nope + half:], wn[nope:nope + half],
                               jnp.zeros((HEAD_PAD - qk,), F32)]).reshape(1, HEAD_PAD)
        return main, rot

    qg, qgs = gains(q_norm_w)
    kg, kgs = gains(k_norm_w)
    inv_freq = ROPE_THETA ** (-jnp.arange(half, dtype=F32) / half)
    invf = jnp.concatenate([jnp.zeros((nope,), F32), inv_freq, inv_freq,
                            jnp.zeros((HEAD_PAD - qk,), F32)]).reshape(1, HEAD_PAD)
    pos_b = jnp.broadcast_to(positions.astype(F32)[:, :, None], (bsz, s, LANES))
    tq = min(ATTN_BLOCK, s)
    q, k, vt = _mla_prep(mla_in, pos_b, invf, q_a_norm_w.reshape(1, -1), kv_a_norm_w.reshape(1, -1),
                         wq, wkv, qg, qgs, kg, kgs, tm, tq)
    o = _attention(q, k, vt, tq)

    w = RWKV_WIDTH
    wwa = jnp.zeros((LANES, 2 * w), F32)
    wwa = wwa.at[:DECAY_LORA, :w].set(rwkv_w2).at[DECAY_LORA:, w:].set(rwkv_a2)
    hid = np.arange(w) // RWKV_HEAD
    bd = jnp.asarray(hid[:, None] == hid[None, :], BF16)
    r_, lw_, k_, v_, a_, b_, g_ = _rwkv_prep(
        slab, rwkv_mu.reshape(1, -1), wwa, rwkv_w0.reshape(1, -1), rwkv_a0.reshape(1, -1), rwkv_g2,
        rwkv_k_k.reshape(1, -1), rwkv_k_a.reshape(1, -1), bd, tm)
    y = _wkv(r_, lw_, k_, v_, a_, b_, WKV_BATCH if bsz % WKV_BATCH == 0 else 1)

    woa = w_o_mla.astype(BF16)
    wr =_pad_cols(w_router, LANES)
    wrh = wr.astype(BF16)
    wrl = (wr - wrh.astype(F32)).astype(BF16)
    br = jnp.pad(b_router, (0, LANES - N_EXPERTS)).reshape(1, LANES)
    x1, h2, route, counts = _post(
        x, mod3, y, r_, k_, v_, g_, o, gates, rwkv_ln_w.reshape(1, -1), rwkv_ln_b.reshape(1, -1),
        rwkv_r_k.reshape(1, -1), bd, rwkv_w_o.astype(BF16), woa, w_out.astype(BF16),
        norm_ffn_w.reshape(1, -1), wrh, wrl, br, tm)

    bm = MOE_BLOCK
    n_rows = t * TOP_K + N_EXPERTS * bm
    n_blocks = n_rows // bm
    cnt_tab = counts[:, :, 0, :N_EXPERTS].reshape(t // tm, N_EXPERTS).astype(jnp.int32)
    total = jnp.sum(cnt_tab, axis=0)
    padded = (total + bm - 1) // bm * bm
    pad_end = jnp.cumsum(padded)
    pad_start = pad_end - padded
    off_tab = jnp.cumsum(cnt_tab, axis=1) - cnt_tab
    dst_tab = pad_start[None, :] + jnp.cumsum(cnt_tab, axis=0) - cnt_tab
    blk_start = jnp.arange(n_blocks, dtype=jnp.int32) * bm
    blk_e = jnp.minimum(jnp.sum((pad_end[None, :] <= blk_start[:, None]).astype(jnp.int32), axis=1),
                        N_EXPERTS - 1)
    blk_valid = jnp.clip((pad_start + total)[blk_e] - blk_start, 0, bm).astype(jnp.int32)
    tabs = (cnt_tab.reshape(-1), off_tab.reshape(-1).astype(jnp.int32), dst_tab.reshape(-1).astype(jnp.int32))
    gap_tab = jnp.concatenate([pad_start + total, padded - total, pad_end[-1:],
                               (n_rows - pad_end[-1:]) // bm]).astype(jnp.int32)

    xs = _dispatch(*tabs, gap_tab, h2.reshape(t, d), route.reshape(t, LANES), n_rows, tm)
    wgu = w_gate_up.reshape(N_EXPERTS, d, D_FF, 2)
    bgu = b_gate_up.reshape(N_EXPERTS, 1, D_FF, 2)
    ys = _moe(blk_e, blk_valid, xs, wgu[..., 0].astype(BF16), wgu[..., 1].astype(BF16),
              bgu[..., 0], bgu[..., 1], w_down.astype(BF16), b_down.reshape(N_EXPERTS, 1, d))
    return _combine(*tabs, x1, route, mod3, ys, tm)


def kernel(x, c, positions, ada_w, ada_b, norm_mix_w, norm_ffn_w, w_in, q_a_norm_w, w_q_up, kv_a_norm_w, w_kv_up, q_norm_w, k_norm_w, w_o_mla, rwkv_mu, rwkv_w0, rwkv_w2, rwkv_a0, rwkv_a2, rwkv_g2, rwkv_k_k, rwkv_k_a, rwkv_r_k, rwkv_ln_w, rwkv_ln_b, rwkv_w_o, w_out, w_router, b_router, w_gate_up, b_gate_up, w_down, b_down):
    depth = ada_w.shape[0]
    for l in range(depth):
        mod = _ada(c, ada_w[l], ada_b[l])
        x = _layer(x, mod, positions, w_in[l], q_a_norm_w[l], w_q_up[l], kv_a_norm_w[l], w_kv_up[l],
                   q_norm_w[l], k_norm_w[l], w_o_mla[l], rwkv_mu[l], rwkv_w0[l], rwkv_w2[l], rwkv_a0[l],
                   rwkv_a2[l], rwkv_g2[l], rwkv_k_k[l], rwkv_k_a[l], rwkv_r_k[l], rwkv_ln_w[l],
                   rwkv_ln_b[l], rwkv_w_o[l], w_out[l], norm_mix_w[l], norm_ffn_w[l], w_router[l],
                   b_router[l], w_gate_up[l], b_gate_up[l], w_down[l], b_down[l])
    return x
```

```python
import functools

import numpy as np
import jax
import jax.numpy as jnp
from jax import lax
from jax.experimental import pallas as pl
from jax.experimental.pallas import tpu as pltpu

F32 = jnp.float32
BF16 = jnp.bfloat16

D_MODEL = 1024
MLA_HEADS = 8
MLA_NOPE = 64
MLA_ROPE = 32
MLA_QK = MLA_NOPE + MLA_ROPE
MLA_V = 64
Q_LORA = 256
KV_LORA = 128
ROPE_THETA = 10000.0
RWKV_HEADS = 8
RWKV_HEAD = 64
RWKV_WIDTH = RWKV_HEADS * RWKV_HEAD
DECAY_LORA = 64
AAA_LORA = 64
GATE_LORA = 128
RWKV_SLAB = 3 * RWKV_WIDTH + DECAY_LORA + AAA_LORA + GATE_LORA
GN_EPS = 64e-5
N_EXPERTS = 32
TOP_K = 4
D_FF = D_MODEL
SWIGLU_LIMIT = 7.0
SWIGLU_ALPHA = 1.702
NORM_EPS = 1e-6
LOG2_E = 1.4426950408889634

LANES = 128
HEAD_PAD = 128
MLA_COLS = 640
WKV_CHUNK = 64
WKV_BATCH = 4
ATTN_BLOCK = 512
ATTN_HEADS = 4
ROUTE_TILE = 256
POST_TILE = 512
RARE_PIECE_BIT = 6
MOE_BLOCK = 512
VMEM_LIMIT = 56 * 1024 * 1024


def _dot(a, b):
    return jnp.dot(a, b, preferred_element_type=F32)


def _dot_nt(a, b):
    return lax.dot_general(a, b, (((1,), (1,)), ((), ())), preferred_element_type=F32)


def _split(x):
    hi = x.astype(BF16)
    lo = (x - hi.astype(F32)).astype(BF16)
    return hi, lo


def _mm3(a, b):
    ah, al = _split(a)
    bh, bl = _split(b)
    return _dot(ah, bh) + (_dot(ah, bl) + _dot(al, bh))


def _mm3_nt(a, b):
    ah, al = _split(a)
    bh, bl = _split(b)
    return _dot_nt(ah, bh) + (_dot_nt(ah, bl) + _dot_nt(al, bh))


def _mm_exact_rhs(a, b_bf16):
    ah, al = _split(a)
    return _dot(ah, b_bf16) + _dot(al, b_bf16)


def _params(*sem):
    return pltpu.CompilerParams(dimension_semantics=sem, vmem_limit_bytes=VMEM_LIMIT)


def _ada_kernel(c_ref, w_ref, b_ref, o_ref):
    c = c_ref[...]
    cond = c * jax.nn.sigmoid(c)
    o_ref[...] = _mm3(cond, w_ref[...]) + b_ref[...]


def _ada(c, w, b):
    bsz, d = c.shape
    n = w.shape[1]
    tn = 1024
    return pl.pallas_call(
        _ada_kernel,
        grid=(n // tn,),
        in_specs=[pl.BlockSpec((bsz, d), lambda j: (0, 0)),
                  pl.BlockSpec((d, tn), lambda j: (0, j)),
                  pl.BlockSpec((1, tn), lambda j: (0, j))],
        out_specs=pl.BlockSpec((bsz, tn), lambda j: (0, j)),
        out_shape=jax.ShapeDtypeStruct((bsz, n), F32),
        compiler_params=_params("parallel"),
        name="ada",
    )(c, w, b.reshape(1, n))


def _inproj_kernel(x_ref, mod_ref, nw_ref, w_ref, mla_ref, slab_ref, gates_ref):
    x = x_ref[0]
    ms = jnp.mean(x * x, axis=-1, keepdims=True)
    y = x * lax.rsqrt(ms + NORM_EPS) * nw_ref[...]
    h = y * (1.0 + mod_ref[0, 1:2, :]) + mod_ref[0, 0:1, :]
    hb = h.astype(BF16)
    mla_ref[0] = _dot(hb, w_ref[:, 0:MLA_COLS])
    slab_ref[0] = _dot(hb, w_ref[:, MLA_COLS:MLA_COLS + RWKV_SLAB])
    gates_ref[0] = _dot(hb, w_ref[:, MLA_COLS + RWKV_SLAB:]).astype(BF16)


def _inproj(x, mod3, norm_w, w_in_p, tm):
    bsz, s, d = x.shape
    ncol = w_in_p.shape[1]
    return pl.pallas_call(
        _inproj_kernel,
        grid=(bsz, s // tm),
        in_specs=[pl.BlockSpec((1, tm, d), lambda b, i: (b, i, 0)),
                  pl.BlockSpec((1, 6, d), lambda b, i: (b, 0, 0)),
                  pl.BlockSpec((1, d), lambda b, i: (0, 0)),
                  pl.BlockSpec((d, ncol), lambda b, i: (0, 0))],
        out_specs=[pl.BlockSpec((1, tm, MLA_COLS), lambda b, i: (b, i, 0)),
                   pl.BlockSpec((1, tm, RWKV_SLAB), lambda b, i: (b, i, 0)),
                   pl.BlockSpec((1, tm, 2 * d), lambda b, i: (b, i, 0))],
        out_shape=[jax.ShapeDtypeStruct((bsz, s, MLA_COLS), F32),
                   jax.ShapeDtypeStruct((bsz, s, RWKV_SLAB), F32),
                   jax.ShapeDtypeStruct((bsz, s, 2 * d), BF16)],
        compiler_params=_params("parallel", "parallel"),
        name="inproj",
    )(x, mod3, norm_w.reshape(1, d), w_in_p)


def _rope_kernel(pos_ref, invf_ref, cos_ref, sin_ref):
    ang = pos_ref[...] * invf_ref[...]
    cos_ref[...] = jnp.cos(ang)
    sin_ref[...] = jnp.sin(ang)


def _rope_table(pos_rep, invf_row):
    n = pos_rep.shape[0]
    tr = min(512, n)
    spec = pl.BlockSpec((tr, LANES), lambda i: (i, 0))
    sds = jax.ShapeDtypeStruct((n, LANES), F32)
    return pl.pallas_call(
        _rope_kernel,
        grid=(n // tr,),
        in_specs=[spec, pl.BlockSpec((1, LANES), lambda i: (0, 0))],
        out_specs=[spec, spec],
        out_shape=[sds, sds],
        compiler_params=_params("parallel"),
        name="rope",
    )(pos_rep, invf_row)


def _mla_prep_kernel(mla_ref, cos_ref, sin_ref, qan_ref, kvn_ref, wq_ref, wkv_ref,
                     qg_ref, qgs_ref, kg_ref, kgs_ref, q_out, k_out, vt_out):
    m = mla_ref[0]
    cq = m[:, 0:Q_LORA]
    ckv = m[:, Q_LORA:Q_LORA + KV_LORA]
    kr = m[:, 384:512]
    krs = m[:, 512:640]
    ql = cq * lax.rsqrt(jnp.mean(cq * cq, axis=-1, keepdims=True) + NORM_EPS) * qan_ref[...]
    kvl = ckv * lax.rsqrt(jnp.mean(ckv * ckv, axis=-1, keepdims=True) + NORM_EPS) * kvn_ref[...]
    qall = _dot(ql.astype(BF16), wq_ref[...])
    kvall = _dot(kvl.astype(BF16), wkv_ref[...])
    cosf = cos_ref[0]
    sinf = sin_ref[0]
    scale = MLA_QK ** -0.5 * LOG2_E
    hw = MLA_HEADS * HEAD_PAD
    for h in range(MLA_HEADS):
        lo, hi = h * HEAD_PAD, (h + 1) * HEAD_PAD
        qh = qall[:, lo:hi]
        qs = qall[:, hw + lo:hw + hi]
        rs = lax.rsqrt(jnp.sum(qh * qh, axis=-1, keepdims=True) * (1.0 / MLA_QK) + NORM_EPS)
        qo = (qh * rs * qg_ref[...]) * cosf + (qs * rs * qgs_ref[...]) * sinf
        q_out[0, h] = (qo * scale).astype(BF16)
        kh = kvall[:, lo:hi] + kr
        rs = lax.rsqrt(jnp.sum(kh * kh, axis=-1, keepdims=True) * (1.0 / MLA_QK) + NORM_EPS)
        ko = (kh * rs * kg_ref[...]) * cosf + (krs * rs * kgs_ref[...]) * sinf
        k_out[0, h] = ko.astype(BF16)
        vt_out[0, h, 0] = kvall[:, hw + lo:hw + hi].T[0:MLA_V, :].astype(BF16)


def _mla_prep(mla_in, cosf, sinf, qan, kvn, wq, wkv, qg, qgs, kg, kgs, tm, tq):
    bsz, s, _ = mla_in.shape
    hw = MLA_HEADS * HEAD_PAD
    per = tq // tm
    row = lambda n: pl.BlockSpec((1, n), lambda b, i: (0, 0))
    out_spec = pl.BlockSpec((1, MLA_HEADS, tm, HEAD_PAD), lambda b, i: (b, 0, i, 0))
    out_sds = jax.ShapeDtypeStruct((bsz, MLA_HEADS, s, HEAD_PAD), BF16)
    vt_spec = pl.BlockSpec((1, MLA_HEADS, 1, MLA_V, tm), lambda b, i: (b, 0, i // per, 0, i % per))
    vt_sds = jax.ShapeDtypeStruct((bsz, MLA_HEADS, s // tq, MLA_V, tq), BF16)
    return pl.pallas_call(
        _mla_prep_kernel,
        grid=(bsz, s // tm),
        in_specs=[pl.BlockSpec((1, tm, MLA_COLS), lambda b, i: (b, i, 0)),
                  pl.BlockSpec((1, tm, LANES), lambda b, i: (b, i, 0)),
                  pl.BlockSpec((1, tm, LANES), lambda b, i: (b, i, 0)),
                  row(Q_LORA), row(KV_LORA),
                  pl.BlockSpec((Q_LORA, 2 * hw), lambda b, i: (0, 0)),
                  pl.BlockSpec((KV_LORA, 2 * hw), lambda b, i: (0, 0)),
                  row(LANES), row(LANES), row(LANES), row(LANES)],
        out_specs=[out_spec, out_spec, vt_spec],
        out_shape=[out_sds, out_sds, vt_sds],
        compiler_params=_params("parallel", "parallel"),
        name="mla_prep",
    )(mla_in, cosf, sinf, qan, kvn, wq, wkv, qg, qgs, kg, kgs)


def _attn_kernel(q_ref, k_ref, vt_ref, o_ref, m_ref, l_ref, acc_ref, *, tq):
    qi = pl.program_id(2)
    heads = range(q_ref.shape[1])
    qs = [q_ref[0, h] for h in heads]
    m_ref[...] = jnp.full_like(m_ref, -jnp.inf)
    l_ref[...] = jnp.zeros_like(l_ref)
    acc_ref[...] = jnp.zeros_like(acc_ref)

    def step(j, mask):
        off = pl.multiple_of(j * tq, tq)
        sts = [_dot_nt(k_ref[0, h, pl.ds(off, tq), :], qs[h]) for h in heads]
        for h in heads:
            st = sts[h] if mask is None else jnp.where(mask, sts[h], -jnp.inf)
            m = m_ref[h]
            m_new = jnp.maximum(m, jnp.max(st, axis=0, keepdims=True))
            p = jnp.exp2(st - m_new)
            alpha = jnp.exp2(m - m_new)
            m_ref[h] = m_new
            l_ref[h] = alpha * l_ref[h] + jnp.sum(p, axis=0, keepdims=True)
            acc_ref[h] = alpha * acc_ref[h] + _dot(vt_ref[0, h, j], p.astype(BF16))

    def body(j, carry):
        step(j, None)
        return carry

    lax.fori_loop(0, qi, body, 0)
    keys = lax.broadcasted_iota(jnp.int32, (tq, tq), 0)
    queries = lax.broadcasted_iota(jnp.int32, (tq, tq), 1)
    step(qi, keys <= queries)
    for h in heads[::2]:
        pair = jnp.concatenate([acc_ref[h] / l_ref[h], acc_ref[h + 1] / l_ref[h + 1]], axis=0)
        o_ref[0, :, h * MLA_V:(h + 2) * MLA_V] = pair.T.astype(BF16)


def _attention(q, k, vt, tq):
    bsz, nh, s, dh = q.shape
    dv = vt.shape[3]
    hp = ATTN_HEADS
    return pl.pallas_call(
        functools.partial(_attn_kernel, tq=tq),
        grid=(bsz, nh // hp, s // tq),
        in_specs=[pl.BlockSpec((1, hp, tq, dh), lambda b, h, i: (b, h, i, 0)),
                  pl.BlockSpec((1, hp, s, dh), lambda b, h, i: (b, h, 0, 0)),
                  pl.BlockSpec((1, hp, s // tq, dv, tq), lambda b, h, i: (b, h, 0, 0, 0))],
        out_specs=pl.BlockSpec((1, tq, hp * dv), lambda b, h, i: (b, i, h)),
        out_shape=jax.ShapeDtypeStruct((bsz, s, nh * dv), BF16),
        scratch_shapes=[pltpu.VMEM((hp, 1, tq), F32), pltpu.VMEM((hp, 1, tq), F32),
                        pltpu.VMEM((hp, dv, tq), F32)],
        compiler_params=_params("parallel", "parallel", "parallel"),
        name="attn",
    )(q, k, vt)


def _softplus(x):
    return jnp.maximum(x, 0.0) + jnp.log(1.0 + jnp.exp(-jnp.abs(x)))


def _rwkv_prep_kernel(slab_ref, mu_ref, wwh_ref, wwl_ref, w0_ref, a0_ref, g2_ref, kk_ref, ka_ref, bd_ref,
                      r_out, lw_out, k_out, v_out, a_out, b_out, g_out, last_ref):
    i = pl.program_id(1)
    tm = slab_ref.shape[1]
    w = RWKV_WIDTH

    @pl.when(i == 0)
    def _():
        last_ref[...] = jnp.zeros_like(last_ref)

    slab = slab_ref[0]
    rolled = pltpu.roll(slab, 1, 0)
    rowi = lax.broadcasted_iota(jnp.int32, slab.shape, 0)
    prev = jnp.where(rowi == 0, last_ref[0:1, :], rolled)
    last_ref[0:1, :] = slab[tm - 1:tm, :]
    p = slab + (prev - slab) * mu_ref[...]
    r = p[:, 0:w]
    k = p[:, w:2 * w]
    v = p[:, 2 * w:3 * w]
    wa = p[:, 3 * w:3 * w + LANES]
    g_lo = p[:, 3 * w + LANES:3 * w + 2 * LANES]
    lane = lax.broadcasted_iota(jnp.int32, wa.shape, 1)
    wa = jnp.where(lane < DECAY_LORA, jnp.tanh(wa), wa)
    wah, wal = _split(wa)
    wa_o = _dot(wah, wwh_ref[...]) + (_dot(wah, wwl_ref[...]) + _dot(wal, wwh_ref[...]))
    log_w = -_softplus(-(w0_ref[...] + wa_o[:, 0:w])) - 0.5
    lw = -jnp.exp(log_w)
    a = jax.nn.sigmoid(a0_ref[...] + wa_o[:, w:2 * w])
    g = _dot(jax.nn.sigmoid(g_lo).astype(BF16), g2_ref[...])
    kk = k * kk_ref[...]
    ss = _mm_exact_rhs(kk * kk, bd_ref[...])
    kk = kk / jnp.maximum(jnp.sqrt(ss), 1e-12)
    r_out[0] = r
    lw_out[0] = lw
    k_out[0] = k * (1.0 + (a - 1.0) * ka_ref[...])
    v_out[0] = v
    a_out[0] = -kk
    b_out[0] = kk * a
    g_out[0] = g


def _rwkv_prep(slab, mu, wwh, wwl, w0, a0, g2, k_k, k_a, bd, tm):
    bsz, s, _ = slab.shape
    w = RWKV_WIDTH
    row = lambda n: pl.BlockSpec((1, n), lambda b, i: (0, 0))
    full = lambda a: pl.BlockSpec(a.shape, lambda b, i: (0, 0))
    out_spec = pl.BlockSpec((1, tm, w), lambda b, i: (b, i, 0))
    out_sds = jax.ShapeDtypeStruct((bsz, s, w), F32)
    return pl.pallas_call(
        _rwkv_prep_kernel,
        grid=(bsz, s // tm),
        in_specs=[pl.BlockSpec((1, tm, RWKV_SLAB), lambda b, i: (b, i, 0)),
                  row(RWKV_SLAB), full(wwh), full(wwl), row(w), row(w), full(g2), row(w), row(w), full(bd)],
        out_specs=[out_spec] * 7,
        out_shape=[out_sds] * 7,
        scratch_shapes=[pltpu.VMEM((8, RWKV_SLAB), F32)],
        compiler_params=_params("parallel", "arbitrary"),
        name="rwkv_prep",
    )(slab, mu, wwh, wwl, w0, a0, g2, k_k, k_a, bd)


def _wkv_kernel(r_ref, lw_ref, k_ref, v_ref, a_ref, b_ref, y_ref, s_ref):
    c = WKV_CHUNK
    n = pl.program_id(1)

    @pl.when(n == 0)
    def _():
        s_ref[...] = jnp.zeros_like(s_ref)

    row = lax.broadcasted_iota(jnp.int32, (LANES, LANES), 0)
    col = lax.broadcasted_iota(jnp.int32, (LANES, LANES), 1)
    lower_strict = row > col
    lower_incl = row >= col
    eye = row == col
    same16 = (row >> 4) == (col >> 4)
    same32 = (row >> 5) == (col >> 5)
    ident = jnp.where(eye, 1.0, 0.0).astype(F32)
    tr = lax.broadcasted_iota(jnp.int32, (c, c), 0)
    tc = lax.broadcasted_iota(jnp.int32, (c, c), 1)
    tri = jnp.where(tr >= tc, 1.0, 0.0).astype(BF16)
    first = lax.broadcasted_iota(jnp.int32, (c, LANES), 1) < RWKV_HEAD

    def stack(x):
        return jnp.concatenate([jnp.where(first, x, 0.0), jnp.where(first, 0.0, x)], axis=0)

    def mm(x, y):
        return _dot(x.astype(BF16), y.astype(BF16))

    nb = r_ref.shape[0]
    probs = [(bi, p) for bi in range(nb) for p in range(RWKV_HEADS // 2)]
    each = lambda f, *xs: [f(*args) for args in zip(*xs)]
    load = lambda ref: [ref[bi, :, p * LANES:(p + 1) * LANES] for bi, p in probs]
    r, lw, k, v, a, b = (load(ref) for ref in (r_ref, lw_ref, k_ref, v_ref, a_ref, b_ref))

    def cumsum(x):
        hi = x.astype(BF16)
        rem = x - hi.astype(F32)
        mid = rem.astype(BF16)
        lo = (rem - mid.astype(F32)).astype(BF16)
        return _dot(tri, hi) + (_dot(tri, mid) + _dot(tri, lo))

    cum = each(cumsum, lw)
    cum_c = [x[c - 1:c, :] for x in cum]
    e_in = each(jnp.exp, cum)
    e_neg = each(lambda x: jnp.exp(-x), cum)
    e_end = each(lambda x, xc: jnp.exp(xc - x), cum, cum_c)
    a_s = each(lambda x, cu, l: stack(x * jnp.exp(cu - l)).astype(BF16), a, cum, lw)
    r_s = each(lambda x, e: stack(x * e), r, e_in)
    b_s = each(lambda x, e: stack(x * e).astype(BF16), b, e_neg)
    k_s = each(lambda x, e: stack(x * e).astype(BF16), k, e_neg)
    b_e = each(lambda x, e: stack(x * e).T.astype(BF16), b, e_end)
    k_e = each(lambda x, e: stack(x * e).T.astype(BF16), k, e_end)
    v_s = each(lambda x: stack(x).astype(BF16), v)

    tt = each(lambda x1, x2, y1, y2: _dot_nt(jnp.concatenate([x1, x2.astype(BF16)], axis=0),
                                             jnp.concatenate([y1, y2], axis=0)), a_s, r_s, b_s, k_s)
    d_ab = [jnp.where(lower_strict, x[0:LANES, 0:LANES], 0.0) for x in tt]
    e_ak = [jnp.where(lower_strict, x[0:LANES, LANES:], 0.0).astype(BF16) for x in tt]
    f_rb = [jnp.where(lower_incl, x[LANES:, 0:LANES], 0.0).astype(BF16) for x in tt]
    f_rk = [jnp.where(lower_incl, x[LANES:, LANES:], 0.0).astype(BF16) for x in tt]

    d16 = [jnp.where(same16, x, 0.0) for x in d_ab]
    d32 = [jnp.where(same32, x, 0.0) for x in d_ab]
    z = each(_dot, e_ak, v_s)
    x2 = each(mm, d16, d16)
    x4 = each(mm, x2, x2)
    x8 = each(mm, x4, x4)
    t = [ident + x for x in d16]
    t = each(lambda t_, x: t_ + mm(t_, x), t, x2)
    t = each(lambda t_, x: t_ + mm(t_, x), t, x4)
    t = each(lambda t_, x: t_ + mm(t_, x), t, x8)
    t = each(lambda t_, hi, lo: t_ + mm(mm(t_, hi - lo), t_), t, d32, d16)
    t = each(lambda t_, hi, lo: t_ + mm(mm(t_, hi - lo), t_), t, d_ab, d32)

    au = each(lambda t_, x, zz: mm(t_, jnp.concatenate([x, zz.astype(BF16)], axis=1)).astype(BF16),
              t, a_s, z)
    g = each(_dot, f_rb, au)
    y0 = each(lambda gg, f, vv: gg[:, LANES:] + _dot(f, vv), g, f_rk, v_s)
    r1 = each(lambda x, gg: x + gg[:, 0:LANES], r_s, g)
    hmat = each(_dot, b_e, au)
    m_mat = each(lambda h, xc: jnp.where(eye, jnp.exp(xc), 0.0) + h[:, 0:LANES], hmat, cum_c)
    n_mat = each(lambda h, ke, vv: h[:, LANES:] + _dot(ke, vv), hmat, k_e, v_s)

    for i, (bi, p) in enumerate(probs):
        s0 = s_ref[i]
        s0b = s0.astype(BF16)
        ys = _dot(r1[i].astype(BF16), s0b) + y0[i]
        s_ref[i] = _dot(m_mat[i].astype(BF16), s0b) + n_mat[i]
        y_ref[bi, :, p * LANES:(p + 1) * LANES] = ys[0:c, :] + ys[c:, :]


def _wkv(r, lw, k, v, a, b, nb):
    bsz, s, w = r.shape
    spec = pl.BlockSpec((nb, WKV_CHUNK, w), lambda bi, n: (bi, n, 0))
    return pl.pallas_call(
        _wkv_kernel,
        grid=(bsz // nb, s // WKV_CHUNK),
        in_specs=[spec] * 6,
        out_specs=spec,
        out_shape=jax.ShapeDtypeStruct((bsz, s, w), F32),
        scratch_shapes=[pltpu.VMEM((nb * RWKV_HEADS // 2, LANES, LANES), F32)],
        compiler_params=_params("parallel", "arbitrary"),
        name="wkv",
    )(r, lw, k, v, a, b)


def _post_kernel(x_ref, mod_ref, y_ref, r_ref, k_ref, v_ref, g_ref, o_ref, gates_ref,
                 lnw_ref, lnb_ref, rk_ref, bd_ref, wor_ref, woa_ref, wout_ref, nfw_ref,
                 wrh_ref, wrl_ref, br_ref,
                 x1_ref, h2_ref, route_ref, cnt_ref):
    tiles = [_post_tile(part, x_ref, mod_ref, y_ref, r_ref, k_ref, v_ref, g_ref, o_ref, gates_ref,
                        lnw_ref, lnb_ref, rk_ref, bd_ref, wor_ref, woa_ref, wout_ref, nfw_ref,
                        wrh_ref, wrl_ref, br_ref, x1_ref, h2_ref, route_ref, cnt_ref)
             for part in range(x_ref.shape[1] // ROUTE_TILE)]
    for _ in zip(*tiles):
        pass


def _post_tile(part, x_ref, mod_ref, y_ref, r_ref, k_ref, v_ref, g_ref, o_ref, gates_ref,
               lnw_ref, lnb_ref, rk_ref, bd_ref, wor_ref, woa_ref, wout_ref, nfw_ref,
               wrh_ref, wrl_ref, br_ref, x1_ref, h2_ref, route_ref, cnt_ref):
    tm = ROUTE_TILE
    rows = slice(part * tm, (part + 1) * tm)
    d = x_ref.shape[2]
    bd = bd_ref[...]
    inv_n = 1.0 / RWKV_HEAD
    y = y_ref[0, rows, :]
    seg = lambda t: _dot(t.astype(BF16), bd)
    mu = seg(y) * inv_n
    dlt = y - mu
    var = seg(dlt * dlt) * inv_n
    yn = dlt * lax.rsqrt(var + GN_EPS) * lnw_ref[...] + lnb_ref[...]
    yield
    v = v_ref[0, rows, :]
    bonus = seg(r_ref[0, rows, :] * k_ref[0, rows, :] * rk_ref[...]) * v
    z = (yn + bonus) * g_ref[0, rows, :]
    o_b = _dot(z.astype(BF16), wor_ref[...])
    o_a = _dot(o_ref[0, rows, :], woa_ref[...])
    yield
    ga = jax.nn.sigmoid(gates_ref[0, rows, 0:d].astype(F32))
    gb = jax.nn.sigmoid(gates_ref[0, rows, d:2 * d].astype(F32))
    merged = ga * o_a + gb * o_b
    mix = _dot(merged.astype(BF16), wout_ref[...])
    yield
    x1 = x_ref[0, rows, :] + mod_ref[0, 2:3, :] * mix
    x1_ref[0, rows, :] = x1
    ms = jnp.mean(x1 * x1, axis=-1, keepdims=True)
    h2 = x1 * lax.rsqrt(ms + NORM_EPS) * nfw_ref[...]
    h2 = h2 * (1.0 + mod_ref[0, 4:5, :]) + mod_ref[0, 3:4, :]
    h2_ref[0, rows, :] = h2
    yield

    hh, hl = _split(h2)
    logits = _dot(hh, wrh_ref[...]) + (_dot(hh, wrl_ref[...]) + _dot(hl, wrh_ref[...])) + br_ref[...]
    lane = lax.broadcasted_iota(jnp.int32, (tm, LANES), 1)
    lanef = lane.astype(F32)
    cur = jnp.where(lane < N_EXPERTS, logits, -jnp.inf)
    yield
    vals, idxs, hots = [], [], []
    for _ in range(TOP_K):
        mx = jnp.max(cur, axis=-1, keepdims=True)
        idx = jnp.min(jnp.where(cur == mx, lanef, float(LANES)), axis=-1, keepdims=True)
        hot = lanef == idx
        cur = jnp.where(hot, -jnp.inf, cur)
        vals.append(mx)
        idxs.append(idx)
        hots.append(hot)
    exps = [jnp.exp(vv - vals[0]) for vv in vals]
    den = exps[0] + exps[1] + exps[2] + exps[3]
    yield
    sel = jnp.zeros((tm, LANES), F32)
    for hot in hots:
        sel = sel + jnp.where(hot, 1.0, 0.0)
    ri = lax.broadcasted_iota(jnp.int32, (tm, tm), 0)
    ci = lax.broadcasted_iota(jnp.int32, (tm, tm), 1)
    below = jnp.where(ri > ci, 1.0, 0.0).astype(BF16)
    before = _dot(below, sel.astype(BF16))
    counts = jnp.broadcast_to(jnp.sum(sel, axis=0, keepdims=True), (8, LANES))
    er = lax.broadcasted_iota(jnp.int32, (LANES, LANES), 0)
    ec = lax.broadcasted_iota(jnp.int32, (LANES, LANES), 1)
    lower = jnp.where(er < ec, 1.0, 0.0).astype(BF16)
    start = _dot(counts.astype(BF16), lower)[0:1, :]
    route = jnp.zeros((tm, LANES), F32)
    for j in range(TOP_K):
        pos = jnp.sum(jnp.where(hots[j], before + start, 0.0), axis=-1, keepdims=True)
        route = jnp.where(lane == j, pos, route)
        route = jnp.where(lane == TOP_K + j, exps[j] / den, route)
    route_ref[0, rows, :] = route
    cnt_ref[0, part] = counts
    yield


def _post(x, mod3, y, r, k, v, g, o, gates, lnw, lnb, rk, bd, wor, woa, wout, nfw, wrh, wrl, br, tm):
    bsz, s, d = x.shape
    w = RWKV_WIDTH
    parts = tm // ROUTE_TILE
    tok = lambda n: pl.BlockSpec((1, tm, n), lambda b, i: (b, i, 0))
    row = lambda n: pl.BlockSpec((1, n), lambda b, i: (0, 0))
    full = lambda a: pl.BlockSpec(a.shape, lambda b, i: (0,) * a.ndim)
    return pl.pallas_call(
        _post_kernel,
        grid=(bsz, s // tm),
        in_specs=[tok(d), pl.BlockSpec((1, 6, d), lambda b, i: (b, 0, 0)),
                  tok(w), tok(w), tok(w), tok(w), tok(w),
                  tok(MLA_HEADS * MLA_V), tok(2 * d),
                  row(w), row(w), row(w), full(bd), full(wor), full(woa), full(wout), row(d),
                  full(wrh), full(wrl), row(LANES)],
        out_specs=[tok(d), tok(d), tok(LANES), pl.BlockSpec((1, parts, 8, LANES), lambda b, i: (b, i, 0, 0))],
        out_shape=[jax.ShapeDtypeStruct((bsz, s, d), F32),
                   jax.ShapeDtypeStruct((bsz, s, d), F32),
                   jax.ShapeDtypeStruct((bsz, s, LANES), F32),
                   jax.ShapeDtypeStruct((bsz, s // ROUTE_TILE, 8, LANES), F32)],
        compiler_params=_params("parallel", "parallel"),
        name="post",
    )(x, mod3, y, r, k, v, g, o, gates, lnw, lnb, rk, bd, wor, woa, wout, nfw, wrh, wrl, br)


def _load_rows(ref, n):
    nc = ref.shape[0] // n
    return jnp.concatenate([ref[pl.ds(c, n, stride=nc), :] for c in range(nc)], axis=1)


def _store_rows(ref, val):
    nc = val.shape[1] // LANES
    for c in range(nc):
        ref[pl.ds(c, val.shape[0], stride=nc), :] = val[:, c * LANES:(c + 1) * LANES]


def _run_copies(tile, cnt_ref, off_ref, dst_ref, tm, make_copy):
    def per_expert(e, carry):
        idx = tile * N_EXPERTS + e
        o = off_ref[idx]
        d0 = dst_ref[idx]
        _pieces(cnt_ref[idx], tm, lambda done, size: make_copy(o + done, d0 + done, size).start())
        return carry

    lax.fori_loop(0, N_EXPERTS, per_expert, 0)


def _pieces(count, limit, fn):
    def emit(bits):
        for b in bits:
            size = 1 << b
            done = count & ~(2 * size - 1)

            @pl.when((count & size) != 0)
            def _():
                fn(done, size)

    bits = range(limit.bit_length() - 1, -1, -1)
    large = [b for b in bits if b >= RARE_PIECE_BIT]
    if large:
        @pl.when(count >= (1 << RARE_PIECE_BIT))
        def _():
            emit(large)
    emit([b for b in bits if b < RARE_PIECE_BIT])


def _zero_fill(gap_ref, xs_ref, zero_ref, sem, nc):
    bm = MOE_BLOCK
    zero_ref[...] = jnp.zeros_like(zero_ref)

    def piece(dst, size):
        return pltpu.make_async_copy(zero_ref.at[pl.ds(0, size * nc), :],
                                     xs_ref.at[pl.ds(pl.multiple_of(dst * nc, nc), size * nc), :], sem)

    def sweep(act):
        def per_expert(e, carry):
            g0 = gap_ref[e]
            _pieces(gap_ref[N_EXPERTS + e], bm, lambda done, size: act(piece(g0 + done, size)))
            return carry

        def tail(blk, carry):
            act(piece(gap_ref[2 * N_EXPERTS] + blk * bm, bm))
            return carry

        lax.fori_loop(0, N_EXPERTS, per_expert, 0)
        lax.fori_loop(0, gap_ref[2 * N_EXPERTS + 1], tail, 0)

    sweep(lambda cp: cp.start())
    sweep(lambda cp: cp.wait())


def _dispatch_kernel(cnt_ref, off_ref, dst_ref, gap_ref, h_ref, route_ref, xs_ref, sorted_ref, zero_ref,
                     sem, zero_sem):
    tm = h_ref.shape[0]

    @pl.when(pl.program_id(0) == 0)
    def _():
        _zero_fill(gap_ref, xs_ref, zero_ref, zero_sem, h_ref.shape[1] // LANES)

    pos_t = route_ref[...].T
    slot = lax.broadcasted_iota(jnp.int32, (TOP_K * tm, tm), 0).astype(F32)
    perm = jnp.where(slot == pos_t[0:1, :], 1.0, 0.0)
    for j in range(1, TOP_K):
        perm = perm + jnp.where(slot == pos_t[j:j + 1, :], 1.0, 0.0)
    _store_rows(sorted_ref, _dot(perm.astype(BF16), h_ref[...].astype(BF16)))
    nc = h_ref.shape[1] // LANES

    def make_copy(src, dst, size):
        return pltpu.make_async_copy(sorted_ref.at[pl.ds(pl.multiple_of(src * nc, nc), size * nc), :],
                                     xs_ref.at[pl.ds(pl.multiple_of(dst * nc, nc), size * nc), :], sem)

    _run_copies(pl.program_id(0), cnt_ref, off_ref, dst_ref, tm, make_copy)
    pltpu.make_async_copy(sorted_ref, xs_ref.at[pl.ds(0, TOP_K * tm * nc), :], sem).wait()


def _dispatch(cnt_tab, off_tab, dst_tab, gap_tab, h2, route, n_rows, tm):
    t, d = h2.shape
    nc = d // LANES
    return pl.pallas_call(
        _dispatch_kernel,
        grid_spec=pltpu.PrefetchScalarGridSpec(
            num_scalar_prefetch=4,
            grid=(t // tm,),
            in_specs=[pl.BlockSpec((tm, d), lambda i, *_: (i, 0)),
                      pl.BlockSpec((tm, LANES), lambda i, *_: (i, 0))],
            out_specs=pl.BlockSpec(memory_space=pl.ANY),
            scratch_shapes=[pltpu.VMEM((TOP_K * tm * nc, LANES), F32), pltpu.VMEM((MOE_BLOCK * nc, LANES), F32),
                            pltpu.SemaphoreType.DMA(()), pltpu.SemaphoreType.DMA(())]),
        out_shape=jax.ShapeDtypeStruct((n_rows * nc, LANES), F32),
        compiler_params=_params("arbitrary"),
        name="dispatch",
    )(cnt_tab, off_tab, dst_tab, gap_tab, h2, route)


def _moe_kernel(blk_e_ref, valid_ref, xs_ref, wg_ref, wu_ref, bg_ref, bu_ref, wd_ref, bd_ref, ys_ref):
    del blk_e_ref
    i = pl.program_id(0)
    valid = valid_ref[i]

    @pl.when(valid > 0)
    def _():
        x = _load_rows(xs_ref, MOE_BLOCK).astype(BF16)
        gate = _dot(x, wg_ref[0]) + bg_ref[0]
        up = _dot(x, wu_ref[0]) + bu_ref[0]
        gate = jnp.minimum(gate, SWIGLU_LIMIT)
        up = jnp.clip(up, -SWIGLU_LIMIT, SWIGLU_LIMIT)
        act = (up + 1.0) * gate * jax.nn.sigmoid(SWIGLU_ALPHA * gate)
        _store_rows(ys_ref, _dot(act.astype(BF16), wd_ref[0]) + bd_ref[0])

    @pl.when(valid == 0)
    def _():
        ys_ref[...] = jnp.zeros_like(ys_ref)


def _moe(blk_e, blk_valid, xs, wg, wu, bg, bu, wd, bd):
    d, ff = wg.shape[1], wg.shape[2]
    nc = d // LANES
    n_rows = xs.shape[0] // nc
    bm = MOE_BLOCK
    wspec = lambda k, n: pl.BlockSpec((1, k, n), lambda i, be, nu: (be[i], 0, 0))
    return pl.pallas_call(
        _moe_kernel,
        grid_spec=pltpu.PrefetchScalarGridSpec(
            num_scalar_prefetch=2,
            grid=(n_rows // bm,),
            in_specs=[pl.BlockSpec((bm * nc, LANES), lambda i, be, nu: (i, 0)),
                      wspec(d, ff), wspec(d, ff), wspec(1, ff), wspec(1, ff), wspec(ff, d), wspec(1, d)],
            out_specs=pl.BlockSpec((bm * nc, LANES), lambda i, be, nu: (i, 0))),
        out_shape=jax.ShapeDtypeStruct((n_rows * nc, LANES), F32),
        compiler_params=_params("arbitrary"),
        name="moe",
    )(blk_e, blk_valid, xs, wg, wu, bg, bu, wd, bd)


def _combine_kernel(cnt_ref, off_ref, dst_ref, x1_ref, route_ref, mod_ref, ys_ref, o_ref, rows_ref, sem):
    tm = x1_ref.shape[1]
    tile = pl.program_id(0) * pl.num_programs(1) + pl.program_id(1)

    nc = x1_ref.shape[2] // LANES

    def make_copy(dst, src, size):
        return pltpu.make_async_copy(ys_ref.at[pl.ds(pl.multiple_of(src * nc, nc), size * nc), :],
                                     rows_ref.at[pl.ds(pl.multiple_of(dst * nc, nc), size * nc), :], sem)

    _run_copies(tile, cnt_ref, off_ref, dst_ref, tm, make_copy)
    route = route_ref[0]
    slot = lax.broadcasted_iota(jnp.int32, (tm, TOP_K * tm), 1).astype(F32)
    mix = jnp.where(slot == route[:, 0:1], route[:, TOP_K:TOP_K + 1], 0.0)
    for j in range(1, TOP_K):
        mix = mix + jnp.where(slot == route[:, j:j + 1], route[:, TOP_K + j:TOP_K + j + 1], 0.0)
    pltpu.make_async_copy(ys_ref.at[pl.ds(0, TOP_K * tm * nc), :], rows_ref, sem).wait()
    acc = _dot(mix.astype(BF16), _load_rows(rows_ref, TOP_K * tm).astype(BF16))
    o_ref[0] = x1_ref[0] + mod_ref[0, 5:6, :] * acc


def _combine(cnt_tab, off_tab, dst_tab, x1, route, mod3, ys, tm):
    bsz, s, d = x1.shape
    return pl.pallas_call(
        _combine_kernel,
        grid_spec=pltpu.PrefetchScalarGridSpec(
            num_scalar_prefetch=3,
            grid=(bsz, s // tm),
            in_specs=[pl.BlockSpec((1, tm, d), lambda b, i, *_: (b, i, 0)),
                      pl.BlockSpec((1, tm, LANES), lambda b, i, *_: (b, i, 0)),
                      pl.BlockSpec((1, 6, d), lambda b, i, *_: (b, 0, 0)),
                      pl.BlockSpec(memory_space=pl.ANY)],
            out_specs=pl.BlockSpec((1, tm, d), lambda b, i, *_: (b, i, 0)),
            scratch_shapes=[pltpu.VMEM((TOP_K * tm * d // LANES, LANES), F32), pltpu.SemaphoreType.DMA(())]),
        out_shape=jax.ShapeDtypeStruct((bsz, s, d), F32),
        compiler_params=_params("arbitrary", "arbitrary"),
        name="combine",
    )(cnt_tab, off_tab, dst_tab, x1, route, mod3, ys)


def _pad_cols(a, n):
    return jnp.pad(a, ((0, 0), (0, n - a.shape[1])))


def _head_blocks(cols_main, cols_rot=None):
    k = cols_main.shape[0]
    out = jnp.zeros((k, MLA_HEADS, HEAD_PAD), F32)
    out = out.at[:, :, :cols_main.shape[2]].set(cols_main)
    return out.reshape(k, MLA_HEADS * HEAD_PAD)


def _layer(x, cond_mod, positions, w_in, q_a_norm_w, w_q_up, kv_a_norm_w, w_kv_up, q_norm_w, k_norm_w,
           w_o_mla, rwkv_mu, rwkv_w0, rwkv_w2, rwkv_a0, rwkv_a2, rwkv_g2, rwkv_k_k, rwkv_k_a, rwkv_r_k,
           rwkv_ln_w, rwkv_ln_b, rwkv_w_o, w_out, norm_mix_w, norm_ffn_w, w_router, b_router,
           w_gate_up, b_gate_up, w_down, b_down):
    bsz, s, d = x.shape
    t = bsz * s
    half = MLA_ROPE // 2
    nope, qk = MLA_NOPE, MLA_QK
    mod3 = cond_mod.reshape(bsz, 6, d)

    o_q, o_kv, o_kr = 0, Q_LORA, Q_LORA + KV_LORA
    o_slab = o_kr + MLA_ROPE
    o_gate = o_slab + RWKV_SLAB
    kr_w = w_in[:, o_kr:o_slab]
    zeros = lambda n: jnp.zeros((d, n), F32)
    kr_blk = jnp.concatenate([zeros(nope), kr_w, zeros(HEAD_PAD - qk)], axis=1)
    kr_rot = jnp.concatenate([zeros(nope), -kr_w[:, half:], kr_w[:, :half], zeros(HEAD_PAD - qk)], axis=1)
    w_in_p = jnp.concatenate([w_in[:, o_q:o_kr], kr_blk, kr_rot, w_in[:, o_slab:]], axis=1).astype(BF16)

    tm = min(256, s)
    mla_in, slab, gates = _inproj(x, mod3, norm_mix_w, w_in_p, tm)

    wq3 = w_q_up.reshape(Q_LORA, MLA_HEADS, qk)
    wq_rot = jnp.concatenate([jnp.zeros((Q_LORA, MLA_HEADS, nope), F32), -wq3[:, :, nope + half:],
                              wq3[:, :, nope:nope + half]], axis=2)
    wq = jnp.concatenate([_head_blocks(wq3), _head_blocks(wq_rot)], axis=1).astype(BF16)
    wkv3 = w_kv_up.reshape(KV_LORA, MLA_HEADS, nope + MLA_V)
    wkv = jnp.concatenate([_head_blocks(wkv3[:, :, :nope]), _head_blocks(wkv3[:, :, nope:])], axis=1).astype(BF16)

    def gains(wn):
        main = jnp.pad(wn, (0, HEAD_PAD - qk)).reshape(1, HEAD_PAD)
        rot = jnp.concatenate([jnp.zeros((nope,), F32), wn[nope + half:], wn[nope:nope + half],
                               jnp.zeros((HEAD_PAD - qk,), F32)]).reshape(1, HEAD_PAD)
        return main, rot

    qg, qgs = gains(q_norm_w)
    kg, kgs = gains(k_norm_w)
    inv_freq = ROPE_THETA ** (-jnp.arange(half, dtype=F32) / half)
    per_row = LANES // half
    pos_rep = jnp.repeat(positions.astype(F32).reshape(t // per_row, per_row), half, axis=1)
    cos16, sin16 = _rope_table(pos_rep, jnp.tile(inv_freq, per_row).reshape(1, LANES))
    cos16 = cos16.reshape(bsz, s, half)
    sin16 = sin16.reshape(bsz, s, half)
    cosf = jnp.concatenate([jnp.ones((bsz, s, nope), F32), cos16, cos16,
                            jnp.ones((bsz, s, HEAD_PAD - qk), F32)], axis=-1)
    sinf = jnp.concatenate([jnp.zeros((bsz, s, nope), F32), sin16, sin16,
                            jnp.zeros((bsz, s, HEAD_PAD - qk), F32)], axis=-1)
    tq = min(ATTN_BLOCK, s)
    q, k, vt = _mla_prep(mla_in, cosf, sinf, q_a_norm_w.reshape(1, -1), kv_a_norm_w.reshape(1, -1),
                         wq, wkv, qg, qgs, kg, kgs, tm, tq)
    o = _attention(q, k, vt, tq)

    w = RWKV_WIDTH
    wwa = jnp.zeros((LANES, 2 * w), F32)
    wwa = wwa.at[:DECAY_LORA, :w].set(rwkv_w2).at[DECAY_LORA:, w:].set(rwkv_a2)
    wwh = wwa.astype(BF16)
    wwl = (wwa - wwh.astype(F32)).astype(BF16)
    hid = np.arange(w) // RWKV_HEAD
    bd = jnp.asarray(hid[:, None] == hid[None, :], BF16)
    r_, lw_, k_, v_, a_, b_, g_ = _rwkv_prep(
        slab, rwkv_mu.reshape(1, -1), wwh, wwl, rwkv_w0.reshape(1, -1), rwkv_a0.reshape(1, -1),
        rwkv_g2.astype(BF16),
        rwkv_k_k.reshape(1, -1), rwkv_k_a.reshape(1, -1), bd, tm)
    y = _wkv(r_, lw_, k_, v_, a_, b_, WKV_BATCH if bsz % WKV_BATCH == 0 else 1)

    woa = w_o_mla.astype(BF16)
    wr =_pad_cols(w_router, LANES)
    wrh = wr.astype(BF16)
    wrl = (wr - wrh.astype(F32)).astype(BF16)
    br = jnp.pad(b_router, (0, LANES - N_EXPERTS)).reshape(1, LANES)
    x1, h2, route, counts = _post(
        x, mod3, y, r_, k_, v_, g_, o, gates, rwkv_ln_w.reshape(1, -1), rwkv_ln_b.reshape(1, -1),
        rwkv_r_k.reshape(1, -1), bd, rwkv_w_o.astype(BF16), woa, w_out.astype(BF16),
        norm_ffn_w.reshape(1, -1), wrh, wrl, br, POST_TILE if s % POST_TILE == 0 else ROUTE_TILE)

    bm = MOE_BLOCK
    n_rows = t * TOP_K + N_EXPERTS * bm
    n_blocks = n_rows // bm
    cnt_tab = counts[:, :, 0, :N_EXPERTS].reshape(t // ROUTE_TILE, N_EXPERTS).astype(jnp.int32)
    total = jnp.sum(cnt_tab, axis=0)
    padded = (total + bm - 1) // bm * bm
    pad_end = jnp.cumsum(padded)
    pad_start = pad_end - padded
    off_tab = jnp.cumsum(cnt_tab, axis=1) - cnt_tab
    dst_tab = pad_start[None, :] + jnp.cumsum(cnt_tab, axis=0) - cnt_tab
    blk_start = jnp.arange(n_blocks, dtype=jnp.int32) * bm
    blk_e = jnp.minimum(jnp.sum((pad_end[None, :] <= blk_start[:, None]).astype(jnp.int32), axis=1),
                        N_EXPERTS - 1)
    blk_valid = jnp.clip((pad_start + total)[blk_e] - blk_start, 0, bm).astype(jnp.int32)
    tabs = (cnt_tab.reshape(-1), off_tab.reshape(-1).astype(jnp.int32), dst_tab.reshape(-1).astype(jnp.int32))
    gap_tab = jnp.concatenate([pad_start + total, padded - total, pad_end[-1:],
                               (n_rows - pad_end[-1:]) // bm]).astype(jnp.int32)

    xs = _dispatch(*tabs, gap_tab, h2.reshape(t, d), route.reshape(t, LANES), n_rows, ROUTE_TILE)
    wgu = w_gate_up.reshape(N_EXPERTS, d, D_FF, 2)
    bgu = b_gate_up.reshape(N_EXPERTS, 1, D_FF, 2)
    ys = _moe(blk_e, blk_valid, xs, wgu[..., 0].astype(BF16), wgu[..., 1].astype(BF16),
              bgu[..., 0], bgu[..., 1], w_down.astype(BF16), b_down.reshape(N_EXPERTS, 1, d))
    return _combine(*tabs, x1, route, mod3, ys, ROUTE_TILE)


def kernel(x, c, positions, ada_w, ada_b, norm_mix_w, norm_ffn_w, w_in, q_a_norm_w, w_q_up, kv_a_norm_w, w_kv_up, q_norm_w, k_norm_w, w_o_mla, rwkv_mu, rwkv_w0, rwkv_w2, rwkv_a0, rwkv_a2, rwkv_g2, rwkv_k_k, rwkv_k_a, rwkv_r_k, rwkv_ln_w, rwkv_ln_b, rwkv_w_o, w_out, w_router, b_router, w_gate_up, b_gate_up, w_down, b_down):
    depth = ada_w.shape[0]
    for l in range(depth):
        mod = _ada(c, ada_w[l], ada_b[l])
        x = _layer(x, mod, positions, w_in[l], q_a_norm_w[l], w_q_up[l], kv_a_norm_w[l], w_kv_up[l],
                   q_norm_w[l], k_norm_w[l], w_o_mla[l], rwkv_mu[l], rwkv_w0[l], rwkv_w2[l], rwkv_a0[l],
                   rwkv_a2[l], rwkv_g2[l], rwkv_k_k[l], rwkv_k_a[l], rwkv_r_k[l], rwkv_ln_w[l],
                   rwkv_ln_b[l], rwkv_w_o[l], w_out[l], norm_mix_w[l], norm_ffn_w[l], w_router[l],
                   b_router[l], w_gate_up[l], b_gate_up[l], w_down[l], b_down[l])
    return x
```

```python
import functools

import numpy as np
import jax
import jax.numpy as jnp
from jax import lax
from jax.experimental import pallas as pl
from jax.experimental.pallas import tpu as pltpu

F32 = jnp.float32
BF16 = jnp.bfloat16

D_MODEL = 1024
MLA_HEADS = 8
MLA_NOPE = 64
MLA_ROPE = 32
MLA_QK = MLA_NOPE + MLA_ROPE
MLA_V = 64
Q_LORA = 256
KV_LORA = 128
ROPE_THETA = 10000.0
RWKV_HEADS = 8
RWKV_HEAD = 64
RWKV_WIDTH = RWKV_HEADS * RWKV_HEAD
DECAY_LORA = 64
AAA_LORA = 64
GATE_LORA = 128
RWKV_SLAB = 3 * RWKV_WIDTH + DECAY_LORA + AAA_LORA + GATE_LORA
GN_EPS = 64e-5
N_EXPERTS = 32
TOP_K = 4
D_FF = D_MODEL
SWIGLU_LIMIT = 7.0
SWIGLU_ALPHA = 1.702
NORM_EPS = 1e-6
LOG2_E = 1.4426950408889634

LANES = 128
HEAD_PAD = 128
MLA_COLS = 640
WKV_CHUNK = 64
WKV_BATCH = 4
ATTN_BLOCK = 512
ATTN_KEY_BLOCK = 512
ATTN_HEADS = 4
ROUTE_TILE = 256
POST_TILE = 512
RARE_PIECE_BIT = 6
MOE_BLOCK = 512
VMEM_LIMIT = 56 * 1024 * 1024


def _dot(a, b):
    return jnp.dot(a, b, preferred_element_type=F32)


def _dot_nt(a, b):
    return lax.dot_general(a, b, (((1,), (1,)), ((), ())), preferred_element_type=F32)


def _split(x):
    hi = x.astype(BF16)
    lo = (x - hi.astype(F32)).astype(BF16)
    return hi, lo


def _mm3(a, b):
    ah, al = _split(a)
    bh, bl = _split(b)
    return _dot(ah, bh) + (_dot(ah, bl) + _dot(al, bh))


def _mm3_nt(a, b):
    ah, al = _split(a)
    bh, bl = _split(b)
    return _dot_nt(ah, bh) + (_dot_nt(ah, bl) + _dot_nt(al, bh))


def _mm_exact_rhs(a, b_bf16):
    ah, al = _split(a)
    return _dot(ah, b_bf16) + _dot(al, b_bf16)


def _params(*sem):
    return pltpu.CompilerParams(dimension_semantics=sem, vmem_limit_bytes=VMEM_LIMIT)


def _ada_kernel(c_ref, w_ref, b_ref, o_ref):
    c = c_ref[...]
    cond = c * jax.nn.sigmoid(c)
    o_ref[...] = _mm3(cond, w_ref[...]) + b_ref[...]


def _ada(c, w, b):
    bsz, d = c.shape
    n = w.shape[1]
    tn = 1024
    return pl.pallas_call(
        _ada_kernel,
        grid=(n // tn,),
        in_specs=[pl.BlockSpec((bsz, d), lambda j: (0, 0)),
                  pl.BlockSpec((d, tn), lambda j: (0, j)),
                  pl.BlockSpec((1, tn), lambda j: (0, j))],
        out_specs=pl.BlockSpec((bsz, tn), lambda j: (0, j)),
        out_shape=jax.ShapeDtypeStruct((bsz, n), F32),
        compiler_params=_params("parallel"),
        name="ada",
    )(c, w, b.reshape(1, n))


def _inproj_kernel(x_ref, mod_ref, nw_ref, w_ref, mla_ref, slab_ref, gates_ref):
    x = x_ref[0]
    ms = jnp.mean(x * x, axis=-1, keepdims=True)
    y = x * lax.rsqrt(ms + NORM_EPS) * nw_ref[...]
    h = y * (1.0 + mod_ref[0, 1:2, :]) + mod_ref[0, 0:1, :]
    hb = h.astype(BF16)
    mla_ref[0] = _dot(hb, w_ref[:, 0:MLA_COLS])
    slab_ref[0] = _dot(hb, w_ref[:, MLA_COLS:MLA_COLS + RWKV_SLAB])
    gates_ref[0] = _dot(hb, w_ref[:, MLA_COLS + RWKV_SLAB:]).astype(BF16)


def _inproj(x, mod3, norm_w, w_in_p, tm):
    bsz, s, d = x.shape
    ncol = w_in_p.shape[1]
    return pl.pallas_call(
        _inproj_kernel,
        grid=(bsz, s // tm),
        in_specs=[pl.BlockSpec((1, tm, d), lambda b, i: (b, i, 0)),
                  pl.BlockSpec((1, 6, d), lambda b, i: (b, 0, 0)),
                  pl.BlockSpec((1, d), lambda b, i: (0, 0)),
                  pl.BlockSpec((d, ncol), lambda b, i: (0, 0))],
        out_specs=[pl.BlockSpec((1, tm, MLA_COLS), lambda b, i: (b, i, 0)),
                   pl.BlockSpec((1, tm, RWKV_SLAB), lambda b, i: (b, i, 0)),
                   pl.BlockSpec((1, tm, 2 * d), lambda b, i: (b, i, 0))],
        out_shape=[jax.ShapeDtypeStruct((bsz, s, MLA_COLS), F32),
                   jax.ShapeDtypeStruct((bsz, s, RWKV_SLAB), F32),
                   jax.ShapeDtypeStruct((bsz, s, 2 * d), BF16)],
        compiler_params=_params("parallel", "parallel"),
        name="inproj",
    )(x, mod3, norm_w.reshape(1, d), w_in_p)


def _rope_kernel(pos_ref, invf_ref, cos_ref, sin_ref):
    ang = pos_ref[...] * invf_ref[...]
    cos_ref[...] = jnp.cos(ang)
    sin_ref[...] = jnp.sin(ang)


def _rope_table(pos_rep, invf_row):
    n = pos_rep.shape[0]
    tr = min(512, n)
    spec = pl.BlockSpec((tr, LANES), lambda i: (i, 0))
    sds = jax.ShapeDtypeStruct((n, LANES), F32)
    return pl.pallas_call(
        _rope_kernel,
        grid=(n // tr,),
        in_specs=[spec, pl.BlockSpec((1, LANES), lambda i: (0, 0))],
        out_specs=[spec, spec],
        out_shape=[sds, sds],
        compiler_params=_params("parallel"),
        name="rope",
    )(pos_rep, invf_row)


def _mla_prep_kernel(mla_ref, cos_ref, sin_ref, qan_ref, kvn_ref, wq_ref, wkv_ref,
                     qg_ref, qgs_ref, kg_ref, kgs_ref, q_out, k_out, vt_out):
    m = mla_ref[0]
    cq = m[:, 0:Q_LORA]
    ckv = m[:, Q_LORA:Q_LORA + KV_LORA]
    kr = m[:, 384:512]
    krs = m[:, 512:640]
    ql = cq * lax.rsqrt(jnp.mean(cq * cq, axis=-1, keepdims=True) + NORM_EPS) * qan_ref[...]
    kvl = ckv * lax.rsqrt(jnp.mean(ckv * ckv, axis=-1, keepdims=True) + NORM_EPS) * kvn_ref[...]
    qall = _dot(ql.astype(BF16), wq_ref[...])
    kvall = _dot(kvl.astype(BF16), wkv_ref[...])
    cosf = cos_ref[0]
    sinf = sin_ref[0]
    scale = MLA_QK ** -0.5 * LOG2_E
    hw = MLA_HEADS * HEAD_PAD
    for h in range(MLA_HEADS):
        lo, hi = h * HEAD_PAD, (h + 1) * HEAD_PAD
        qh = qall[:, lo:hi]
        qs = qall[:, hw + lo:hw + hi]
        rs = lax.rsqrt(jnp.sum(qh * qh, axis=-1, keepdims=True) * (1.0 / MLA_QK) + NORM_EPS)
        qo = (qh * rs * qg_ref[...]) * cosf + (qs * rs * qgs_ref[...]) * sinf
        q_out[0, h] = (qo * scale).astype(BF16)
        kh = kvall[:, lo:hi] + kr
        rs = lax.rsqrt(jnp.sum(kh * kh, axis=-1, keepdims=True) * (1.0 / MLA_QK) + NORM_EPS)
        ko = (kh * rs * kg_ref[...]) * cosf + (krs * rs * kgs_ref[...]) * sinf
        k_out[0, h] = ko.astype(BF16)
        v_t = kvall[:, hw + lo:hw + hi].T[0:MLA_V, :].astype(BF16)
        width = vt_out.shape[4]
        for c in range(vt_out.shape[2]):
            vt_out[0, h, c] = v_t[:, c * width:(c + 1) * width]


def _mla_prep(mla_in, cosf, sinf, qan, kvn, wq, wkv, qg, qgs, kg, kgs, tm, tk):
    bsz, s, _ = mla_in.shape
    hw = MLA_HEADS * HEAD_PAD
    row = lambda n: pl.BlockSpec((1, n), lambda b, i: (0, 0))
    out_spec = pl.BlockSpec((1, MLA_HEADS, tm, HEAD_PAD), lambda b, i: (b, 0, i, 0))
    out_sds = jax.ShapeDtypeStruct((bsz, MLA_HEADS, s, HEAD_PAD), BF16)
    if tk >= tm:
        per = tk // tm
        vt_spec = pl.BlockSpec((1, MLA_HEADS, 1, MLA_V, tm), lambda b, i: (b, 0, i // per, 0, i % per))
    else:
        vt_spec = pl.BlockSpec((1, MLA_HEADS, tm // tk, MLA_V, tk), lambda b, i: (b, 0, i, 0, 0))
    vt_sds = jax.ShapeDtypeStruct((bsz, MLA_HEADS, s // tk, MLA_V, tk), BF16)
    return pl.pallas_call(
        _mla_prep_kernel,
        grid=(bsz, s // tm),
        in_specs=[pl.BlockSpec((1, tm, MLA_COLS), lambda b, i: (b, i, 0)),
                  pl.BlockSpec((1, tm, LANES), lambda b, i: (b, i, 0)),
                  pl.BlockSpec((1, tm, LANES), lambda b, i: (b, i, 0)),
                  row(Q_LORA), row(KV_LORA),
                  pl.BlockSpec((Q_LORA, 2 * hw), lambda b, i: (0, 0)),
                  pl.BlockSpec((KV_LORA, 2 * hw), lambda b, i: (0, 0)),
                  row(LANES), row(LANES), row(LANES), row(LANES)],
        out_specs=[out_spec, out_spec, vt_spec],
        out_shape=[out_sds, out_sds, vt_sds],
        compiler_params=_params("parallel", "parallel"),
        name="mla_prep",
    )(mla_in, cosf, sinf, qan, kvn, wq, wkv, qg, qgs, kg, kgs)


def _attn_kernel(q_ref, k_ref, vt_ref, o_ref, m_ref, l_ref, acc_ref, *, tq):
    qi = pl.program_id(2)
    heads = range(q_ref.shape[1])
    qs = [q_ref[0, h] for h in heads]
    m_ref[...] = jnp.full_like(m_ref, -jnp.inf)
    l_ref[...] = jnp.zeros_like(l_ref)
    acc_ref[...] = jnp.zeros_like(acc_ref)

    tk = vt_ref.shape[4]
    per = tq // tk

    def step(j, mask):
        off = pl.multiple_of(j * tk, tk)
        sts = [_dot_nt(k_ref[0, h, pl.ds(off, tk), :], qs[h]) for h in heads]
        for h in heads:
            st = sts[h] if mask is None else jnp.where(mask, sts[h], -jnp.inf)
            m = m_ref[h]
            m_new = jnp.maximum(m, jnp.max(st, axis=0, keepdims=True))
            p = jnp.exp2(st - m_new)
            alpha = jnp.exp2(m - m_new)
            m_ref[h] = m_new
            l_ref[h] = alpha * l_ref[h] + jnp.sum(p, axis=0, keepdims=True)
            acc_ref[h] = alpha * acc_ref[h] + _dot(vt_ref[0, h, j], p.astype(BF16))

    def body(j, carry):
        step(j, None)
        return carry

    lax.fori_loop(0, qi * per, body, 0)
    keys = lax.broadcasted_iota(jnp.int32, (tk, tq), 0)
    queries = lax.broadcasted_iota(jnp.int32, (tk, tq), 1)
    for part in range(per):
        step(qi * per + part, keys + part * tk <= queries)
    for h in heads[::2]:
        pair = jnp.concatenate([acc_ref[h] / l_ref[h], acc_ref[h + 1] / l_ref[h + 1]], axis=0)
        o_ref[0, :, h * MLA_V:(h + 2) * MLA_V] = pair.T.astype(BF16)


def _attention(q, k, vt, tq):
    bsz, nh, s, dh = q.shape
    dv, tk = vt.shape[3], vt.shape[4]
    hp = ATTN_HEADS
    return pl.pallas_call(
        functools.partial(_attn_kernel, tq=tq),
        grid=(bsz, nh // hp, s // tq),
        in_specs=[pl.BlockSpec((1, hp, tq, dh), lambda b, h, i: (b, h, i, 0)),
                  pl.BlockSpec((1, hp, s, dh), lambda b, h, i: (b, h, 0, 0)),
                  pl.BlockSpec((1, hp, s // tk, dv, tk), lambda b, h, i: (b, h, 0, 0, 0))],
        out_specs=pl.BlockSpec((1, tq, hp * dv), lambda b, h, i: (b, i, h)),
        out_shape=jax.ShapeDtypeStruct((bsz, s, nh * dv), BF16),
        scratch_shapes=[pltpu.VMEM((hp, 1, tq), F32), pltpu.VMEM((hp, 1, tq), F32),
                        pltpu.VMEM((hp, dv, tq), F32)],
        compiler_params=_params("parallel", "parallel", "parallel"),
        name="attn",
    )(q, k, vt)


def _softplus(x):
    return jnp.maximum(x, 0.0) + jnp.log(1.0 + jnp.exp(-jnp.abs(x)))


def _rwkv_prep_kernel(slab_ref, mu_ref, wwh_ref, wwl_ref, w0_ref, a0_ref, g2_ref, kk_ref, ka_ref, bd_ref,
                      r_out, lw_out, k_out, v_out, a_out, b_out, g_out, last_ref):
    i = pl.program_id(1)
    tm = slab_ref.shape[1]
    w = RWKV_WIDTH

    @pl.when(i == 0)
    def _():
        last_ref[...] = jnp.zeros_like(last_ref)

    slab = slab_ref[0]
    rolled = pltpu.roll(slab, 1, 0)
    rowi = lax.broadcasted_iota(jnp.int32, slab.shape, 0)
    prev = jnp.where(rowi == 0, last_ref[0:1, :], rolled)
    last_ref[0:1, :] = slab[tm - 1:tm, :]
    p = slab + (prev - slab) * mu_ref[...]
    r = p[:, 0:w]
    k = p[:, w:2 * w]
    v = p[:, 2 * w:3 * w]
    wa = p[:, 3 * w:3 * w + LANES]
    g_lo = p[:, 3 * w + LANES:3 * w + 2 * LANES]
    lane = lax.broadcasted_iota(jnp.int32, wa.shape, 1)
    wa = jnp.where(lane < DECAY_LORA, jnp.tanh(wa), wa)
    wah, wal = _split(wa)
    wa_o = _dot(wah, wwh_ref[...]) + (_dot(wah, wwl_ref[...]) + _dot(wal, wwh_ref[...]))
    log_w = -_softplus(-(w0_ref[...] + wa_o[:, 0:w])) - 0.5
    lw = -jnp.exp(log_w)
    a = jax.nn.sigmoid(a0_ref[...] + wa_o[:, w:2 * w])
    g = _dot(jax.nn.sigmoid(g_lo).astype(BF16), g2_ref[...])
    kk = k * kk_ref[...]
    ss = _mm_exact_rhs(kk * kk, bd_ref[...])
    kk = kk / jnp.maximum(jnp.sqrt(ss), 1e-12)
    r_out[0] = r
    lw_out[0] = lw
    k_out[0] = k * (1.0 + (a - 1.0) * ka_ref[...])
    v_out[0] = v
    a_out[0] = -kk
    b_out[0] = kk * a
    g_out[0] = g


def _rwkv_prep(slab, mu, wwh, wwl, w0, a0, g2, k_k, k_a, bd, tm):
    bsz, s, _ = slab.shape
    w = RWKV_WIDTH
    row = lambda n: pl.BlockSpec((1, n), lambda b, i: (0, 0))
    full = lambda a: pl.BlockSpec(a.shape, lambda b, i: (0, 0))
    out_spec = pl.BlockSpec((1, tm, w), lambda b, i: (b, i, 0))
    out_sds = jax.ShapeDtypeStruct((bsz, s, w), F32)
    return pl.pallas_call(
        _rwkv_prep_kernel,
        grid=(bsz, s // tm),
        in_specs=[pl.BlockSpec((1, tm, RWKV_SLAB), lambda b, i: (b, i, 0)),
                  row(RWKV_SLAB), full(wwh), full(wwl), row(w), row(w), full(g2), row(w), row(w), full(bd)],
        out_specs=[out_spec] * 7,
        out_shape=[out_sds] * 7,
        scratch_shapes=[pltpu.VMEM((8, RWKV_SLAB), F32)],
        compiler_params=_params("parallel", "arbitrary"),
        name="rwkv_prep",
    )(slab, mu, wwh, wwl, w0, a0, g2, k_k, k_a, bd)


def _wkv_kernel(r_ref, lw_ref, k_ref, v_ref, a_ref, b_ref, y_ref, s_ref):
    c = WKV_CHUNK
    n = pl.program_id(1)

    @pl.when(n == 0)
    def _():
        s_ref[...] = jnp.zeros_like(s_ref)

    row = lax.broadcasted_iota(jnp.int32, (LANES, LANES), 0)
    col = lax.broadcasted_iota(jnp.int32, (LANES, LANES), 1)
    lower_strict = row > col
    lower_incl = row >= col
    eye = row == col
    same16 = (row >> 4) == (col >> 4)
    same32 = (row >> 5) == (col >> 5)
    ident = jnp.where(eye, 1.0, 0.0).astype(F32)
    tr = lax.broadcasted_iota(jnp.int32, (c, c), 0)
    tc = lax.broadcasted_iota(jnp.int32, (c, c), 1)
    tri = jnp.where(tr >= tc, 1.0, 0.0).astype(BF16)
    first = lax.broadcasted_iota(jnp.int32, (c, LANES), 1) < RWKV_HEAD

    def stack(x):
        return jnp.concatenate([jnp.where(first, x, 0.0), jnp.where(first, 0.0, x)], axis=0)

    def mm(x, y):
        return _dot(x.astype(BF16), y.astype(BF16))

    nb = r_ref.shape[0]
    probs = [(bi, p) for bi in range(nb) for p in range(RWKV_HEADS // 2)]
    each = lambda f, *xs: [f(*args) for args in zip(*xs)]
    load = lambda ref: [ref[bi, :, p * LANES:(p + 1) * LANES] for bi, p in probs]
    r, lw, k, v, a, b = (load(ref) for ref in (r_ref, lw_ref, k_ref, v_ref, a_ref, b_ref))

    def cumsum(x):
        hi = x.astype(BF16)
        rem = x - hi.astype(F32)
        mid = rem.astype(BF16)
        lo = (rem - mid.astype(F32)).astype(BF16)
        return _dot(tri, hi) + (_dot(tri, mid) + _dot(tri, lo))

    cum = each(cumsum, lw)
    cum_c = [x[c - 1:c, :] for x in cum]
    e_in = each(jnp.exp, cum)
    e_neg = each(lambda x: jnp.exp(-x), cum)
    e_end = each(lambda x, xc: jnp.exp(xc - x), cum, cum_c)
    a_s = each(lambda x, cu, l: stack(x * jnp.exp(cu - l)).astype(BF16), a, cum, lw)
    r_s = each(lambda x, e: stack(x * e), r, e_in)
    b_s = each(lambda x, e: stack(x * e).astype(BF16), b, e_neg)
    k_s = each(lambda x, e: stack(x * e).astype(BF16), k, e_neg)
    b_e = each(lambda x, e: stack(x * e).T.astype(BF16), b, e_end)
    k_e = each(lambda x, e: stack(x * e).T.astype(BF16), k, e_end)
    v_s = each(lambda x: stack(x).astype(BF16), v)

    tt = each(lambda x1, x2, y1, y2: _dot_nt(jnp.concatenate([x1, x2.astype(BF16)], axis=0),
                                             jnp.concatenate([y1, y2], axis=0)), a_s, r_s, b_s, k_s)
    d_ab = [jnp.where(lower_strict, x[0:LANES, 0:LANES], 0.0) for x in tt]
    e_ak = [jnp.where(lower_strict, x[0:LANES, LANES:], 0.0).astype(BF16) for x in tt]
    f_rb = [jnp.where(lower_incl, x[LANES:, 0:LANES], 0.0).astype(BF16) for x in tt]
    f_rk = [jnp.where(lower_incl, x[LANES:, LANES:], 0.0).astype(BF16) for x in tt]

    d16 = [jnp.where(same16, x, 0.0) for x in d_ab]
    d32 = [jnp.where(same32, x, 0.0) for x in d_ab]
    z = each(_dot, e_ak, v_s)
    x2 = each(mm, d16, d16)
    x4 = each(mm, x2, x2)
    x8 = each(mm, x4, x4)
    t = [ident + x for x in d16]
    t = each(lambda t_, x: t_ + mm(t_, x), t, x2)
    t = each(lambda t_, x: t_ + mm(t_, x), t, x4)
    t = each(lambda t_, x: t_ + mm(t_, x), t, x8)
    t = each(lambda t_, hi, lo: t_ + mm(mm(t_, hi - lo), t_), t, d32, d16)
    t = each(lambda t_, hi, lo: t_ + mm(mm(t_, hi - lo), t_), t, d_ab, d32)

    au = each(lambda t_, x, zz: mm(t_, jnp.concatenate([x, zz.astype(BF16)], axis=1)).astype(BF16),
              t, a_s, z)
    g = each(_dot, f_rb, au)
    y0 = each(lambda gg, f, vv: gg[:, LANES:] + _dot(f, vv), g, f_rk, v_s)
    r1 = each(lambda x, gg: x + gg[:, 0:LANES], r_s, g)
    hmat = each(_dot, b_e, au)
    m_mat = each(lambda h, xc: jnp.where(eye, jnp.exp(xc), 0.0) + h[:, 0:LANES], hmat, cum_c)
    n_mat = each(lambda h, ke, vv: h[:, LANES:] + _dot(ke, vv), hmat, k_e, v_s)

    for i, (bi, p) in enumerate(probs):
        s0 = s_ref[i]
        s0b = s0.astype(BF16)
        ys = _dot(r1[i].astype(BF16), s0b) + y0[i]
        s_ref[i] = _dot(m_mat[i].astype(BF16), s0b) + n_mat[i]
        y_ref[bi, :, p * LANES:(p + 1) * LANES] = ys[0:c, :] + ys[c:, :]


def _wkv(r, lw, k, v, a, b, nb):
    bsz, s, w = r.shape
    spec = pl.BlockSpec((nb, WKV_CHUNK, w), lambda bi, n: (bi, n, 0))
    return pl.pallas_call(
        _wkv_kernel,
        grid=(bsz // nb, s // WKV_CHUNK),
        in_specs=[spec] * 6,
        out_specs=spec,
        out_shape=jax.ShapeDtypeStruct((bsz, s, w), F32),
        scratch_shapes=[pltpu.VMEM((nb * RWKV_HEADS // 2, LANES, LANES), F32)],
        compiler_params=_params("parallel", "arbitrary"),
        name="wkv",
    )(r, lw, k, v, a, b)


def _post_kernel(x_ref, mod_ref, y_ref, r_ref, k_ref, v_ref, g_ref, o_ref, gates_ref,
                 lnw_ref, lnb_ref, rk_ref, bd_ref, wor_ref, woa_ref, wout_ref, nfw_ref,
                 wrh_ref, wrl_ref, br_ref,
                 x1_ref, h2_ref, route_ref, cnt_ref):
    tiles = [_post_tile(part, x_ref, mod_ref, y_ref, r_ref, k_ref, v_ref, g_ref, o_ref, gates_ref,
                        lnw_ref, lnb_ref, rk_ref, bd_ref, wor_ref, woa_ref, wout_ref, nfw_ref,
                        wrh_ref, wrl_ref, br_ref, x1_ref, h2_ref, route_ref, cnt_ref)
             for part in range(x_ref.shape[1] // ROUTE_TILE)]
    for _ in zip(*tiles):
        pass


def _post_tile(part, x_ref, mod_ref, y_ref, r_ref, k_ref, v_ref, g_ref, o_ref, gates_ref,
               lnw_ref, lnb_ref, rk_ref, bd_ref, wor_ref, woa_ref, wout_ref, nfw_ref,
               wrh_ref, wrl_ref, br_ref, x1_ref, h2_ref, route_ref, cnt_ref):
    tm = ROUTE_TILE
    rows = slice(part * tm, (part + 1) * tm)
    d = x_ref.shape[2]
    bd = bd_ref[...]
    inv_n = 1.0 / RWKV_HEAD
    y = y_ref[0, rows, :]
    seg = lambda t: _dot(t.astype(BF16), bd)
    mu = seg(y) * inv_n
    dlt = y - mu
    var = seg(dlt * dlt) * inv_n
    yn = dlt * lax.rsqrt(var + GN_EPS) * lnw_ref[...] + lnb_ref[...]
    yield
    v = v_ref[0, rows, :]
    bonus = seg(r_ref[0, rows, :] * k_ref[0, rows, :] * rk_ref[...]) * v
    z = (yn + bonus) * g_ref[0, rows, :]
    o_b = _dot(z.astype(BF16), wor_ref[...])
    o_a = _dot(o_ref[0, rows, :], woa_ref[...])
    yield
    ga = jax.nn.sigmoid(gates_ref[0, rows, 0:d].astype(F32))
    gb = jax.nn.sigmoid(gates_ref[0, rows, d:2 * d].astype(F32))
    merged = ga * o_a + gb * o_b
    mix = _dot(merged.astype(BF16), wout_ref[...])
    yield
    x1 = x_ref[0, rows, :] + mod_ref[0, 2:3, :] * mix
    x1_ref[0, rows, :] = x1
    ms = jnp.mean(x1 * x1, axis=-1, keepdims=True)
    h2 = x1 * lax.rsqrt(ms + NORM_EPS) * nfw_ref[...]
    h2 = h2 * (1.0 + mod_ref[0, 4:5, :]) + mod_ref[0, 3:4, :]
    h2_ref[0, rows, :] = h2
    yield

    hh, hl = _split(h2)
    logits = _dot(hh, wrh_ref[...]) + (_dot(hh, wrl_ref[...]) + _dot(hl, wrh_ref[...])) + br_ref[...]
    lane = lax.broadcasted_iota(jnp.int32, (tm, LANES), 1)
    lanef = lane.astype(F32)
    cur = jnp.where(lane < N_EXPERTS, logits, -jnp.inf)
    yield
    vals, idxs, hots = [], [], []
    for _ in range(TOP_K):
        mx = jnp.max(cur, axis=-1, keepdims=True)
        idx = jnp.min(jnp.where(cur == mx, lanef, float(LANES)), axis=-1, keepdims=True)
        hot = lanef == idx
        cur = jnp.where(hot, -jnp.inf, cur)
        vals.append(mx)
        idxs.append(idx)
        hots.append(hot)
    exps = [jnp.exp(vv - vals[0]) for vv in vals]
    den = exps[0] + exps[1] + exps[2] + exps[3]
    yield
    sel = jnp.zeros((tm, LANES), F32)
    for hot in hots:
        sel = sel + jnp.where(hot, 1.0, 0.0)
    ri = lax.broadcasted_iota(jnp.int32, (tm, tm), 0)
    ci = lax.broadcasted_iota(jnp.int32, (tm, tm), 1)
    below = jnp.where(ri > ci, 1.0, 0.0).astype(BF16)
    before = _dot(below, sel.astype(BF16))
    counts = jnp.broadcast_to(jnp.sum(sel, axis=0, keepdims=True), (8, LANES))
    er = lax.broadcasted_iota(jnp.int32, (LANES, LANES), 0)
    ec = lax.broadcasted_iota(jnp.int32, (LANES, LANES), 1)
    lower = jnp.where(er < ec, 1.0, 0.0).astype(BF16)
    start = _dot(counts.astype(BF16), lower)[0:1, :]
    route = jnp.zeros((tm, LANES), F32)
    for j in range(TOP_K):
        pos = jnp.sum(jnp.where(hots[j], before + start, 0.0), axis=-1, keepdims=True)
        route = jnp.where(lane == j, pos, route)
        route = jnp.where(lane == TOP_K + j, exps[j] / den, route)
    route_ref[0, rows, :] = route
    cnt_ref[0, part] = counts
    yield


def _post(x, mod3, y, r, k, v, g, o, gates, lnw, lnb, rk, bd, wor, woa, wout, nfw, wrh, wrl, br, tm):
    bsz, s, d = x.shape
    w = RWKV_WIDTH
    parts = tm // ROUTE_TILE
    tok = lambda n: pl.BlockSpec((1, tm, n), lambda b, i: (b, i, 0))
    row = lambda n: pl.BlockSpec((1, n), lambda b, i: (0, 0))
    full = lambda a: pl.BlockSpec(a.shape, lambda b, i: (0,) * a.ndim)
    return pl.pallas_call(
        _post_kernel,
        grid=(bsz, s // tm),
        in_specs=[tok(d), pl.BlockSpec((1, 6, d), lambda b, i: (b, 0, 0)),
                  tok(w), tok(w), tok(w), tok(w), tok(w),
                  tok(MLA_HEADS * MLA_V), tok(2 * d),
                  row(w), row(w), row(w), full(bd), full(wor), full(woa), full(wout), row(d),
                  full(wrh), full(wrl), row(LANES)],
        out_specs=[tok(d), tok(d), tok(LANES), pl.BlockSpec((1, parts, 8, LANES), lambda b, i: (b, i, 0, 0))],
        out_shape=[jax.ShapeDtypeStruct((bsz, s, d), F32),
                   jax.ShapeDtypeStruct((bsz, s, d), F32),
                   jax.ShapeDtypeStruct((bsz, s, LANES), F32),
                   jax.ShapeDtypeStruct((bsz, s // ROUTE_TILE, 8, LANES), F32)],
        compiler_params=_params("parallel", "parallel"),
        name="post",
    )(x, mod3, y, r, k, v, g, o, gates, lnw, lnb, rk, bd, wor, woa, wout, nfw, wrh, wrl, br)


def _load_rows(ref, n):
    nc = ref.shape[0] // n
    return jnp.concatenate([ref[pl.ds(c, n, stride=nc), :] for c in range(nc)], axis=1)


def _store_rows(ref, val):
    nc = val.shape[1] // LANES
    for c in range(nc):
        ref[pl.ds(c, val.shape[0], stride=nc), :] = val[:, c * LANES:(c + 1) * LANES]


def _run_copies(tile, cnt_ref, off_ref, dst_ref, tm, make_copy):
    def per_expert(e, carry):
        idx = tile * N_EXPERTS + e
        o = off_ref[idx]
        d0 = dst_ref[idx]
        _pieces(cnt_ref[idx], tm, lambda done, size: make_copy(o + done, d0 + done, size).start())
        return carry

    lax.fori_loop(0, N_EXPERTS, per_expert, 0)


def _pieces(count, limit, fn):
    def emit(bits):
        for b in bits:
            size = 1 << b
            done = count & ~(2 * size - 1)

            @pl.when((count & size) != 0)
            def _():
                fn(done, size)

    bits = range(limit.bit_length() - 1, -1, -1)
    large = [b for b in bits if b >= RARE_PIECE_BIT]
    if large:
        @pl.when(count >= (1 << RARE_PIECE_BIT))
        def _():
            emit(large)
    emit([b for b in bits if b < RARE_PIECE_BIT])


def _zero_fill(gap_ref, xs_ref, zero_ref, sem, nc):
    bm = MOE_BLOCK
    zero_ref[...] = jnp.zeros_like(zero_ref)

    def piece(dst, size):
        return pltpu.make_async_copy(zero_ref.at[pl.ds(0, size * nc), :],
                                     xs_ref.at[pl.ds(pl.multiple_of(dst * nc, nc), size * nc), :], sem)

    def sweep(act):
        def per_expert(e, carry):
            g0 = gap_ref[e]
            _pieces(gap_ref[N_EXPERTS + e], bm, lambda done, size: act(piece(g0 + done, size)))
            return carry

        def tail(blk, carry):
            act(piece(gap_ref[2 * N_EXPERTS] + blk * bm, bm))
            return carry

        lax.fori_loop(0, N_EXPERTS, per_expert, 0)
        lax.fori_loop(0, gap_ref[2 * N_EXPERTS + 1], tail, 0)

    sweep(lambda cp: cp.start())
    sweep(lambda cp: cp.wait())


def _dispatch_kernel(cnt_ref, off_ref, dst_ref, gap_ref, h_ref, route_ref, xs_ref, sorted_ref, zero_ref,
                     sem, zero_sem):
    tm = h_ref.shape[0]

    @pl.when(pl.program_id(0) == 0)
    def _():
        _zero_fill(gap_ref, xs_ref, zero_ref, zero_sem, h_ref.shape[1] // LANES)

    pos_t = route_ref[...].T
    slot = lax.broadcasted_iota(jnp.int32, (TOP_K * tm, tm), 0).astype(F32)
    perm = jnp.where(slot == pos_t[0:1, :], 1.0, 0.0)
    for j in range(1, TOP_K):
        perm = perm + jnp.where(slot == pos_t[j:j + 1, :], 1.0, 0.0)
    i = pl.program_id(0)
    cur = i % 2
    buf = sorted_ref.at[cur]
    _store_rows(buf, _dot(perm.astype(BF16), h_ref[...].astype(BF16)))
    nc = h_ref.shape[1] // LANES

    def make_copy(src, dst, size):
        return pltpu.make_async_copy(buf.at[pl.ds(pl.multiple_of(src * nc, nc), size * nc), :],
                                     xs_ref.at[pl.ds(pl.multiple_of(dst * nc, nc), size * nc), :], sem.at[cur])

    _run_copies(i, cnt_ref, off_ref, dst_ref, tm, make_copy)

    def drain(which):
        pltpu.make_async_copy(sorted_ref.at[which], xs_ref.at[pl.ds(0, TOP_K * tm * nc), :], sem.at[which]).wait()

    @pl.when(i > 0)
    def _():
        drain(1 - cur)

    @pl.when(i == pl.num_programs(0) - 1)
    def _():
        drain(cur)


def _dispatch(cnt_tab, off_tab, dst_tab, gap_tab, h2, route, n_rows, tm):
    t, d = h2.shape
    nc = d // LANES
    return pl.pallas_call(
        _dispatch_kernel,
        grid_spec=pltpu.PrefetchScalarGridSpec(
            num_scalar_prefetch=4,
            grid=(t // tm,),
            in_specs=[pl.BlockSpec((tm, d), lambda i, *_: (i, 0)),
                      pl.BlockSpec((tm, LANES), lambda i, *_: (i, 0))],
            out_specs=pl.BlockSpec(memory_space=pl.ANY),
            scratch_shapes=[pltpu.VMEM((2, TOP_K * tm * nc, LANES), F32), pltpu.VMEM((MOE_BLOCK * nc, LANES), F32),
                            pltpu.SemaphoreType.DMA((2,)), pltpu.SemaphoreType.DMA(())]),
        out_shape=jax.ShapeDtypeStruct((n_rows * nc, LANES), F32),
        compiler_params=_params("arbitrary"),
        name="dispatch",
    )(cnt_tab, off_tab, dst_tab, gap_tab, h2, route)


def _moe_kernel(blk_e_ref, valid_ref, xs_ref, wg_ref, wu_ref, bg_ref, bu_ref, wd_ref, bd_ref, ys_ref):
    del blk_e_ref
    i = pl.program_id(0)
    valid = valid_ref[i]

    @pl.when(valid > 0)
    def _():
        x = _load_rows(xs_ref, MOE_BLOCK).astype(BF16)
        gate = _dot_nt(x, wg_ref[0]) + bg_ref[0]
        up = _dot_nt(x, wu_ref[0]) + bu_ref[0]
        gate = jnp.minimum(gate, SWIGLU_LIMIT)
        up = jnp.clip(up, -SWIGLU_LIMIT, SWIGLU_LIMIT)
        act = (up + 1.0) * gate * jax.nn.sigmoid(SWIGLU_ALPHA * gate)
        _store_rows(ys_ref, _dot(act.astype(BF16), wd_ref[0]) + bd_ref[0])

    @pl.when(valid == 0)
    def _():
        ys_ref[...] = jnp.zeros_like(ys_ref)


def _moe(blk_e, blk_valid, xs, wg, wu, bg, bu, wd, bd):
    ff, d = wg.shape[1], wg.shape[2]
    nc = d // LANES
    n_rows = xs.shape[0] // nc
    bm = MOE_BLOCK
    wspec = lambda k, n: pl.BlockSpec((1, k, n), lambda i, be, nu: (be[i], 0, 0))
    return pl.pallas_call(
        _moe_kernel,
        grid_spec=pltpu.PrefetchScalarGridSpec(
            num_scalar_prefetch=2,
            grid=(n_rows // bm,),
            in_specs=[pl.BlockSpec((bm * nc, LANES), lambda i, be, nu: (i, 0)),
                      wspec(ff, d), wspec(ff, d), wspec(1, ff), wspec(1, ff), wspec(ff, d), wspec(1, d)],
            out_specs=pl.BlockSpec((bm * nc, LANES), lambda i, be, nu: (i, 0))),
        out_shape=jax.ShapeDtypeStruct((n_rows * nc, LANES), F32),
        compiler_params=_params("arbitrary"),
        name="moe",
    )(blk_e, blk_valid, xs, wg, wu, bg, bu, wd, bd)


def _combine_kernel(cnt_ref, off_ref, dst_ref, x1_ref, route_ref, mod_ref, ys_ref, o_ref, rows_ref, sem):
    tm = x1_ref.shape[1]
    tile = pl.program_id(0) * pl.num_programs(1) + pl.program_id(1)
    n_tiles = pl.num_programs(0) * pl.num_programs(1)
    nc = x1_ref.shape[2] // LANES
    cur = tile % 2

    def fetch(which_tile):
        which = which_tile % 2

        def make_copy(dst, src, size):
            return pltpu.make_async_copy(
                ys_ref.at[pl.ds(pl.multiple_of(src * nc, nc), size * nc), :],
                rows_ref.at[which, pl.ds(pl.multiple_of(dst * nc, nc), size * nc), :], sem.at[which])

        _run_copies(which_tile, cnt_ref, off_ref, dst_ref, tm, make_copy)

    @pl.when(tile == 0)
    def _():
        fetch(tile)

    @pl.when(tile + 1 < n_tiles)
    def _():
        fetch(tile + 1)

    route = route_ref[0]
    slot = lax.broadcasted_iota(jnp.int32, (tm, TOP_K * tm), 1).astype(F32)
    mix = jnp.where(slot == route[:, 0:1], route[:, TOP_K:TOP_K + 1], 0.0)
    for j in range(1, TOP_K):
        mix = mix + jnp.where(slot == route[:, j:j + 1], route[:, TOP_K + j:TOP_K + j + 1], 0.0)
    pltpu.make_async_copy(ys_ref.at[pl.ds(0, TOP_K * tm * nc), :], rows_ref.at[cur], sem.at[cur]).wait()
    acc = _dot(mix.astype(BF16), _load_rows(rows_ref.at[cur], TOP_K * tm).astype(BF16))
    o_ref[0] = x1_ref[0] + mod_ref[0, 5:6, :] * acc


def _combine(cnt_tab, off_tab, dst_tab, x1, route, mod3, ys, tm):
    bsz, s, d = x1.shape
    return pl.pallas_call(
        _combine_kernel,
        grid_spec=pltpu.PrefetchScalarGridSpec(
            num_scalar_prefetch=3,
            grid=(bsz, s // tm),
            in_specs=[pl.BlockSpec((1, tm, d), lambda b, i, *_: (b, i, 0)),
                      pl.BlockSpec((1, tm, LANES), lambda b, i, *_: (b, i, 0)),
                      pl.BlockSpec((1, 6, d), lambda b, i, *_: (b, 0, 0)),
                      pl.BlockSpec(memory_space=pl.ANY)],
            out_specs=pl.BlockSpec((1, tm, d), lambda b, i, *_: (b, i, 0)),
            scratch_shapes=[pltpu.VMEM((2, TOP_K * tm * d // LANES, LANES), F32), pltpu.SemaphoreType.DMA((2,))]),
        out_shape=jax.ShapeDtypeStruct((bsz, s, d), F32),
        compiler_params=_params("arbitrary", "arbitrary"),
        name="combine",
    )(cnt_tab, off_tab, dst_tab, x1, route, mod3, ys)


def _pad_cols(a, n):
    return jnp.pad(a, ((0, 0), (0, n - a.shape[1])))


def _head_blocks(cols_main, cols_rot=None):
    k = cols_main.shape[0]
    out = jnp.zeros((k, MLA_HEADS, HEAD_PAD), F32)
    out = out.at[:, :, :cols_main.shape[2]].set(cols_main)
    return out.reshape(k, MLA_HEADS * HEAD_PAD)


def _layer(x, cond_mod, positions, w_in, q_a_norm_w, w_q_up, kv_a_norm_w, w_kv_up, q_norm_w, k_norm_w,
           w_o_mla, rwkv_mu, rwkv_w0, rwkv_w2, rwkv_a0, rwkv_a2, rwkv_g2, rwkv_k_k, rwkv_k_a, rwkv_r_k,
           rwkv_ln_w, rwkv_ln_b, rwkv_w_o, w_out, norm_mix_w, norm_ffn_w, w_router, b_router,
           w_gate_up, b_gate_up, w_down, b_down):
    bsz, s, d = x.shape
    t = bsz * s
    half = MLA_ROPE // 2
    nope, qk = MLA_NOPE, MLA_QK
    mod3 = cond_mod.reshape(bsz, 6, d)

    o_q, o_kv, o_kr = 0, Q_LORA, Q_LORA + KV_LORA
    o_slab = o_kr + MLA_ROPE
    o_gate = o_slab + RWKV_SLAB
    kr_w = w_in[:, o_kr:o_slab]
    zeros = lambda n: jnp.zeros((d, n), F32)
    kr_blk = jnp.concatenate([zeros(nope), kr_w, zeros(HEAD_PAD - qk)], axis=1)
    kr_rot = jnp.concatenate([zeros(nope), -kr_w[:, half:], kr_w[:, :half], zeros(HEAD_PAD - qk)], axis=1)
    w_in_p = jnp.concatenate([w_in[:, o_q:o_kr], kr_blk, kr_rot, w_in[:, o_slab:]], axis=1).astype(BF16)

    tm = min(256, s)
    mla_in, slab, gates = _inproj(x, mod3, norm_mix_w, w_in_p, tm)

    wq3 = w_q_up.reshape(Q_LORA, MLA_HEADS, qk)
    wq_rot = jnp.concatenate([jnp.zeros((Q_LORA, MLA_HEADS, nope), F32), -wq3[:, :, nope + half:],
                              wq3[:, :, nope:nope + half]], axis=2)
    wq = jnp.concatenate([_head_blocks(wq3), _head_blocks(wq_rot)], axis=1).astype(BF16)
    wkv3 = w_kv_up.reshape(KV_LORA, MLA_HEADS, nope + MLA_V)
    wkv = jnp.concatenate([_head_blocks(wkv3[:, :, :nope]), _head_blocks(wkv3[:, :, nope:])], axis=1).astype(BF16)

    def gains(wn):
        main = jnp.pad(wn, (0, HEAD_PAD - qk)).reshape(1, HEAD_PAD)
        rot = jnp.concatenate([jnp.zeros((nope,), F32), wn[nope + half:], wn[nope:nope + half],
                               jnp.zeros((HEAD_PAD - qk,), F32)]).reshape(1, HEAD_PAD)
        return main, rot

    qg, qgs = gains(q_norm_w)
    kg, kgs = gains(k_norm_w)
    inv_freq = ROPE_THETA ** (-jnp.arange(half, dtype=F32) / half)
    per_row = LANES // half
    pos_rep = jnp.repeat(positions.astype(F32).reshape(t // per_row, per_row), half, axis=1)
    cos16, sin16 = _rope_table(pos_rep, jnp.tile(inv_freq, per_row).reshape(1, LANES))
    cos16 = cos16.reshape(bsz, s, half)
    sin16 = sin16.reshape(bsz, s, half)
    cosf = jnp.concatenate([jnp.ones((bsz, s, nope), F32), cos16, cos16,
                            jnp.ones((bsz, s, HEAD_PAD - qk), F32)], axis=-1)
    sinf = jnp.concatenate([jnp.zeros((bsz, s, nope), F32), sin16, sin16,
                            jnp.zeros((bsz, s, HEAD_PAD - qk), F32)], axis=-1)
    tq = min(ATTN_BLOCK, s)
    q, k, vt = _mla_prep(mla_in, cosf, sinf, q_a_norm_w.reshape(1, -1), kv_a_norm_w.reshape(1, -1),
                         wq, wkv, qg, qgs, kg, kgs, tm, min(ATTN_KEY_BLOCK, s))
    o = _attention(q, k, vt, tq)

    w = RWKV_WIDTH
    wwa = jnp.zeros((LANES, 2 * w), F32)
    wwa = wwa.at[:DECAY_LORA, :w].set(rwkv_w2).at[DECAY_LORA:, w:].set(rwkv_a2)
    wwh = wwa.astype(BF16)
    wwl = (wwa - wwh.astype(F32)).astype(BF16)
    hid = np.arange(w) // RWKV_HEAD
    bd = jnp.asarray(hid[:, None] == hid[None, :], BF16)
    r_, lw_, k_, v_, a_, b_, g_ = _rwkv_prep(
        slab, rwkv_mu.reshape(1, -1), wwh, wwl, rwkv_w0.reshape(1, -1), rwkv_a0.reshape(1, -1),
        rwkv_g2.astype(BF16),
        rwkv_k_k.reshape(1, -1), rwkv_k_a.reshape(1, -1), bd, tm)
    y = _wkv(r_, lw_, k_, v_, a_, b_, WKV_BATCH if bsz % WKV_BATCH == 0 else 1)

    woa = w_o_mla.astype(BF16)
    wr =_pad_cols(w_router, LANES)
    wrh = wr.astype(BF16)
    wrl = (wr - wrh.astype(F32)).astype(BF16)
    br = jnp.pad(b_router, (0, LANES - N_EXPERTS)).reshape(1, LANES)
    x1, h2, route, counts = _post(
        x, mod3, y, r_, k_, v_, g_, o, gates, rwkv_ln_w.reshape(1, -1), rwkv_ln_b.reshape(1, -1),
        rwkv_r_k.reshape(1, -1), bd, rwkv_w_o.astype(BF16), woa, w_out.astype(BF16),
        norm_ffn_w.reshape(1, -1), wrh, wrl, br, POST_TILE if s % POST_TILE == 0 else ROUTE_TILE)

    bm = MOE_BLOCK
    n_rows = t * TOP_K + N_EXPERTS * bm
    n_blocks = n_rows // bm
    cnt_tab = counts[:, :, 0, :N_EXPERTS].reshape(t // ROUTE_TILE, N_EXPERTS).astype(jnp.int32)
    total = jnp.sum(cnt_tab, axis=0)
    padded = (total + bm - 1) // bm * bm
    pad_end = jnp.cumsum(padded)
    pad_start = pad_end - padded
    off_tab = jnp.cumsum(cnt_tab, axis=1) - cnt_tab
    dst_tab = pad_start[None, :] + jnp.cumsum(cnt_tab, axis=0) - cnt_tab
    blk_start = jnp.arange(n_blocks, dtype=jnp.int32) * bm
    blk_e = jnp.minimum(jnp.sum((pad_end[None, :] <= blk_start[:, None]).astype(jnp.int32), axis=1),
                        N_EXPERTS - 1)
    blk_valid = jnp.clip((pad_start + total)[blk_e] - blk_start, 0, bm).astype(jnp.int32)
    tabs = (cnt_tab.reshape(-1), off_tab.reshape(-1).astype(jnp.int32), dst_tab.reshape(-1).astype(jnp.int32))
    gap_tab = jnp.concatenate([pad_start + total, padded - total, pad_end[-1:],
                               (n_rows - pad_end[-1:]) // bm]).astype(jnp.int32)

    xs = _dispatch(*tabs, gap_tab, h2.reshape(t, d), route.reshape(t, LANES), n_rows, ROUTE_TILE)
    wgu = w_gate_up.transpose(0, 2, 1).reshape(N_EXPERTS, D_FF, 2, d)
    bgu = b_gate_up.reshape(N_EXPERTS, 1, D_FF, 2)
    ys = _moe(blk_e, blk_valid, xs, wgu[:, :, 0, :].astype(BF16), wgu[:, :, 1, :].astype(BF16),
              bgu[..., 0], bgu[..., 1], w_down.astype(BF16), b_down.reshape(N_EXPERTS, 1, d))
    return _combine(*tabs, x1, route, mod3, ys, ROUTE_TILE)


def kernel(x, c, positions, ada_w, ada_b, norm_mix_w, norm_ffn_w, w_in, q_a_norm_w, w_q_up, kv_a_norm_w, w_kv_up, q_norm_w, k_norm_w, w_o_mla, rwkv_mu, rwkv_w0, rwkv_w2, rwkv_a0, rwkv_a2, rwkv_g2, rwkv_k_k, rwkv_k_a, rwkv_r_k, rwkv_ln_w, rwkv_ln_b, rwkv_w_o, w_out, w_router, b_router, w_gate_up, b_gate_up, w_down, b_down):
    depth = ada_w.shape[0]
    for l in range(depth):
        mod = _ada(c, ada_w[l], ada_b[l])
        x = _layer(x, mod, positions, w_in[l], q_a_norm_w[l], w_q_up[l], kv_a_norm_w[l], w_kv_up[l],
                   q_norm_w[l], k_norm_w[l], w_o_mla[l], rwkv_mu[l], rwkv_w0[l], rwkv_w2[l], rwkv_a0[l],
                   rwkv_a2[l], rwkv_g2[l], rwkv_k_k[l], rwkv_k_a[l], rwkv_r_k[l], rwkv_ln_w[l],
                   rwkv_ln_b[l], rwkv_w_o[l], w_out[l], norm_mix_w[l], norm_ffn_w[l], w_router[l],
                   b_router[l], w_gate_up[l], b_gate_up[l], w_down[l], b_down[l])
    return x
```

```python
import functools
import itertools

import numpy as np
import jax
import jax.numpy as jnp
from jax import lax
from jax.experimental import pallas as pl
from jax.experimental.pallas import tpu as pltpu

F32 = jnp.float32
BF16 = jnp.bfloat16

D_MODEL = 1024
MLA_HEADS = 8
MLA_NOPE = 64
MLA_ROPE = 32
MLA_QK = MLA_NOPE + MLA_ROPE
MLA_V = 64
Q_LORA = 256
KV_LORA = 128
ROPE_THETA = 10000.0
RWKV_HEADS = 8
RWKV_HEAD = 64
RWKV_WIDTH = RWKV_HEADS * RWKV_HEAD
DECAY_LORA = 64
AAA_LORA = 64
GATE_LORA = 128
RWKV_SLAB = 3 * RWKV_WIDTH + DECAY_LORA + AAA_LORA + GATE_LORA
GN_EPS = 64e-5
N_EXPERTS = 32
TOP_K = 4
D_FF = D_MODEL
SWIGLU_LIMIT = 7.0
SWIGLU_ALPHA = 1.702
NORM_EPS = 1e-6
LOG2_E = 1.4426950408889634

LANES = 128
HEAD_PAD = 128
MLA_COLS = 640
WKV_CHUNK = 64
WKV_BATCH = 4
ATTN_BLOCK = 512
ATTN_KEY_BLOCK = 512
ATTN_HEADS = 4
ROUTE_TILE = 256
POST_TILE = 512
RARE_PIECE_BIT = 6
MOE_BLOCK = 512
VMEM_LIMIT = 56 * 1024 * 1024


def _dot(a, b):
    return jnp.dot(a, b, preferred_element_type=F32)


def _dot_nt(a, b):
    return lax.dot_general(a, b, (((1,), (1,)), ((), ())), preferred_element_type=F32)


def _split(x):
    hi = x.astype(BF16)
    lo = (x - hi.astype(F32)).astype(BF16)
    return hi, lo


def _mm3(a, b):
    ah, al = _split(a)
    bh, bl = _split(b)
    return _dot(ah, bh) + (_dot(ah, bl) + _dot(al, bh))


def _mm3_nt(a, b):
    ah, al = _split(a)
    bh, bl = _split(b)
    return _dot_nt(ah, bh) + (_dot_nt(ah, bl) + _dot_nt(al, bh))


def _mm_exact_rhs(a, b_bf16):
    ah, al = _split(a)
    return _dot(ah, b_bf16) + _dot(al, b_bf16)


def _params(*sem):
    return pltpu.CompilerParams(dimension_semantics=sem, vmem_limit_bytes=VMEM_LIMIT)


def _ada_kernel(c_ref, w_ref, b_ref, o_ref):
    c = c_ref[...]
    cond = c * jax.nn.sigmoid(c)
    o_ref[...] = _mm3(cond, w_ref[...]) + b_ref[...]


def _ada(c, w, b):
    bsz, d = c.shape
    n = w.shape[1]
    tn = 1024
    return pl.pallas_call(
        _ada_kernel,
        grid=(n // tn,),
        in_specs=[pl.BlockSpec((bsz, d), lambda j: (0, 0)),
                  pl.BlockSpec((d, tn), lambda j: (0, j)),
                  pl.BlockSpec((1, tn), lambda j: (0, j))],
        out_specs=pl.BlockSpec((bsz, tn), lambda j: (0, j)),
        out_shape=jax.ShapeDtypeStruct((bsz, n), F32),
        compiler_params=_params("parallel"),
        name="ada",
    )(c, w, b.reshape(1, n))


def _front_kernel(x_ref, mod_ref, nw_ref, w_ref,
                  cos_ref, sin_ref, qan_ref, kvn_ref, wq_ref, wkv_ref, qg_ref, qgs_ref, kg_ref, kgs_ref,
                  mu_ref, wwh_ref, wwl_ref, w0_ref, a0_ref, g2_ref, kk_ref, ka_ref, bd_ref,
                  gates_ref, q_out, k_out, vt_out, r_out, lw_out, k2_out, v_out, a_out, b_out, g_out,
                  last_ref):
    x = x_ref[0]
    ms = jnp.mean(x * x, axis=-1, keepdims=True)
    y = x * lax.rsqrt(ms + NORM_EPS) * nw_ref[...]
    h = y * (1.0 + mod_ref[0, 1:2, :]) + mod_ref[0, 0:1, :]
    hb = h.astype(BF16)
    mla = _dot(hb, w_ref[:, 0:MLA_COLS])
    slab = _dot(hb, w_ref[:, MLA_COLS:MLA_COLS + RWKV_SLAB])
    gates_ref[0] = _dot(hb, w_ref[:, MLA_COLS + RWKV_SLAB:]).astype(BF16)
    mla_prep = _mla_prep_body(mla, cos_ref, sin_ref, qan_ref, kvn_ref, wq_ref, wkv_ref,
                              qg_ref, qgs_ref, kg_ref, kgs_ref, q_out, k_out, vt_out)
    rwkv_prep = _rwkv_prep_body(slab, pl.program_id(1) == 0, mu_ref, wwh_ref, wwl_ref, w0_ref, a0_ref, g2_ref,
                                kk_ref, ka_ref, bd_ref, r_out, lw_out, k2_out, v_out, a_out, b_out, g_out,
                                last_ref)
    for _ in itertools.zip_longest(mla_prep, rwkv_prep):
        pass


def _front(x, mod3, norm_w, w_in_p, cosf, sinf, qan, kvn, wq, wkv, qg, qgs, kg, kgs,
           mu, wwh, wwl, w0, a0, g2, k_k, k_a, bd, tm, tk):
    bsz, s, d = x.shape
    w = RWKV_WIDTH
    tok = lambda n: pl.BlockSpec((1, tm, n), lambda b, i: (b, i, 0))
    row = lambda n: pl.BlockSpec((1, n), lambda b, i: (0, 0))
    full = lambda a: pl.BlockSpec(a.shape, lambda b, i: (0,) * a.ndim)
    head_spec = pl.BlockSpec((1, MLA_HEADS, tm, HEAD_PAD), lambda b, i: (b, 0, i, 0))
    head_sds = jax.ShapeDtypeStruct((bsz, MLA_HEADS, s, HEAD_PAD), BF16)
    if tk >= tm:
        per = tk // tm
        vt_spec = pl.BlockSpec((1, MLA_HEADS, 1, MLA_V, tm), lambda b, i: (b, 0, i // per, 0, i % per))
    else:
        vt_spec = pl.BlockSpec((1, MLA_HEADS, tm // tk, MLA_V, tk), lambda b, i: (b, 0, i, 0, 0))
    vt_sds = jax.ShapeDtypeStruct((bsz, MLA_HEADS, s // tk, MLA_V, tk), BF16)
    stream_sds = jax.ShapeDtypeStruct((bsz, s, w), F32)
    return pl.pallas_call(
        _front_kernel,
        grid=(bsz, s // tm),
        in_specs=[tok(d), pl.BlockSpec((1, 6, d), lambda b, i: (b, 0, 0)), row(d), full(w_in_p),
                  tok(LANES), tok(LANES), row(Q_LORA), row(KV_LORA), full(wq), full(wkv),
                  row(LANES), row(LANES), row(LANES), row(LANES),
                  row(RWKV_SLAB), full(wwh), full(wwl), row(w), row(w), full(g2), row(w), row(w), full(bd)],
        out_specs=[tok(2 * d), head_spec, head_spec, vt_spec] + [tok(w)] * 7,
        out_shape=[jax.ShapeDtypeStruct((bsz, s, 2 * d), BF16), head_sds, head_sds, vt_sds] + [stream_sds] * 7,
        scratch_shapes=[pltpu.VMEM((8, RWKV_SLAB), F32)],
        compiler_params=_params("parallel", "arbitrary"),
        name="front",
    )(x, mod3, norm_w.reshape(1, d), w_in_p, cosf, sinf, qan, kvn, wq, wkv, qg, qgs, kg, kgs,
      mu, wwh, wwl, w0, a0, g2, k_k, k_a, bd)


def _rope_kernel(pos_ref, invf_ref, cos_ref, sin_ref):
    ang = pos_ref[...] * invf_ref[...]
    cos_ref[...] = jnp.cos(ang)
    sin_ref[...] = jnp.sin(ang)


def _rope_table(pos_rep, invf_row):
    n = pos_rep.shape[0]
    tr = min(512, n)
    spec = pl.BlockSpec((tr, LANES), lambda i: (i, 0))
    sds = jax.ShapeDtypeStruct((n, LANES), F32)
    return pl.pallas_call(
        _rope_kernel,
        grid=(n // tr,),
        in_specs=[spec, pl.BlockSpec((1, LANES), lambda i: (0, 0))],
        out_specs=[spec, spec],
        out_shape=[sds, sds],
        compiler_params=_params("parallel"),
        name="rope",
    )(pos_rep, invf_row)


def _mla_prep_body(m, cos_ref, sin_ref, qan_ref, kvn_ref, wq_ref, wkv_ref,
                   qg_ref, qgs_ref, kg_ref, kgs_ref, q_out, k_out, vt_out):
    cq = m[:, 0:Q_LORA]
    ckv = m[:, Q_LORA:Q_LORA + KV_LORA]
    kr = m[:, 384:512]
    krs = m[:, 512:640]
    ql = cq * lax.rsqrt(jnp.mean(cq * cq, axis=-1, keepdims=True) + NORM_EPS) * qan_ref[...]
    kvl = ckv * lax.rsqrt(jnp.mean(ckv * ckv, axis=-1, keepdims=True) + NORM_EPS) * kvn_ref[...]
    qall = _dot(ql.astype(BF16), wq_ref[...])
    kvall = _dot(kvl.astype(BF16), wkv_ref[...])
    cosf = cos_ref[0]
    sinf = sin_ref[0]
    scale = MLA_QK ** -0.5 * LOG2_E
    hw = MLA_HEADS * HEAD_PAD
    for h in range(MLA_HEADS):
        lo, hi = h * HEAD_PAD, (h + 1) * HEAD_PAD
        qh = qall[:, lo:hi]
        qs = qall[:, hw + lo:hw + hi]
        rs = lax.rsqrt(jnp.sum(qh * qh, axis=-1, keepdims=True) * (1.0 / MLA_QK) + NORM_EPS)
        qo = (qh * rs * qg_ref[...]) * cosf + (qs * rs * qgs_ref[...]) * sinf
        q_out[0, h] = (qo * scale).astype(BF16)
        kh = kvall[:, lo:hi] + kr
        rs = lax.rsqrt(jnp.sum(kh * kh, axis=-1, keepdims=True) * (1.0 / MLA_QK) + NORM_EPS)
        ko = (kh * rs * kg_ref[...]) * cosf + (krs * rs * kgs_ref[...]) * sinf
        k_out[0, h] = ko.astype(BF16)
        v_t = kvall[:, hw + lo:hw + hi].T[0:MLA_V, :].astype(BF16)
        width = vt_out.shape[4]
        for c in range(vt_out.shape[2]):
            vt_out[0, h, c] = v_t[:, c * width:(c + 1) * width]
        yield


def _attn_kernel(q_ref, k_ref, vt_ref, o_ref, m_ref, l_ref, acc_ref, *, tq):
    qi = pl.program_id(2)
    heads = range(q_ref.shape[1])
    qs = [q_ref[0, h] for h in heads]
    m_ref[...] = jnp.full_like(m_ref, -jnp.inf)
    l_ref[...] = jnp.zeros_like(l_ref)
    acc_ref[...] = jnp.zeros_like(acc_ref)

    tk = vt_ref.shape[4]
    per = tq // tk

    def step(j, mask):
        off = pl.multiple_of(j * tk, tk)
        sts = [_dot_nt(k_ref[0, h, pl.ds(off, tk), :], qs[h]) for h in heads]
        for h in heads:
            st = sts[h] if mask is None else jnp.where(mask, sts[h], -jnp.inf)
            m = m_ref[h]
            m_new = jnp.maximum(m, jnp.max(st, axis=0, keepdims=True))
            p = jnp.exp2(st - m_new)
            alpha = jnp.exp2(m - m_new)
            m_ref[h] = m_new
            l_ref[h] = alpha * l_ref[h] + jnp.sum(p, axis=0, keepdims=True)
            acc_ref[h] = alpha * acc_ref[h] + _dot(vt_ref[0, h, j], p.astype(BF16))

    def body(j, carry):
        step(j, None)
        return carry

    lax.fori_loop(0, qi * per, body, 0)
    keys = lax.broadcasted_iota(jnp.int32, (tk, tq), 0)
    queries = lax.broadcasted_iota(jnp.int32, (tk, tq), 1)
    for part in range(per):
        step(qi * per + part, keys + part * tk <= queries)
    for h in heads[::2]:
        pair = jnp.concatenate([acc_ref[h] / l_ref[h], acc_ref[h + 1] / l_ref[h + 1]], axis=0)
        o_ref[0, :, h * MLA_V:(h + 2) * MLA_V] = pair.T.astype(BF16)


def _attention(q, k, vt, tq):
    bsz, nh, s, dh = q.shape
    dv, tk = vt.shape[3], vt.shape[4]
    hp = ATTN_HEADS
    return pl.pallas_call(
        functools.partial(_attn_kernel, tq=tq),
        grid=(bsz, nh // hp, s // tq),
        in_specs=[pl.BlockSpec((1, hp, tq, dh), lambda b, h, i: (b, h, i, 0)),
                  pl.BlockSpec((1, hp, s, dh), lambda b, h, i: (b, h, 0, 0)),
                  pl.BlockSpec((1, hp, s // tk, dv, tk), lambda b, h, i: (b, h, 0, 0, 0))],
        out_specs=pl.BlockSpec((1, tq, hp * dv), lambda b, h, i: (b, i, h)),
        out_shape=jax.ShapeDtypeStruct((bsz, s, nh * dv), BF16),
        scratch_shapes=[pltpu.VMEM((hp, 1, tq), F32), pltpu.VMEM((hp, 1, tq), F32),
                        pltpu.VMEM((hp, dv, tq), F32)],
        compiler_params=_params("parallel", "parallel", "parallel"),
        name="attn",
    )(q, k, vt)


def _softplus(x):
    return jnp.maximum(x, 0.0) + jnp.log(1.0 + jnp.exp(-jnp.abs(x)))


def _rwkv_prep_body(slab, first_tile, mu_ref, wwh_ref, wwl_ref, w0_ref, a0_ref, g2_ref, kk_ref, ka_ref, bd_ref,
                    r_out, lw_out, k_out, v_out, a_out, b_out, g_out, last_ref):
    tm = slab.shape[0]
    w = RWKV_WIDTH
    rolled = pltpu.roll(slab, 1, 0)
    rowi = lax.broadcasted_iota(jnp.int32, slab.shape, 0)
    carried = jnp.where(first_tile, 0.0, last_ref[0:1, :])
    prev = jnp.where(rowi == 0, carried, rolled)
    last_ref[0:1, :] = slab[tm - 1:tm, :]
    p = slab + (prev - slab) * mu_ref[...]
    r = p[:, 0:w]
    k = p[:, w:2 * w]
    v = p[:, 2 * w:3 * w]
    r_out[0] = r
    v_out[0] = v
    yield
    wa = p[:, 3 * w:3 * w + LANES]
    g_lo = p[:, 3 * w + LANES:3 * w + 2 * LANES]
    lane = lax.broadcasted_iota(jnp.int32, wa.shape, 1)
    wa = jnp.where(lane < DECAY_LORA, jnp.tanh(wa), wa)
    wah, wal = _split(wa)
    wa_o = _dot(wah, wwh_ref[...]) + (_dot(wah, wwl_ref[...]) + _dot(wal, wwh_ref[...]))
    g_out[0] = _dot(jax.nn.sigmoid(g_lo).astype(BF16), g2_ref[...])
    yield
    log_w = -_softplus(-(w0_ref[...] + wa_o[:, 0:w])) - 0.5
    lw_out[0] = -jnp.exp(log_w)
    yield
    a = jax.nn.sigmoid(a0_ref[...] + wa_o[:, w:2 * w])
    k_out[0] = k * (1.0 + (a - 1.0) * ka_ref[...])
    yield
    kk = k * kk_ref[...]
    ss = _mm_exact_rhs(kk * kk, bd_ref[...])
    kk = kk / jnp.maximum(jnp.sqrt(ss), 1e-12)
    a_out[0] = -kk
    b_out[0] = kk * a
    yield


def _wkv_kernel(r_ref, lw_ref, k_ref, v_ref, a_ref, b_ref, y_ref, s_ref):
    c = WKV_CHUNK
    n = pl.program_id(1)

    @pl.when(n == 0)
    def _():
        s_ref[...] = jnp.zeros_like(s_ref)

    row = lax.broadcasted_iota(jnp.int32, (LANES, LANES), 0)
    col = lax.broadcasted_iota(jnp.int32, (LANES, LANES), 1)
    lower_strict = row > col
    lower_incl = row >= col
    eye = row == col
    same16 = (row >> 4) == (col >> 4)
    same32 = (row >> 5) == (col >> 5)
    ident = jnp.where(eye, 1.0, 0.0).astype(F32)
    tr = lax.broadcasted_iota(jnp.int32, (c, c), 0)
    tc = lax.broadcasted_iota(jnp.int32, (c, c), 1)
    tri = jnp.where(tr >= tc, 1.0, 0.0).astype(BF16)
    first = lax.broadcasted_iota(jnp.int32, (c, LANES), 1) < RWKV_HEAD

    def stack(x):
        return jnp.concatenate([jnp.where(first, x, 0.0), jnp.where(first, 0.0, x)], axis=0)

    def mm(x, y):
        return _dot(x.astype(BF16), y.astype(BF16))

    nb = r_ref.shape[0]
    probs = [(bi, p) for bi in range(nb) for p in range(RWKV_HEADS // 2)]
    each = lambda f, *xs: [f(*args) for args in zip(*xs)]
    load = lambda ref: [ref[bi, :, p * LANES:(p + 1) * LANES] for bi, p in probs]
    r, lw, k, v, a, b = (load(ref) for ref in (r_ref, lw_ref, k_ref, v_ref, a_ref, b_ref))

    def cumsum(x):
        hi = x.astype(BF16)
        rem = x - hi.astype(F32)
        mid = rem.astype(BF16)
        lo = (rem - mid.astype(F32)).astype(BF16)
        return _dot(tri, hi) + (_dot(tri, mid) + _dot(tri, lo))

    cum = each(cumsum, lw)
    cum_c = [x[c - 1:c, :] for x in cum]
    e_in = each(jnp.exp, cum)
    e_neg = each(lambda x: jnp.exp(-x), cum)
    e_end = each(lambda x, xc: jnp.exp(xc - x), cum, cum_c)
    a_s = each(lambda x, cu, l: stack(x * jnp.exp(cu - l)).astype(BF16), a, cum, lw)
    r_s = each(lambda x, e: stack(x * e), r, e_in)
    b_s = each(lambda x, e: stack(x * e).astype(BF16), b, e_neg)
    k_s = each(lambda x, e: stack(x * e).astype(BF16), k, e_neg)
    b_e = each(lambda x, e: stack(x * e).T.astype(BF16), b, e_end)
    k_e = each(lambda x, e: stack(x * e).T.astype(BF16), k, e_end)
    v_s = each(lambda x: stack(x).astype(BF16), v)

    tt = each(lambda x1, x2, y1, y2: _dot_nt(jnp.concatenate([x1, x2.astype(BF16)], axis=0),
                                             jnp.concatenate([y1, y2], axis=0)), a_s, r_s, b_s, k_s)
    d_ab = [jnp.where(lower_strict, x[0:LANES, 0:LANES], 0.0) for x in tt]
    e_ak = [jnp.where(lower_strict, x[0:LANES, LANES:], 0.0).astype(BF16) for x in tt]
    f_rb = [jnp.where(lower_incl, x[LANES:, 0:LANES], 0.0).astype(BF16) for x in tt]
    f_rk = [jnp.where(lower_incl, x[LANES:, LANES:], 0.0).astype(BF16) for x in tt]

    d16 = [jnp.where(same16, x, 0.0) for x in d_ab]
    d32 = [jnp.where(same32, x, 0.0) for x in d_ab]
    z = each(_dot, e_ak, v_s)
    x2 = each(mm, d16, d16)
    x4 = each(mm, x2, x2)
    x8 = each(mm, x4, x4)
    t = [ident + x for x in d16]
    t = each(lambda t_, x: t_ + mm(t_, x), t, x2)
    t = each(lambda t_, x: t_ + mm(t_, x), t, x4)
    t = each(lambda t_, x: t_ + mm(t_, x), t, x8)
    t = each(lambda t_, hi, lo: t_ + mm(mm(t_, hi - lo), t_), t, d32, d16)
    t = each(lambda t_, hi, lo: t_ + mm(mm(t_, hi - lo), t_), t, d_ab, d32)

    au = each(lambda t_, x, zz: mm(t_, jnp.concatenate([x, zz.astype(BF16)], axis=1)).astype(BF16),
              t, a_s, z)
    g = each(_dot, f_rb, au)
    y0 = each(lambda gg, f, vv: gg[:, LANES:] + _dot(f, vv), g, f_rk, v_s)
    r1 = each(lambda x, gg: x + gg[:, 0:LANES], r_s, g)
    hmat = each(_dot, b_e, au)
    m_mat = each(lambda h, xc: jnp.where(eye, jnp.exp(xc), 0.0) + h[:, 0:LANES], hmat, cum_c)
    n_mat = each(lambda h, ke, vv: h[:, LANES:] + _dot(ke, vv), hmat, k_e, v_s)

    for i, (bi, p) in enumerate(probs):
        s0 = s_ref[i]
        s0b = s0.astype(BF16)
        ys = _dot(r1[i].astype(BF16), s0b) + y0[i]
        s_ref[i] = _dot(m_mat[i].astype(BF16), s0b) + n_mat[i]
        y_ref[bi, :, p * LANES:(p + 1) * LANES] = ys[0:c, :] + ys[c:, :]


def _wkv(r, lw, k, v, a, b, nb):
    bsz, s, w = r.shape
    spec = pl.BlockSpec((nb, WKV_CHUNK, w), lambda bi, n: (bi, n, 0))
    return pl.pallas_call(
        _wkv_kernel,
        grid=(bsz // nb, s // WKV_CHUNK),
        in_specs=[spec] * 6,
        out_specs=spec,
        out_shape=jax.ShapeDtypeStruct((bsz, s, w), F32),
        scratch_shapes=[pltpu.VMEM((nb * RWKV_HEADS // 2, LANES, LANES), F32)],
        compiler_params=_params("parallel", "arbitrary"),
        name="wkv",
    )(r, lw, k, v, a, b)


def _post_kernel(x_ref, mod_ref, y_ref, r_ref, k_ref, v_ref, g_ref, o_ref, gates_ref,
                 lnw_ref, lnb_ref, rk_ref, bd_ref, wor_ref, woa_ref, wout_ref, nfw_ref,
                 wrh_ref, wrl_ref, br_ref,
                 x1_ref, h2_ref, route_ref, cnt_ref):
    tiles = [_post_tile(part, x_ref, mod_ref, y_ref, r_ref, k_ref, v_ref, g_ref, o_ref, gates_ref,
                        lnw_ref, lnb_ref, rk_ref, bd_ref, wor_ref, woa_ref, wout_ref, nfw_ref,
                        wrh_ref, wrl_ref, br_ref, x1_ref, h2_ref, route_ref, cnt_ref)
             for part in range(x_ref.shape[1] // ROUTE_TILE)]
    for _ in zip(*tiles):
        pass


def _post_tile(part, x_ref, mod_ref, y_ref, r_ref, k_ref, v_ref, g_ref, o_ref, gates_ref,
               lnw_ref, lnb_ref, rk_ref, bd_ref, wor_ref, woa_ref, wout_ref, nfw_ref,
               wrh_ref, wrl_ref, br_ref, x1_ref, h2_ref, route_ref, cnt_ref):
    tm = ROUTE_TILE
    rows = slice(part * tm, (part + 1) * tm)
    d = x_ref.shape[2]
    bd = bd_ref[...]
    inv_n = 1.0 / RWKV_HEAD
    y = y_ref[0, rows, :]
    seg = lambda t: _dot(t.astype(BF16), bd)
    mu = seg(y) * inv_n
    dlt = y - mu
    var = seg(dlt * dlt) * inv_n
    yn = dlt * lax.rsqrt(var + GN_EPS) * lnw_ref[...] + lnb_ref[...]
    yield
    v = v_ref[0, rows, :]
    bonus = seg(r_ref[0, rows, :] * k_ref[0, rows, :] * rk_ref[...]) * v
    z = (yn + bonus) * g_ref[0, rows, :]
    o_b = _dot(z.astype(BF16), wor_ref[...])
    o_a = _dot(o_ref[0, rows, :], woa_ref[...])
    yield
    ga = jax.nn.sigmoid(gates_ref[0, rows, 0:d].astype(F32))
    gb = jax.nn.sigmoid(gates_ref[0, rows, d:2 * d].astype(F32))
    merged = ga * o_a + gb * o_b
    mix = _dot(merged.astype(BF16), wout_ref[...])
    yield
    x1 = x_ref[0, rows, :] + mod_ref[0, 2:3, :] * mix
    x1_ref[0, rows, :] = x1
    ms = jnp.mean(x1 * x1, axis=-1, keepdims=True)
    h2 = x1 * lax.rsqrt(ms + NORM_EPS) * nfw_ref[...]
    h2 = h2 * (1.0 + mod_ref[0, 4:5, :]) + mod_ref[0, 3:4, :]
    h2_ref[0, rows, :] = h2
    yield

    hh, hl = _split(h2)
    logits = _dot(hh, wrh_ref[...]) + (_dot(hh, wrl_ref[...]) + _dot(hl, wrh_ref[...])) + br_ref[...]
    lane = lax.broadcasted_iota(jnp.int32, (tm, LANES), 1)
    lanef = lane.astype(F32)
    cur = jnp.where(lane < N_EXPERTS, logits, -jnp.inf)
    yield
    vals, idxs, hots = [], [], []
    for _ in range(TOP_K):
        mx = jnp.max(cur, axis=-1, keepdims=True)
        idx = jnp.min(jnp.where(cur == mx, lanef, float(LANES)), axis=-1, keepdims=True)
        hot = lanef == idx
        cur = jnp.where(hot, -jnp.inf, cur)
        vals.append(mx)
        idxs.append(idx)
        hots.append(hot)
    exps = [jnp.exp(vv - vals[0]) for vv in vals]
    den = exps[0] + exps[1] + exps[2] + exps[3]
    yield
    sel = jnp.zeros((tm, LANES), F32)
    for hot in hots:
        sel = sel + jnp.where(hot, 1.0, 0.0)
    ri = lax.broadcasted_iota(jnp.int32, (tm, tm), 0)
    ci = lax.broadcasted_iota(jnp.int32, (tm, tm), 1)
    below = jnp.where(ri > ci, 1.0, 0.0).astype(BF16)
    before = _dot(below, sel.astype(BF16))
    counts = jnp.broadcast_to(jnp.sum(sel, axis=0, keepdims=True), (8, LANES))
    er = lax.broadcasted_iota(jnp.int32, (LANES, LANES), 0)
    ec = lax.broadcasted_iota(jnp.int32, (LANES, LANES), 1)
    lower = jnp.where(er < ec, 1.0, 0.0).astype(BF16)
    start = _dot(counts.astype(BF16), lower)[0:1, :]
    route = jnp.zeros((tm, LANES), F32)
    for j in range(TOP_K):
        pos = jnp.sum(jnp.where(hots[j], before + start, 0.0), axis=-1, keepdims=True)
        route = jnp.where(lane == j, pos, route)
        route = jnp.where(lane == TOP_K + j, exps[j] / den, route)
    route_ref[0, rows, :] = route
    cnt_ref[0, part] = counts
    yield


def _post(x, mod3, y, r, k, v, g, o, gates, lnw, lnb, rk, bd, wor, woa, wout, nfw, wrh, wrl, br, tm):
    bsz, s, d = x.shape
    w = RWKV_WIDTH
    parts = tm // ROUTE_TILE
    tok = lambda n: pl.BlockSpec((1, tm, n), lambda b, i: (b, i, 0))
    row = lambda n: pl.BlockSpec((1, n), lambda b, i: (0, 0))
    full = lambda a: pl.BlockSpec(a.shape, lambda b, i: (0,) * a.ndim)
    return pl.pallas_call(
        _post_kernel,
        grid=(bsz, s // tm),
        in_specs=[tok(d), pl.BlockSpec((1, 6, d), lambda b, i: (b, 0, 0)),
                  tok(w), tok(w), tok(w), tok(w), tok(w),
                  tok(MLA_HEADS * MLA_V), tok(2 * d),
                  row(w), row(w), row(w), full(bd), full(wor), full(woa), full(wout), row(d),
                  full(wrh), full(wrl), row(LANES)],
        out_specs=[tok(d), tok(d), tok(LANES), pl.BlockSpec((1, parts, 8, LANES), lambda b, i: (b, i, 0, 0))],
        out_shape=[jax.ShapeDtypeStruct((bsz, s, d), F32),
                   jax.ShapeDtypeStruct((bsz, s, d), F32),
                   jax.ShapeDtypeStruct((bsz, s, LANES), F32),
                   jax.ShapeDtypeStruct((bsz, s // ROUTE_TILE, 8, LANES), F32)],
        compiler_params=_params("parallel", "parallel"),
        name="post",
    )(x, mod3, y, r, k, v, g, o, gates, lnw, lnb, rk, bd, wor, woa, wout, nfw, wrh, wrl, br)


def _load_rows(ref, n):
    nc = ref.shape[0] // n
    return jnp.concatenate([ref[pl.ds(c, n, stride=nc), :] for c in range(nc)], axis=1)


def _store_rows(ref, val):
    nc = val.shape[1] // LANES
    for c in range(nc):
        ref[pl.ds(c, val.shape[0], stride=nc), :] = val[:, c * LANES:(c + 1) * LANES]


def _run_copies(tile, cnt_ref, off_ref, dst_ref, tm, make_copy):
    def per_expert(e, carry):
        idx = tile * N_EXPERTS + e
        o = off_ref[idx]
        d0 = dst_ref[idx]
        _pieces(cnt_ref[idx], tm, lambda done, size: make_copy(o + done, d0 + done, size).start())
        return carry

    lax.fori_loop(0, N_EXPERTS, per_expert, 0)


def _pieces(count, limit, fn):
    def emit(bits):
        for b in bits:
            size = 1 << b
            done = count & ~(2 * size - 1)

            @pl.when((count & size) != 0)
            def _():
                fn(done, size)

    bits = range(limit.bit_length() - 1, -1, -1)
    large = [b for b in bits if b >= RARE_PIECE_BIT]
    if large:
        @pl.when(count >= (1 << RARE_PIECE_BIT))
        def _():
            emit(large)
    emit([b for b in bits if b < RARE_PIECE_BIT])


def _zero_fill(gap_ref, xs_ref, zero_ref, sem, nc):
    bm = MOE_BLOCK
    zero_ref[...] = jnp.zeros_like(zero_ref)

    def piece(dst, size):
        return pltpu.make_async_copy(zero_ref.at[pl.ds(0, size * nc), :],
                                     xs_ref.at[pl.ds(pl.multiple_of(dst * nc, nc), size * nc), :], sem)

    def sweep(act):
        def per_expert(e, carry):
            g0 = gap_ref[e]
            _pieces(gap_ref[N_EXPERTS + e], bm, lambda done, size: act(piece(g0 + done, size)))
            return carry

        def tail(blk, carry):
            act(piece(gap_ref[2 * N_EXPERTS] + blk * bm, bm))
            return carry

        lax.fori_loop(0, N_EXPERTS, per_expert, 0)
        lax.fori_loop(0, gap_ref[2 * N_EXPERTS + 1], tail, 0)

    sweep(lambda cp: cp.start())
    sweep(lambda cp: cp.wait())


def _dispatch_kernel(cnt_ref, off_ref, dst_ref, gap_ref, h_ref, route_ref, xs_ref, sorted_ref, zero_ref,
                     sem, zero_sem):
    tm = h_ref.shape[0]

    @pl.when(pl.program_id(0) == 0)
    def _():
        _zero_fill(gap_ref, xs_ref, zero_ref, zero_sem, h_ref.shape[1] // LANES)

    pos_t = route_ref[...].T
    slot = lax.broadcasted_iota(jnp.int32, (TOP_K * tm, tm), 0).astype(F32)
    perm = jnp.where(slot == pos_t[0:1, :], 1.0, 0.0)
    for j in range(1, TOP_K):
        perm = perm + jnp.where(slot == pos_t[j:j + 1, :], 1.0, 0.0)
    i = pl.program_id(0)
    cur = i % 2
    buf = sorted_ref.at[cur]
    _store_rows(buf, _dot(perm.astype(BF16), h_ref[...].astype(BF16)))
    nc = h_ref.shape[1] // LANES

    def make_copy(src, dst, size):
        return pltpu.make_async_copy(buf.at[pl.ds(pl.multiple_of(src * nc, nc), size * nc), :],
                                     xs_ref.at[pl.ds(pl.multiple_of(dst * nc, nc), size * nc), :], sem.at[cur])

    _run_copies(i, cnt_ref, off_ref, dst_ref, tm, make_copy)

    def drain(which):
        pltpu.make_async_copy(sorted_ref.at[which], xs_ref.at[pl.ds(0, TOP_K * tm * nc), :], sem.at[which]).wait()

    @pl.when(i > 0)
    def _():
        drain(1 - cur)

    @pl.when(i == pl.num_programs(0) - 1)
    def _():
        drain(cur)


def _dispatch(cnt_tab, off_tab, dst_tab, gap_tab, h2, route, n_rows, tm):
    t, d = h2.shape
    nc = d // LANES
    return pl.pallas_call(
        _dispatch_kernel,
        grid_spec=pltpu.PrefetchScalarGridSpec(
            num_scalar_prefetch=4,
            grid=(t // tm,),
            in_specs=[pl.BlockSpec((tm, d), lambda i, *_: (i, 0)),
                      pl.BlockSpec((tm, LANES), lambda i, *_: (i, 0))],
            out_specs=pl.BlockSpec(memory_space=pl.ANY),
            scratch_shapes=[pltpu.VMEM((2, TOP_K * tm * nc, LANES), F32), pltpu.VMEM((MOE_BLOCK * nc, LANES), F32),
                            pltpu.SemaphoreType.DMA((2,)), pltpu.SemaphoreType.DMA(())]),
        out_shape=jax.ShapeDtypeStruct((n_rows * nc, LANES), F32),
        compiler_params=_params("arbitrary"),
        name="dispatch",
    )(cnt_tab, off_tab, dst_tab, gap_tab, h2, route)


def _split_gate_up_kernel(w_ref, g_ref, u_ref, t_ref):
    half = LANES // 2
    nblk = w_ref.shape[1] // LANES
    for c in range(w_ref.shape[2] // LANES):
        w_t = w_ref[0, :, c * LANES:(c + 1) * LANES].T
        for b in range(nblk):
            t_ref[b] = w_t[:, b * LANES:(b + 1) * LANES]
        for out_ref, first in ((g_ref, 0), (u_ref, 1)):
            rows = [t_ref[b, pl.ds(first, half, stride=2), :] for b in range(nblk)]
            out_ref[0, c * half:(c + 1) * half, :] = jnp.concatenate(rows, axis=1).astype(BF16)


def _split_gate_up(w):
    e, d, ff2 = w.shape
    ff = ff2 // 2
    spec = pl.BlockSpec((1, ff, d), lambda i: (i, 0, 0))
    sds = jax.ShapeDtypeStruct((e, ff, d), BF16)
    return pl.pallas_call(
        _split_gate_up_kernel,
        grid=(e,),
        in_specs=[pl.BlockSpec((1, d, ff2), lambda i: (i, 0, 0))],
        out_specs=[spec, spec],
        out_shape=[sds, sds],
        scratch_shapes=[pltpu.VMEM((d // LANES, LANES, LANES), F32)],
        compiler_params=_params("parallel"),
        name="split_gate_up",
    )(w)


def _moe_kernel(blk_e_ref, valid_ref, xs_ref, wg_ref, wu_ref, bg_ref, bu_ref, wd_ref, bd_ref, ys_ref):
    del blk_e_ref
    i = pl.program_id(0)
    valid = valid_ref[i]

    @pl.when(valid > 0)
    def _():
        x = _load_rows(xs_ref, MOE_BLOCK).astype(BF16)
        gate = _dot_nt(x, wg_ref[0]) + bg_ref[0]
        up = _dot_nt(x, wu_ref[0]) + bu_ref[0]
        gate = jnp.minimum(gate, SWIGLU_LIMIT)
        up = jnp.clip(up, -SWIGLU_LIMIT, SWIGLU_LIMIT)
        act = (up + 1.0) * gate * jax.nn.sigmoid(SWIGLU_ALPHA * gate)
        _store_rows(ys_ref, _dot(act.astype(BF16), wd_ref[0]) + bd_ref[0])

    @pl.when(valid == 0)
    def _():
        ys_ref[...] = jnp.zeros_like(ys_ref)


def _moe(blk_e, blk_valid, xs, wg, wu, bg, bu, wd, bd):
    ff, d = wg.shape[1], wg.shape[2]
    nc = d // LANES
    n_rows = xs.shape[0] // nc
    bm = MOE_BLOCK
    wspec = lambda k, n: pl.BlockSpec((1, k, n), lambda i, be, nu: (be[i], 0, 0))
    return pl.pallas_call(
        _moe_kernel,
        grid_spec=pltpu.PrefetchScalarGridSpec(
            num_scalar_prefetch=2,
            grid=(n_rows // bm,),
            in_specs=[pl.BlockSpec((bm * nc, LANES), lambda i, be, nu: (i, 0)),
                      wspec(ff, d), wspec(ff, d), wspec(1, ff), wspec(1, ff), wspec(ff, d), wspec(1, d)],
            out_specs=pl.BlockSpec((bm * nc, LANES), lambda i, be, nu: (i, 0))),
        out_shape=jax.ShapeDtypeStruct((n_rows * nc, LANES), F32),
        compiler_params=_params("arbitrary"),
        name="moe",
    )(blk_e, blk_valid, xs, wg, wu, bg, bu, wd, bd)


def _combine_kernel(cnt_ref, off_ref, dst_ref, x1_ref, route_ref, mod_ref, ys_ref, o_ref, rows_ref, sem):
    tm = x1_ref.shape[1]
    tile = pl.program_id(0) * pl.num_programs(1) + pl.program_id(1)
    n_tiles = pl.num_programs(0) * pl.num_programs(1)
    nc = x1_ref.shape[2] // LANES
    cur = tile % 2

    def fetch(which_tile):
        which = which_tile % 2

        def make_copy(dst, src, size):
            return pltpu.make_async_copy(
                ys_ref.at[pl.ds(pl.multiple_of(src * nc, nc), size * nc), :],
                rows_ref.at[which, pl.ds(pl.multiple_of(dst * nc, nc), size * nc), :], sem.at[which])

        _run_copies(which_tile, cnt_ref, off_ref, dst_ref, tm, make_copy)

    @pl.when(tile == 0)
    def _():
        fetch(tile)

    @pl.when(tile + 1 < n_tiles)
    def _():
        fetch(tile + 1)

    route = route_ref[0]
    slot = lax.broadcasted_iota(jnp.int32, (tm, TOP_K * tm), 1).astype(F32)
    mix = jnp.where(slot == route[:, 0:1], route[:, TOP_K:TOP_K + 1], 0.0)
    for j in range(1, TOP_K):
        mix = mix + jnp.where(slot == route[:, j:j + 1], route[:, TOP_K + j:TOP_K + j + 1], 0.0)
    pltpu.make_async_copy(ys_ref.at[pl.ds(0, TOP_K * tm * nc), :], rows_ref.at[cur], sem.at[cur]).wait()
    acc = _dot(mix.astype(BF16), _load_rows(rows_ref.at[cur], TOP_K * tm).astype(BF16))
    o_ref[0] = x1_ref[0] + mod_ref[0, 5:6, :] * acc


def _combine(cnt_tab, off_tab, dst_tab, x1, route, mod3, ys, tm):
    bsz, s, d = x1.shape
    return pl.pallas_call(
        _combine_kernel,
        grid_spec=pltpu.PrefetchScalarGridSpec(
            num_scalar_prefetch=3,
            grid=(bsz, s // tm),
            in_specs=[pl.BlockSpec((1, tm, d), lambda b, i, *_: (b, i, 0)),
                      pl.BlockSpec((1, tm, LANES), lambda b, i, *_: (b, i, 0)),
                      pl.BlockSpec((1, 6, d), lambda b, i, *_: (b, 0, 0)),
                      pl.BlockSpec(memory_space=pl.ANY)],
            out_specs=pl.BlockSpec((1, tm, d), lambda b, i, *_: (b, i, 0)),
            scratch_shapes=[pltpu.VMEM((2, TOP_K * tm * d // LANES, LANES), F32), pltpu.SemaphoreType.DMA((2,))]),
        out_shape=jax.ShapeDtypeStruct((bsz, s, d), F32),
        compiler_params=_params("arbitrary", "arbitrary"),
        name="combine",
    )(cnt_tab, off_tab, dst_tab, x1, route, mod3, ys)


def _pad_cols(a, n):
    return jnp.pad(a, ((0, 0), (0, n - a.shape[1])))


def _head_blocks(cols_main, cols_rot=None):
    k = cols_main.shape[0]
    out = jnp.zeros((k, MLA_HEADS, HEAD_PAD), F32)
    out = out.at[:, :, :cols_main.shape[2]].set(cols_main)
    return out.reshape(k, MLA_HEADS * HEAD_PAD)


def _layer(x, cond_mod, positions, w_in, q_a_norm_w, w_q_up, kv_a_norm_w, w_kv_up, q_norm_w, k_norm_w,
           w_o_mla, rwkv_mu, rwkv_w0, rwkv_w2, rwkv_a0, rwkv_a2, rwkv_g2, rwkv_k_k, rwkv_k_a, rwkv_r_k,
           rwkv_ln_w, rwkv_ln_b, rwkv_w_o, w_out, norm_mix_w, norm_ffn_w, w_router, b_router,
           w_gate_up, b_gate_up, w_down, b_down):
    bsz, s, d = x.shape
    t = bsz * s
    half = MLA_ROPE // 2
    nope, qk = MLA_NOPE, MLA_QK
    mod3 = cond_mod.reshape(bsz, 6, d)

    o_q, o_kv, o_kr = 0, Q_LORA, Q_LORA + KV_LORA
    o_slab = o_kr + MLA_ROPE
    o_gate = o_slab + RWKV_SLAB
    kr_w = w_in[:, o_kr:o_slab]
    zeros = lambda n: jnp.zeros((d, n), F32)
    kr_blk = jnp.concatenate([zeros(nope), kr_w, zeros(HEAD_PAD - qk)], axis=1)
    kr_rot = jnp.concatenate([zeros(nope), -kr_w[:, half:], kr_w[:, :half], zeros(HEAD_PAD - qk)], axis=1)
    w_in_p = jnp.concatenate([w_in[:, o_q:o_kr], kr_blk, kr_rot, w_in[:, o_slab:]], axis=1).astype(BF16)

    tm = min(256, s)

    wq3 = w_q_up.reshape(Q_LORA, MLA_HEADS, qk)
    wq_rot = jnp.concatenate([jnp.zeros((Q_LORA, MLA_HEADS, nope), F32), -wq3[:, :, nope + half:],
                              wq3[:, :, nope:nope + half]], axis=2)
    wq = jnp.concatenate([_head_blocks(wq3), _head_blocks(wq_rot)], axis=1).astype(BF16)
    wkv3 = w_kv_up.reshape(KV_LORA, MLA_HEADS, nope + MLA_V)
    wkv = jnp.concatenate([_head_blocks(wkv3[:, :, :nope]), _head_blocks(wkv3[:, :, nope:])], axis=1).astype(BF16)

    def gains(wn):
        main = jnp.pad(wn, (0, HEAD_PAD - qk)).reshape(1, HEAD_PAD)
        rot = jnp.concatenate([jnp.zeros((nope,), F32), wn[nope + half:], wn[nope:nope + half],
                               jnp.zeros((HEAD_PAD - qk,), F32)]).reshape(1, HEAD_PAD)
        return main, rot

    qg, qgs = gains(q_norm_w)
    kg, kgs = gains(k_norm_w)
    inv_freq = ROPE_THETA ** (-jnp.arange(half, dtype=F32) / half)
    per_row = LANES // half
    pos_rep = jnp.repeat(positions.astype(F32).reshape(t // per_row, per_row), half, axis=1)
    cos16, sin16 = _rope_table(pos_rep, jnp.tile(inv_freq, per_row).reshape(1, LANES))
    cos16 = cos16.reshape(bsz, s, half)
    sin16 = sin16.reshape(bsz, s, half)
    cosf = jnp.concatenate([jnp.ones((bsz, s, nope), F32), cos16, cos16,
                            jnp.ones((bsz, s, HEAD_PAD - qk), F32)], axis=-1)
    sinf = jnp.concatenate([jnp.zeros((bsz, s, nope), F32), sin16, sin16,
                            jnp.zeros((bsz, s, HEAD_PAD - qk), F32)], axis=-1)

    w = RWKV_WIDTH
    wwa = jnp.zeros((LANES, 2 * w), F32)
    wwa = wwa.at[:DECAY_LORA, :w].set(rwkv_w2).at[DECAY_LORA:, w:].set(rwkv_a2)
    wwh = wwa.astype(BF16)
    wwl = (wwa - wwh.astype(F32)).astype(BF16)
    hid = np.arange(w) // RWKV_HEAD
    bd = jnp.asarray(hid[:, None] == hid[None, :], BF16)
    gates, q, k, vt, r_, lw_, k_, v_, a_, b_, g_ = _front(
        x, mod3, norm_mix_w, w_in_p, cosf, sinf, q_a_norm_w.reshape(1, -1), kv_a_norm_w.reshape(1, -1),
        wq, wkv, qg, qgs, kg, kgs, rwkv_mu.reshape(1, -1), wwh, wwl, rwkv_w0.reshape(1, -1),
        rwkv_a0.reshape(1, -1), rwkv_g2.astype(BF16), rwkv_k_k.reshape(1, -1), rwkv_k_a.reshape(1, -1), bd,
        tm, min(ATTN_KEY_BLOCK, s))
    o = _attention(q, k, vt, min(ATTN_BLOCK, s))
    y = _wkv(r_, lw_, k_, v_, a_, b_, WKV_BATCH if bsz % WKV_BATCH == 0 else 1)

    woa = w_o_mla.astype(BF16)
    wr =_pad_cols(w_router, LANES)
    wrh = wr.astype(BF16)
    wrl = (wr - wrh.astype(F32)).astype(BF16)
    br = jnp.pad(b_router, (0, LANES - N_EXPERTS)).reshape(1, LANES)
    x1, h2, route, counts = _post(
        x, mod3, y, r_, k_, v_, g_, o, gates, rwkv_ln_w.reshape(1, -1), rwkv_ln_b.reshape(1, -1),
        rwkv_r_k.reshape(1, -1), bd, rwkv_w_o.astype(BF16), woa, w_out.astype(BF16),
        norm_ffn_w.reshape(1, -1), wrh, wrl, br, POST_TILE if s % POST_TILE == 0 else ROUTE_TILE)

    bm = MOE_BLOCK
    n_rows = t * TOP_K + N_EXPERTS * bm
    n_blocks = n_rows // bm
    cnt_tab = counts[:, :, 0, :N_EXPERTS].reshape(t // ROUTE_TILE, N_EXPERTS).astype(jnp.int32)
    total = jnp.sum(cnt_tab, axis=0)
    padded = (total + bm - 1) // bm * bm
    pad_end = jnp.cumsum(padded)
    pad_start = pad_end - padded
    off_tab = jnp.cumsum(cnt_tab, axis=1) - cnt_tab
    dst_tab = pad_start[None, :] + jnp.cumsum(cnt_tab, axis=0) - cnt_tab
    blk_start = jnp.arange(n_blocks, dtype=jnp.int32) * bm
    blk_e = jnp.minimum(jnp.sum((pad_end[None, :] <= blk_start[:, None]).astype(jnp.int32), axis=1),
                        N_EXPERTS - 1)
    blk_valid = jnp.clip((pad_start + total)[blk_e] - blk_start, 0, bm).astype(jnp.int32)
    tabs = (cnt_tab.reshape(-1), off_tab.reshape(-1).astype(jnp.int32), dst_tab.reshape(-1).astype(jnp.int32))
    gap_tab = jnp.concatenate([pad_start + total, padded - total, pad_end[-1:],
                               (n_rows - pad_end[-1:]) // bm]).astype(jnp.int32)

    xs = _dispatch(*tabs, gap_tab, h2.reshape(t, d), route.reshape(t, LANES), n_rows, ROUTE_TILE)
    wg_t, wu_t = _split_gate_up(w_gate_up)
    bgu = b_gate_up.reshape(N_EXPERTS, 1, D_FF, 2)
    ys = _moe(blk_e, blk_valid, xs, wg_t, wu_t,
              bgu[..., 0], bgu[..., 1], w_down.astype(BF16), b_down.reshape(N_EXPERTS, 1, d))
    return _combine(*tabs, x1, route, mod3, ys, ROUTE_TILE)


def kernel(x, c, positions, ada_w, ada_b, norm_mix_w, norm_ffn_w, w_in, q_a_norm_w, w_q_up, kv_a_norm_w, w_kv_up, q_norm_w, k_norm_w, w_o_mla, rwkv_mu, rwkv_w0, rwkv_w2, rwkv_a0, rwkv_a2, rwkv_g2, rwkv_k_k, rwkv_k_a, rwkv_r_k, rwkv_ln_w, rwkv_ln_b, rwkv_w_o, w_out, w_router, b_router, w_gate_up, b_gate_up, w_down, b_down):
    depth = ada_w.shape[0]
    for l in range(depth):
        mod = _ada(c, ada_w[l], ada_b[l])
        x = _layer(x, mod, positions, w_in[l], q_a_norm_w[l], w_q_up[l], kv_a_norm_w[l], w_kv_up[l],
                   q_norm_w[l], k_norm_w[l], w_o_mla[l], rwkv_mu[l], rwkv_w0[l], rwkv_w2[l], rwkv_a0[l],
                   rwkv_a2[l], rwkv_g2[l], rwkv_k_k[l], rwkv_k_a[l], rwkv_r_k[l], rwkv_ln_w[l],
                   rwkv_ln_b[l], rwkv_w_o[l], w_out[l], norm_mix_w[l], norm_ffn_w[l], w_router[l],
                   b_router[l], w_gate_up[l], b_gate_up[l], w_down[l], b_down[l])
    return x
```

```python
import functools

import numpy as np
import jax
import jax.numpy as jnp
from jax import lax
from jax.experimental import pallas as pl
from jax.experimental.pallas import tpu as pltpu

F32 = jnp.float32
BF16 = jnp.bfloat16

D_MODEL = 1024
MLA_HEADS = 8
MLA_NOPE = 64
MLA_ROPE = 32
MLA_QK = MLA_NOPE + MLA_ROPE
MLA_V = 64
VT_ROWS = 80
Q_LORA = 256
KV_LORA = 128
ROPE_THETA = 10000.0
RWKV_HEADS = 8
RWKV_HEAD = 64
RWKV_WIDTH = RWKV_HEADS * RWKV_HEAD
DECAY_LORA = 64
AAA_LORA = 64
GATE_LORA = 128
RWKV_SLAB = 3 * RWKV_WIDTH + DECAY_LORA + AAA_LORA + GATE_LORA
GN_EPS = 64e-5
N_EXPERTS = 32
TOP_K = 4
D_FF = D_MODEL
SWIGLU_LIMIT = 7.0
SWIGLU_ALPHA = 1.702
NORM_EPS = 1e-6
LOG2_E = 1.4426950408889634

LANES = 128
HEAD_PAD = 128
MLA_COLS = 640
WKV_CHUNK = 64
WKV_BATCH = 4
ATTN_BLOCK = 512
ATTN_KEY_BLOCK = 512
ATTN_HEADS = 8
ROUTE_TILE = 256
POST_TILE = 512
RARE_PIECE_BIT = 6
MOE_BLOCK = 512
VMEM_LIMIT = 56 * 1024 * 1024


def _dot(a, b):
    return jnp.dot(a, b, preferred_element_type=F32)


def _dot_nt(a, b):
    return lax.dot_general(a, b, (((1,), (1,)), ((), ())), preferred_element_type=F32)


def _split(x):
    hi = x.astype(BF16)
    lo = (x - hi.astype(F32)).astype(BF16)
    return hi, lo


def _mm3(a, b):
    ah, al = _split(a)
    bh, bl = _split(b)
    return _dot(ah, bh) + (_dot(ah, bl) + _dot(al, bh))


def _mm3_nt(a, b):
    ah, al = _split(a)
    bh, bl = _split(b)
    return _dot_nt(ah, bh) + (_dot_nt(ah, bl) + _dot_nt(al, bh))


def _mm_exact_rhs(a, b_bf16):
    ah, al = _split(a)
    return _dot(ah, b_bf16) + _dot(al, b_bf16)


def _trace_in_turn(stages):
    live = list(stages)
    while live:
        for gen in list(live):
            try:
                next(gen)
            except StopIteration:
                live.remove(gen)


def _params(*sem):
    return pltpu.CompilerParams(dimension_semantics=sem, vmem_limit_bytes=VMEM_LIMIT)


def _ada_kernel(c_ref, w_ref, b_ref, o_ref):
    c = c_ref[...]
    cond = c * jax.nn.sigmoid(c)
    o_ref[...] = _mm3(cond, w_ref[...]) + b_ref[...]


def _ada(c, w, b):
    bsz, d = c.shape
    n = w.shape[1]
    tn = 1024
    return pl.pallas_call(
        _ada_kernel,
        grid=(n // tn,),
        in_specs=[pl.BlockSpec((bsz, d), lambda j: (0, 0)),
                  pl.BlockSpec((d, tn), lambda j: (0, j)),
                  pl.BlockSpec((1, tn), lambda j: (0, j))],
        out_specs=pl.BlockSpec((bsz, tn), lambda j: (0, j)),
        out_shape=jax.ShapeDtypeStruct((bsz, n), F32),
        compiler_params=_params("parallel"),
        name="ada",
    )(c, w, b.reshape(1, n))


def _front_kernel(x_ref, mod_ref, nw_ref, w_ref,
                  cos_ref, sin_ref, qan_ref, kvn_ref, wq_ref, wkv_ref, qg_ref, qgs_ref, kg_ref, kgs_ref,
                  mu_ref, wwh_ref, wwl_ref, w0_ref, a0_ref, g2_ref, kk_ref, ka_ref, bd_ref,
                  gates_ref, q_out, k_out, vt_out, r_out, lw_out, k2_out, v_out, a_out, b_out, g_out,
                  last_ref):
    x = x_ref[0]
    ms = jnp.mean(x * x, axis=-1, keepdims=True)
    y = x * lax.rsqrt(ms + NORM_EPS) * nw_ref[...]
    h = y * (1.0 + mod_ref[0, 1:2, :]) + mod_ref[0, 0:1, :]
    hb = h.astype(BF16)
    mla = _dot(hb, w_ref[:, 0:MLA_COLS])
    slab = _dot(hb, w_ref[:, MLA_COLS:MLA_COLS + RWKV_SLAB])
    gates_ref[0] = _dot(hb, w_ref[:, MLA_COLS + RWKV_SLAB:]).astype(BF16)
    mla_prep = _mla_prep_body(mla, cos_ref, sin_ref, qan_ref, kvn_ref, wq_ref, wkv_ref,
                              qg_ref, qgs_ref, kg_ref, kgs_ref, q_out, k_out, vt_out)
    rwkv_prep = _rwkv_prep_body(slab, pl.program_id(1) == 0, mu_ref, wwh_ref, wwl_ref, w0_ref, a0_ref, g2_ref,
                                kk_ref, ka_ref, bd_ref, r_out, lw_out, k2_out, v_out, a_out, b_out, g_out,
                                last_ref)
    _trace_in_turn([mla_prep, rwkv_prep])


def _front(x, mod3, norm_w, w_in_p, cosf, sinf, qan, kvn, wq, wkv, qg, qgs, kg, kgs,
           mu, wwh, wwl, w0, a0, g2, k_k, k_a, bd, tm, tk):
    bsz, s, d = x.shape
    w = RWKV_WIDTH
    tok = lambda n: pl.BlockSpec((1, tm, n), lambda b, i: (b, i, 0))
    row = lambda n: pl.BlockSpec((1, n), lambda b, i: (0, 0))
    full = lambda a: pl.BlockSpec(a.shape, lambda b, i: (0,) * a.ndim)
    head_spec = pl.BlockSpec((1, MLA_HEADS, tm, HEAD_PAD), lambda b, i: (b, 0, i, 0))
    head_sds = jax.ShapeDtypeStruct((bsz, MLA_HEADS, s, HEAD_PAD), BF16)
    if tk >= tm:
        per = tk // tm
        vt_spec = pl.BlockSpec((1, MLA_HEADS, 1, VT_ROWS, tm), lambda b, i: (b, 0, i // per, 0, i % per))
    else:
        vt_spec = pl.BlockSpec((1, MLA_HEADS, tm // tk, VT_ROWS, tk), lambda b, i: (b, 0, i, 0, 0))
    vt_sds = jax.ShapeDtypeStruct((bsz, MLA_HEADS, s // tk, VT_ROWS, tk), BF16)
    stream_sds = jax.ShapeDtypeStruct((bsz, s, w), F32)
    return pl.pallas_call(
        _front_kernel,
        grid=(bsz, s // tm),
        in_specs=[tok(d), pl.BlockSpec((1, 6, d), lambda b, i: (b, 0, 0)), row(d), full(w_in_p),
                  tok(LANES), tok(LANES), row(Q_LORA), row(KV_LORA), full(wq), full(wkv),
                  row(LANES), row(LANES), row(LANES), row(LANES),
                  row(RWKV_SLAB), full(wwh), full(wwl), row(w), row(w), full(g2), row(w), row(w), full(bd)],
        out_specs=[tok(2 * d), head_spec, head_spec, vt_spec] + [tok(w)] * 7,
        out_shape=[jax.ShapeDtypeStruct((bsz, s, 2 * d), BF16), head_sds, head_sds, vt_sds] + [stream_sds] * 7,
        scratch_shapes=[pltpu.VMEM((8, RWKV_SLAB), F32)],
        compiler_params=_params("parallel", "arbitrary"),
        name="front",
    )(x, mod3, norm_w.reshape(1, d), w_in_p, cosf, sinf, qan, kvn, wq, wkv, qg, qgs, kg, kgs,
      mu, wwh, wwl, w0, a0, g2, k_k, k_a, bd)


def _rope_kernel(pos_ref, invf_ref, cos_ref, sin_ref):
    ang = pos_ref[...] * invf_ref[...]
    cos_ref[...] = jnp.cos(ang)
    sin_ref[...] = jnp.sin(ang)


def _rope_table(pos_rep, invf_row):
    n = pos_rep.shape[0]
    tr = min(512, n)
    spec = pl.BlockSpec((tr, LANES), lambda i: (i, 0))
    sds = jax.ShapeDtypeStruct((n, LANES), F32)
    return pl.pallas_call(
        _rope_kernel,
        grid=(n // tr,),
        in_specs=[spec, pl.BlockSpec((1, LANES), lambda i: (0, 0))],
        out_specs=[spec, spec],
        out_shape=[sds, sds],
        compiler_params=_params("parallel"),
        name="rope",
    )(pos_rep, invf_row)


def _mla_prep_body(m, cos_ref, sin_ref, qan_ref, kvn_ref, wq_ref, wkv_ref,
                   qg_ref, qgs_ref, kg_ref, kgs_ref, q_out, k_out, vt_out):
    cq = m[:, 0:Q_LORA]
    ckv = m[:, Q_LORA:Q_LORA + KV_LORA]
    kr = m[:, 384:512]
    krs = m[:, 512:640]
    ql = cq * lax.rsqrt(jnp.mean(cq * cq, axis=-1, keepdims=True) + NORM_EPS) * qan_ref[...]
    kvl = ckv * lax.rsqrt(jnp.mean(ckv * ckv, axis=-1, keepdims=True) + NORM_EPS) * kvn_ref[...]
    qall = _dot(ql.astype(BF16), wq_ref[...])
    kvall = _dot(kvl.astype(BF16), wkv_ref[...])
    cosf = cos_ref[0]
    sinf = sin_ref[0]
    scale = MLA_QK ** -0.5 * LOG2_E
    hw = MLA_HEADS * HEAD_PAD
    for h in range(MLA_HEADS):
        lo, hi = h * HEAD_PAD, (h + 1) * HEAD_PAD
        qh = qall[:, lo:hi]
        qs = qall[:, hw + lo:hw + hi]
        rs = lax.rsqrt(jnp.sum(qh * qh, axis=-1, keepdims=True) * (1.0 / MLA_QK) + NORM_EPS)
        qo = (qh * rs * qg_ref[...]) * cosf + (qs * rs * qgs_ref[...]) * sinf
        q_out[0, h] = (qo * scale).astype(BF16)
        kh = kvall[:, lo:hi] + kr
        rs = lax.rsqrt(jnp.sum(kh * kh, axis=-1, keepdims=True) * (1.0 / MLA_QK) + NORM_EPS)
        ko = (kh * rs * kg_ref[...]) * cosf + (krs * rs * kgs_ref[...]) * sinf
        k_out[0, h] = ko.astype(BF16)
        extra = lax.broadcasted_iota(jnp.int32, (VT_ROWS - MLA_V, m.shape[0]), 0)
        v_t = jnp.concatenate([kvall[:, hw + lo:hw + hi].T[0:MLA_V, :],
                               jnp.where(extra == 0, 1.0, 0.0)], axis=0).astype(BF16)
        width = vt_out.shape[4]
        for c in range(vt_out.shape[2]):
            vt_out[0, h, c] = v_t[:, c * width:(c + 1) * width]
        yield


def _attn_kernel(q_ref, k_ref, vt_ref, o_ref, m_ref, acc_ref, *, tq):
    qi = pl.program_id(2)
    heads = range(q_ref.shape[1])
    qs = [q_ref[0, h] for h in heads]
    m_ref[...] = jnp.full_like(m_ref, -jnp.inf)
    acc_ref[...] = jnp.zeros_like(acc_ref)

    tk = vt_ref.shape[4]
    per = tq // tk

    def step(j, mask):
        off = pl.multiple_of(j * tk, tk)
        sts = [_dot_nt(k_ref[0, h, pl.ds(off, tk), :], qs[h]) for h in heads]
        for h in heads:
            st = sts[h] if mask is None else jnp.where(mask, sts[h], -jnp.inf)
            m = m_ref[h]
            m_new = jnp.maximum(m, jnp.max(st, axis=0, keepdims=True))
            p = jnp.exp2(st - m_new)
            alpha = jnp.exp2(m - m_new)
            m_ref[h] = m_new
            acc_ref[h] = alpha * acc_ref[h] + _dot(vt_ref[0, h, j], p.astype(BF16))

    def body(j, carry):
        step(j, None)
        return carry

    lax.fori_loop(0, qi * per, body, 0)
    keys = lax.broadcasted_iota(jnp.int32, (tk, tq), 0)
    queries = lax.broadcasted_iota(jnp.int32, (tk, tq), 1)
    for part in range(per):
        step(qi * per + part, keys + part * tk <= queries)
    out = lambda h: acc_ref[h, 0:MLA_V, :] / acc_ref[h, MLA_V:MLA_V + 1, :]
    for h in heads[::2]:
        o_ref[0, :, h * MLA_V:(h + 2) * MLA_V] = jnp.concatenate([out(h), out(h + 1)], axis=0).T.astype(BF16)


def _attention(q, k, vt, tq):
    bsz, nh, s, dh = q.shape
    dv, tk = vt.shape[3], vt.shape[4]
    hp = ATTN_HEADS
    return pl.pallas_call(
        functools.partial(_attn_kernel, tq=tq),
        grid=(bsz, nh // hp, s // tq),
        in_specs=[pl.BlockSpec((1, hp, tq, dh), lambda b, h, i: (b, h, i, 0)),
                  pl.BlockSpec((1, hp, s, dh), lambda b, h, i: (b, h, 0, 0)),
                  pl.BlockSpec((1, hp, s // tk, dv, tk), lambda b, h, i: (b, h, 0, 0, 0))],
        out_specs=pl.BlockSpec((1, tq, hp * MLA_V), lambda b, h, i: (b, i, h)),
        out_shape=jax.ShapeDtypeStruct((bsz, s, nh * MLA_V), BF16),
        scratch_shapes=[pltpu.VMEM((hp, 1, tq), F32), pltpu.VMEM((hp, dv, tq), F32)],
        compiler_params=_params("parallel", "parallel", "parallel"),
        name="attn",
    )(q, k, vt)


def _softplus(x):
    return jnp.maximum(x, 0.0) + jnp.log(1.0 + jnp.exp(-jnp.abs(x)))


def _rwkv_prep_body(slab, first_tile, mu_ref, wwh_ref, wwl_ref, w0_ref, a0_ref, g2_ref, kk_ref, ka_ref, bd_ref,
                    r_out, lw_out, k_out, v_out, a_out, b_out, g_out, last_ref):
    tm = slab.shape[0]
    w = RWKV_WIDTH
    rolled = pltpu.roll(slab, 1, 0)
    rowi = lax.broadcasted_iota(jnp.int32, slab.shape, 0)
    carried = jnp.where(first_tile, 0.0, last_ref[0:1, :])
    prev = jnp.where(rowi == 0, carried, rolled)
    last_ref[0:1, :] = slab[tm - 1:tm, :]
    p = slab + (prev - slab) * mu_ref[...]
    r = p[:, 0:w]
    k = p[:, w:2 * w]
    v = p[:, 2 * w:3 * w]
    r_out[0] = r
    v_out[0] = v
    yield
    wa = p[:, 3 * w:3 * w + LANES]
    g_lo = p[:, 3 * w + LANES:3 * w + 2 * LANES]
    lane = lax.broadcasted_iota(jnp.int32, wa.shape, 1)
    wa = jnp.where(lane < DECAY_LORA, jnp.tanh(wa), wa)
    wah, wal = _split(wa)
    wa_o = _dot(wah, wwh_ref[...]) + (_dot(wah, wwl_ref[...]) + _dot(wal, wwh_ref[...]))
    g_out[0] = _dot(jax.nn.sigmoid(g_lo).astype(BF16), g2_ref[...])
    yield
    log_w = -_softplus(-(w0_ref[...] + wa_o[:, 0:w])) - 0.5
    lw_out[0] = -jnp.exp(log_w)
    yield
    a = jax.nn.sigmoid(a0_ref[...] + wa_o[:, w:2 * w])
    k_out[0] = k * (1.0 + (a - 1.0) * ka_ref[...])
    yield
    kk = k * kk_ref[...]
    ss = _mm_exact_rhs(kk * kk, bd_ref[...])
    kk = kk / jnp.maximum(jnp.sqrt(ss), 1e-12)
    a_out[0] = -kk
    b_out[0] = kk * a
    yield


def _wkv_kernel(r_ref, lw_ref, k_ref, v_ref, a_ref, b_ref, y_ref, s_ref):
    c = WKV_CHUNK
    n = pl.program_id(1)

    @pl.when(n == 0)
    def _():
        s_ref[...] = jnp.zeros_like(s_ref)

    row = lax.broadcasted_iota(jnp.int32, (LANES, LANES), 0)
    col = lax.broadcasted_iota(jnp.int32, (LANES, LANES), 1)
    lower_strict = row > col
    lower_incl = row >= col
    eye = row == col
    same16 = (row >> 4) == (col >> 4)
    same32 = (row >> 5) == (col >> 5)
    ident = jnp.where(eye, 1.0, 0.0).astype(F32)
    tr = lax.broadcasted_iota(jnp.int32, (c, c), 0)
    tc = lax.broadcasted_iota(jnp.int32, (c, c), 1)
    tri = jnp.where(tr >= tc, 1.0, 0.0).astype(BF16)
    first = lax.broadcasted_iota(jnp.int32, (c, LANES), 1) < RWKV_HEAD

    def stack(x):
        return jnp.concatenate([jnp.where(first, x, 0.0), jnp.where(first, 0.0, x)], axis=0)

    def mm(x, y):
        return _dot(x.astype(BF16), y.astype(BF16))

    nb = r_ref.shape[0]
    probs = [(bi, p) for bi in range(nb) for p in range(RWKV_HEADS // 2)]
    each = lambda f, *xs: [f(*args) for args in zip(*xs)]
    load = lambda ref: [ref[bi, :, p * LANES:(p + 1) * LANES] for bi, p in probs]
    r, lw, k, v, a, b = (load(ref) for ref in (r_ref, lw_ref, k_ref, v_ref, a_ref, b_ref))

    def cumsum(x):
        hi = x.astype(BF16)
        rem = x - hi.astype(F32)
        mid = rem.astype(BF16)
        lo = (rem - mid.astype(F32)).astype(BF16)
        return _dot(tri, hi) + (_dot(tri, mid) + _dot(tri, lo))

    cum = each(cumsum, lw)
    cum_c = [x[c - 1:c, :] for x in cum]
    e_in = each(jnp.exp, cum)
    e_neg = each(lambda x: jnp.exp(-x), cum)
    e_end = each(lambda x, xc: jnp.exp(xc - x), cum, cum_c)
    a_s = each(lambda x, cu, l: stack(x * jnp.exp(cu - l)).astype(BF16), a, cum, lw)
    r_s = each(lambda x, e: stack(x * e), r, e_in)
    b_s = each(lambda x, e: stack(x * e).astype(BF16), b, e_neg)
    k_s = each(lambda x, e: stack(x * e).astype(BF16), k, e_neg)
    b_e = each(lambda x, e: stack(x * e).T.astype(BF16), b, e_end)
    k_e = each(lambda x, e: stack(x * e).T.astype(BF16), k, e_end)
    v_s = each(lambda x: stack(x).astype(BF16), v)

    tt = each(lambda x1, x2, y1, y2: _dot_nt(jnp.concatenate([x1, x2.astype(BF16)], axis=0),
                                             jnp.concatenate([y1, y2], axis=0)), a_s, r_s, b_s, k_s)
    d_ab = [jnp.where(lower_strict, x[0:LANES, 0:LANES], 0.0) for x in tt]
    e_ak = [jnp.where(lower_strict, x[0:LANES, LANES:], 0.0).astype(BF16) for x in tt]
    f_rb = [jnp.where(lower_incl, x[LANES:, 0:LANES], 0.0).astype(BF16) for x in tt]
    f_rk = [jnp.where(lower_incl, x[LANES:, LANES:], 0.0).astype(BF16) for x in tt]

    d16 = [jnp.where(same16, x, 0.0) for x in d_ab]
    d32 = [jnp.where(same32, x, 0.0) for x in d_ab]
    z = each(_dot, e_ak, v_s)
    x2 = each(mm, d16, d16)
    x4 = each(mm, x2, x2)
    x8 = each(mm, x4, x4)
    t = [ident + x for x in d16]
    t = each(lambda t_, x: t_ + mm(t_, x), t, x2)
    t = each(lambda t_, x: t_ + mm(t_, x), t, x4)
    t = each(lambda t_, x: t_ + mm(t_, x), t, x8)
    t = each(lambda t_, hi, lo: t_ + mm(mm(t_, hi - lo), t_), t, d32, d16)
    t = each(lambda t_, hi, lo: t_ + mm(mm(t_, hi - lo), t_), t, d_ab, d32)

    au = each(lambda t_, x, zz: mm(t_, jnp.concatenate([x, zz.astype(BF16)], axis=1)).astype(BF16),
              t, a_s, z)
    g = each(_dot, f_rb, au)
    y0 = each(lambda gg, f, vv: gg[:, LANES:] + _dot(f, vv), g, f_rk, v_s)
    r1 = each(lambda x, gg: x + gg[:, 0:LANES], r_s, g)
    hmat = each(_dot, b_e, au)
    m_mat = each(lambda h, xc: jnp.where(eye, jnp.exp(xc), 0.0) + h[:, 0:LANES], hmat, cum_c)
    n_mat = each(lambda h, ke, vv: h[:, LANES:] + _dot(ke, vv), hmat, k_e, v_s)

    for i, (bi, p) in enumerate(probs):
        s0 = s_ref[i]
        s0b = s0.astype(BF16)
        ys = _dot(r1[i].astype(BF16), s0b) + y0[i]
        s_ref[i] = _dot(m_mat[i].astype(BF16), s0b) + n_mat[i]
        y_ref[bi, :, p * LANES:(p + 1) * LANES] = ys[0:c, :] + ys[c:, :]


def _wkv(r, lw, k, v, a, b, nb):
    bsz, s, w = r.shape
    spec = pl.BlockSpec((nb, WKV_CHUNK, w), lambda bi, n: (bi, n, 0))
    return pl.pallas_call(
        _wkv_kernel,
        grid=(bsz // nb, s // WKV_CHUNK),
        in_specs=[spec] * 6,
        out_specs=spec,
        out_shape=jax.ShapeDtypeStruct((bsz, s, w), F32),
        scratch_shapes=[pltpu.VMEM((nb * RWKV_HEADS // 2, LANES, LANES), F32)],
        compiler_params=_params("parallel", "arbitrary"),
        name="wkv",
    )(r, lw, k, v, a, b)


def _post_kernel(x_ref, mod_ref, y_ref, r_ref, k_ref, v_ref, g_ref, o_ref, gates_ref,
                 lnw_ref, lnb_ref, rk_ref, bd_ref, wor_ref, woa_ref, wout_ref, nfw_ref,
                 wrh_ref, wrl_ref, br_ref,
                 x1_ref, h2_ref, route_ref, cnt_ref):
    tiles = [_post_tile(part, x_ref, mod_ref, y_ref, r_ref, k_ref, v_ref, g_ref, o_ref, gates_ref,
                        lnw_ref, lnb_ref, rk_ref, bd_ref, wor_ref, woa_ref, wout_ref, nfw_ref,
                        wrh_ref, wrl_ref, br_ref, x1_ref, h2_ref, route_ref, cnt_ref)
             for part in range(x_ref.shape[1] // ROUTE_TILE)]
    _trace_in_turn(tiles)


def _post_tile(part, x_ref, mod_ref, y_ref, r_ref, k_ref, v_ref, g_ref, o_ref, gates_ref,
               lnw_ref, lnb_ref, rk_ref, bd_ref, wor_ref, woa_ref, wout_ref, nfw_ref,
               wrh_ref, wrl_ref, br_ref, x1_ref, h2_ref, route_ref, cnt_ref):
    tm = ROUTE_TILE
    rows = slice(part * tm, (part + 1) * tm)
    d = x_ref.shape[2]
    bd = bd_ref[...]
    inv_n = 1.0 / RWKV_HEAD
    y = y_ref[0, rows, :]
    seg = lambda t: _dot(t.astype(BF16), bd)
    mu = seg(y) * inv_n
    dlt = y - mu
    var = seg(dlt * dlt) * inv_n
    yn = dlt * lax.rsqrt(var + GN_EPS) * lnw_ref[...] + lnb_ref[...]
    yield
    v = v_ref[0, rows, :]
    bonus = seg(r_ref[0, rows, :] * k_ref[0, rows, :] * rk_ref[...]) * v
    z = (yn + bonus) * g_ref[0, rows, :]
    o_b = _dot(z.astype(BF16), wor_ref[...])
    o_a = _dot(o_ref[0, rows, :], woa_ref[...])
    yield
    ga = jax.nn.sigmoid(gates_ref[0, rows, 0:d].astype(F32))
    gb = jax.nn.sigmoid(gates_ref[0, rows, d:2 * d].astype(F32))
    merged = ga * o_a + gb * o_b
    mix = _dot(merged.astype(BF16), wout_ref[...])
    yield
    x1 = x_ref[0, rows, :] + mod_ref[0, 2:3, :] * mix
    x1_ref[0, rows, :] = x1
    ms = jnp.mean(x1 * x1, axis=-1, keepdims=True)
    h2 = x1 * lax.rsqrt(ms + NORM_EPS) * nfw_ref[...]
    h2 = h2 * (1.0 + mod_ref[0, 4:5, :]) + mod_ref[0, 3:4, :]
    h2_ref[0, rows, :] = h2
    yield

    hh, hl = _split(h2)
    logits = _dot(hh, wrh_ref[...]) + (_dot(hh, wrl_ref[...]) + _dot(hl, wrh_ref[...])) + br_ref[...]
    lane = lax.broadcasted_iota(jnp.int32, (tm, LANES), 1)
    lanef = lane.astype(F32)
    cur = jnp.where(lane < N_EXPERTS, logits, -jnp.inf)
    yield
    vals, idxs, hots = [], [], []
    for _ in range(TOP_K):
        mx = jnp.max(cur, axis=-1, keepdims=True)
        idx = jnp.min(jnp.where(cur == mx, lanef, float(LANES)), axis=-1, keepdims=True)
        hot = lanef == idx
        cur = jnp.where(hot, -jnp.inf, cur)
        vals.append(mx)
        idxs.append(idx)
        hots.append(hot)
    exps = [jnp.exp(vv - vals[0]) for vv in vals]
    den = exps[0] + exps[1] + exps[2] + exps[3]
    yield
    sel = jnp.zeros((tm, LANES), F32)
    for hot in hots:
        sel = sel + jnp.where(hot, 1.0, 0.0)
    ri = lax.broadcasted_iota(jnp.int32, (tm, tm), 0)
    ci = lax.broadcasted_iota(jnp.int32, (tm, tm), 1)
    below = jnp.where(ri > ci, 1.0, 0.0).astype(BF16)
    before = _dot(below, sel.astype(BF16))
    counts = jnp.broadcast_to(jnp.sum(sel, axis=0, keepdims=True), (8, LANES))
    er = lax.broadcasted_iota(jnp.int32, (LANES, LANES), 0)
    ec = lax.broadcasted_iota(jnp.int32, (LANES, LANES), 1)
    lower = jnp.where(er < ec, 1.0, 0.0).astype(BF16)
    start = _dot(counts.astype(BF16), lower)[0:1, :]
    route = jnp.zeros((tm, LANES), F32)
    for j in range(TOP_K):
        pos = jnp.sum(jnp.where(hots[j], before + start, 0.0), axis=-1, keepdims=True)
        route = jnp.where(lane == j, pos, route)
        route = jnp.where(lane == TOP_K + j, exps[j] / den, route)
    route_ref[0, rows, :] = route
    cnt_ref[0, part] = counts
    yield


def _post(x, mod3, y, r, k, v, g, o, gates, lnw, lnb, rk, bd, wor, woa, wout, nfw, wrh, wrl, br, tm):
    bsz, s, d = x.shape
    w = RWKV_WIDTH
    parts = tm // ROUTE_TILE
    tok = lambda n: pl.BlockSpec((1, tm, n), lambda b, i: (b, i, 0))
    row = lambda n: pl.BlockSpec((1, n), lambda b, i: (0, 0))
    full = lambda a: pl.BlockSpec(a.shape, lambda b, i: (0,) * a.ndim)
    return pl.pallas_call(
        _post_kernel,
        grid=(bsz, s // tm),
        in_specs=[tok(d), pl.BlockSpec((1, 6, d), lambda b, i: (b, 0, 0)),
                  tok(w), tok(w), tok(w), tok(w), tok(w),
                  tok(MLA_HEADS * MLA_V), tok(2 * d),
                  row(w), row(w), row(w), full(bd), full(wor), full(woa), full(wout), row(d),
                  full(wrh), full(wrl), row(LANES)],
        out_specs=[tok(d), tok(d), tok(LANES), pl.BlockSpec((1, parts, 8, LANES), lambda b, i: (b, i, 0, 0))],
        out_shape=[jax.ShapeDtypeStruct((bsz, s, d), F32),
                   jax.ShapeDtypeStruct((bsz, s, d), F32),
                   jax.ShapeDtypeStruct((bsz, s, LANES), F32),
                   jax.ShapeDtypeStruct((bsz, s // ROUTE_TILE, 8, LANES), F32)],
        compiler_params=_params("parallel", "parallel"),
        name="post",
    )(x, mod3, y, r, k, v, g, o, gates, lnw, lnb, rk, bd, wor, woa, wout, nfw, wrh, wrl, br)


def _load_rows(ref, n):
    nc = ref.shape[0] // n
    return jnp.concatenate([ref[pl.ds(c, n, stride=nc), :] for c in range(nc)], axis=1)


def _store_rows(ref, val):
    nc = val.shape[1] // LANES
    for c in range(nc):
        ref[pl.ds(c, val.shape[0], stride=nc), :] = val[:, c * LANES:(c + 1) * LANES]


def _run_copies(tile, cnt_ref, off_ref, dst_ref, tm, make_copy):
    def per_expert(e, carry):
        idx = tile * N_EXPERTS + e
        o = off_ref[idx]
        d0 = dst_ref[idx]
        _pieces(cnt_ref[idx], tm, lambda done, size: make_copy(o + done, d0 + done, size).start())
        return carry

    lax.fori_loop(0, N_EXPERTS, per_expert, 0)


def _pieces(count, limit, fn):
    def emit(bits):
        for b in bits:
            size = 1 << b
            done = count & ~(2 * size - 1)

            @pl.when((count & size) != 0)
            def _():
                fn(done, size)

    bits = range(limit.bit_length() - 1, -1, -1)
    large = [b for b in bits if b >= RARE_PIECE_BIT]
    if large:
        @pl.when(count >= (1 << RARE_PIECE_BIT))
        def _():
            emit(large)
    emit([b for b in bits if b < RARE_PIECE_BIT])


def _zero_fill(gap_ref, xs_ref, zero_ref, sem, nc):
    bm = MOE_BLOCK
    zero_ref[...] = jnp.zeros_like(zero_ref)

    def piece(dst, size):
        return pltpu.make_async_copy(zero_ref.at[pl.ds(0, size * nc), :],
                                     xs_ref.at[pl.ds(pl.multiple_of(dst * nc, nc), size * nc), :], sem)

    def sweep(act):
        def per_expert(e, carry):
            g0 = gap_ref[e]
            _pieces(gap_ref[N_EXPERTS + e], bm, lambda done, size: act(piece(g0 + done, size)))
            return carry

        def tail(blk, carry):
            act(piece(gap_ref[2 * N_EXPERTS] + blk * bm, bm))
            return carry

        lax.fori_loop(0, N_EXPERTS, per_expert, 0)
        lax.fori_loop(0, gap_ref[2 * N_EXPERTS + 1], tail, 0)

    sweep(lambda cp: cp.start())
    sweep(lambda cp: cp.wait())


def _dispatch_kernel(cnt_ref, off_ref, dst_ref, gap_ref, h_ref, route_ref, xs_ref, sorted_ref, zero_ref,
                     sem, zero_sem):
    tm = h_ref.shape[0]

    @pl.when(pl.program_id(0) == 0)
    def _():
        _zero_fill(gap_ref, xs_ref, zero_ref, zero_sem, h_ref.shape[1] // LANES)

    pos_t = route_ref[...].T
    slot = lax.broadcasted_iota(jnp.int32, (TOP_K * tm, tm), 0).astype(F32)
    perm = jnp.where(slot == pos_t[0:1, :], 1.0, 0.0)
    for j in range(1, TOP_K):
        perm = perm + jnp.where(slot == pos_t[j:j + 1, :], 1.0, 0.0)
    i = pl.program_id(0)
    cur = i % 2
    buf = sorted_ref.at[cur]
    _store_rows(buf, _dot(perm.astype(BF16), h_ref[...].astype(BF16)))
    nc = h_ref.shape[1] // LANES

    def make_copy(src, dst, size):
        return pltpu.make_async_copy(buf.at[pl.ds(pl.multiple_of(src * nc, nc), size * nc), :],
                                     xs_ref.at[pl.ds(pl.multiple_of(dst * nc, nc), size * nc), :], sem.at[cur])

    _run_copies(i, cnt_ref, off_ref, dst_ref, tm, make_copy)

    def drain(which):
        pltpu.make_async_copy(sorted_ref.at[which], xs_ref.at[pl.ds(0, TOP_K * tm * nc), :], sem.at[which]).wait()

    @pl.when(i > 0)
    def _():
        drain(1 - cur)

    @pl.when(i == pl.num_programs(0) - 1)
    def _():
        drain(cur)


def _dispatch(cnt_tab, off_tab, dst_tab, gap_tab, h2, route, n_rows, tm):
    t, d = h2.shape
    nc = d // LANES
    return pl.pallas_call(
        _dispatch_kernel,
        grid_spec=pltpu.PrefetchScalarGridSpec(
            num_scalar_prefetch=4,
            grid=(t // tm,),
            in_specs=[pl.BlockSpec((tm, d), lambda i, *_: (i, 0)),
                      pl.BlockSpec((tm, LANES), lambda i, *_: (i, 0))],
            out_specs=pl.BlockSpec(memory_space=pl.ANY),
            scratch_shapes=[pltpu.VMEM((2, TOP_K * tm * nc, LANES), F32), pltpu.VMEM((MOE_BLOCK * nc, LANES), F32),
                            pltpu.SemaphoreType.DMA((2,)), pltpu.SemaphoreType.DMA(())]),
        out_shape=jax.ShapeDtypeStruct((n_rows * nc, LANES), F32),
        compiler_params=_params("arbitrary"),
        name="dispatch",
    )(cnt_tab, off_tab, dst_tab, gap_tab, h2, route)


def _split_gate_up_kernel(w_ref, g_ref, u_ref, t_ref):
    half = LANES // 2
    nblk = w_ref.shape[1] // LANES
    for c in range(w_ref.shape[2] // LANES):
        w_t = w_ref[0, :, c * LANES:(c + 1) * LANES].T
        for b in range(nblk):
            t_ref[b] = w_t[:, b * LANES:(b + 1) * LANES]
        for out_ref, first in ((g_ref, 0), (u_ref, 1)):
            rows = [t_ref[b, pl.ds(first, half, stride=2), :] for b in range(nblk)]
            out_ref[0, c * half:(c + 1) * half, :] = jnp.concatenate(rows, axis=1).astype(BF16)


def _split_gate_up(w):
    e, d, ff2 = w.shape
    ff = ff2 // 2
    spec = pl.BlockSpec((1, ff, d), lambda i: (i, 0, 0))
    sds = jax.ShapeDtypeStruct((e, ff, d), BF16)
    return pl.pallas_call(
        _split_gate_up_kernel,
        grid=(e,),
        in_specs=[pl.BlockSpec((1, d, ff2), lambda i: (i, 0, 0))],
        out_specs=[spec, spec],
        out_shape=[sds, sds],
        scratch_shapes=[pltpu.VMEM((d // LANES, LANES, LANES), F32)],
        compiler_params=_params("parallel"),
        name="split_gate_up",
    )(w)


def _moe_kernel(blk_e_ref, valid_ref, xs_ref, wg_ref, wu_ref, bg_ref, bu_ref, wd_ref, bd_ref, ys_ref):
    del blk_e_ref
    i = pl.program_id(0)
    valid = valid_ref[i]

    @pl.when(valid > 0)
    def _():
        x = _load_rows(xs_ref, MOE_BLOCK).astype(BF16)
        gate = _dot_nt(x, wg_ref[0]) + bg_ref[0]
        up = _dot_nt(x, wu_ref[0]) + bu_ref[0]
        gate = jnp.minimum(gate, SWIGLU_LIMIT)
        up = jnp.clip(up, -SWIGLU_LIMIT, SWIGLU_LIMIT)
        act = (up + 1.0) * gate * jax.nn.sigmoid(SWIGLU_ALPHA * gate)
        _store_rows(ys_ref, _dot(act.astype(BF16), wd_ref[0]) + bd_ref[0])

    @pl.when(valid == 0)
    def _():
        ys_ref[...] = jnp.zeros_like(ys_ref)


def _moe(blk_e, blk_valid, xs, wg, wu, bg, bu, wd, bd):
    ff, d = wg.shape[1], wg.shape[2]
    nc = d // LANES
    n_rows = xs.shape[0] // nc
    bm = MOE_BLOCK
    wspec = lambda k, n: pl.BlockSpec((1, k, n), lambda i, be, nu: (be[i], 0, 0))
    return pl.pallas_call(
        _moe_kernel,
        grid_spec=pltpu.PrefetchScalarGridSpec(
            num_scalar_prefetch=2,
            grid=(n_rows // bm,),
            in_specs=[pl.BlockSpec((bm * nc, LANES), lambda i, be, nu: (i, 0)),
                      wspec(ff, d), wspec(ff, d), wspec(1, ff), wspec(1, ff), wspec(ff, d), wspec(1, d)],
            out_specs=pl.BlockSpec((bm * nc, LANES), lambda i, be, nu: (i, 0))),
        out_shape=jax.ShapeDtypeStruct((n_rows * nc, LANES), F32),
        compiler_params=_params("arbitrary"),
        name="moe",
    )(blk_e, blk_valid, xs, wg, wu, bg, bu, wd, bd)


def _combine_kernel(cnt_ref, off_ref, dst_ref, x1_ref, route_ref, mod_ref, ys_ref, o_ref, rows_ref, sem):
    tm = x1_ref.shape[1]
    tile = pl.program_id(0) * pl.num_programs(1) + pl.program_id(1)
    n_tiles = pl.num_programs(0) * pl.num_programs(1)
    nc = x1_ref.shape[2] // LANES
    cur = tile % 2

    def fetch(which_tile):
        which = which_tile % 2

        def make_copy(dst, src, size):
            return pltpu.make_async_copy(
                ys_ref.at[pl.ds(pl.multiple_of(src * nc, nc), size * nc), :],
                rows_ref.at[which, pl.ds(pl.multiple_of(dst * nc, nc), size * nc), :], sem.at[which])

        _run_copies(which_tile, cnt_ref, off_ref, dst_ref, tm, make_copy)

    @pl.when(tile == 0)
    def _():
        fetch(tile)

    @pl.when(tile + 1 < n_tiles)
    def _():
        fetch(tile + 1)

    route = route_ref[0]
    slot = lax.broadcasted_iota(jnp.int32, (tm, TOP_K * tm), 1).astype(F32)
    mix = jnp.where(slot == route[:, 0:1], route[:, TOP_K:TOP_K + 1], 0.0)
    for j in range(1, TOP_K):
        mix = mix + jnp.where(slot == route[:, j:j + 1], route[:, TOP_K + j:TOP_K + j + 1], 0.0)
    pltpu.make_async_copy(ys_ref.at[pl.ds(0, TOP_K * tm * nc), :], rows_ref.at[cur], sem.at[cur]).wait()
    acc = _dot(mix.astype(BF16), _load_rows(rows_ref.at[cur], TOP_K * tm).astype(BF16))
    o_ref[0] = x1_ref[0] + mod_ref[0, 5:6, :] * acc


def _combine(cnt_tab, off_tab, dst_tab, x1, route, mod3, ys, tm):
    bsz, s, d = x1.shape
    return pl.pallas_call(
        _combine_kernel,
        grid_spec=pltpu.PrefetchScalarGridSpec(
            num_scalar_prefetch=3,
            grid=(bsz, s // tm),
            in_specs=[pl.BlockSpec((1, tm, d), lambda b, i, *_: (b, i, 0)),
                      pl.BlockSpec((1, tm, LANES), lambda b, i, *_: (b, i, 0)),
                      pl.BlockSpec((1, 6, d), lambda b, i, *_: (b, 0, 0)),
                      pl.BlockSpec(memory_space=pl.ANY)],
            out_specs=pl.BlockSpec((1, tm, d), lambda b, i, *_: (b, i, 0)),
            scratch_shapes=[pltpu.VMEM((2, TOP_K * tm * d // LANES, LANES), F32), pltpu.SemaphoreType.DMA((2,))]),
        out_shape=jax.ShapeDtypeStruct((bsz, s, d), F32),
        compiler_params=_params("arbitrary", "arbitrary"),
        name="combine",
    )(cnt_tab, off_tab, dst_tab, x1, route, mod3, ys)


def _pad_cols(a, n):
    return jnp.pad(a, ((0, 0), (0, n - a.shape[1])))


def _head_blocks(cols_main, cols_rot=None):
    k = cols_main.shape[0]
    out = jnp.zeros((k, MLA_HEADS, HEAD_PAD), F32)
    out = out.at[:, :, :cols_main.shape[2]].set(cols_main)
    return out.reshape(k, MLA_HEADS * HEAD_PAD)


def _layer(x, cond_mod, positions, w_in, q_a_norm_w, w_q_up, kv_a_norm_w, w_kv_up, q_norm_w, k_norm_w,
           w_o_mla, rwkv_mu, rwkv_w0, rwkv_w2, rwkv_a0, rwkv_a2, rwkv_g2, rwkv_k_k, rwkv_k_a, rwkv_r_k,
           rwkv_ln_w, rwkv_ln_b, rwkv_w_o, w_out, norm_mix_w, norm_ffn_w, w_router, b_router,
           w_gate_up, b_gate_up, w_down, b_down):
    bsz, s, d = x.shape
    t = bsz * s
    half = MLA_ROPE // 2
    nope, qk = MLA_NOPE, MLA_QK
    mod3 = cond_mod.reshape(bsz, 6, d)

    o_q, o_kv, o_kr = 0, Q_LORA, Q_LORA + KV_LORA
    o_slab = o_kr + MLA_ROPE
    o_gate = o_slab + RWKV_SLAB
    kr_w = w_in[:, o_kr:o_slab]
    zeros = lambda n: jnp.zeros((d, n), F32)
    kr_blk = jnp.concatenate([zeros(nope), kr_w, zeros(HEAD_PAD - qk)], axis=1)
    kr_rot = jnp.concatenate([zeros(nope), -kr_w[:, half:], kr_w[:, :half], zeros(HEAD_PAD - qk)], axis=1)
    w_in_p = jnp.concatenate([w_in[:, o_q:o_kr], kr_blk, kr_rot, w_in[:, o_slab:]], axis=1).astype(BF16)

    tm = min(256, s)

    wq3 = w_q_up.reshape(Q_LORA, MLA_HEADS, qk)
    wq_rot = jnp.concatenate([jnp.zeros((Q_LORA, MLA_HEADS, nope), F32), -wq3[:, :, nope + half:],
                              wq3[:, :, nope:nope + half]], axis=2)
    wq = jnp.concatenate([_head_blocks(wq3), _head_blocks(wq_rot)], axis=1).astype(BF16)
    wkv3 = w_kv_up.reshape(KV_LORA, MLA_HEADS, nope + MLA_V)
    wkv = jnp.concatenate([_head_blocks(wkv3[:, :, :nope]), _head_blocks(wkv3[:, :, nope:])], axis=1).astype(BF16)

    def gains(wn):
        main = jnp.pad(wn, (0, HEAD_PAD - qk)).reshape(1, HEAD_PAD)
        rot = jnp.concatenate([jnp.zeros((nope,), F32), wn[nope + half:], wn[nope:nope + half],
                               jnp.zeros((HEAD_PAD - qk,), F32)]).reshape(1, HEAD_PAD)
        return main, rot

    qg, qgs = gains(q_norm_w)
    kg, kgs = gains(k_norm_w)
    inv_freq = ROPE_THETA ** (-jnp.arange(half, dtype=F32) / half)
    per_row = LANES // half
    pos_rep = jnp.repeat(positions.astype(F32).reshape(t // per_row, per_row), half, axis=1)
    cos16, sin16 = _rope_table(pos_rep, jnp.tile(inv_freq, per_row).reshape(1, LANES))
    cos16 = cos16.reshape(bsz, s, half)
    sin16 = sin16.reshape(bsz, s, half)
    cosf = jnp.concatenate([jnp.ones((bsz, s, nope), F32), cos16, cos16,
                            jnp.ones((bsz, s, HEAD_PAD - qk), F32)], axis=-1)
    sinf = jnp.concatenate([jnp.zeros((bsz, s, nope), F32), sin16, sin16,
                            jnp.zeros((bsz, s, HEAD_PAD - qk), F32)], axis=-1)

    w = RWKV_WIDTH
    wwa = jnp.zeros((LANES, 2 * w), F32)
    wwa = wwa.at[:DECAY_LORA, :w].set(rwkv_w2).at[DECAY_LORA:, w:].set(rwkv_a2)
    wwh = wwa.astype(BF16)
    wwl = (wwa - wwh.astype(F32)).astype(BF16)
    hid = np.arange(w) // RWKV_HEAD
    bd = jnp.asarray(hid[:, None] == hid[None, :], BF16)
    gates, q, k, vt, r_, lw_, k_, v_, a_, b_, g_ = _front(
        x, mod3, norm_mix_w, w_in_p, cosf, sinf, q_a_norm_w.reshape(1, -1), kv_a_norm_w.reshape(1, -1),
        wq, wkv, qg, qgs, kg, kgs, rwkv_mu.reshape(1, -1), wwh, wwl, rwkv_w0.reshape(1, -1),
        rwkv_a0.reshape(1, -1), rwkv_g2.astype(BF16), rwkv_k_k.reshape(1, -1), rwkv_k_a.reshape(1, -1), bd,
        tm, min(ATTN_KEY_BLOCK, s))
    o = _attention(q, k, vt, min(ATTN_BLOCK, s))
    y = _wkv(r_, lw_, k_, v_, a_, b_, WKV_BATCH if bsz % WKV_BATCH == 0 else 1)

    woa = w_o_mla.astype(BF16)
    wr =_pad_cols(w_router, LANES)
    wrh = wr.astype(BF16)
    wrl = (wr - wrh.astype(F32)).astype(BF16)
    br = jnp.pad(b_router, (0, LANES - N_EXPERTS)).reshape(1, LANES)
    x1, h2, route, counts = _post(
        x, mod3, y, r_, k_, v_, g_, o, gates, rwkv_ln_w.reshape(1, -1), rwkv_ln_b.reshape(1, -1),
        rwkv_r_k.reshape(1, -1), bd, rwkv_w_o.astype(BF16), woa, w_out.astype(BF16),
        norm_ffn_w.reshape(1, -1), wrh, wrl, br, POST_TILE if s % POST_TILE == 0 else ROUTE_TILE)

    bm = MOE_BLOCK
    n_rows = t * TOP_K + N_EXPERTS * bm
    n_blocks = n_rows // bm
    cnt_tab = counts[:, :, 0, :N_EXPERTS].reshape(t // ROUTE_TILE, N_EXPERTS).astype(jnp.int32)
    total = jnp.sum(cnt_tab, axis=0)
    padded = (total + bm - 1) // bm * bm
    pad_end = jnp.cumsum(padded)
    pad_start = pad_end - padded
    off_tab = jnp.cumsum(cnt_tab, axis=1) - cnt_tab
    dst_tab = pad_start[None, :] + jnp.cumsum(cnt_tab, axis=0) - cnt_tab
    blk_start = jnp.arange(n_blocks, dtype=jnp.int32) * bm
    blk_e = jnp.minimum(jnp.sum((pad_end[None, :] <= blk_start[:, None]).astype(jnp.int32), axis=1),
                        N_EXPERTS - 1)
    blk_valid = jnp.clip((pad_start + total)[blk_e] - blk_start, 0, bm).astype(jnp.int32)
    tabs = (cnt_tab.reshape(-1), off_tab.reshape(-1).astype(jnp.int32), dst_tab.reshape(-1).astype(jnp.int32))
    gap_tab = jnp.concatenate([pad_start + total, padded - total, pad_end[-1:],
                               (n_rows - pad_end[-1:]) // bm]).astype(jnp.int32)

    xs = _dispatch(*tabs, gap_tab, h2.reshape(t, d), route.reshape(t, LANES), n_rows, ROUTE_TILE)
    wg_t, wu_t = _split_gate_up(w_gate_up)
    bgu = b_gate_up.reshape(N_EXPERTS, 1, D_FF, 2)
    ys = _moe(blk_e, blk_valid, xs, wg_t, wu_t,
              bgu[..., 0], bgu[..., 1], w_down.astype(BF16), b_down.reshape(N_EXPERTS, 1, d))
    return _combine(*tabs, x1, route, mod3, ys, ROUTE_TILE)


def kernel(x, c, positions, ada_w, ada_b, norm_mix_w, norm_ffn_w, w_in, q_a_norm_w, w_q_up, kv_a_norm_w, w_kv_up, q_norm_w, k_norm_w, w_o_mla, rwkv_mu, rwkv_w0, rwkv_w2, rwkv_a0, rwkv_a2, rwkv_g2, rwkv_k_k, rwkv_k_a, rwkv_r_k, rwkv_ln_w, rwkv_ln_b, rwkv_w_o, w_out, w_router, b_router, w_gate_up, b_gate_up, w_down, b_down):
    depth = ada_w.shape[0]
    for l in range(depth):
        mod = _ada(c, ada_w[l], ada_b[l])
        x = _layer(x, mod, positions, w_in[l], q_a_norm_w[l], w_q_up[l], kv_a_norm_w[l], w_kv_up[l],
                   q_norm_w[l], k_norm_w[l], w_o_mla[l], rwkv_mu[l], rwkv_w0[l], rwkv_w2[l], rwkv_a0[l],
                   rwkv_a2[l], rwkv_g2[l], rwkv_k_k[l], rwkv_k_a[l], rwkv_r_k[l], rwkv_ln_w[l],
                   rwkv_ln_b[l], rwkv_w_o[l], w_out[l], norm_mix_w[l], norm_ffn_w[l], w_router[l],
                   b_router[l], w_gate_up[l], b_gate_up[l], w_down[l], b_down[l])
    return x
```

```python
import functools

import numpy as np
import jax
import jax.numpy as jnp
from jax import lax
from jax.experimental import pallas as pl
from jax.experimental.pallas import tpu as pltpu

F32 = jnp.float32
BF16 = jnp.bfloat16

D_MODEL = 1024
MLA_HEADS = 8
MLA_NOPE = 64
MLA_ROPE = 32
MLA_QK = MLA_NOPE + MLA_ROPE
MLA_V = 64
VT_ROWS = 80
Q_LORA = 256
KV_LORA = 128
ROPE_THETA = 10000.0
RWKV_HEADS = 8
RWKV_HEAD = 64
RWKV_WIDTH = RWKV_HEADS * RWKV_HEAD
DECAY_LORA = 64
AAA_LORA = 64
GATE_LORA = 128
RWKV_SLAB = 3 * RWKV_WIDTH + DECAY_LORA + AAA_LORA + GATE_LORA
GN_EPS = 64e-5
N_EXPERTS = 32
TOP_K = 4
D_FF = D_MODEL
SWIGLU_LIMIT = 7.0
SWIGLU_ALPHA = 1.702
NORM_EPS = 1e-6
LOG2_E = 1.4426950408889634

LANES = 128
HEAD_PAD = 128
MLA_COLS = 640
WKV_CHUNK = 64
WKV_BATCH = 4
ATTN_BLOCK = 512
ATTN_KEY_BLOCK = 512
ATTN_HEADS = 8
ROUTE_TILE = 256
POST_TILE = 512
RARE_PIECE_BIT = 6
MOE_BLOCK = 512
VMEM_LIMIT = 56 * 1024 * 1024


def _dot(a, b):
    return jnp.dot(a, b, preferred_element_type=F32)


def _dot_nt(a, b):
    return lax.dot_general(a, b, (((1,), (1,)), ((), ())), preferred_element_type=F32)


def _split(x):
    hi = x.astype(BF16)
    lo = (x - hi.astype(F32)).astype(BF16)
    return hi, lo


def _mm3(a, b):
    ah, al = _split(a)
    bh, bl = _split(b)
    return _dot(ah, bh) + (_dot(ah, bl) + _dot(al, bh))


def _mm3_nt(a, b):
    ah, al = _split(a)
    bh, bl = _split(b)
    return _dot_nt(ah, bh) + (_dot_nt(ah, bl) + _dot_nt(al, bh))


def _mm_exact_rhs(a, b_bf16):
    ah, al = _split(a)
    return _dot(ah, b_bf16) + _dot(al, b_bf16)


def _trace_in_turn(stages):
    live = list(stages)
    while live:
        for gen in list(live):
            try:
                next(gen)
            except StopIteration:
                live.remove(gen)


def _params(*sem):
    return pltpu.CompilerParams(dimension_semantics=sem, vmem_limit_bytes=VMEM_LIMIT)


def _ada_kernel(c_ref, w_ref, b_ref, o_ref):
    c = c_ref[...]
    cond = c * jax.nn.sigmoid(c)
    o_ref[...] = _mm3(cond, w_ref[...]) + b_ref[...]


def _ada(c, w, b):
    bsz, d = c.shape
    n = w.shape[1]
    tn = 1024
    return pl.pallas_call(
        _ada_kernel,
        grid=(n // tn,),
        in_specs=[pl.BlockSpec((bsz, d), lambda j: (0, 0)),
                  pl.BlockSpec((d, tn), lambda j: (0, j)),
                  pl.BlockSpec((1, tn), lambda j: (0, j))],
        out_specs=pl.BlockSpec((bsz, tn), lambda j: (0, j)),
        out_shape=jax.ShapeDtypeStruct((bsz, n), F32),
        compiler_params=_params("parallel"),
        name="ada",
    )(c, w, b.reshape(1, n))


def _front_kernel(x_ref, mod_ref, nw_ref, w_ref,
                  cos_ref, sin_ref, qan_ref, kvn_ref, wq_ref, wkv_ref, qg_ref, qgs_ref, kg_ref, kgs_ref,
                  mu_ref, wwh_ref, wwl_ref, w0_ref, a0_ref, g2_ref, kk_ref, ka_ref, bd_ref,
                  gates_ref, q_out, k_out, vt_out, r_out, lw_out, k2_out, v_out, a_out, b_out, g_out,
                  last_ref):
    x = x_ref[0]
    ms = jnp.mean(x * x, axis=-1, keepdims=True)
    y = x * lax.rsqrt(ms + NORM_EPS) * nw_ref[...]
    h = y * (1.0 + mod_ref[0, 1:2, :]) + mod_ref[0, 0:1, :]
    hb = h.astype(BF16)
    mla = _dot(hb, w_ref[:, 0:MLA_COLS])
    slab = _dot(hb, w_ref[:, MLA_COLS:MLA_COLS + RWKV_SLAB])
    gates_ref[0] = _dot(hb, w_ref[:, MLA_COLS + RWKV_SLAB:]).astype(BF16)
    mla_prep = _mla_prep_body(mla, cos_ref, sin_ref, qan_ref, kvn_ref, wq_ref, wkv_ref,
                              qg_ref, qgs_ref, kg_ref, kgs_ref, q_out, k_out, vt_out)
    rwkv_prep = _rwkv_prep_body(slab, pl.program_id(1) == 0, mu_ref, wwh_ref, wwl_ref, w0_ref, a0_ref, g2_ref,
                                kk_ref, ka_ref, bd_ref, r_out, lw_out, k2_out, v_out, a_out, b_out, g_out,
                                last_ref)
    _trace_in_turn([mla_prep, rwkv_prep])


def _front(x, mod3, norm_w, w_in_p, cosf, sinf, qan, kvn, wq, wkv, qg, qgs, kg, kgs,
           mu, wwh, wwl, w0, a0, g2, k_k, k_a, bd, tm, tk):
    bsz, s, d = x.shape
    w = RWKV_WIDTH
    tok = lambda n: pl.BlockSpec((1, tm, n), lambda b, i: (b, i, 0))
    row = lambda n: pl.BlockSpec((1, n), lambda b, i: (0, 0))
    full = lambda a: pl.BlockSpec(a.shape, lambda b, i: (0,) * a.ndim)
    head_spec = pl.BlockSpec((1, MLA_HEADS, tm, HEAD_PAD), lambda b, i: (b, 0, i, 0))
    head_sds = jax.ShapeDtypeStruct((bsz, MLA_HEADS, s, HEAD_PAD), BF16)
    if tk >= tm:
        per = tk // tm
        vt_spec = pl.BlockSpec((1, MLA_HEADS, 1, VT_ROWS, tm), lambda b, i: (b, 0, i // per, 0, i % per))
    else:
        vt_spec = pl.BlockSpec((1, MLA_HEADS, tm // tk, VT_ROWS, tk), lambda b, i: (b, 0, i, 0, 0))
    vt_sds = jax.ShapeDtypeStruct((bsz, MLA_HEADS, s // tk, VT_ROWS, tk), BF16)
    stream_sds = jax.ShapeDtypeStruct((bsz, s, w), F32)
    return pl.pallas_call(
        _front_kernel,
        grid=(bsz, s // tm),
        in_specs=[tok(d), pl.BlockSpec((1, 6, d), lambda b, i: (b, 0, 0)), row(d), full(w_in_p),
                  tok(LANES), tok(LANES), row(Q_LORA), row(KV_LORA), full(wq), full(wkv),
                  row(LANES), row(LANES), row(LANES), row(LANES),
                  row(RWKV_SLAB), full(wwh), full(wwl), row(w), row(w), full(g2), row(w), row(w), full(bd)],
        out_specs=[tok(2 * d), head_spec, head_spec, vt_spec] + [tok(w)] * 7,
        out_shape=[jax.ShapeDtypeStruct((bsz, s, 2 * d), BF16), head_sds, head_sds, vt_sds] + [stream_sds] * 7,
        scratch_shapes=[pltpu.VMEM((8, RWKV_SLAB), F32)],
        compiler_params=_params("parallel", "arbitrary"),
        name="front",
    )(x, mod3, norm_w.reshape(1, d), w_in_p, cosf, sinf, qan, kvn, wq, wkv, qg, qgs, kg, kgs,
      mu, wwh, wwl, w0, a0, g2, k_k, k_a, bd)


def _rope_kernel(pos_ref, invf_ref, cos_ref, sin_ref):
    ang = pos_ref[...] * invf_ref[...]
    cos_ref[...] = jnp.cos(ang)
    sin_ref[...] = jnp.sin(ang)


def _rope_table(pos_rep, invf_row):
    n = pos_rep.shape[0]
    tr = min(512, n)
    spec = pl.BlockSpec((tr, LANES), lambda i: (i, 0))
    sds = jax.ShapeDtypeStruct((n, LANES), F32)
    return pl.pallas_call(
        _rope_kernel,
        grid=(n // tr,),
        in_specs=[spec, pl.BlockSpec((1, LANES), lambda i: (0, 0))],
        out_specs=[spec, spec],
        out_shape=[sds, sds],
        compiler_params=_params("parallel"),
        name="rope",
    )(pos_rep, invf_row)


def _mla_prep_body(m, cos_ref, sin_ref, qan_ref, kvn_ref, wq_ref, wkv_ref,
                   qg_ref, qgs_ref, kg_ref, kgs_ref, q_out, k_out, vt_out):
    cq = m[:, 0:Q_LORA]
    ckv = m[:, Q_LORA:Q_LORA + KV_LORA]
    kr = m[:, 384:512]
    krs = m[:, 512:640]
    ql = cq * lax.rsqrt(jnp.mean(cq * cq, axis=-1, keepdims=True) + NORM_EPS) * qan_ref[...]
    kvl = ckv * lax.rsqrt(jnp.mean(ckv * ckv, axis=-1, keepdims=True) + NORM_EPS) * kvn_ref[...]
    qall = _dot(ql.astype(BF16), wq_ref[...])
    kvall = _dot(kvl.astype(BF16), wkv_ref[...])
    cosf = cos_ref[0]
    sinf = sin_ref[0]
    scale = MLA_QK ** -0.5 * LOG2_E
    hw = MLA_HEADS * HEAD_PAD
    for h in range(MLA_HEADS):
        lo, hi = h * HEAD_PAD, (h + 1) * HEAD_PAD
        qh = qall[:, lo:hi]
        qs = qall[:, hw + lo:hw + hi]
        rs = lax.rsqrt(jnp.sum(qh * qh, axis=-1, keepdims=True) * (1.0 / MLA_QK) + NORM_EPS)
        qo = (qh * rs * qg_ref[...]) * cosf + (qs * rs * qgs_ref[...]) * sinf
        q_out[0, h] = (qo * scale).astype(BF16)
        kh = kvall[:, lo:hi] + kr
        rs = lax.rsqrt(jnp.sum(kh * kh, axis=-1, keepdims=True) * (1.0 / MLA_QK) + NORM_EPS)
        ko = (kh * rs * kg_ref[...]) * cosf + (krs * rs * kgs_ref[...]) * sinf
        k_out[0, h] = ko.astype(BF16)
        extra = lax.broadcasted_iota(jnp.int32, (VT_ROWS - MLA_V, m.shape[0]), 0)
        v_t = jnp.concatenate([kvall[:, hw + lo:hw + hi].T[0:MLA_V, :],
                               jnp.where(extra == 0, 1.0, 0.0)], axis=0).astype(BF16)
        width = vt_out.shape[4]
        for c in range(vt_out.shape[2]):
            vt_out[0, h, c] = v_t[:, c * width:(c + 1) * width]
        yield


def _attn_kernel(q_ref, k_ref, vt_ref, o_ref, m_ref, acc_ref, *, tq):
    qi = pl.program_id(2)
    heads = range(q_ref.shape[1])
    qs = [q_ref[0, h] for h in heads]
    m_ref[...] = jnp.full_like(m_ref, -jnp.inf)
    acc_ref[...] = jnp.zeros_like(acc_ref)

    tk = vt_ref.shape[4]
    per = tq // tk

    def step(j, mask):
        off = pl.multiple_of(j * tk, tk)
        sts = [_dot_nt(k_ref[0, h, pl.ds(off, tk), :], qs[h]) for h in heads]
        for h in heads:
            st = sts[h] if mask is None else jnp.where(mask, sts[h], -jnp.inf)
            m = m_ref[h]
            m_new = jnp.maximum(m, jnp.max(st, axis=0, keepdims=True))
            p = jnp.exp2(st - m_new)
            alpha = jnp.exp2(m - m_new)
            m_ref[h] = m_new
            acc_ref[h] = alpha * acc_ref[h] + _dot(vt_ref[0, h, j], p.astype(BF16))

    def body(j, carry):
        step(j, None)
        return carry

    lax.fori_loop(0, qi * per, body, 0)
    keys = lax.broadcasted_iota(jnp.int32, (tk, tq), 0)
    queries = lax.broadcasted_iota(jnp.int32, (tk, tq), 1)
    for part in range(per):
        step(qi * per + part, keys + part * tk <= queries)
    out = lambda h: acc_ref[h, 0:MLA_V, :] / acc_ref[h, MLA_V:MLA_V + 1, :]
    for h in heads[::2]:
        o_ref[0, :, h * MLA_V:(h + 2) * MLA_V] = jnp.concatenate([out(h), out(h + 1)], axis=0).T.astype(BF16)


def _attention(q, k, vt, tq):
    bsz, nh, s, dh = q.shape
    dv, tk = vt.shape[3], vt.shape[4]
    hp = ATTN_HEADS
    return pl.pallas_call(
        functools.partial(_attn_kernel, tq=tq),
        grid=(bsz, nh // hp, s // tq),
        in_specs=[pl.BlockSpec((1, hp, tq, dh), lambda b, h, i: (b, h, i, 0)),
                  pl.BlockSpec((1, hp, s, dh), lambda b, h, i: (b, h, 0, 0)),
                  pl.BlockSpec((1, hp, s // tk, dv, tk), lambda b, h, i: (b, h, 0, 0, 0))],
        out_specs=pl.BlockSpec((1, tq, hp * MLA_V), lambda b, h, i: (b, i, h)),
        out_shape=jax.ShapeDtypeStruct((bsz, s, nh * MLA_V), BF16),
        scratch_shapes=[pltpu.VMEM((hp, 1, tq), F32), pltpu.VMEM((hp, dv, tq), F32)],
        compiler_params=_params("parallel", "parallel", "parallel"),
        name="attn",
    )(q, k, vt)


def _softplus(x):
    return jnp.maximum(x, 0.0) + jnp.log(1.0 + jnp.exp(-jnp.abs(x)))


def _rwkv_prep_body(slab, first_tile, mu_ref, wwh_ref, wwl_ref, w0_ref, a0_ref, g2_ref, kk_ref, ka_ref, bd_ref,
                    r_out, lw_out, k_out, v_out, a_out, b_out, g_out, last_ref):
    tm = slab.shape[0]
    w = RWKV_WIDTH
    rolled = pltpu.roll(slab, 1, 0)
    rowi = lax.broadcasted_iota(jnp.int32, slab.shape, 0)
    carried = jnp.where(first_tile, 0.0, last_ref[0:1, :])
    prev = jnp.where(rowi == 0, carried, rolled)
    last_ref[0:1, :] = slab[tm - 1:tm, :]
    p = slab + (prev - slab) * mu_ref[...]
    r = p[:, 0:w]
    k = p[:, w:2 * w]
    v = p[:, 2 * w:3 * w]
    r_out[0] = r
    v_out[0] = v
    yield
    wa = p[:, 3 * w:3 * w + LANES]
    g_lo = p[:, 3 * w + LANES:3 * w + 2 * LANES]
    lane = lax.broadcasted_iota(jnp.int32, wa.shape, 1)
    wa = jnp.where(lane < DECAY_LORA, jnp.tanh(wa), wa)
    wah, wal = _split(wa)
    wa_o = _dot(wah, wwh_ref[...]) + (_dot(wah, wwl_ref[...]) + _dot(wal, wwh_ref[...]))
    g_out[0] = _dot(jax.nn.sigmoid(g_lo).astype(BF16), g2_ref[...])
    yield
    log_w = -_softplus(-(w0_ref[...] + wa_o[:, 0:w])) - 0.5
    lw_out[0] = -jnp.exp(log_w)
    yield
    a = jax.nn.sigmoid(a0_ref[...] + wa_o[:, w:2 * w])
    k_out[0] = k * (1.0 + (a - 1.0) * ka_ref[...])
    yield
    kk = k * kk_ref[...]
    ss = _mm_exact_rhs(kk * kk, bd_ref[...])
    kk = kk / jnp.maximum(jnp.sqrt(ss), 1e-12)
    a_out[0] = -kk
    b_out[0] = kk * a
    yield


def _wkv_kernel(r_ref, lw_ref, k_ref, v_ref, a_ref, b_ref, y_ref, s_ref):
    c = WKV_CHUNK
    n = pl.program_id(1)

    @pl.when(n == 0)
    def _():
        s_ref[...] = jnp.zeros_like(s_ref)

    row = lax.broadcasted_iota(jnp.int32, (LANES, LANES), 0)
    col = lax.broadcasted_iota(jnp.int32, (LANES, LANES), 1)
    lower_strict = row > col
    lower_incl = row >= col
    eye = row == col
    same16 = (row >> 4) == (col >> 4)
    same32 = (row >> 5) == (col >> 5)
    ident = jnp.where(eye, 1.0, 0.0).astype(F32)
    tr = lax.broadcasted_iota(jnp.int32, (c, c), 0)
    tc = lax.broadcasted_iota(jnp.int32, (c, c), 1)
    tri = jnp.where(tr >= tc, 1.0, 0.0).astype(BF16)
    first = lax.broadcasted_iota(jnp.int32, (c, LANES), 1) < RWKV_HEAD

    def stack(x):
        return jnp.concatenate([jnp.where(first, x, 0.0), jnp.where(first, 0.0, x)], axis=0)

    def mm(x, y):
        return _dot(x.astype(BF16), y.astype(BF16))

    nb = r_ref.shape[0]
    probs = [(bi, p) for bi in range(nb) for p in range(RWKV_HEADS // 2)]
    each = lambda f, *xs: [f(*args) for args in zip(*xs)]
    load = lambda ref: [ref[bi, :, p * LANES:(p + 1) * LANES] for bi, p in probs]
    r, lw, k, v, a, b = (load(ref) for ref in (r_ref, lw_ref, k_ref, v_ref, a_ref, b_ref))

    def cumsum(x):
        hi = x.astype(BF16)
        rem = x - hi.astype(F32)
        mid = rem.astype(BF16)
        lo = (rem - mid.astype(F32)).astype(BF16)
        return _dot(tri, hi) + (_dot(tri, mid) + _dot(tri, lo))

    cum = each(cumsum, lw)
    cum_c = [x[c - 1:c, :] for x in cum]
    e_in = each(jnp.exp, cum)
    e_neg = each(lambda x: jnp.exp(-x), cum)
    e_end = each(lambda x, xc: jnp.exp(xc - x), cum, cum_c)
    a_s = each(lambda x, cu, l: stack(x * jnp.exp(cu - l)).astype(BF16), a, cum, lw)
    r_s = each(lambda x, e: stack(x * e), r, e_in)
    b_s = each(lambda x, e: stack(x * e).astype(BF16), b, e_neg)
    k_s = each(lambda x, e: stack(x * e).astype(BF16), k, e_neg)
    b_e = each(lambda x, e: stack(x * e).T.astype(BF16), b, e_end)
    k_e = each(lambda x, e: stack(x * e).T.astype(BF16), k, e_end)
    v_s = each(lambda x: stack(x).astype(BF16), v)

    tt = each(lambda x1, x2, y1, y2: _dot_nt(jnp.concatenate([x1, x2.astype(BF16)], axis=0),
                                             jnp.concatenate([y1, y2], axis=0)), a_s, r_s, b_s, k_s)
    d_ab = [jnp.where(lower_strict, x[0:LANES, 0:LANES], 0.0) for x in tt]
    e_ak = [jnp.where(lower_strict, x[0:LANES, LANES:], 0.0).astype(BF16) for x in tt]
    f_rb = [jnp.where(lower_incl, x[LANES:, 0:LANES], 0.0).astype(BF16) for x in tt]
    f_rk = [jnp.where(lower_incl, x[LANES:, LANES:], 0.0).astype(BF16) for x in tt]

    d16 = [jnp.where(same16, x, 0.0) for x in d_ab]
    d32 = [jnp.where(same32, x, 0.0) for x in d_ab]
    z = each(_dot, e_ak, v_s)
    x2 = each(mm, d16, d16)
    x4 = each(mm, x2, x2)
    x8 = each(mm, x4, x4)
    t = [ident + x for x in d16]
    t = each(lambda t_, x: t_ + mm(t_, x), t, x2)
    t = each(lambda t_, x: t_ + mm(t_, x), t, x4)
    t = each(lambda t_, x: t_ + mm(t_, x), t, x8)
    t = each(lambda t_, hi, lo: t_ + mm(mm(t_, hi - lo), t_), t, d32, d16)
    t = each(lambda t_, hi, lo: t_ + mm(mm(t_, hi - lo), t_), t, d_ab, d32)

    au = each(lambda t_, x, zz: mm(t_, jnp.concatenate([x, zz.astype(BF16)], axis=1)).astype(BF16),
              t, a_s, z)
    g = each(_dot, f_rb, au)
    y0 = each(lambda gg, f, vv: gg[:, LANES:] + _dot(f, vv), g, f_rk, v_s)
    r1 = each(lambda x, gg: x + gg[:, 0:LANES], r_s, g)
    hmat = each(_dot, b_e, au)
    m_mat = each(lambda h, xc: jnp.where(eye, jnp.exp(xc), 0.0) + h[:, 0:LANES], hmat, cum_c)
    n_mat = each(lambda h, ke, vv: h[:, LANES:] + _dot(ke, vv), hmat, k_e, v_s)

    for i, (bi, p) in enumerate(probs):
        s0 = s_ref[i]
        s0b = s0.astype(BF16)
        ys = _dot(r1[i].astype(BF16), s0b) + y0[i]
        s_ref[i] = _dot(m_mat[i].astype(BF16), s0b) + n_mat[i]
        y_ref[bi, :, p * LANES:(p + 1) * LANES] = ys[0:c, :] + ys[c:, :]


def _wkv(r, lw, k, v, a, b, nb):
    bsz, s, w = r.shape
    spec = pl.BlockSpec((nb, WKV_CHUNK, w), lambda bi, n: (bi, n, 0))
    return pl.pallas_call(
        _wkv_kernel,
        grid=(bsz // nb, s // WKV_CHUNK),
        in_specs=[spec] * 6,
        out_specs=spec,
        out_shape=jax.ShapeDtypeStruct((bsz, s, w), F32),
        scratch_shapes=[pltpu.VMEM((nb * RWKV_HEADS // 2, LANES, LANES), F32)],
        compiler_params=_params("parallel", "arbitrary"),
        name="wkv",
    )(r, lw, k, v, a, b)


def _post_kernel(x_ref, mod_ref, y_ref, r_ref, k_ref, v_ref, g_ref, o_ref, gates_ref,
                 lnw_ref, lnb_ref, rk_ref, bd_ref, wor_ref, woa_ref, wout_ref, nfw_ref,
                 wrh_ref, wrl_ref, br_ref,
                 x1_ref, h2_ref, route_ref, cnt_ref):
    tiles = [_post_tile(part, x_ref, mod_ref, y_ref, r_ref, k_ref, v_ref, g_ref, o_ref, gates_ref,
                        lnw_ref, lnb_ref, rk_ref, bd_ref, wor_ref, woa_ref, wout_ref, nfw_ref,
                        wrh_ref, wrl_ref, br_ref, x1_ref, h2_ref, route_ref, cnt_ref)
             for part in range(x_ref.shape[1] // ROUTE_TILE)]
    _trace_in_turn(tiles)


def _post_tile(part, x_ref, mod_ref, y_ref, r_ref, k_ref, v_ref, g_ref, o_ref, gates_ref,
               lnw_ref, lnb_ref, rk_ref, bd_ref, wor_ref, woa_ref, wout_ref, nfw_ref,
               wrh_ref, wrl_ref, br_ref, x1_ref, h2_ref, route_ref, cnt_ref):
    tm = ROUTE_TILE
    rows = slice(part * tm, (part + 1) * tm)
    d = x_ref.shape[2]
    bd = bd_ref[...]
    inv_n = 1.0 / RWKV_HEAD
    y = y_ref[0, rows, :]
    seg = lambda t: _dot(t.astype(BF16), bd)
    mu = seg(y) * inv_n
    dlt = y - mu
    var = seg(dlt * dlt) * inv_n
    yn = dlt * lax.rsqrt(var + GN_EPS) * lnw_ref[...] + lnb_ref[...]
    yield
    v = v_ref[0, rows, :]
    bonus = seg(r_ref[0, rows, :] * k_ref[0, rows, :] * rk_ref[...]) * v
    z = (yn + bonus) * g_ref[0, rows, :]
    o_b = _dot(z.astype(BF16), wor_ref[...])
    o_a = _dot(o_ref[0, rows, :], woa_ref[...])
    yield
    ga = jax.nn.sigmoid(gates_ref[0, rows, 0:d].astype(F32))
    gb = jax.nn.sigmoid(gates_ref[0, rows, d:2 * d].astype(F32))
    merged = ga * o_a + gb * o_b
    mix = _dot(merged.astype(BF16), wout_ref[...])
    yield
    x1 = x_ref[0, rows, :] + mod_ref[0, 2:3, :] * mix
    x1_ref[0, rows, :] = x1
    ms = jnp.mean(x1 * x1, axis=-1, keepdims=True)
    h2 = x1 * lax.rsqrt(ms + NORM_EPS) * nfw_ref[...]
    h2 = h2 * (1.0 + mod_ref[0, 4:5, :]) + mod_ref[0, 3:4, :]
    h2_ref[0, rows, :] = h2
    yield

    hh, hl = _split(h2)
    logits = _dot(hh, wrh_ref[...]) + (_dot(hh, wrl_ref[...]) + _dot(hl, wrh_ref[...])) + br_ref[...]
    lane = lax.broadcasted_iota(jnp.int32, (tm, LANES), 1)
    lanef = lane.astype(F32)
    cur = jnp.where(lane < N_EXPERTS, logits, -jnp.inf)
    yield
    vals, idxs, hots = [], [], []
    for _ in range(TOP_K):
        mx = jnp.max(cur, axis=-1, keepdims=True)
        idx = jnp.min(jnp.where(cur == mx, lanef, float(LANES)), axis=-1, keepdims=True)
        hot = lanef == idx
        cur = jnp.where(hot, -jnp.inf, cur)
        vals.append(mx)
        idxs.append(idx)
        hots.append(hot)
    exps = [jnp.exp(vv - vals[0]) for vv in vals]
    den = exps[0] + exps[1] + exps[2] + exps[3]
    yield
    sel = jnp.zeros((tm, LANES), F32)
    for hot in hots:
        sel = sel + jnp.where(hot, 1.0, 0.0)
    ri = lax.broadcasted_iota(jnp.int32, (tm, tm), 0)
    ci = lax.broadcasted_iota(jnp.int32, (tm, tm), 1)
    below = jnp.where(ri > ci, 1.0, 0.0).astype(BF16)
    before = _dot(below, sel.astype(BF16))
    counts = jnp.broadcast_to(jnp.sum(sel, axis=0, keepdims=True), (8, LANES))
    er = lax.broadcasted_iota(jnp.int32, (LANES, LANES), 0)
    ec = lax.broadcasted_iota(jnp.int32, (LANES, LANES), 1)
    lower = jnp.where(er < ec, 1.0, 0.0).astype(BF16)
    start = _dot(counts.astype(BF16), lower)[0:1, :]
    route = jnp.zeros((tm, LANES), F32)
    for j in range(TOP_K):
        pos = jnp.sum(jnp.where(hots[j], before + start, 0.0), axis=-1, keepdims=True)
        route = jnp.where(lane == j, pos, route)
        route = jnp.where(lane == TOP_K + j, exps[j] / den, route)
    route_ref[0, rows, :] = route
    cnt_ref[0, part] = counts
    yield


def _post(x, mod3, y, r, k, v, g, o, gates, lnw, lnb, rk, bd, wor, woa, wout, nfw, wrh, wrl, br, tm):
    bsz, s, d = x.shape
    w = RWKV_WIDTH
    parts = tm // ROUTE_TILE
    tok = lambda n: pl.BlockSpec((1, tm, n), lambda b, i: (b, i, 0))
    row = lambda n: pl.BlockSpec((1, n), lambda b, i: (0, 0))
    full = lambda a: pl.BlockSpec(a.shape, lambda b, i: (0,) * a.ndim)
    return pl.pallas_call(
        _post_kernel,
        grid=(bsz, s // tm),
        in_specs=[tok(d), pl.BlockSpec((1, 6, d), lambda b, i: (b, 0, 0)),
                  tok(w), tok(w), tok(w), tok(w), tok(w),
                  tok(MLA_HEADS * MLA_V), tok(2 * d),
                  row(w), row(w), row(w), full(bd), full(wor), full(woa), full(wout), row(d),
                  full(wrh), full(wrl), row(LANES)],
        out_specs=[tok(d), tok(d), tok(LANES), pl.BlockSpec((1, parts, 8, LANES), lambda b, i: (b, i, 0, 0))],
        out_shape=[jax.ShapeDtypeStruct((bsz, s, d), F32),
                   jax.ShapeDtypeStruct((bsz, s, d), F32),
                   jax.ShapeDtypeStruct((bsz, s, LANES), F32),
                   jax.ShapeDtypeStruct((bsz, s // ROUTE_TILE, 8, LANES), F32)],
        compiler_params=_params("parallel", "parallel"),
        name="post",
    )(x, mod3, y, r, k, v, g, o, gates, lnw, lnb, rk, bd, wor, woa, wout, nfw, wrh, wrl, br)


def _load_rows(ref, n):
    nc = ref.shape[0] // n
    return jnp.concatenate([ref[pl.ds(c, n, stride=nc), :] for c in range(nc)], axis=1)


def _store_rows(ref, val):
    nc = val.shape[1] // LANES
    for c in range(nc):
        ref[pl.ds(c, val.shape[0], stride=nc), :] = val[:, c * LANES:(c + 1) * LANES]


def _run_copies(tile, n_ref, src_ref, dst_ref, tm, make_copy):
    nbits = tm.bit_length()
    for b in range(nbits):
        group = tile * nbits + b

        def piece(j, carry, group=group, size=1 << b):
            k = group * N_EXPERTS + j
            make_copy(src_ref[k], dst_ref[k], size).start()
            return carry

        lax.fori_loop(0, n_ref[group], piece, 0)


def _pieces(count, limit, fn):
    def emit(bits):
        for b in bits:
            size = 1 << b
            done = count & ~(2 * size - 1)

            @pl.when((count & size) != 0)
            def _():
                fn(done, size)

    bits = range(limit.bit_length() - 1, -1, -1)
    large = [b for b in bits if b >= RARE_PIECE_BIT]
    if large:
        @pl.when(count >= (1 << RARE_PIECE_BIT))
        def _():
            emit(large)
    emit([b for b in bits if b < RARE_PIECE_BIT])


def _zero_fill(gap_ref, xs_ref, zero_ref, sem, nc):
    bm = MOE_BLOCK
    zero_ref[...] = jnp.zeros_like(zero_ref)

    def piece(dst, size):
        return pltpu.make_async_copy(zero_ref.at[pl.ds(0, size * nc), :],
                                     xs_ref.at[pl.ds(pl.multiple_of(dst * nc, nc), size * nc), :], sem)

    def sweep(act):
        def per_expert(e, carry):
            g0 = gap_ref[e]
            _pieces(gap_ref[N_EXPERTS + e], bm, lambda done, size: act(piece(g0 + done, size)))
            return carry

        def tail(blk, carry):
            act(piece(gap_ref[2 * N_EXPERTS] + blk * bm, bm))
            return carry

        lax.fori_loop(0, N_EXPERTS, per_expert, 0)
        lax.fori_loop(0, gap_ref[2 * N_EXPERTS + 1], tail, 0)

    sweep(lambda cp: cp.start())
    sweep(lambda cp: cp.wait())


def _dispatch_kernel(cnt_ref, off_ref, dst_ref, gap_ref, h_ref, route_ref, xs_ref, sorted_ref, zero_ref,
                     sem, zero_sem):
    tm = h_ref.shape[0]

    @pl.when(pl.program_id(0) == 0)
    def _():
        _zero_fill(gap_ref, xs_ref, zero_ref, zero_sem, h_ref.shape[1] // LANES)

    pos_t = route_ref[...].T
    slot = lax.broadcasted_iota(jnp.int32, (TOP_K * tm, tm), 0).astype(F32)
    perm = jnp.where(slot == pos_t[0:1, :], 1.0, 0.0)
    for j in range(1, TOP_K):
        perm = perm + jnp.where(slot == pos_t[j:j + 1, :], 1.0, 0.0)
    i = pl.program_id(0)
    cur = i % 2
    buf = sorted_ref.at[cur]
    _store_rows(buf, _dot(perm.astype(BF16), h_ref[...].astype(BF16)))
    nc = h_ref.shape[1] // LANES

    def make_copy(src, dst, size):
        return pltpu.make_async_copy(buf.at[pl.ds(pl.multiple_of(src * nc, nc), size * nc), :],
                                     xs_ref.at[pl.ds(pl.multiple_of(dst * nc, nc), size * nc), :], sem.at[cur])

    _run_copies(i, cnt_ref, off_ref, dst_ref, tm, make_copy)

    def drain(which):
        pltpu.make_async_copy(sorted_ref.at[which], xs_ref.at[pl.ds(0, TOP_K * tm * nc), :], sem.at[which]).wait()

    @pl.when(i > 0)
    def _():
        drain(1 - cur)

    @pl.when(i == pl.num_programs(0) - 1)
    def _():
        drain(cur)


def _dispatch(cnt_tab, off_tab, dst_tab, gap_tab, h2, route, n_rows, tm):
    t, d = h2.shape
    nc = d // LANES
    return pl.pallas_call(
        _dispatch_kernel,
        grid_spec=pltpu.PrefetchScalarGridSpec(
            num_scalar_prefetch=4,
            grid=(t // tm,),
            in_specs=[pl.BlockSpec((tm, d), lambda i, *_: (i, 0)),
                      pl.BlockSpec((tm, LANES), lambda i, *_: (i, 0))],
            out_specs=pl.BlockSpec(memory_space=pl.ANY),
            scratch_shapes=[pltpu.VMEM((2, TOP_K * tm * nc, LANES), F32), pltpu.VMEM((MOE_BLOCK * nc, LANES), F32),
                            pltpu.SemaphoreType.DMA((2,)), pltpu.SemaphoreType.DMA(())]),
        out_shape=jax.ShapeDtypeStruct((n_rows * nc, LANES), F32),
        compiler_params=_params("arbitrary"),
        name="dispatch",
    )(cnt_tab, off_tab, dst_tab, gap_tab, h2, route)


def _split_gate_up_kernel(w_ref, g_ref, u_ref, t_ref):
    half = LANES // 2
    nblk = w_ref.shape[1] // LANES
    for c in range(w_ref.shape[2] // LANES):
        w_t = w_ref[0, :, c * LANES:(c + 1) * LANES].T
        for b in range(nblk):
            t_ref[b] = w_t[:, b * LANES:(b + 1) * LANES]
        for out_ref, first in ((g_ref, 0), (u_ref, 1)):
            rows = [t_ref[b, pl.ds(first, half, stride=2), :] for b in range(nblk)]
            out_ref[0, c * half:(c + 1) * half, :] = jnp.concatenate(rows, axis=1).astype(BF16)


def _split_gate_up(w):
    e, d, ff2 = w.shape
    ff = ff2 // 2
    spec = pl.BlockSpec((1, ff, d), lambda i: (i, 0, 0))
    sds = jax.ShapeDtypeStruct((e, ff, d), BF16)
    return pl.pallas_call(
        _split_gate_up_kernel,
        grid=(e,),
        in_specs=[pl.BlockSpec((1, d, ff2), lambda i: (i, 0, 0))],
        out_specs=[spec, spec],
        out_shape=[sds, sds],
        scratch_shapes=[pltpu.VMEM((d // LANES, LANES, LANES), F32)],
        compiler_params=_params("parallel"),
        name="split_gate_up",
    )(w)


def _moe_kernel(blk_e_ref, valid_ref, xs_ref, wg_ref, wu_ref, bg_ref, bu_ref, wd_ref, bd_ref, ys_ref):
    del blk_e_ref
    i = pl.program_id(0)
    valid = valid_ref[i]

    @pl.when(valid > 0)
    def _():
        x = _load_rows(xs_ref, MOE_BLOCK).astype(BF16)
        gate = _dot_nt(x, wg_ref[0]) + bg_ref[0]
        up = _dot_nt(x, wu_ref[0]) + bu_ref[0]
        gate = jnp.minimum(gate, SWIGLU_LIMIT)
        up = jnp.clip(up, -SWIGLU_LIMIT, SWIGLU_LIMIT)
        act = (up + 1.0) * gate * jax.nn.sigmoid(SWIGLU_ALPHA * gate)
        _store_rows(ys_ref, _dot(act.astype(BF16), wd_ref[0]) + bd_ref[0])

    @pl.when(valid == 0)
    def _():
        ys_ref[...] = jnp.zeros_like(ys_ref)


def _moe(blk_e, blk_valid, xs, wg, wu, bg, bu, wd, bd):
    ff, d = wg.shape[1], wg.shape[2]
    nc = d // LANES
    n_rows = xs.shape[0] // nc
    bm = MOE_BLOCK
    wspec = lambda k, n: pl.BlockSpec((1, k, n), lambda i, be, nu: (be[i], 0, 0))
    return pl.pallas_call(
        _moe_kernel,
        grid_spec=pltpu.PrefetchScalarGridSpec(
            num_scalar_prefetch=2,
            grid=(n_rows // bm,),
            in_specs=[pl.BlockSpec((bm * nc, LANES), lambda i, be, nu: (i, 0)),
                      wspec(ff, d), wspec(ff, d), wspec(1, ff), wspec(1, ff), wspec(ff, d), wspec(1, d)],
            out_specs=pl.BlockSpec((bm * nc, LANES), lambda i, be, nu: (i, 0))),
        out_shape=jax.ShapeDtypeStruct((n_rows * nc, LANES), F32),
        compiler_params=_params("arbitrary"),
        name="moe",
    )(blk_e, blk_valid, xs, wg, wu, bg, bu, wd, bd)


def _combine_kernel(cnt_ref, off_ref, dst_ref, x1_ref, route_ref, mod_ref, ys_ref, o_ref, rows_ref, sem):
    tm = x1_ref.shape[1]
    tile = pl.program_id(0) * pl.num_programs(1) + pl.program_id(1)
    n_tiles = pl.num_programs(0) * pl.num_programs(1)
    nc = x1_ref.shape[2] // LANES
    cur = tile % 2

    def fetch(which_tile):
        which = which_tile % 2

        def make_copy(dst, src, size):
            return pltpu.make_async_copy(
                ys_ref.at[pl.ds(pl.multiple_of(src * nc, nc), size * nc), :],
                rows_ref.at[which, pl.ds(pl.multiple_of(dst * nc, nc), size * nc), :], sem.at[which])

        _run_copies(which_tile, cnt_ref, off_ref, dst_ref, tm, make_copy)

    @pl.when(tile == 0)
    def _():
        fetch(tile)

    @pl.when(tile + 1 < n_tiles)
    def _():
        fetch(tile + 1)

    route = route_ref[0]
    slot = lax.broadcasted_iota(jnp.int32, (tm, TOP_K * tm), 1).astype(F32)
    mix = jnp.where(slot == route[:, 0:1], route[:, TOP_K:TOP_K + 1], 0.0)
    for j in range(1, TOP_K):
        mix = mix + jnp.where(slot == route[:, j:j + 1], route[:, TOP_K + j:TOP_K + j + 1], 0.0)
    pltpu.make_async_copy(ys_ref.at[pl.ds(0, TOP_K * tm * nc), :], rows_ref.at[cur], sem.at[cur]).wait()
    acc = _dot(mix.astype(BF16), _load_rows(rows_ref.at[cur], TOP_K * tm).astype(BF16))
    o_ref[0] = x1_ref[0] + mod_ref[0, 5:6, :] * acc


def _combine(cnt_tab, off_tab, dst_tab, x1, route, mod3, ys, tm):
    bsz, s, d = x1.shape
    return pl.pallas_call(
        _combine_kernel,
        grid_spec=pltpu.PrefetchScalarGridSpec(
            num_scalar_prefetch=3,
            grid=(bsz, s // tm),
            in_specs=[pl.BlockSpec((1, tm, d), lambda b, i, *_: (b, i, 0)),
                      pl.BlockSpec((1, tm, LANES), lambda b, i, *_: (b, i, 0)),
                      pl.BlockSpec((1, 6, d), lambda b, i, *_: (b, 0, 0)),
                      pl.BlockSpec(memory_space=pl.ANY)],
            out_specs=pl.BlockSpec((1, tm, d), lambda b, i, *_: (b, i, 0)),
            scratch_shapes=[pltpu.VMEM((2, TOP_K * tm * d // LANES, LANES), F32), pltpu.SemaphoreType.DMA((2,))]),
        out_shape=jax.ShapeDtypeStruct((bsz, s, d), F32),
        compiler_params=_params("arbitrary", "arbitrary"),
        name="combine",
    )(cnt_tab, off_tab, dst_tab, x1, route, mod3, ys)


def _piece_tables(cnt_tab, off_tab, dst_tab, tm):
    nbits = tm.bit_length()
    bits = jnp.arange(nbits, dtype=jnp.int32)[None, :, None]
    cnt = cnt_tab[:, None, :]
    active = (cnt >> bits) & 1
    done = cnt & ~((2 << bits) - 1)
    rank = jnp.cumsum(active, axis=-1) - active
    slot = jnp.arange(N_EXPERTS, dtype=jnp.int32)[None, None, :, None]
    place = (active[:, :, None, :] == 1) & (rank[:, :, None, :] == slot)
    compact = lambda v: jnp.sum(jnp.where(place, (v[:, None, :] + done)[:, :, None, :], 0), axis=-1)
    flat = lambda v: v.reshape(-1).astype(jnp.int32)
    return flat(jnp.sum(active, axis=-1)), flat(compact(off_tab)), flat(compact(dst_tab))


def _pad_cols(a, n):
    return jnp.pad(a, ((0, 0), (0, n - a.shape[1])))


def _head_blocks(cols_main, cols_rot=None):
    k = cols_main.shape[0]
    out = jnp.zeros((k, MLA_HEADS, HEAD_PAD), F32)
    out = out.at[:, :, :cols_main.shape[2]].set(cols_main)
    return out.reshape(k, MLA_HEADS * HEAD_PAD)


def _layer(x, cond_mod, positions, w_in, q_a_norm_w, w_q_up, kv_a_norm_w, w_kv_up, q_norm_w, k_norm_w,
           w_o_mla, rwkv_mu, rwkv_w0, rwkv_w2, rwkv_a0, rwkv_a2, rwkv_g2, rwkv_k_k, rwkv_k_a, rwkv_r_k,
           rwkv_ln_w, rwkv_ln_b, rwkv_w_o, w_out, norm_mix_w, norm_ffn_w, w_router, b_router,
           w_gate_up, b_gate_up, w_down, b_down):
    bsz, s, d = x.shape
    t = bsz * s
    half = MLA_ROPE // 2
    nope, qk = MLA_NOPE, MLA_QK
    mod3 = cond_mod.reshape(bsz, 6, d)

    o_q, o_kv, o_kr = 0, Q_LORA, Q_LORA + KV_LORA
    o_slab = o_kr + MLA_ROPE
    o_gate = o_slab + RWKV_SLAB
    kr_w = w_in[:, o_kr:o_slab]
    zeros = lambda n: jnp.zeros((d, n), F32)
    kr_blk = jnp.concatenate([zeros(nope), kr_w, zeros(HEAD_PAD - qk)], axis=1)
    kr_rot = jnp.concatenate([zeros(nope), -kr_w[:, half:], kr_w[:, :half], zeros(HEAD_PAD - qk)], axis=1)
    w_in_p = jnp.concatenate([w_in[:, o_q:o_kr], kr_blk, kr_rot, w_in[:, o_slab:]], axis=1).astype(BF16)

    tm = min(256, s)

    wq3 = w_q_up.reshape(Q_LORA, MLA_HEADS, qk)
    wq_rot = jnp.concatenate([jnp.zeros((Q_LORA, MLA_HEADS, nope), F32), -wq3[:, :, nope + half:],
                              wq3[:, :, nope:nope + half]], axis=2)
    wq = jnp.concatenate([_head_blocks(wq3), _head_blocks(wq_rot)], axis=1).astype(BF16)
    wkv3 = w_kv_up.reshape(KV_LORA, MLA_HEADS, nope + MLA_V)
    wkv = jnp.concatenate([_head_blocks(wkv3[:, :, :nope]), _head_blocks(wkv3[:, :, nope:])], axis=1).astype(BF16)

    def gains(wn):
        main = jnp.pad(wn, (0, HEAD_PAD - qk)).reshape(1, HEAD_PAD)
        rot = jnp.concatenate([jnp.zeros((nope,), F32), wn[nope + half:], wn[nope:nope + half],
                               jnp.zeros((HEAD_PAD - qk,), F32)]).reshape(1, HEAD_PAD)
        return main, rot

    qg, qgs = gains(q_norm_w)
    kg, kgs = gains(k_norm_w)
    inv_freq = ROPE_THETA ** (-jnp.arange(half, dtype=F32) / half)
    per_row = LANES // half
    pos_rep = jnp.repeat(positions.astype(F32).reshape(t // per_row, per_row), half, axis=1)
    cos16, sin16 = _rope_table(pos_rep, jnp.tile(inv_freq, per_row).reshape(1, LANES))
    cos16 = cos16.reshape(bsz, s, half)
    sin16 = sin16.reshape(bsz, s, half)
    cosf = jnp.concatenate([jnp.ones((bsz, s, nope), F32), cos16, cos16,
                            jnp.ones((bsz, s, HEAD_PAD - qk), F32)], axis=-1)
    sinf = jnp.concatenate([jnp.zeros((bsz, s, nope), F32), sin16, sin16,
                            jnp.zeros((bsz, s, HEAD_PAD - qk), F32)], axis=-1)

    w = RWKV_WIDTH
    wwa = jnp.zeros((LANES, 2 * w), F32)
    wwa = wwa.at[:DECAY_LORA, :w].set(rwkv_w2).at[DECAY_LORA:, w:].set(rwkv_a2)
    wwh = wwa.astype(BF16)
    wwl = (wwa - wwh.astype(F32)).astype(BF16)
    hid = np.arange(w) // RWKV_HEAD
    bd = jnp.asarray(hid[:, None] == hid[None, :], BF16)
    gates, q, k, vt, r_, lw_, k_, v_, a_, b_, g_ = _front(
        x, mod3, norm_mix_w, w_in_p, cosf, sinf, q_a_norm_w.reshape(1, -1), kv_a_norm_w.reshape(1, -1),
        wq, wkv, qg, qgs, kg, kgs, rwkv_mu.reshape(1, -1), wwh, wwl, rwkv_w0.reshape(1, -1),
        rwkv_a0.reshape(1, -1), rwkv_g2.astype(BF16), rwkv_k_k.reshape(1, -1), rwkv_k_a.reshape(1, -1), bd,
        tm, min(ATTN_KEY_BLOCK, s))
    o = _attention(q, k, vt, min(ATTN_BLOCK, s))
    y = _wkv(r_, lw_, k_, v_, a_, b_, WKV_BATCH if bsz % WKV_BATCH == 0 else 1)

    woa = w_o_mla.astype(BF16)
    wr =_pad_cols(w_router, LANES)
    wrh = wr.astype(BF16)
    wrl = (wr - wrh.astype(F32)).astype(BF16)
    br = jnp.pad(b_router, (0, LANES - N_EXPERTS)).reshape(1, LANES)
    x1, h2, route, counts = _post(
        x, mod3, y, r_, k_, v_, g_, o, gates, rwkv_ln_w.reshape(1, -1), rwkv_ln_b.reshape(1, -1),
        rwkv_r_k.reshape(1, -1), bd, rwkv_w_o.astype(BF16), woa, w_out.astype(BF16),
        norm_ffn_w.reshape(1, -1), wrh, wrl, br, POST_TILE if s % POST_TILE == 0 else ROUTE_TILE)

    bm = MOE_BLOCK
    n_rows = t * TOP_K + N_EXPERTS * bm
    n_blocks = n_rows // bm
    cnt_tab = counts[:, :, 0, :N_EXPERTS].reshape(t // ROUTE_TILE, N_EXPERTS).astype(jnp.int32)
    total = jnp.sum(cnt_tab, axis=0)
    padded = (total + bm - 1) // bm * bm
    pad_end = jnp.cumsum(padded)
    pad_start = pad_end - padded
    off_tab = jnp.cumsum(cnt_tab, axis=1) - cnt_tab
    dst_tab = pad_start[None, :] + jnp.cumsum(cnt_tab, axis=0) - cnt_tab
    blk_start = jnp.arange(n_blocks, dtype=jnp.int32) * bm
    blk_e = jnp.minimum(jnp.sum((pad_end[None, :] <= blk_start[:, None]).astype(jnp.int32), axis=1),
                        N_EXPERTS - 1)
    blk_valid = jnp.clip((pad_start + total)[blk_e] - blk_start, 0, bm).astype(jnp.int32)
    tabs = _piece_tables(cnt_tab, off_tab, dst_tab, ROUTE_TILE)
    gap_tab = jnp.concatenate([pad_start + total, padded - total, pad_end[-1:],
                               (n_rows - pad_end[-1:]) // bm]).astype(jnp.int32)

    xs = _dispatch(*tabs, gap_tab, h2.reshape(t, d), route.reshape(t, LANES), n_rows, ROUTE_TILE)
    wg_t, wu_t = _split_gate_up(w_gate_up)
    bgu = b_gate_up.reshape(N_EXPERTS, 1, D_FF, 2)
    ys = _moe(blk_e, blk_valid, xs, wg_t, wu_t,
              bgu[..., 0], bgu[..., 1], w_down.astype(BF16), b_down.reshape(N_EXPERTS, 1, d))
    return _combine(*tabs, x1, route, mod3, ys, ROUTE_TILE)


def kernel(x, c, positions, ada_w, ada_b, norm_mix_w, norm_ffn_w, w_in, q_a_norm_w, w_q_up, kv_a_norm_w, w_kv_up, q_norm_w, k_norm_w, w_o_mla, rwkv_mu, rwkv_w0, rwkv_w2, rwkv_a0, rwkv_a2, rwkv_g2, rwkv_k_k, rwkv_k_a, rwkv_r_k, rwkv_ln_w, rwkv_ln_b, rwkv_w_o, w_out, w_router, b_router, w_gate_up, b_gate_up, w_down, b_down):
    depth = ada_w.shape[0]
    for l in range(depth):
        mod = _ada(c, ada_w[l], ada_b[l])
        x = _layer(x, mod, positions, w_in[l], q_a_norm_w[l], w_q_up[l], kv_a_norm_w[l], w_kv_up[l],
                   q_norm_w[l], k_norm_w[l], w_o_mla[l], rwkv_mu[l], rwkv_w0[l], rwkv_w2[l], rwkv_a0[l],
                   rwkv_a2[l], rwkv_g2[l], rwkv_k_k[l], rwkv_k_a[l], rwkv_r_k[l], rwkv_ln_w[l],
                   rwkv_ln_b[l], rwkv_w_o[l], w_out[l], norm_mix_w[l], norm_ffn_w[l], w_router[l],
                   b_router[l], w_gate_up[l], b_gate_up[l], w_down[l], b_down[l])
    return x
```

```python
import functools

import numpy as np
import jax
import jax.numpy as jnp
from jax import lax
from jax.experimental import pallas as pl
from jax.experimental.pallas import tpu as pltpu

F32 = jnp.float32
BF16 = jnp.bfloat16

D_MODEL = 1024
MLA_HEADS = 8
MLA_NOPE = 64
MLA_ROPE = 32
MLA_QK = MLA_NOPE + MLA_ROPE
MLA_V = 64
VT_ROWS = 80
Q_LORA = 256
KV_LORA = 128
ROPE_THETA = 10000.0
RWKV_HEADS = 8
RWKV_HEAD = 64
RWKV_WIDTH = RWKV_HEADS * RWKV_HEAD
DECAY_LORA = 64
AAA_LORA = 64
GATE_LORA = 128
RWKV_SLAB = 3 * RWKV_WIDTH + DECAY_LORA + AAA_LORA + GATE_LORA
GN_EPS = 64e-5
N_EXPERTS = 32
TOP_K = 4
D_FF = D_MODEL
SWIGLU_LIMIT = 7.0
SWIGLU_ALPHA = 1.702
NORM_EPS = 1e-6
LOG2_E = 1.4426950408889634

LANES = 128
HEAD_PAD = 128
MLA_COLS = 640
WKV_CHUNK = 64
WKV_BATCH = 4
ATTN_BLOCK = 512
ATTN_KEY_BLOCK = 512
ATTN_HEADS = 8
ROUTE_TILE = 256
POST_TILE = 512
RARE_PIECE_BIT = 6
MOE_BLOCK = 512
VMEM_LIMIT = 56 * 1024 * 1024


def _dot(a, b):
    return jnp.dot(a, b, preferred_element_type=F32)


def _dot_nt(a, b):
    return lax.dot_general(a, b, (((1,), (1,)), ((), ())), preferred_element_type=F32)


def _split(x):
    hi = x.astype(BF16)
    lo = (x - hi.astype(F32)).astype(BF16)
    return hi, lo


def _mm3(a, b):
    ah, al = _split(a)
    bh, bl = _split(b)
    return _dot(ah, bh) + (_dot(ah, bl) + _dot(al, bh))


def _trace_in_turn(stages):
    live = list(stages)
    while live:
        for gen in list(live):
            try:
                next(gen)
            except StopIteration:
                live.remove(gen)


def _params(*sem):
    return pltpu.CompilerParams(dimension_semantics=sem, vmem_limit_bytes=VMEM_LIMIT)


def _ada_kernel(c_ref, w_ref, b_ref, o_ref):
    c = c_ref[...]
    cond = c * jax.nn.sigmoid(c)
    o_ref[...] = _mm3(cond, w_ref[...]) + b_ref[...]


def _ada(c, w, b):
    bsz, d = c.shape
    n = w.shape[1]
    tn = 1024
    return pl.pallas_call(
        _ada_kernel,
        grid=(n // tn,),
        in_specs=[pl.BlockSpec((bsz, d), lambda j: (0, 0)),
                  pl.BlockSpec((d, tn), lambda j: (0, j)),
                  pl.BlockSpec((1, tn), lambda j: (0, j))],
        out_specs=pl.BlockSpec((bsz, tn), lambda j: (0, j)),
        out_shape=jax.ShapeDtypeStruct((bsz, n), F32),
        compiler_params=_params("parallel"),
        name="ada",
    )(c, w, b.reshape(1, n))


def _front_kernel(x_ref, mod_ref, nw_ref, w_ref,
                  cos_ref, sin_ref, qan_ref, kvn_ref, wq_ref, wkv_ref, qg_ref, qgs_ref, kg_ref, kgs_ref,
                  mu_ref, wws_ref, w0_ref, a0_ref, g2_ref, kk_ref, ka_ref, bd_ref,
                  gates_ref, q_out, k_out, vt_out, r_out, lw_out, k2_out, v_out, a_out, b_out, g_out,
                  last_ref):
    x = x_ref[0]
    ms = jnp.mean(x * x, axis=-1, keepdims=True)
    y = x * lax.rsqrt(ms + NORM_EPS) * nw_ref[...]
    h = y * (1.0 + mod_ref[0, 1:2, :]) + mod_ref[0, 0:1, :]
    hb = h.astype(BF16)
    mla = _dot(hb, w_ref[:, 0:MLA_COLS])
    slab = _dot(hb, w_ref[:, MLA_COLS:MLA_COLS + RWKV_SLAB])
    gates_ref[0] = _dot(hb, w_ref[:, MLA_COLS + RWKV_SLAB:]).astype(BF16)
    mla_prep = _mla_prep_body(mla, cos_ref, sin_ref, qan_ref, kvn_ref, wq_ref, wkv_ref,
                              qg_ref, qgs_ref, kg_ref, kgs_ref, q_out, k_out, vt_out)
    rwkv_prep = _rwkv_prep_body(slab, pl.program_id(1) == 0, mu_ref, wws_ref, w0_ref, a0_ref, g2_ref,
                                kk_ref, ka_ref, bd_ref, r_out, lw_out, k2_out, v_out, a_out, b_out, g_out,
                                last_ref)
    _trace_in_turn([mla_prep, rwkv_prep])


def _front(x, mod3, norm_w, w_in_p, cosf, sinf, qan, kvn, wq, wkv, qg, qgs, kg, kgs,
           mu, wws, w0, a0, g2, k_k, k_a, bd, tm, tk):
    bsz, s, d = x.shape
    w = RWKV_WIDTH
    tok = lambda n: pl.BlockSpec((1, tm, n), lambda b, i: (b, i, 0))
    row = lambda n: pl.BlockSpec((1, n), lambda b, i: (0, 0))
    full = lambda a: pl.BlockSpec(a.shape, lambda b, i: (0,) * a.ndim)
    head_spec = pl.BlockSpec((1, MLA_HEADS, tm, HEAD_PAD), lambda b, i: (b, 0, i, 0))
    head_sds = jax.ShapeDtypeStruct((bsz, MLA_HEADS, s, HEAD_PAD), BF16)
    if tk >= tm:
        per = tk // tm
        vt_spec = pl.BlockSpec((1, MLA_HEADS, 1, VT_ROWS, tm), lambda b, i: (b, 0, i // per, 0, i % per))
    else:
        vt_spec = pl.BlockSpec((1, MLA_HEADS, tm // tk, VT_ROWS, tk), lambda b, i: (b, 0, i, 0, 0))
    vt_sds = jax.ShapeDtypeStruct((bsz, MLA_HEADS, s // tk, VT_ROWS, tk), BF16)
    stream_sds = jax.ShapeDtypeStruct((bsz, s, w), F32)
    return pl.pallas_call(
        _front_kernel,
        grid=(bsz, s // tm),
        in_specs=[tok(d), pl.BlockSpec((1, 6, d), lambda b, i: (b, 0, 0)), row(d), full(w_in_p),
                  tok(LANES), tok(LANES), row(Q_LORA), row(KV_LORA), full(wq), full(wkv),
                  row(LANES), row(LANES), row(LANES), row(LANES),
                  row(RWKV_SLAB), full(wws), row(w), row(w), full(g2), row(w), row(w), full(bd)],
        out_specs=[tok(2 * d), head_spec, head_spec, vt_spec] + [tok(w)] * 7,
        out_shape=[jax.ShapeDtypeStruct((bsz, s, 2 * d), BF16), head_sds, head_sds, vt_sds] + [stream_sds] * 7,
        scratch_shapes=[pltpu.VMEM((8, RWKV_SLAB), F32)],
        compiler_params=_params("parallel", "arbitrary"),
        name="front",
    )(x, mod3, norm_w.reshape(1, d), w_in_p, cosf, sinf, qan, kvn, wq, wkv, qg, qgs, kg, kgs,
      mu, wws, w0, a0, g2, k_k, k_a, bd)


def _rope_kernel(pos_ref, invf_ref, cos_ref, sin_ref):
    ang = pos_ref[...] * invf_ref[...]
    cos_ref[...] = jnp.cos(ang)
    sin_ref[...] = jnp.sin(ang)


def _rope_table(pos_rep, invf_row):
    n = pos_rep.shape[0]
    tr = min(512, n)
    spec = pl.BlockSpec((tr, LANES), lambda i: (i, 0))
    sds = jax.ShapeDtypeStruct((n, LANES), F32)
    return pl.pallas_call(
        _rope_kernel,
        grid=(n // tr,),
        in_specs=[spec, pl.BlockSpec((1, LANES), lambda i: (0, 0))],
        out_specs=[spec, spec],
        out_shape=[sds, sds],
        compiler_params=_params("parallel"),
        name="rope",
    )(pos_rep, invf_row)


def _mla_prep_body(m, cos_ref, sin_ref, qan_ref, kvn_ref, wq_ref, wkv_ref,
                   qg_ref, qgs_ref, kg_ref, kgs_ref, q_out, k_out, vt_out):
    cq = m[:, 0:Q_LORA]
    ckv = m[:, Q_LORA:Q_LORA + KV_LORA]
    kr = m[:, 384:512]
    krs = m[:, 512:640]
    ql = cq * lax.rsqrt(jnp.mean(cq * cq, axis=-1, keepdims=True) + NORM_EPS) * qan_ref[...]
    kvl = ckv * lax.rsqrt(jnp.mean(ckv * ckv, axis=-1, keepdims=True) + NORM_EPS) * kvn_ref[...]
    qall = _dot(ql.astype(BF16), wq_ref[...])
    kvall = _dot(kvl.astype(BF16), wkv_ref[...])
    cosf = cos_ref[0]
    sinf = sin_ref[0]
    scale = MLA_QK ** -0.5 * LOG2_E
    hw = MLA_HEADS * HEAD_PAD
    for h in range(MLA_HEADS):
        lo, hi = h * HEAD_PAD, (h + 1) * HEAD_PAD
        qh = qall[:, lo:hi]
        qs = qall[:, hw + lo:hw + hi]
        rs = lax.rsqrt(jnp.sum(qh * qh, axis=-1, keepdims=True) * (1.0 / MLA_QK) + NORM_EPS)
        qo = (qh * rs * qg_ref[...]) * cosf + (qs * rs * qgs_ref[...]) * sinf
        q_out[0, h] = (qo * scale).astype(BF16)
        kh = kvall[:, lo:hi] + kr
        rs = lax.rsqrt(jnp.sum(kh * kh, axis=-1, keepdims=True) * (1.0 / MLA_QK) + NORM_EPS)
        ko = (kh * rs * kg_ref[...]) * cosf + (krs * rs * kgs_ref[...]) * sinf
        k_out[0, h] = ko.astype(BF16)
        extra = lax.broadcasted_iota(jnp.int32, (VT_ROWS - MLA_V, m.shape[0]), 0)
        v_t = jnp.concatenate([kvall[:, hw + lo:hw + hi].T[0:MLA_V, :],
                               jnp.where(extra == 0, 1.0, 0.0)], axis=0).astype(BF16)
        width = vt_out.shape[4]
        for c in range(vt_out.shape[2]):
            vt_out[0, h, c] = v_t[:, c * width:(c + 1) * width]
        yield


def _attn_kernel(q_ref, k_ref, vt_ref, o_ref, m_ref, acc_ref, *, tq):
    qi = pl.program_id(2)
    heads = range(q_ref.shape[1])
    qs = [q_ref[0, h] for h in heads]
    m_ref[...] = jnp.full_like(m_ref, -jnp.inf)
    acc_ref[...] = jnp.zeros_like(acc_ref)

    tk = vt_ref.shape[4]
    per = tq // tk

    def step(j, causal=False, k0=0, nk=tk, q0=0, nq=tq, kbase=0):
        off = pl.multiple_of(j * tk, tk)
        sts = [_dot_nt(k_ref[0, h, pl.ds(off + k0, nk), :], qs[h][q0:q0 + nq]) for h in heads]
        if causal:
            keys = lax.broadcasted_iota(jnp.int32, (nk, nq), 0) + (k0 + kbase)
            queries = lax.broadcasted_iota(jnp.int32, (nk, nq), 1) + q0
        for h in heads:
            st = jnp.where(keys <= queries, sts[h], -jnp.inf) if causal else sts[h]
            m = m_ref[h, :, q0:q0 + nq]
            m_new = jnp.maximum(m, jnp.max(st, axis=0, keepdims=True))
            p = jnp.exp2(st - m_new)
            alpha = jnp.exp2(m - m_new)
            m_ref[h, :, q0:q0 + nq] = m_new
            acc_ref[h, :, q0:q0 + nq] = (alpha * acc_ref[h, :, q0:q0 + nq]
                                         + _dot(vt_ref[0, h, j, :, k0:k0 + nk], p.astype(BF16)))

    def body(j, carry):
        step(j)
        return carry

    lax.fori_loop(0, qi * per, body, 0)
    if per == 1:
        step(qi, True, 0, tk // 2)
        step(qi, True, tk // 2, tk // 2, tq // 2, tq // 2)
    else:
        for part in range(per):
            step(qi * per + part, True, 0, tk, part * tk, tq - part * tk, part * tk)
    out = lambda h: acc_ref[h, 0:MLA_V, :] / acc_ref[h, MLA_V:MLA_V + 1, :]
    for h in heads[::2]:
        o_ref[0, :, h * MLA_V:(h + 2) * MLA_V] = jnp.concatenate([out(h), out(h + 1)], axis=0).T.astype(BF16)


def _attention(q, k, vt, tq):
    bsz, nh, s, dh = q.shape
    dv, tk = vt.shape[3], vt.shape[4]
    hp = ATTN_HEADS
    return pl.pallas_call(
        functools.partial(_attn_kernel, tq=tq),
        grid=(bsz, nh // hp, s // tq),
        in_specs=[pl.BlockSpec((1, hp, tq, dh), lambda b, h, i: (b, h, i, 0)),
                  pl.BlockSpec((1, hp, s, dh), lambda b, h, i: (b, h, 0, 0)),
                  pl.BlockSpec((1, hp, s // tk, dv, tk), lambda b, h, i: (b, h, 0, 0, 0))],
        out_specs=pl.BlockSpec((1, tq, hp * MLA_V), lambda b, h, i: (b, i, h)),
        out_shape=jax.ShapeDtypeStruct((bsz, s, nh * MLA_V), BF16),
        scratch_shapes=[pltpu.VMEM((hp, 1, tq), F32), pltpu.VMEM((hp, dv, tq), F32)],
        compiler_params=_params("parallel", "parallel", "parallel"),
        name="attn",
    )(q, k, vt)


def _softplus(x):
    return jnp.maximum(x, 0.0) + jnp.log(1.0 + jnp.exp(-jnp.abs(x)))


def _rwkv_prep_body(slab, first_tile, mu_ref, wws_ref, w0_ref, a0_ref, g2_ref, kk_ref, ka_ref, bd_ref,
                    r_out, lw_out, k_out, v_out, a_out, b_out, g_out, last_ref):
    tm = slab.shape[0]
    w = RWKV_WIDTH
    rolled = pltpu.roll(slab, 1, 0)
    rowi = lax.broadcasted_iota(jnp.int32, slab.shape, 0)
    carried = jnp.where(first_tile, 0.0, last_ref[0:1, :])
    prev = jnp.where(rowi == 0, carried, rolled)
    last_ref[0:1, :] = slab[tm - 1:tm, :]
    p = slab + (prev - slab) * mu_ref[...]
    r = p[:, 0:w]
    k = p[:, w:2 * w]
    v = p[:, 2 * w:3 * w]
    r_out[0] = r
    v_out[0] = v
    yield
    wa = p[:, 3 * w:3 * w + LANES]
    g_lo = p[:, 3 * w + LANES:3 * w + 2 * LANES]
    lane = lax.broadcasted_iota(jnp.int32, wa.shape, 1)
    wa = jnp.where(lane < DECAY_LORA, jnp.tanh(wa), wa)
    wah, wal = _split(wa)
    wa_o = _dot(jnp.concatenate([wah, wah, wal], axis=1), wws_ref[...])
    g_out[0] = _dot(jax.nn.sigmoid(g_lo).astype(BF16), g2_ref[...])
    yield
    log_w = -_softplus(-(w0_ref[...] + wa_o[:, 0:w])) - 0.5
    lw_out[0] = -jnp.exp(log_w)
    yield
    a = jax.nn.sigmoid(a0_ref[...] + wa_o[:, w:2 * w])
    k_out[0] = k * (1.0 + (a - 1.0) * ka_ref[...])
    yield
    kk = k * kk_ref[...]
    ss = _dot((kk * kk).astype(BF16), bd_ref[...])
    kk = kk / jnp.maximum(jnp.sqrt(ss), 1e-12)
    a_out[0] = -kk
    b_out[0] = kk * a
    yield


def _wkv_kernel(r_ref, lw_ref, k_ref, v_ref, a_ref, b_ref, y_ref, s_ref):
    c = WKV_CHUNK
    n = pl.program_id(1)

    @pl.when(n == 0)
    def _():
        s_ref[...] = jnp.zeros_like(s_ref)

    row = lax.broadcasted_iota(jnp.int32, (LANES, LANES), 0)
    col = lax.broadcasted_iota(jnp.int32, (LANES, LANES), 1)
    lower_strict = row > col
    lower_incl = row >= col
    eye = row == col
    same16 = (row >> 4) == (col >> 4)
    same32 = (row >> 5) == (col >> 5)
    ident = jnp.where(eye, 1.0, 0.0).astype(F32)
    tr = lax.broadcasted_iota(jnp.int32, (c, c), 0)
    tc = lax.broadcasted_iota(jnp.int32, (c, c), 1)
    tri = jnp.where(tr >= tc, 1.0, 0.0).astype(BF16)
    first = lax.broadcasted_iota(jnp.int32, (c, LANES), 1) < RWKV_HEAD

    def stack(x):
        return jnp.concatenate([jnp.where(first, x, 0.0), jnp.where(first, 0.0, x)], axis=0)

    def mm(x, y):
        return _dot(x.astype(BF16), y.astype(BF16))

    nb = r_ref.shape[0]
    probs = [(bi, p) for bi in range(nb) for p in range(RWKV_HEADS // 2)]
    each = lambda f, *xs: [f(*args) for args in zip(*xs)]
    load = lambda ref: [ref[bi, :, p * LANES:(p + 1) * LANES] for bi, p in probs]
    r, lw, k, v, a, b = (load(ref) for ref in (r_ref, lw_ref, k_ref, v_ref, a_ref, b_ref))

    def cumsum(x):
        hi = x.astype(BF16)
        rem = x - hi.astype(F32)
        mid = rem.astype(BF16)
        lo = (rem - mid.astype(F32)).astype(BF16)
        return _dot(tri, hi) + (_dot(tri, mid) + _dot(tri, lo))

    cum = each(cumsum, lw)
    cum_c = [x[c - 1:c, :] for x in cum]
    e_in = each(jnp.exp, cum)
    e_neg = each(lambda x: jnp.exp(-x), cum)
    e_end = each(lambda x, xc: jnp.exp(xc - x), cum, cum_c)
    a_s = each(lambda x, cu, l: stack(x * jnp.exp(cu - l)).astype(BF16), a, cum, lw)
    r_s = each(lambda x, e: stack(x * e), r, e_in)
    b_s = each(lambda x, e: stack(x * e).astype(BF16), b, e_neg)
    k_s = each(lambda x, e: stack(x * e).astype(BF16), k, e_neg)
    b_e = each(lambda x, e: stack(x * e).T.astype(BF16), b, e_end)
    k_e = each(lambda x, e: stack(x * e).T.astype(BF16), k, e_end)
    v_s = each(lambda x: stack(x).astype(BF16), v)

    tt = each(lambda x1, x2, y1, y2: _dot_nt(jnp.concatenate([x1, x2.astype(BF16)], axis=0),
                                             jnp.concatenate([y1, y2], axis=0)), a_s, r_s, b_s, k_s)
    d_ab = [jnp.where(lower_strict, x[0:LANES, 0:LANES], 0.0) for x in tt]
    e_ak = [jnp.where(lower_strict, x[0:LANES, LANES:], 0.0).astype(BF16) for x in tt]
    f_rb = [jnp.where(lower_incl, x[LANES:, 0:LANES], 0.0).astype(BF16) for x in tt]
    f_rk = [jnp.where(lower_incl, x[LANES:, LANES:], 0.0).astype(BF16) for x in tt]

    d16 = [jnp.where(same16, x, 0.0) for x in d_ab]
    d32 = [jnp.where(same32, x, 0.0) for x in d_ab]
    z = each(_dot, e_ak, v_s)
    x2 = each(mm, d16, d16)
    x4 = each(mm, x2, x2)
    x8 = each(mm, x4, x4)
    t = [ident + x for x in d16]
    t = each(lambda t_, x: t_ + mm(t_, x), t, x2)
    t = each(lambda t_, x: t_ + mm(t_, x), t, x4)
    t = each(lambda t_, x: t_ + mm(t_, x), t, x8)
    t = each(lambda t_, hi, lo: t_ + mm(mm(t_, hi - lo), t_), t, d32, d16)
    t = each(lambda t_, hi, lo: t_ + mm(mm(t_, hi - lo), t_), t, d_ab, d32)

    au = each(lambda t_, x, zz: mm(t_, jnp.concatenate([x, zz.astype(BF16)], axis=1)).astype(BF16),
              t, a_s, z)
    g = each(_dot, f_rb, au)
    y0 = each(lambda gg, f, vv: gg[:, LANES:] + _dot(f, vv), g, f_rk, v_s)
    r1 = each(lambda x, gg: x + gg[:, 0:LANES], r_s, g)
    hmat = each(_dot, b_e, au)
    m_mat = each(lambda h, xc: jnp.where(eye, jnp.exp(xc), 0.0) + h[:, 0:LANES], hmat, cum_c)
    n_mat = each(lambda h, ke, vv: h[:, LANES:] + _dot(ke, vv), hmat, k_e, v_s)

    for i, (bi, p) in enumerate(probs):
        s0 = s_ref[i]
        s0b = s0.astype(BF16)
        ys = _dot(r1[i].astype(BF16), s0b) + y0[i]
        s_ref[i] = _dot(m_mat[i].astype(BF16), s0b) + n_mat[i]
        y_ref[bi, :, p * LANES:(p + 1) * LANES] = ys[0:c, :] + ys[c:, :]


def _wkv(r, lw, k, v, a, b, nb):
    bsz, s, w = r.shape
    spec = pl.BlockSpec((nb, WKV_CHUNK, w), lambda bi, n: (bi, n, 0))
    return pl.pallas_call(
        _wkv_kernel,
        grid=(bsz // nb, s // WKV_CHUNK),
        in_specs=[spec] * 6,
        out_specs=spec,
        out_shape=jax.ShapeDtypeStruct((bsz, s, w), F32),
        scratch_shapes=[pltpu.VMEM((nb * RWKV_HEADS // 2, LANES, LANES), F32)],
        compiler_params=_params("parallel", "arbitrary"),
        name="wkv",
    )(r, lw, k, v, a, b)


def _post_kernel(x_ref, mod_ref, y_ref, r_ref, k_ref, v_ref, g_ref, o_ref, gates_ref,
                 lnw_ref, lnb_ref, rk_ref, bd_ref, wor_ref, woa_ref, wout_ref, nfw_ref,
                 wrh_ref, wrl_ref, br_ref,
                 x1_ref, h2_ref, route_ref, cnt_ref):
    tiles = [_post_tile(part, x_ref, mod_ref, y_ref, r_ref, k_ref, v_ref, g_ref, o_ref, gates_ref,
                        lnw_ref, lnb_ref, rk_ref, bd_ref, wor_ref, woa_ref, wout_ref, nfw_ref,
                        wrh_ref, wrl_ref, br_ref, x1_ref, h2_ref, route_ref, cnt_ref)
             for part in range(x_ref.shape[1] // ROUTE_TILE)]
    _trace_in_turn(tiles)


def _post_tile(part, x_ref, mod_ref, y_ref, r_ref, k_ref, v_ref, g_ref, o_ref, gates_ref,
               lnw_ref, lnb_ref, rk_ref, bd_ref, wor_ref, woa_ref, wout_ref, nfw_ref,
               wrh_ref, wrl_ref, br_ref, x1_ref, h2_ref, route_ref, cnt_ref):
    tm = ROUTE_TILE
    rows = slice(part * tm, (part + 1) * tm)
    d = x_ref.shape[2]
    bd = bd_ref[...]
    inv_n = 1.0 / RWKV_HEAD
    y = y_ref[0, rows, :]
    seg = lambda t: _dot(t.astype(BF16), bd)
    mu = seg(y) * inv_n
    dlt = y - mu
    var = seg(dlt * dlt) * inv_n
    yn = dlt * lax.rsqrt(var + GN_EPS) * lnw_ref[...] + lnb_ref[...]
    yield
    v = v_ref[0, rows, :]
    bonus = seg(r_ref[0, rows, :] * k_ref[0, rows, :] * rk_ref[...]) * v
    z = (yn + bonus) * g_ref[0, rows, :]
    o_b = _dot(z.astype(BF16), wor_ref[...])
    o_a = _dot(o_ref[0, rows, :], woa_ref[...])
    yield
    ga = jax.nn.sigmoid(gates_ref[0, rows, 0:d].astype(F32))
    gb = jax.nn.sigmoid(gates_ref[0, rows, d:2 * d].astype(F32))
    merged = ga * o_a + gb * o_b
    mix = _dot(merged.astype(BF16), wout_ref[...])
    yield
    x1 = x_ref[0, rows, :] + mod_ref[0, 2:3, :] * mix
    x1_ref[0, rows, :] = x1
    ms = jnp.mean(x1 * x1, axis=-1, keepdims=True)
    h2 = x1 * lax.rsqrt(ms + NORM_EPS) * nfw_ref[...]
    h2 = h2 * (1.0 + mod_ref[0, 4:5, :]) + mod_ref[0, 3:4, :]
    h2_ref[0, rows, :] = h2
    yield

    hh, hl = _split(h2)
    logits = _dot(hh, wrh_ref[...]) + (_dot(hh, wrl_ref[...]) + _dot(hl, wrh_ref[...])) + br_ref[...]
    lane = lax.broadcasted_iota(jnp.int32, (tm, LANES), 1)
    lanef = lane.astype(F32)
    cur = jnp.where(lane < N_EXPERTS, logits, -jnp.inf)
    yield
    vals, idxs, hots = [], [], []
    for _ in range(TOP_K):
        mx = jnp.max(cur, axis=-1, keepdims=True)
        idx = jnp.min(jnp.where(cur == mx, lanef, float(LANES)), axis=-1, keepdims=True)
        hot = lanef == idx
        cur = jnp.where(hot, -jnp.inf, cur)
        vals.append(mx)
        idxs.append(idx)
        hots.append(hot)
    exps = [jnp.exp(vv - vals[0]) for vv in vals]
    den = exps[0] + exps[1] + exps[2] + exps[3]
    yield
    sel = jnp.zeros((tm, LANES), F32)
    for hot in hots:
        sel = sel + jnp.where(hot, 1.0, 0.0)
    ri = lax.broadcasted_iota(jnp.int32, (tm, tm), 0)
    ci = lax.broadcasted_iota(jnp.int32, (tm, tm), 1)
    below = jnp.where(ri > ci, 1.0, 0.0).astype(BF16)
    before = _dot(below, sel.astype(BF16))
    counts = jnp.broadcast_to(jnp.sum(sel, axis=0, keepdims=True), (8, LANES))
    er = lax.broadcasted_iota(jnp.int32, (LANES, LANES), 0)
    ec = lax.broadcasted_iota(jnp.int32, (LANES, LANES), 1)
    lower = jnp.where(er < ec, 1.0, 0.0).astype(BF16)
    start = _dot(counts.astype(BF16), lower)[0:1, :]
    route = jnp.zeros((tm, LANES), F32)
    for j in range(TOP_K):
        pos = jnp.sum(jnp.where(hots[j], before + start, 0.0), axis=-1, keepdims=True)
        route = jnp.where(lane == j, pos, route)
        route = jnp.where(lane == TOP_K + j, exps[j] / den, route)
    route_ref[0, rows, :] = route
    cnt_ref[0, part] = counts
    yield


def _post(x, mod3, y, r, k, v, g, o, gates, lnw, lnb, rk, bd, wor, woa, wout, nfw, wrh, wrl, br, tm):
    bsz, s, d = x.shape
    w = RWKV_WIDTH
    parts = tm // ROUTE_TILE
    tok = lambda n: pl.BlockSpec((1, tm, n), lambda b, i: (b, i, 0))
    row = lambda n: pl.BlockSpec((1, n), lambda b, i: (0, 0))
    full = lambda a: pl.BlockSpec(a.shape, lambda b, i: (0,) * a.ndim)
    return pl.pallas_call(
        _post_kernel,
        grid=(bsz, s // tm),
        in_specs=[tok(d), pl.BlockSpec((1, 6, d), lambda b, i: (b, 0, 0)),
                  tok(w), tok(w), tok(w), tok(w), tok(w),
                  tok(MLA_HEADS * MLA_V), tok(2 * d),
                  row(w), row(w), row(w), full(bd), full(wor), full(woa), full(wout), row(d),
                  full(wrh), full(wrl), row(LANES)],
        out_specs=[tok(d), tok(d), tok(LANES), pl.BlockSpec((1, parts, 8, LANES), lambda b, i: (b, i, 0, 0))],
        out_shape=[jax.ShapeDtypeStruct((bsz, s, d), F32),
                   jax.ShapeDtypeStruct((bsz, s, d), F32),
                   jax.ShapeDtypeStruct((bsz, s, LANES), F32),
                   jax.ShapeDtypeStruct((bsz, s // ROUTE_TILE, 8, LANES), F32)],
        compiler_params=_params("parallel", "parallel"),
        name="post",
    )(x, mod3, y, r, k, v, g, o, gates, lnw, lnb, rk, bd, wor, woa, wout, nfw, wrh, wrl, br)


def _load_rows(ref, n):
    nc = ref.shape[0] // n
    return jnp.concatenate([ref[pl.ds(c, n, stride=nc), :] for c in range(nc)], axis=1)


def _store_rows(ref, val):
    nc = val.shape[1] // LANES
    for c in range(nc):
        ref[pl.ds(c, val.shape[0], stride=nc), :] = val[:, c * LANES:(c + 1) * LANES]


def _run_copies(tile, n_ref, src_ref, dst_ref, tm, make_copy):
    nbits = tm.bit_length()
    for b in range(nbits):
        group = tile * nbits + b

        def piece(j, carry, group=group, size=1 << b):
            k = group * N_EXPERTS + j
            make_copy(src_ref[k], dst_ref[k], size).start()
            return carry

        lax.fori_loop(0, n_ref[group], piece, 0)


def _pieces(count, limit, fn):
    def emit(bits):
        for b in bits:
            size = 1 << b
            done = count & ~(2 * size - 1)

            @pl.when((count & size) != 0)
            def _():
                fn(done, size)

    bits = range(limit.bit_length() - 1, -1, -1)
    large = [b for b in bits if b >= RARE_PIECE_BIT]
    if large:
        @pl.when(count >= (1 << RARE_PIECE_BIT))
        def _():
            emit(large)
    emit([b for b in bits if b < RARE_PIECE_BIT])


def _zero_fill(gap_ref, xs_ref, zero_ref, sem, nc):
    bm = MOE_BLOCK
    zero_ref[...] = jnp.zeros_like(zero_ref)

    def piece(dst, size):
        return pltpu.make_async_copy(zero_ref.at[pl.ds(0, size * nc), :],
                                     xs_ref.at[pl.ds(pl.multiple_of(dst * nc, nc), size * nc), :], sem)

    def sweep(act):
        def per_expert(e, carry):
            g0 = gap_ref[e]
            _pieces(gap_ref[N_EXPERTS + e], bm, lambda done, size: act(piece(g0 + done, size)))
            return carry

        def tail(blk, carry):
            act(piece(gap_ref[2 * N_EXPERTS] + blk * bm, bm))
            return carry

        lax.fori_loop(0, N_EXPERTS, per_expert, 0)
        lax.fori_loop(0, gap_ref[2 * N_EXPERTS + 1], tail, 0)

    sweep(lambda cp: cp.start())
    sweep(lambda cp: cp.wait())


def _dispatch_kernel(cnt_ref, off_ref, dst_ref, gap_ref, h_ref, route_ref, xs_ref, sorted_ref, zero_ref,
                     sem, zero_sem):
    tm = h_ref.shape[0]

    @pl.when(pl.program_id(0) == 0)
    def _():
        _zero_fill(gap_ref, xs_ref, zero_ref, zero_sem, h_ref.shape[1] // LANES)

    pos_t = route_ref[...].T
    slot = lax.broadcasted_iota(jnp.int32, (TOP_K * tm, tm), 0).astype(F32)
    perm = jnp.where(slot == pos_t[0:1, :], 1.0, 0.0)
    for j in range(1, TOP_K):
        perm = perm + jnp.where(slot == pos_t[j:j + 1, :], 1.0, 0.0)
    i = pl.program_id(0)
    cur = i % 2
    buf = sorted_ref.at[cur]
    _store_rows(buf, _dot(perm.astype(BF16), h_ref[...].astype(BF16)))
    nc = h_ref.shape[1] // LANES

    def make_copy(src, dst, size):
        return pltpu.make_async_copy(buf.at[pl.ds(pl.multiple_of(src * nc, nc), size * nc), :],
                                     xs_ref.at[pl.ds(pl.multiple_of(dst * nc, nc), size * nc), :], sem.at[cur])

    _run_copies(i, cnt_ref, off_ref, dst_ref, tm, make_copy)

    def drain(which):
        pltpu.make_async_copy(sorted_ref.at[which], xs_ref.at[pl.ds(0, TOP_K * tm * nc), :], sem.at[which]).wait()

    @pl.when(i > 0)
    def _():
        drain(1 - cur)

    @pl.when(i == pl.num_programs(0) - 1)
    def _():
        drain(cur)


def _dispatch(cnt_tab, off_tab, dst_tab, gap_tab, h2, route, n_rows, tm):
    t, d = h2.shape
    nc = d // LANES
    return pl.pallas_call(
        _dispatch_kernel,
        grid_spec=pltpu.PrefetchScalarGridSpec(
            num_scalar_prefetch=4,
            grid=(t // tm,),
            in_specs=[pl.BlockSpec((tm, d), lambda i, *_: (i, 0)),
                      pl.BlockSpec((tm, LANES), lambda i, *_: (i, 0))],
            out_specs=pl.BlockSpec(memory_space=pl.ANY),
            scratch_shapes=[pltpu.VMEM((2, TOP_K * tm * nc, LANES), F32), pltpu.VMEM((MOE_BLOCK * nc, LANES), F32),
                            pltpu.SemaphoreType.DMA((2,)), pltpu.SemaphoreType.DMA(())]),
        out_shape=jax.ShapeDtypeStruct((n_rows * nc, LANES), F32),
        compiler_params=_params("arbitrary"),
        name="dispatch",
    )(cnt_tab, off_tab, dst_tab, gap_tab, h2, route)


def _split_gate_up_kernel(w_ref, g_ref, u_ref, t_ref):
    half = LANES // 2
    nblk = w_ref.shape[1] // LANES
    for c in range(w_ref.shape[2] // LANES):
        w_t = w_ref[0, :, c * LANES:(c + 1) * LANES].T
        for b in range(nblk):
            t_ref[b] = w_t[:, b * LANES:(b + 1) * LANES]
        for out_ref, first in ((g_ref, 0), (u_ref, 1)):
            rows = [t_ref[b, pl.ds(first, half, stride=2), :] for b in range(nblk)]
            out_ref[0, c * half:(c + 1) * half, :] = jnp.concatenate(rows, axis=1).astype(BF16)


def _split_gate_up(w):
    e, d, ff2 = w.shape
    ff = ff2 // 2
    spec = pl.BlockSpec((1, ff, d), lambda i: (i, 0, 0))
    sds = jax.ShapeDtypeStruct((e, ff, d), BF16)
    return pl.pallas_call(
        _split_gate_up_kernel,
        grid=(e,),
        in_specs=[pl.BlockSpec((1, d, ff2), lambda i: (i, 0, 0))],
        out_specs=[spec, spec],
        out_shape=[sds, sds],
        scratch_shapes=[pltpu.VMEM((d // LANES, LANES, LANES), F32)],
        compiler_params=_params("parallel"),
        name="split_gate_up",
    )(w)


def _moe_kernel(blk_e_ref, valid_ref, xs_ref, wg_ref, wu_ref, bg_ref, bu_ref, wd_ref, bd_ref, ys_ref):
    del blk_e_ref
    i = pl.program_id(0)
    valid = valid_ref[i]

    def expert(src, dst, rows):
        x = _load_rows(src, rows).astype(BF16)
        gate = _dot_nt(x, wg_ref[0]) + bg_ref[0]
        up = _dot_nt(x, wu_ref[0]) + bu_ref[0]
        gate = jnp.minimum(gate, SWIGLU_LIMIT)
        up = jnp.clip(up, -SWIGLU_LIMIT, SWIGLU_LIMIT)
        act = (up + 1.0) * gate * jax.nn.sigmoid(SWIGLU_ALPHA * gate)
        _store_rows(dst, _dot(act.astype(BF16), wd_ref[0]) + bd_ref[0])

    half = MOE_BLOCK // 2
    cut = half * (xs_ref.shape[0] // MOE_BLOCK)

    @pl.when(valid > half)
    def _():
        expert(xs_ref, ys_ref, MOE_BLOCK)

    @pl.when(jnp.logical_and(valid > 0, valid <= half))
    def _():
        expert(xs_ref.at[pl.ds(0, cut), :], ys_ref.at[pl.ds(0, cut), :], half)
        ys_ref[pl.ds(cut, cut), :] = jnp.zeros((cut, LANES), F32)

    @pl.when(valid == 0)
    def _():
        ys_ref[...] = jnp.zeros_like(ys_ref)


def _moe(blk_e, blk_valid, xs, wg, wu, bg, bu, wd, bd):
    ff, d = wg.shape[1], wg.shape[2]
    nc = d // LANES
    n_rows = xs.shape[0] // nc
    bm = MOE_BLOCK
    wspec = lambda k, n: pl.BlockSpec((1, k, n), lambda i, be, nu: (be[i], 0, 0))
    return pl.pallas_call(
        _moe_kernel,
        grid_spec=pltpu.PrefetchScalarGridSpec(
            num_scalar_prefetch=2,
            grid=(n_rows // bm,),
            in_specs=[pl.BlockSpec((bm * nc, LANES), lambda i, be, nu: (i, 0)),
                      wspec(ff, d), wspec(ff, d), wspec(1, ff), wspec(1, ff), wspec(ff, d), wspec(1, d)],
            out_specs=pl.BlockSpec((bm * nc, LANES), lambda i, be, nu: (i, 0))),
        out_shape=jax.ShapeDtypeStruct((n_rows * nc, LANES), F32),
        compiler_params=_params("arbitrary"),
        name="moe",
    )(blk_e, blk_valid, xs, wg, wu, bg, bu, wd, bd)


def _combine_kernel(cnt_ref, off_ref, dst_ref, x1_ref, route_ref, mod_ref, ys_ref, o_ref, rows_ref, sem):
    tm = x1_ref.shape[1]
    tile = pl.program_id(0) * pl.num_programs(1) + pl.program_id(1)
    n_tiles = pl.num_programs(0) * pl.num_programs(1)
    nc = x1_ref.shape[2] // LANES
    cur = tile % 2

    def fetch(which_tile):
        which = which_tile % 2

        def make_copy(dst, src, size):
            return pltpu.make_async_copy(
                ys_ref.at[pl.ds(pl.multiple_of(src * nc, nc), size * nc), :],
                rows_ref.at[which, pl.ds(pl.multiple_of(dst * nc, nc), size * nc), :], sem.at[which])

        _run_copies(which_tile, cnt_ref, off_ref, dst_ref, tm, make_copy)

    @pl.when(tile == 0)
    def _():
        fetch(tile)

    @pl.when(tile + 1 < n_tiles)
    def _():
        fetch(tile + 1)

    route = route_ref[0]
    slot = lax.broadcasted_iota(jnp.int32, (tm, TOP_K * tm), 1).astype(F32)
    mix = jnp.where(slot == route[:, 0:1], route[:, TOP_K:TOP_K + 1], 0.0)
    for j in range(1, TOP_K):
        mix = mix + jnp.where(slot == route[:, j:j + 1], route[:, TOP_K + j:TOP_K + j + 1], 0.0)
    pltpu.make_async_copy(ys_ref.at[pl.ds(0, TOP_K * tm * nc), :], rows_ref.at[cur], sem.at[cur]).wait()
    acc = _dot(mix.astype(BF16), _load_rows(rows_ref.at[cur], TOP_K * tm).astype(BF16))
    o_ref[0] = x1_ref[0] + mod_ref[0, 5:6, :] * acc


def _combine(cnt_tab, off_tab, dst_tab, x1, route, mod3, ys, tm):
    bsz, s, d = x1.shape
    return pl.pallas_call(
        _combine_kernel,
        grid_spec=pltpu.PrefetchScalarGridSpec(
            num_scalar_prefetch=3,
            grid=(bsz, s // tm),
            in_specs=[pl.BlockSpec((1, tm, d), lambda b, i, *_: (b, i, 0)),
                      pl.BlockSpec((1, tm, LANES), lambda b, i, *_: (b, i, 0)),
                      pl.BlockSpec((1, 6, d), lambda b, i, *_: (b, 0, 0)),
                      pl.BlockSpec(memory_space=pl.ANY)],
            out_specs=pl.BlockSpec((1, tm, d), lambda b, i, *_: (b, i, 0)),
            scratch_shapes=[pltpu.VMEM((2, TOP_K * tm * d // LANES, LANES), F32), pltpu.SemaphoreType.DMA((2,))]),
        out_shape=jax.ShapeDtypeStruct((bsz, s, d), F32),
        compiler_params=_params("arbitrary", "arbitrary"),
        name="combine",
    )(cnt_tab, off_tab, dst_tab, x1, route, mod3, ys)


def _piece_tables(cnt_tab, off_tab, dst_tab, tm):
    nbits = tm.bit_length()
    bits = jnp.arange(nbits, dtype=jnp.int32)[None, :, None]
    cnt = cnt_tab[:, None, :]
    active = (cnt >> bits) & 1
    done = cnt & ~((2 << bits) - 1)
    rank = jnp.cumsum(active, axis=-1) - active
    slot = jnp.arange(N_EXPERTS, dtype=jnp.int32)[None, None, :, None]
    place = (active[:, :, None, :] == 1) & (rank[:, :, None, :] == slot)
    compact = lambda v: jnp.sum(jnp.where(place, (v[:, None, :] + done)[:, :, None, :], 0), axis=-1)
    flat = lambda v: v.reshape(-1).astype(jnp.int32)
    return flat(jnp.sum(active, axis=-1)), flat(compact(off_tab)), flat(compact(dst_tab))


def _pad_cols(a, n):
    return jnp.pad(a, ((0, 0), (0, n - a.shape[1])))


def _head_blocks(cols_main, cols_rot=None):
    k = cols_main.shape[0]
    out = jnp.zeros((k, MLA_HEADS, HEAD_PAD), F32)
    out = out.at[:, :, :cols_main.shape[2]].set(cols_main)
    return out.reshape(k, MLA_HEADS * HEAD_PAD)


def _layer(x, cond_mod, positions, w_in, q_a_norm_w, w_q_up, kv_a_norm_w, w_kv_up, q_norm_w, k_norm_w,
           w_o_mla, rwkv_mu, rwkv_w0, rwkv_w2, rwkv_a0, rwkv_a2, rwkv_g2, rwkv_k_k, rwkv_k_a, rwkv_r_k,
           rwkv_ln_w, rwkv_ln_b, rwkv_w_o, w_out, norm_mix_w, norm_ffn_w, w_router, b_router,
           w_gate_up, b_gate_up, w_down, b_down):
    bsz, s, d = x.shape
    t = bsz * s
    half = MLA_ROPE // 2
    nope, qk = MLA_NOPE, MLA_QK
    mod3 = cond_mod.reshape(bsz, 6, d)

    o_q, o_kv, o_kr = 0, Q_LORA, Q_LORA + KV_LORA
    o_slab = o_kr + MLA_ROPE
    o_gate = o_slab + RWKV_SLAB
    kr_w = w_in[:, o_kr:o_slab]
    zeros = lambda n: jnp.zeros((d, n), F32)
    kr_blk = jnp.concatenate([zeros(nope), kr_w, zeros(HEAD_PAD - qk)], axis=1)
    kr_rot = jnp.concatenate([zeros(nope), -kr_w[:, half:], kr_w[:, :half], zeros(HEAD_PAD - qk)], axis=1)
    w_in_p = jnp.concatenate([w_in[:, o_q:o_kr], kr_blk, kr_rot, w_in[:, o_slab:]], axis=1).astype(BF16)

    tm = min(256, s)

    wq3 = w_q_up.reshape(Q_LORA, MLA_HEADS, qk)
    wq_rot = jnp.concatenate([jnp.zeros((Q_LORA, MLA_HEADS, nope), F32), -wq3[:, :, nope + half:],
                              wq3[:, :, nope:nope + half]], axis=2)
    wq = jnp.concatenate([_head_blocks(wq3), _head_blocks(wq_rot)], axis=1).astype(BF16)
    wkv3 = w_kv_up.reshape(KV_LORA, MLA_HEADS, nope + MLA_V)
    wkv = jnp.concatenate([_head_blocks(wkv3[:, :, :nope]), _head_blocks(wkv3[:, :, nope:])], axis=1).astype(BF16)

    def gains(wn):
        main = jnp.pad(wn, (0, HEAD_PAD - qk)).reshape(1, HEAD_PAD)
        rot = jnp.concatenate([jnp.zeros((nope,), F32), wn[nope + half:], wn[nope:nope + half],
                               jnp.zeros((HEAD_PAD - qk,), F32)]).reshape(1, HEAD_PAD)
        return main, rot

    qg, qgs = gains(q_norm_w)
    kg, kgs = gains(k_norm_w)
    inv_freq = ROPE_THETA ** (-jnp.arange(half, dtype=F32) / half)
    per_row = LANES // half
    pos_rep = jnp.repeat(positions.astype(F32).reshape(t // per_row, per_row), half, axis=1)
    cos16, sin16 = _rope_table(pos_rep, jnp.tile(inv_freq, per_row).reshape(1, LANES))
    cos16 = cos16.reshape(bsz, s, half)
    sin16 = sin16.reshape(bsz, s, half)
    cosf = jnp.concatenate([jnp.ones((bsz, s, nope), F32), cos16, cos16,
                            jnp.ones((bsz, s, HEAD_PAD - qk), F32)], axis=-1)
    sinf = jnp.concatenate([jnp.zeros((bsz, s, nope), F32), sin16, sin16,
                            jnp.zeros((bsz, s, HEAD_PAD - qk), F32)], axis=-1)

    w = RWKV_WIDTH
    wwa = jnp.zeros((LANES, 2 * w), F32)
    wwa = wwa.at[:DECAY_LORA, :w].set(rwkv_w2).at[DECAY_LORA:, w:].set(rwkv_a2)
    wwh = wwa.astype(BF16)
    wwl = (wwa - wwh.astype(F32)).astype(BF16)
    wws = jnp.concatenate([wwh, wwl, wwh], axis=0)
    hid = np.arange(w) // RWKV_HEAD
    bd = jnp.asarray(hid[:, None] == hid[None, :], BF16)
    gates, q, k, vt, r_, lw_, k_, v_, a_, b_, g_ = _front(
        x, mod3, norm_mix_w, w_in_p, cosf, sinf, q_a_norm_w.reshape(1, -1), kv_a_norm_w.reshape(1, -1),
        wq, wkv, qg, qgs, kg, kgs, rwkv_mu.reshape(1, -1), wws, rwkv_w0.reshape(1, -1),
        rwkv_a0.reshape(1, -1), rwkv_g2.astype(BF16), rwkv_k_k.reshape(1, -1), rwkv_k_a.reshape(1, -1), bd,
        tm, min(ATTN_KEY_BLOCK, s))
    o = _attention(q, k, vt, min(ATTN_BLOCK, s))
    y = _wkv(r_, lw_, k_, v_, a_, b_, WKV_BATCH if bsz % WKV_BATCH == 0 else 1)

    woa = w_o_mla.astype(BF16)
    wr =_pad_cols(w_router, LANES)
    wrh = wr.astype(BF16)
    wrl = (wr - wrh.astype(F32)).astype(BF16)
    br = jnp.pad(b_router, (0, LANES - N_EXPERTS)).reshape(1, LANES)
    x1, h2, route, counts = _post(
        x, mod3, y, r_, k_, v_, g_, o, gates, rwkv_ln_w.reshape(1, -1), rwkv_ln_b.reshape(1, -1),
        rwkv_r_k.reshape(1, -1), bd, rwkv_w_o.astype(BF16), woa, w_out.astype(BF16),
        norm_ffn_w.reshape(1, -1), wrh, wrl, br, POST_TILE if s % POST_TILE == 0 else ROUTE_TILE)

    bm = MOE_BLOCK
    n_rows = t * TOP_K + N_EXPERTS * bm
    n_blocks = n_rows // bm
    cnt_tab = counts[:, :, 0, :N_EXPERTS].reshape(t // ROUTE_TILE, N_EXPERTS).astype(jnp.int32)
    total = jnp.sum(cnt_tab, axis=0)
    padded = (total + bm - 1) // bm * bm
    pad_end = jnp.cumsum(padded)
    pad_start = pad_end - padded
    off_tab = jnp.cumsum(cnt_tab, axis=1) - cnt_tab
    dst_tab = pad_start[None, :] + jnp.cumsum(cnt_tab, axis=0) - cnt_tab
    blk_start = jnp.arange(n_blocks, dtype=jnp.int32) * bm
    blk_e = jnp.minimum(jnp.sum((pad_end[None, :] <= blk_start[:, None]).astype(jnp.int32), axis=1),
                        N_EXPERTS - 1)
    blk_valid = jnp.clip((pad_start + total)[blk_e] - blk_start, 0, bm).astype(jnp.int32)
    tabs = _piece_tables(cnt_tab, off_tab, dst_tab, ROUTE_TILE)
    gap_tab = jnp.concatenate([pad_start + total, padded - total, pad_end[-1:],
                               (n_rows - pad_end[-1:]) // bm]).astype(jnp.int32)

    xs = _dispatch(*tabs, gap_tab, h2.reshape(t, d), route.reshape(t, LANES), n_rows, ROUTE_TILE)
    wg_t, wu_t = _split_gate_up(w_gate_up)
    bgu = b_gate_up.reshape(N_EXPERTS, 1, D_FF, 2)
    ys = _moe(blk_e, blk_valid, xs, wg_t, wu_t,
              bgu[..., 0], bgu[..., 1], w_down.astype(BF16), b_down.reshape(N_EXPERTS, 1, d))
    return _combine(*tabs, x1, route, mod3, ys, ROUTE_TILE)


def kernel(x, c, positions, ada_w, ada_b, norm_mix_w, norm_ffn_w, w_in, q_a_norm_w, w_q_up, kv_a_norm_w, w_kv_up, q_norm_w, k_norm_w, w_o_mla, rwkv_mu, rwkv_w0, rwkv_w2, rwkv_a0, rwkv_a2, rwkv_g2, rwkv_k_k, rwkv_k_a, rwkv_r_k, rwkv_ln_w, rwkv_ln_b, rwkv_w_o, w_out, w_router, b_router, w_gate_up, b_gate_up, w_down, b_down):
    depth = ada_w.shape[0]
    for l in range(depth):
        mod = _ada(c, ada_w[l], ada_b[l])
        x = _layer(x, mod, positions, w_in[l], q_a_norm_w[l], w_q_up[l], kv_a_norm_w[l], w_kv_up[l],
                   q_norm_w[l], k_norm_w[l], w_o_mla[l], rwkv_mu[l], rwkv_w0[l], rwkv_w2[l], rwkv_a0[l],
                   rwkv_a2[l], rwkv_g2[l], rwkv_k_k[l], rwkv_k_a[l], rwkv_r_k[l], rwkv_ln_w[l],
                   rwkv_ln_b[l], rwkv_w_o[l], w_out[l], norm_mix_w[l], norm_ffn_w[l], w_router[l],
                   b_router[l], w_gate_up[l], b_gate_up[l], w_down[l], b_down[l])
    return x
```

```python
import functools

import numpy as np
import jax
import jax.numpy as jnp
from jax import lax
from jax.experimental import pallas as pl
from jax.experimental.pallas import tpu as pltpu

F32 = jnp.float32
BF16 = jnp.bfloat16

D_MODEL = 1024
MLA_HEADS = 8
MLA_NOPE = 64
MLA_ROPE = 32
MLA_QK = MLA_NOPE + MLA_ROPE
MLA_V = 64
VT_ROWS = 80
Q_LORA = 256
KV_LORA = 128
ROPE_THETA = 10000.0
RWKV_HEADS = 8
RWKV_HEAD = 64
RWKV_WIDTH = RWKV_HEADS * RWKV_HEAD
DECAY_LORA = 64
AAA_LORA = 64
GATE_LORA = 128
RWKV_SLAB = 3 * RWKV_WIDTH + DECAY_LORA + AAA_LORA + GATE_LORA
GN_EPS = 64e-5
N_EXPERTS = 32
TOP_K = 4
D_FF = D_MODEL
SWIGLU_LIMIT = 7.0
SWIGLU_ALPHA = 1.702
NORM_EPS = 1e-6
LOG2_E = 1.4426950408889634

LANES = 128
HEAD_PAD = 128
MLA_COLS = 640
WKV_CHUNK = 64
WKV_BATCH = 8
ATTN_BLOCK = 512
ATTN_KEY_BLOCK = 512
ATTN_HEADS = 8
ROUTE_TILE = 256
POST_TILE = 512
RARE_PIECE_BIT = 6
MOE_BLOCK = 512
VMEM_LIMIT = 56 * 1024 * 1024


def _dot(a, b):
    return jnp.dot(a, b, preferred_element_type=F32)


def _dot_nt(a, b):
    return lax.dot_general(a, b, (((1,), (1,)), ((), ())), preferred_element_type=F32)


def _split(x):
    hi = x.astype(BF16)
    lo = (x - hi.astype(F32)).astype(BF16)
    return hi, lo


def _mm3(a, b):
    ah, al = _split(a)
    bh, bl = _split(b)
    return _dot(ah, bh) + (_dot(ah, bl) + _dot(al, bh))


def _trace_in_turn(stages):
    live = list(stages)
    while live:
        for gen in list(live):
            try:
                next(gen)
            except StopIteration:
                live.remove(gen)


def _params(*sem):
    return pltpu.CompilerParams(dimension_semantics=sem, vmem_limit_bytes=VMEM_LIMIT)


def _ada_kernel(c_ref, w_ref, b_ref, o_ref):
    c = c_ref[...]
    cond = c * jax.nn.sigmoid(c)
    o_ref[...] = _mm3(cond, w_ref[...]) + b_ref[...]


def _ada(c, w, b):
    bsz, d = c.shape
    n = w.shape[1]
    tn = 1024
    return pl.pallas_call(
        _ada_kernel,
        grid=(n // tn,),
        in_specs=[pl.BlockSpec((bsz, d), lambda j: (0, 0)),
                  pl.BlockSpec((d, tn), lambda j: (0, j)),
                  pl.BlockSpec((1, tn), lambda j: (0, j))],
        out_specs=pl.BlockSpec((bsz, tn), lambda j: (0, j)),
        out_shape=jax.ShapeDtypeStruct((bsz, n), F32),
        compiler_params=_params("parallel"),
        name="ada",
    )(c, w, b.reshape(1, n))


def _front_kernel(x_ref, mod_ref, nw_ref, w_ref,
                  cos_ref, sin_ref, qan_ref, kvn_ref, wq_ref, wkv_ref, qg_ref, qgs_ref, kg_ref, kgs_ref,
                  mu_ref, wws_ref, w0_ref, a0_ref, g2_ref, kk_ref, ka_ref, bd_ref,
                  gates_ref, q_out, k_out, vt_out, r_out, lw_out, k2_out, v_out, a_out, b_out, g_out,
                  last_ref):
    x = x_ref[0]
    ms = jnp.mean(x * x, axis=-1, keepdims=True)
    y = x * lax.rsqrt(ms + NORM_EPS) * nw_ref[...]
    h = y * (1.0 + mod_ref[0, 1:2, :]) + mod_ref[0, 0:1, :]
    hb = h.astype(BF16)
    mla = _dot(hb, w_ref[:, 0:MLA_COLS])
    slab = _dot(hb, w_ref[:, MLA_COLS:MLA_COLS + RWKV_SLAB])
    gates_ref[0] = _dot(hb, w_ref[:, MLA_COLS + RWKV_SLAB:]).astype(BF16)
    mla_prep = _mla_prep_body(mla, cos_ref, sin_ref, qan_ref, kvn_ref, wq_ref, wkv_ref,
                              qg_ref, qgs_ref, kg_ref, kgs_ref, q_out, k_out, vt_out)
    rwkv_prep = _rwkv_prep_body(slab, pl.program_id(1) == 0, mu_ref, wws_ref, w0_ref, a0_ref, g2_ref,
                                kk_ref, ka_ref, bd_ref, r_out, lw_out, k2_out, v_out, a_out, b_out, g_out,
                                last_ref)
    _trace_in_turn([mla_prep, rwkv_prep])


def _front(x, mod3, norm_w, w_in_p, cosf, sinf, qan, kvn, wq, wkv, qg, qgs, kg, kgs,
           mu, wws, w0, a0, g2, k_k, k_a, bd, tm, tk):
    bsz, s, d = x.shape
    w = RWKV_WIDTH
    tok = lambda n: pl.BlockSpec((1, tm, n), lambda b, i: (b, i, 0))
    row = lambda n: pl.BlockSpec((1, n), lambda b, i: (0, 0))
    full = lambda a: pl.BlockSpec(a.shape, lambda b, i: (0,) * a.ndim)
    head_spec = pl.BlockSpec((1, MLA_HEADS, tm, HEAD_PAD), lambda b, i: (b, 0, i, 0))
    head_sds = jax.ShapeDtypeStruct((bsz, MLA_HEADS, s, HEAD_PAD), BF16)
    if tk >= tm:
        per = tk // tm
        vt_spec = pl.BlockSpec((1, MLA_HEADS, 1, VT_ROWS, tm), lambda b, i: (b, 0, i // per, 0, i % per))
    else:
        vt_spec = pl.BlockSpec((1, MLA_HEADS, tm // tk, VT_ROWS, tk), lambda b, i: (b, 0, i, 0, 0))
    vt_sds = jax.ShapeDtypeStruct((bsz, MLA_HEADS, s // tk, VT_ROWS, tk), BF16)
    stream_sds = jax.ShapeDtypeStruct((bsz, s, w), F32)
    return pl.pallas_call(
        _front_kernel,
        grid=(bsz, s // tm),
        in_specs=[tok(d), pl.BlockSpec((1, 6, d), lambda b, i: (b, 0, 0)), row(d), full(w_in_p),
                  tok(LANES), tok(LANES), row(Q_LORA), row(KV_LORA), full(wq), full(wkv),
                  row(LANES), row(LANES), row(LANES), row(LANES),
                  row(RWKV_SLAB), full(wws), row(w), row(w), full(g2), row(w), row(w), full(bd)],
        out_specs=[tok(2 * d), head_spec, head_spec, vt_spec] + [tok(w)] * 7,
        out_shape=[jax.ShapeDtypeStruct((bsz, s, 2 * d), BF16), head_sds, head_sds, vt_sds] + [stream_sds] * 7,
        scratch_shapes=[pltpu.VMEM((8, RWKV_SLAB), F32)],
        compiler_params=_params("parallel", "arbitrary"),
        name="front",
    )(x, mod3, norm_w.reshape(1, d), w_in_p, cosf, sinf, qan, kvn, wq, wkv, qg, qgs, kg, kgs,
      mu, wws, w0, a0, g2, k_k, k_a, bd)


def _rope_kernel(pos_ref, invf_ref, cos_ref, sin_ref):
    ang = pos_ref[...] * invf_ref[...]
    cos_ref[...] = jnp.cos(ang)
    sin_ref[...] = jnp.sin(ang)


def _rope_table(pos_rep, invf_row):
    n = pos_rep.shape[0]
    tr = min(512, n)
    spec = pl.BlockSpec((tr, LANES), lambda i: (i, 0))
    sds = jax.ShapeDtypeStruct((n, LANES), F32)
    return pl.pallas_call(
        _rope_kernel,
        grid=(n // tr,),
        in_specs=[spec, pl.BlockSpec((1, LANES), lambda i: (0, 0))],
        out_specs=[spec, spec],
        out_shape=[sds, sds],
        compiler_params=_params("parallel"),
        name="rope",
    )(pos_rep, invf_row)


def _mla_prep_body(m, cos_ref, sin_ref, qan_ref, kvn_ref, wq_ref, wkv_ref,
                   qg_ref, qgs_ref, kg_ref, kgs_ref, q_out, k_out, vt_out):
    cq = m[:, 0:Q_LORA]
    ckv = m[:, Q_LORA:Q_LORA + KV_LORA]
    kr = m[:, 384:512]
    krs = m[:, 512:640]
    ql = cq * lax.rsqrt(jnp.mean(cq * cq, axis=-1, keepdims=True) + NORM_EPS) * qan_ref[...]
    kvl = ckv * lax.rsqrt(jnp.mean(ckv * ckv, axis=-1, keepdims=True) + NORM_EPS) * kvn_ref[...]
    qall = _dot(ql.astype(BF16), wq_ref[...])
    kvall = _dot(kvl.astype(BF16), wkv_ref[...])
    cosf = cos_ref[0]
    sinf = sin_ref[0]
    scale = MLA_QK ** -0.5 * LOG2_E
    hw = MLA_HEADS * HEAD_PAD
    for h in range(MLA_HEADS):
        lo, hi = h * HEAD_PAD, (h + 1) * HEAD_PAD
        qh = qall[:, lo:hi]
        qs = qall[:, hw + lo:hw + hi]
        rs = lax.rsqrt(jnp.sum(qh * qh, axis=-1, keepdims=True) * (1.0 / MLA_QK) + NORM_EPS)
        qo = (qh * rs * qg_ref[...]) * cosf + (qs * rs * qgs_ref[...]) * sinf
        q_out[0, h] = (qo * scale).astype(BF16)
        kh = kvall[:, lo:hi] + kr
        rs = lax.rsqrt(jnp.sum(kh * kh, axis=-1, keepdims=True) * (1.0 / MLA_QK) + NORM_EPS)
        ko = (kh * rs * kg_ref[...]) * cosf + (krs * rs * kgs_ref[...]) * sinf
        k_out[0, h] = ko.astype(BF16)
        extra = lax.broadcasted_iota(jnp.int32, (VT_ROWS - MLA_V, m.shape[0]), 0)
        v_t = jnp.concatenate([kvall[:, hw + lo:hw + hi].T[0:MLA_V, :],
                               jnp.where(extra == 0, 1.0, 0.0)], axis=0).astype(BF16)
        width = vt_out.shape[4]
        for c in range(vt_out.shape[2]):
            vt_out[0, h, c] = v_t[:, c * width:(c + 1) * width]
        yield


def _attn_kernel(q_ref, k_ref, vt_ref, o_ref, m_ref, acc_ref, *, tq):
    qi = pl.program_id(2)
    heads = range(q_ref.shape[1])
    qs = [q_ref[0, h] for h in heads]
    m_ref[...] = jnp.full_like(m_ref, -jnp.inf)
    acc_ref[...] = jnp.zeros_like(acc_ref)

    tk = vt_ref.shape[4]
    per = tq // tk

    def step(j, causal=False, k0=0, nk=tk, q0=0, nq=tq, kbase=0):
        off = pl.multiple_of(j * tk, tk)
        sts = [_dot_nt(k_ref[0, h, pl.ds(off + k0, nk), :], qs[h][q0:q0 + nq]) for h in heads]
        if causal:
            keys = lax.broadcasted_iota(jnp.int32, (nk, nq), 0) + (k0 + kbase)
            queries = lax.broadcasted_iota(jnp.int32, (nk, nq), 1) + q0
        for h in heads:
            st = jnp.where(keys <= queries, sts[h], -jnp.inf) if causal else sts[h]
            m = m_ref[h, :, q0:q0 + nq]
            m_new = jnp.maximum(m, jnp.max(st, axis=0, keepdims=True))
            p = jnp.exp2(st - m_new)
            alpha = jnp.exp2(m - m_new)
            m_ref[h, :, q0:q0 + nq] = m_new
            acc_ref[h, :, q0:q0 + nq] = (alpha * acc_ref[h, :, q0:q0 + nq]
                                         + _dot(vt_ref[0, h, j, :, k0:k0 + nk], p.astype(BF16)))

    def body(j, carry):
        step(j)
        return carry

    lax.fori_loop(0, qi * per, body, 0)
    if per == 1:
        step(qi, True, 0, tk // 2)
        step(qi, True, tk // 2, tk // 2, tq // 2, tq // 2)
    else:
        for part in range(per):
            step(qi * per + part, True, 0, tk, part * tk, tq - part * tk, part * tk)
    out = lambda h: acc_ref[h, 0:MLA_V, :] / acc_ref[h, MLA_V:MLA_V + 1, :]
    for h in heads[::2]:
        o_ref[0, :, h * MLA_V:(h + 2) * MLA_V] = jnp.concatenate([out(h), out(h + 1)], axis=0).T.astype(BF16)


def _attention(q, k, vt, tq):
    bsz, nh, s, dh = q.shape
    dv, tk = vt.shape[3], vt.shape[4]
    hp = ATTN_HEADS
    return pl.pallas_call(
        functools.partial(_attn_kernel, tq=tq),
        grid=(bsz, nh // hp, s // tq),
        in_specs=[pl.BlockSpec((1, hp, tq, dh), lambda b, h, i: (b, h, i, 0)),
                  pl.BlockSpec((1, hp, s, dh), lambda b, h, i: (b, h, 0, 0)),
                  pl.BlockSpec((1, hp, s // tk, dv, tk), lambda b, h, i: (b, h, 0, 0, 0))],
        out_specs=pl.BlockSpec((1, tq, hp * MLA_V), lambda b, h, i: (b, i, h)),
        out_shape=jax.ShapeDtypeStruct((bsz, s, nh * MLA_V), BF16),
        scratch_shapes=[pltpu.VMEM((hp, 1, tq), F32), pltpu.VMEM((hp, dv, tq), F32)],
        compiler_params=_params("parallel", "parallel", "parallel"),
        name="attn",
    )(q, k, vt)


def _softplus(x):
    return jnp.maximum(x, 0.0) + jnp.log(1.0 + jnp.exp(-jnp.abs(x)))


def _rwkv_prep_body(slab, first_tile, mu_ref, wws_ref, w0_ref, a0_ref, g2_ref, kk_ref, ka_ref, bd_ref,
                    r_out, lw_out, k_out, v_out, a_out, b_out, g_out, last_ref):
    tm = slab.shape[0]
    w = RWKV_WIDTH
    rolled = pltpu.roll(slab, 1, 0)
    rowi = lax.broadcasted_iota(jnp.int32, slab.shape, 0)
    carried = jnp.where(first_tile, 0.0, last_ref[0:1, :])
    prev = jnp.where(rowi == 0, carried, rolled)
    last_ref[0:1, :] = slab[tm - 1:tm, :]
    p = slab + (prev - slab) * mu_ref[...]
    r = p[:, 0:w]
    k = p[:, w:2 * w]
    v = p[:, 2 * w:3 * w]
    r_out[0] = r
    v_out[0] = v
    yield
    wa = p[:, 3 * w:3 * w + LANES]
    g_lo = p[:, 3 * w + LANES:3 * w + 2 * LANES]
    lane = lax.broadcasted_iota(jnp.int32, wa.shape, 1)
    wa = jnp.where(lane < DECAY_LORA, jnp.tanh(wa), wa)
    wah, wal = _split(wa)
    wa_o = _dot(jnp.concatenate([wah, wah, wal], axis=1), wws_ref[...])
    g_out[0] = _dot(jax.nn.sigmoid(g_lo).astype(BF16), g2_ref[...])
    yield
    log_w = -_softplus(-(w0_ref[...] + wa_o[:, 0:w])) - 0.5
    lw_out[0] = -jnp.exp(log_w)
    yield
    a = jax.nn.sigmoid(a0_ref[...] + wa_o[:, w:2 * w])
    k_out[0] = k * (1.0 + (a - 1.0) * ka_ref[...])
    yield
    kk = k * kk_ref[...]
    ss = _dot((kk * kk).astype(BF16), bd_ref[...])
    kk = kk / jnp.maximum(jnp.sqrt(ss), 1e-12)
    a_out[0] = -kk
    b_out[0] = kk * a
    yield


def _wkv_kernel(r_ref, lw_ref, k_ref, v_ref, a_ref, b_ref, y_ref, s_ref):
    c = WKV_CHUNK
    n = pl.program_id(1)

    @pl.when(n == 0)
    def _():
        s_ref[...] = jnp.zeros_like(s_ref)

    row = lax.broadcasted_iota(jnp.int32, (LANES, LANES), 0)
    col = lax.broadcasted_iota(jnp.int32, (LANES, LANES), 1)
    lower_strict = row > col
    lower_incl = row >= col
    eye = row == col
    same16 = (row >> 4) == (col >> 4)
    same32 = (row >> 5) == (col >> 5)
    ident = jnp.where(eye, 1.0, 0.0).astype(F32)
    tr = lax.broadcasted_iota(jnp.int32, (c, c), 0)
    tc = lax.broadcasted_iota(jnp.int32, (c, c), 1)
    tri = jnp.where(tr >= tc, 1.0, 0.0).astype(BF16)
    first = lax.broadcasted_iota(jnp.int32, (c, LANES), 1) < RWKV_HEAD

    def stack(x):
        return jnp.concatenate([jnp.where(first, x, 0.0), jnp.where(first, 0.0, x)], axis=0)

    def mm(x, y):
        return _dot(x.astype(BF16), y.astype(BF16))

    nb = r_ref.shape[0]
    probs = [(bi, p) for bi in range(nb) for p in range(RWKV_HEADS // 2)]
    each = lambda f, *xs: [f(*args) for args in zip(*xs)]
    load = lambda ref: [ref[bi, :, p * LANES:(p + 1) * LANES] for bi, p in probs]
    r, lw, k, v, a, b = (load(ref) for ref in (r_ref, lw_ref, k_ref, v_ref, a_ref, b_ref))

    def cumsum(x):
        hi = x.astype(BF16)
        rem = x - hi.astype(F32)
        mid = rem.astype(BF16)
        lo = (rem - mid.astype(F32)).astype(BF16)
        return _dot(tri, hi) + (_dot(tri, mid) + _dot(tri, lo))

    cum = each(cumsum, lw)
    cum_c = [x[c - 1:c, :] for x in cum]
    e_in = each(jnp.exp, cum)
    e_neg = each(lambda x: jnp.exp(-x), cum)
    e_end = each(lambda x, xc: jnp.exp(xc - x), cum, cum_c)
    a_s = each(lambda x, cu, l: stack(x * jnp.exp(cu - l)).astype(BF16), a, cum, lw)
    r_s = each(lambda x, e: stack(x * e), r, e_in)
    b_s = each(lambda x, e: stack(x * e).astype(BF16), b, e_neg)
    k_s = each(lambda x, e: stack(x * e).astype(BF16), k, e_neg)
    b_e = each(lambda x, e: stack(x * e).T.astype(BF16), b, e_end)
    k_e = each(lambda x, e: stack(x * e).T.astype(BF16), k, e_end)
    v_s = each(lambda x: stack(x).astype(BF16), v)

    tt = each(lambda x1, x2, y1, y2: _dot_nt(jnp.concatenate([x1, x2.astype(BF16)], axis=0),
                                             jnp.concatenate([y1, y2], axis=0)), a_s, r_s, b_s, k_s)
    d_ab = [jnp.where(lower_strict, x[0:LANES, 0:LANES], 0.0) for x in tt]
    e_ak = [jnp.where(lower_strict, x[0:LANES, LANES:], 0.0).astype(BF16) for x in tt]
    f_rb = [jnp.where(lower_incl, x[LANES:, 0:LANES], 0.0).astype(BF16) for x in tt]
    f_rk = [jnp.where(lower_incl, x[LANES:, LANES:], 0.0).astype(BF16) for x in tt]

    d16 = [jnp.where(same16, x, 0.0) for x in d_ab]
    d32 = [jnp.where(same32, x, 0.0) for x in d_ab]
    z = each(_dot, e_ak, v_s)
    x2 = each(mm, d16, d16)
    x4 = each(mm, x2, x2)
    x8 = each(mm, x4, x4)
    t = [ident + x for x in d16]
    t = each(lambda t_, x: t_ + mm(t_, x), t, x2)
    t = each(lambda t_, x: t_ + mm(t_, x), t, x4)
    t = each(lambda t_, x: t_ + mm(t_, x), t, x8)
    t = each(lambda t_, hi, lo: t_ + mm(mm(t_, hi - lo), t_), t, d32, d16)
    t = each(lambda t_, hi, lo: t_ + mm(mm(t_, hi - lo), t_), t, d_ab, d32)

    au = each(lambda t_, x, zz: mm(t_, jnp.concatenate([x, zz.astype(BF16)], axis=1)).astype(BF16),
              t, a_s, z)
    g = each(_dot, f_rb, au)
    y0 = each(lambda gg, f, vv: gg[:, LANES:] + _dot(f, vv), g, f_rk, v_s)
    r1 = each(lambda x, gg: x + gg[:, 0:LANES], r_s, g)
    hmat = each(_dot, b_e, au)
    m_mat = each(lambda h, xc: jnp.where(eye, jnp.exp(xc), 0.0) + h[:, 0:LANES], hmat, cum_c)
    n_mat = each(lambda h, ke, vv: h[:, LANES:] + _dot(ke, vv), hmat, k_e, v_s)

    for i, (bi, p) in enumerate(probs):
        s0 = s_ref[i]
        s0b = s0.astype(BF16)
        ys = _dot(r1[i].astype(BF16), s0b) + y0[i]
        s_ref[i] = _dot(m_mat[i].astype(BF16), s0b) + n_mat[i]
        y_ref[bi, :, p * LANES:(p + 1) * LANES] = ys[0:c, :] + ys[c:, :]


def _wkv(r, lw, k, v, a, b, nb):
    bsz, s, w = r.shape
    spec = pl.BlockSpec((nb, WKV_CHUNK, w), lambda bi, n: (bi, n, 0))
    return pl.pallas_call(
        _wkv_kernel,
        grid=(bsz // nb, s // WKV_CHUNK),
        in_specs=[spec] * 6,
        out_specs=spec,
        out_shape=jax.ShapeDtypeStruct((bsz, s, w), F32),
        scratch_shapes=[pltpu.VMEM((nb * RWKV_HEADS // 2, LANES, LANES), F32)],
        compiler_params=_params("parallel", "arbitrary"),
        name="wkv",
    )(r, lw, k, v, a, b)


def _post_kernel(x_ref, mod_ref, y_ref, r_ref, k_ref, v_ref, g_ref, o_ref, gates_ref,
                 lnw_ref, lnb_ref, rk_ref, bd_ref, wor_ref, woa_ref, wout_ref, nfw_ref,
                 wrh_ref, wrl_ref, br_ref,
                 x1_ref, h2_ref, route_ref, cnt_ref):
    tiles = [_post_tile(part, x_ref, mod_ref, y_ref, r_ref, k_ref, v_ref, g_ref, o_ref, gates_ref,
                        lnw_ref, lnb_ref, rk_ref, bd_ref, wor_ref, woa_ref, wout_ref, nfw_ref,
                        wrh_ref, wrl_ref, br_ref, x1_ref, h2_ref, route_ref, cnt_ref)
             for part in range(x_ref.shape[1] // ROUTE_TILE)]
    _trace_in_turn(tiles)


def _post_tile(part, x_ref, mod_ref, y_ref, r_ref, k_ref, v_ref, g_ref, o_ref, gates_ref,
               lnw_ref, lnb_ref, rk_ref, bd_ref, wor_ref, woa_ref, wout_ref, nfw_ref,
               wrh_ref, wrl_ref, br_ref, x1_ref, h2_ref, route_ref, cnt_ref):
    tm = ROUTE_TILE
    rows = slice(part * tm, (part + 1) * tm)
    d = x_ref.shape[2]
    bd = bd_ref[...]
    inv_n = 1.0 / RWKV_HEAD
    y = y_ref[0, rows, :]
    seg = lambda t: _dot(t.astype(BF16), bd)
    mu = seg(y) * inv_n
    dlt = y - mu
    var = seg(dlt * dlt) * inv_n
    yn = dlt * lax.rsqrt(var + GN_EPS) * lnw_ref[...] + lnb_ref[...]
    yield
    v = v_ref[0, rows, :]
    bonus = seg(r_ref[0, rows, :] * k_ref[0, rows, :] * rk_ref[...]) * v
    z = (yn + bonus) * g_ref[0, rows, :]
    o_b = _dot(z.astype(BF16), wor_ref[...])
    o_a = _dot(o_ref[0, rows, :], woa_ref[...])
    yield
    ga = jax.nn.sigmoid(gates_ref[0, rows, 0:d].astype(F32))
    gb = jax.nn.sigmoid(gates_ref[0, rows, d:2 * d].astype(F32))
    merged = ga * o_a + gb * o_b
    mix = _dot(merged.astype(BF16), wout_ref[...])
    yield
    x1 = x_ref[0, rows, :] + mod_ref[0, 2:3, :] * mix
    x1_ref[0, rows, :] = x1
    ms = jnp.mean(x1 * x1, axis=-1, keepdims=True)
    h2 = x1 * lax.rsqrt(ms + NORM_EPS) * nfw_ref[...]
    h2 = h2 * (1.0 + mod_ref[0, 4:5, :]) + mod_ref[0, 3:4, :]
    h2_ref[0, rows, :] = h2
    yield

    hh, hl = _split(h2)
    logits = _dot(hh, wrh_ref[...]) + (_dot(hh, wrl_ref[...]) + _dot(hl, wrh_ref[...])) + br_ref[...]
    lane = lax.broadcasted_iota(jnp.int32, (tm, LANES), 1)
    lanef = lane.astype(F32)
    cur = jnp.where(lane < N_EXPERTS, logits, -jnp.inf)
    yield
    vals, idxs, hots = [], [], []
    for _ in range(TOP_K):
        mx = jnp.max(cur, axis=-1, keepdims=True)
        idx = jnp.min(jnp.where(cur == mx, lanef, float(LANES)), axis=-1, keepdims=True)
        hot = lanef == idx
        cur = jnp.where(hot, -jnp.inf, cur)
        vals.append(mx)
        idxs.append(idx)
        hots.append(hot)
    exps = [jnp.exp(vv - vals[0]) for vv in vals]
    den = exps[0] + exps[1] + exps[2] + exps[3]
    yield
    sel = jnp.zeros((tm, LANES), F32)
    for hot in hots:
        sel = sel + jnp.where(hot, 1.0, 0.0)
    ri = lax.broadcasted_iota(jnp.int32, (tm, tm), 0)
    ci = lax.broadcasted_iota(jnp.int32, (tm, tm), 1)
    below = jnp.where(ri > ci, 1.0, 0.0).astype(BF16)
    before = _dot(below, sel.astype(BF16))
    counts = jnp.broadcast_to(jnp.sum(sel, axis=0, keepdims=True), (8, LANES))
    er = lax.broadcasted_iota(jnp.int32, (LANES, LANES), 0)
    ec = lax.broadcasted_iota(jnp.int32, (LANES, LANES), 1)
    lower = jnp.where(er < ec, 1.0, 0.0).astype(BF16)
    start = _dot(counts.astype(BF16), lower)[0:1, :]
    route = jnp.zeros((tm, LANES), F32)
    for j in range(TOP_K):
        pos = jnp.sum(jnp.where(hots[j], before + start, 0.0), axis=-1, keepdims=True)
        route = jnp.where(lane == j, pos, route)
        route = jnp.where(lane == TOP_K + j, exps[j] / den, route)
    route_ref[0, rows, :] = route
    cnt_ref[0, part] = counts
    yield


def _post(x, mod3, y, r, k, v, g, o, gates, lnw, lnb, rk, bd, wor, woa, wout, nfw, wrh, wrl, br, tm):
    bsz, s, d = x.shape
    w = RWKV_WIDTH
    parts = tm // ROUTE_TILE
    tok = lambda n: pl.BlockSpec((1, tm, n), lambda b, i: (b, i, 0))
    row = lambda n: pl.BlockSpec((1, n), lambda b, i: (0, 0))
    full = lambda a: pl.BlockSpec(a.shape, lambda b, i: (0,) * a.ndim)
    return pl.pallas_call(
        _post_kernel,
        grid=(bsz, s // tm),
        in_specs=[tok(d), pl.BlockSpec((1, 6, d), lambda b, i: (b, 0, 0)),
                  tok(w), tok(w), tok(w), tok(w), tok(w),
                  tok(MLA_HEADS * MLA_V), tok(2 * d),
                  row(w), row(w), row(w), full(bd), full(wor), full(woa), full(wout), row(d),
                  full(wrh), full(wrl), row(LANES)],
        out_specs=[tok(d), tok(d), tok(LANES), pl.BlockSpec((1, parts, 8, LANES), lambda b, i: (b, i, 0, 0))],
        out_shape=[jax.ShapeDtypeStruct((bsz, s, d), F32),
                   jax.ShapeDtypeStruct((bsz, s, d), F32),
                   jax.ShapeDtypeStruct((bsz, s, LANES), F32),
                   jax.ShapeDtypeStruct((bsz, s // ROUTE_TILE, 8, LANES), F32)],
        compiler_params=_params("parallel", "parallel"),
        name="post",
    )(x, mod3, y, r, k, v, g, o, gates, lnw, lnb, rk, bd, wor, woa, wout, nfw, wrh, wrl, br)


def _load_rows(ref, n):
    nc = ref.shape[0] // n
    return jnp.concatenate([ref[pl.ds(c, n, stride=nc), :] for c in range(nc)], axis=1)


def _store_rows(ref, val):
    nc = val.shape[1] // LANES
    for c in range(nc):
        ref[pl.ds(c, val.shape[0], stride=nc), :] = val[:, c * LANES:(c + 1) * LANES]


def _run_copies(tile, n_ref, src_ref, dst_ref, tm, make_copy):
    nbits = tm.bit_length()
    for b in range(nbits):
        group = tile * nbits + b

        def piece(j, carry, group=group, size=1 << b):
            k = group * N_EXPERTS + j
            make_copy(src_ref[k], dst_ref[k], size).start()
            return carry

        lax.fori_loop(0, n_ref[group], piece, 0)


def _pieces(count, limit, fn):
    def emit(bits):
        for b in bits:
            size = 1 << b
            done = count & ~(2 * size - 1)

            @pl.when((count & size) != 0)
            def _():
                fn(done, size)

    bits = range(limit.bit_length() - 1, -1, -1)
    large = [b for b in bits if b >= RARE_PIECE_BIT]
    if large:
        @pl.when(count >= (1 << RARE_PIECE_BIT))
        def _():
            emit(large)
    emit([b for b in bits if b < RARE_PIECE_BIT])


def _zero_fill(gap_ref, xs_ref, zero_ref, sem, nc):
    bm = MOE_BLOCK
    zero_ref[...] = jnp.zeros_like(zero_ref)

    def piece(dst, size):
        return pltpu.make_async_copy(zero_ref.at[pl.ds(0, size * nc), :],
                                     xs_ref.at[pl.ds(pl.multiple_of(dst * nc, nc), size * nc), :], sem)

    def sweep(act):
        def per_expert(e, carry):
            g0 = gap_ref[e]
            _pieces(gap_ref[N_EXPERTS + e], bm, lambda done, size: act(piece(g0 + done, size)))
            return carry

        def tail(blk, carry):
            act(piece(gap_ref[2 * N_EXPERTS] + blk * bm, bm))
            return carry

        lax.fori_loop(0, N_EXPERTS, per_expert, 0)
        lax.fori_loop(0, gap_ref[2 * N_EXPERTS + 1], tail, 0)

    sweep(lambda cp: cp.start())
    sweep(lambda cp: cp.wait())


def _dispatch_kernel(cnt_ref, off_ref, dst_ref, gap_ref, h_ref, route_ref, xs_ref, sorted_ref, zero_ref,
                     sem, zero_sem):
    tm = h_ref.shape[0]

    @pl.when(pl.program_id(0) == 0)
    def _():
        _zero_fill(gap_ref, xs_ref, zero_ref, zero_sem, h_ref.shape[1] // LANES)

    pos_t = route_ref[...].T
    slot = lax.broadcasted_iota(jnp.int32, (TOP_K * tm, tm), 0).astype(F32)
    perm = jnp.where(slot == pos_t[0:1, :], 1.0, 0.0)
    for j in range(1, TOP_K):
        perm = perm + jnp.where(slot == pos_t[j:j + 1, :], 1.0, 0.0)
    i = pl.program_id(0)
    cur = i % 2
    buf = sorted_ref.at[cur]
    _store_rows(buf, _dot(perm.astype(BF16), h_ref[...].astype(BF16)))
    nc = h_ref.shape[1] // LANES

    def make_copy(src, dst, size):
        return pltpu.make_async_copy(buf.at[pl.ds(pl.multiple_of(src * nc, nc), size * nc), :],
                                     xs_ref.at[pl.ds(pl.multiple_of(dst * nc, nc), size * nc), :], sem.at[cur])

    _run_copies(i, cnt_ref, off_ref, dst_ref, tm, make_copy)

    def drain(which):
        pltpu.make_async_copy(sorted_ref.at[which], xs_ref.at[pl.ds(0, TOP_K * tm * nc), :], sem.at[which]).wait()

    @pl.when(i > 0)
    def _():
        drain(1 - cur)

    @pl.when(i == pl.num_programs(0) - 1)
    def _():
        drain(cur)


def _dispatch(cnt_tab, off_tab, dst_tab, gap_tab, h2, route, n_rows, tm):
    t, d = h2.shape
    nc = d // LANES
    return pl.pallas_call(
        _dispatch_kernel,
        grid_spec=pltpu.PrefetchScalarGridSpec(
            num_scalar_prefetch=4,
            grid=(t // tm,),
            in_specs=[pl.BlockSpec((tm, d), lambda i, *_: (i, 0)),
                      pl.BlockSpec((tm, LANES), lambda i, *_: (i, 0))],
            out_specs=pl.BlockSpec(memory_space=pl.ANY),
            scratch_shapes=[pltpu.VMEM((2, TOP_K * tm * nc, LANES), F32), pltpu.VMEM((MOE_BLOCK * nc, LANES), F32),
                            pltpu.SemaphoreType.DMA((2,)), pltpu.SemaphoreType.DMA(())]),
        out_shape=jax.ShapeDtypeStruct((n_rows * nc, LANES), F32),
        compiler_params=_params("arbitrary"),
        name="dispatch",
    )(cnt_tab, off_tab, dst_tab, gap_tab, h2, route)


def _split_gate_up_kernel(w_ref, g_ref, u_ref, t_ref):
    half = LANES // 2
    nblk = w_ref.shape[1] // LANES
    for c in range(w_ref.shape[2] // LANES):
        w_t = w_ref[0, :, c * LANES:(c + 1) * LANES].T
        for b in range(nblk):
            t_ref[b] = w_t[:, b * LANES:(b + 1) * LANES]
        for out_ref, first in ((g_ref, 0), (u_ref, 1)):
            rows = [t_ref[b, pl.ds(first, half, stride=2), :] for b in range(nblk)]
            out_ref[0, c * half:(c + 1) * half, :] = jnp.concatenate(rows, axis=1).astype(BF16)


def _split_gate_up(w):
    e, d, ff2 = w.shape
    ff = ff2 // 2
    spec = pl.BlockSpec((1, ff, d), lambda i: (i, 0, 0))
    sds = jax.ShapeDtypeStruct((e, ff, d), BF16)
    return pl.pallas_call(
        _split_gate_up_kernel,
        grid=(e,),
        in_specs=[pl.BlockSpec((1, d, ff2), lambda i: (i, 0, 0))],
        out_specs=[spec, spec],
        out_shape=[sds, sds],
        scratch_shapes=[pltpu.VMEM((d // LANES, LANES, LANES), F32)],
        compiler_params=_params("parallel"),
        name="split_gate_up",
    )(w)


def _moe_kernel(blk_e_ref, valid_ref, xs_ref, wg_ref, wu_ref, bg_ref, bu_ref, wd_ref, bd_ref, ys_ref):
    del blk_e_ref
    i = pl.program_id(0)
    valid = valid_ref[i]

    @pl.when(valid > 0)
    def _():
        x = _load_rows(xs_ref, MOE_BLOCK).astype(BF16)
        gate = _dot_nt(x, wg_ref[0]) + bg_ref[0]
        up = _dot_nt(x, wu_ref[0]) + bu_ref[0]
        gate = jnp.minimum(gate, SWIGLU_LIMIT)
        up = jnp.clip(up, -SWIGLU_LIMIT, SWIGLU_LIMIT)
        act = (up + 1.0) * gate * jax.nn.sigmoid(SWIGLU_ALPHA * gate)
        _store_rows(ys_ref, _dot(act.astype(BF16), wd_ref[0]) + bd_ref[0])

    @pl.when(valid == 0)
    def _():
        ys_ref[...] = jnp.zeros_like(ys_ref)


def _moe(blk_e, blk_valid, xs, wg, wu, bg, bu, wd, bd):
    ff, d = wg.shape[1], wg.shape[2]
    nc = d // LANES
    n_rows = xs.shape[0] // nc
    bm = MOE_BLOCK
    wspec = lambda k, n: pl.BlockSpec((1, k, n), lambda i, be, nu: (be[i], 0, 0))
    return pl.pallas_call(
        _moe_kernel,
        grid_spec=pltpu.PrefetchScalarGridSpec(
            num_scalar_prefetch=2,
            grid=(n_rows // bm,),
            in_specs=[pl.BlockSpec((bm * nc, LANES), lambda i, be, nu: (i, 0)),
                      wspec(ff, d), wspec(ff, d), wspec(1, ff), wspec(1, ff), wspec(ff, d), wspec(1, d)],
            out_specs=pl.BlockSpec((bm * nc, LANES), lambda i, be, nu: (i, 0))),
        out_shape=jax.ShapeDtypeStruct((n_rows * nc, LANES), F32),
        compiler_params=_params("arbitrary"),
        name="moe",
    )(blk_e, blk_valid, xs, wg, wu, bg, bu, wd, bd)


def _combine_kernel(cnt_ref, off_ref, dst_ref, x1_ref, route_ref, mod_ref, ys_ref, o_ref, rows_ref, sem):
    tm = x1_ref.shape[1]
    tile = pl.program_id(0) * pl.num_programs(1) + pl.program_id(1)
    n_tiles = pl.num_programs(0) * pl.num_programs(1)
    nc = x1_ref.shape[2] // LANES
    cur = tile % 2

    def fetch(which_tile):
        which = which_tile % 2

        def make_copy(dst, src, size):
            return pltpu.make_async_copy(
                ys_ref.at[pl.ds(pl.multiple_of(src * nc, nc), size * nc), :],
                rows_ref.at[which, pl.ds(pl.multiple_of(dst * nc, nc), size * nc), :], sem.at[which])

        _run_copies(which_tile, cnt_ref, off_ref, dst_ref, tm, make_copy)

    @pl.when(tile == 0)
    def _():
        fetch(tile)

    @pl.when(tile + 1 < n_tiles)
    def _():
        fetch(tile + 1)

    route = route_ref[0]
    slot = lax.broadcasted_iota(jnp.int32, (tm, TOP_K * tm), 1).astype(F32)
    mix = jnp.where(slot == route[:, 0:1], route[:, TOP_K:TOP_K + 1], 0.0)
    for j in range(1, TOP_K):
        mix = mix + jnp.where(slot == route[:, j:j + 1], route[:, TOP_K + j:TOP_K + j + 1], 0.0)
    pltpu.make_async_copy(ys_ref.at[pl.ds(0, TOP_K * tm * nc), :], rows_ref.at[cur], sem.at[cur]).wait()
    acc = _dot(mix.astype(BF16), _load_rows(rows_ref.at[cur], TOP_K * tm).astype(BF16))
    o_ref[0] = x1_ref[0] + mod_ref[0, 5:6, :] * acc


def _combine(cnt_tab, off_tab, dst_tab, x1, route, mod3, ys, tm):
    bsz, s, d = x1.shape
    return pl.pallas_call(
        _combine_kernel,
        grid_spec=pltpu.PrefetchScalarGridSpec(
            num_scalar_prefetch=3,
            grid=(bsz, s // tm),
            in_specs=[pl.BlockSpec((1, tm, d), lambda b, i, *_: (b, i, 0)),
                      pl.BlockSpec((1, tm, LANES), lambda b, i, *_: (b, i, 0)),
                      pl.BlockSpec((1, 6, d), lambda b, i, *_: (b, 0, 0)),
                      pl.BlockSpec(memory_space=pl.ANY)],
            out_specs=pl.BlockSpec((1, tm, d), lambda b, i, *_: (b, i, 0)),
            scratch_shapes=[pltpu.VMEM((2, TOP_K * tm * d // LANES, LANES), F32), pltpu.SemaphoreType.DMA((2,))]),
        out_shape=jax.ShapeDtypeStruct((bsz, s, d), F32),
        compiler_params=_params("arbitrary", "arbitrary"),
        name="combine",
    )(cnt_tab, off_tab, dst_tab, x1, route, mod3, ys)


def _piece_tables(cnt_tab, off_tab, dst_tab, tm):
    nbits = tm.bit_length()
    bits = jnp.arange(nbits, dtype=jnp.int32)[None, :, None]
    cnt = cnt_tab[:, None, :]
    active = (cnt >> bits) & 1
    done = cnt & ~((2 << bits) - 1)
    rank = jnp.cumsum(active, axis=-1) - active
    slot = jnp.arange(N_EXPERTS, dtype=jnp.int32)[None, None, :, None]
    place = (active[:, :, None, :] == 1) & (rank[:, :, None, :] == slot)
    compact = lambda v: jnp.sum(jnp.where(place, (v[:, None, :] + done)[:, :, None, :], 0), axis=-1)
    flat = lambda v: v.reshape(-1).astype(jnp.int32)
    return flat(jnp.sum(active, axis=-1)), flat(compact(off_tab)), flat(compact(dst_tab))


def _pad_cols(a, n):
    return jnp.pad(a, ((0, 0), (0, n - a.shape[1])))


def _head_blocks(cols_main, cols_rot=None):
    k = cols_main.shape[0]
    out = jnp.zeros((k, MLA_HEADS, HEAD_PAD), F32)
    out = out.at[:, :, :cols_main.shape[2]].set(cols_main)
    return out.reshape(k, MLA_HEADS * HEAD_PAD)


def _layer(x, cond_mod, positions, w_in, q_a_norm_w, w_q_up, kv_a_norm_w, w_kv_up, q_norm_w, k_norm_w,
           w_o_mla, rwkv_mu, rwkv_w0, rwkv_w2, rwkv_a0, rwkv_a2, rwkv_g2, rwkv_k_k, rwkv_k_a, rwkv_r_k,
           rwkv_ln_w, rwkv_ln_b, rwkv_w_o, w_out, norm_mix_w, norm_ffn_w, w_router, b_router,
           w_gate_up, b_gate_up, w_down, b_down):
    bsz, s, d = x.shape
    t = bsz * s
    half = MLA_ROPE // 2
    nope, qk = MLA_NOPE, MLA_QK
    mod3 = cond_mod.reshape(bsz, 6, d)

    o_q, o_kv, o_kr = 0, Q_LORA, Q_LORA + KV_LORA
    o_slab = o_kr + MLA_ROPE
    o_gate = o_slab + RWKV_SLAB
    kr_w = w_in[:, o_kr:o_slab]
    zeros = lambda n: jnp.zeros((d, n), F32)
    kr_blk = jnp.concatenate([zeros(nope), kr_w, zeros(HEAD_PAD - qk)], axis=1)
    kr_rot = jnp.concatenate([zeros(nope), -kr_w[:, half:], kr_w[:, :half], zeros(HEAD_PAD - qk)], axis=1)
    w_in_p = jnp.concatenate([w_in[:, o_q:o_kr], kr_blk, kr_rot, w_in[:, o_slab:]], axis=1).astype(BF16)

    tm = min(256, s)

    wq3 = w_q_up.reshape(Q_LORA, MLA_HEADS, qk)
    wq_rot = jnp.concatenate([jnp.zeros((Q_LORA, MLA_HEADS, nope), F32), -wq3[:, :, nope + half:],
                              wq3[:, :, nope:nope + half]], axis=2)
    wq = jnp.concatenate([_head_blocks(wq3), _head_blocks(wq_rot)], axis=1).astype(BF16)
    wkv3 = w_kv_up.reshape(KV_LORA, MLA_HEADS, nope + MLA_V)
    wkv = jnp.concatenate([_head_blocks(wkv3[:, :, :nope]), _head_blocks(wkv3[:, :, nope:])], axis=1).astype(BF16)

    def gains(wn):
        main = jnp.pad(wn, (0, HEAD_PAD - qk)).reshape(1, HEAD_PAD)
        rot = jnp.concatenate([jnp.zeros((nope,), F32), wn[nope + half:], wn[nope:nope + half],
                               jnp.zeros((HEAD_PAD - qk,), F32)]).reshape(1, HEAD_PAD)
        return main, rot

    qg, qgs = gains(q_norm_w)
    kg, kgs = gains(k_norm_w)
    inv_freq = ROPE_THETA ** (-jnp.arange(half, dtype=F32) / half)
    per_row = LANES // half
    pos_rep = jnp.repeat(positions.astype(F32).reshape(t // per_row, per_row), half, axis=1)
    cos16, sin16 = _rope_table(pos_rep, jnp.tile(inv_freq, per_row).reshape(1, LANES))
    cos16 = cos16.reshape(bsz, s, half)
    sin16 = sin16.reshape(bsz, s, half)
    cosf = jnp.concatenate([jnp.ones((bsz, s, nope), F32), cos16, cos16,
                            jnp.ones((bsz, s, HEAD_PAD - qk), F32)], axis=-1)
    sinf = jnp.concatenate([jnp.zeros((bsz, s, nope), F32), sin16, sin16,
                            jnp.zeros((bsz, s, HEAD_PAD - qk), F32)], axis=-1)

    w = RWKV_WIDTH
    wwa = jnp.zeros((LANES, 2 * w), F32)
    wwa = wwa.at[:DECAY_LORA, :w].set(rwkv_w2).at[DECAY_LORA:, w:].set(rwkv_a2)
    wwh = wwa.astype(BF16)
    wwl = (wwa - wwh.astype(F32)).astype(BF16)
    wws = jnp.concatenate([wwh, wwl, wwh], axis=0)
    hid = np.arange(w) // RWKV_HEAD
    bd = jnp.asarray(hid[:, None] == hid[None, :], BF16)
    gates, q, k, vt, r_, lw_, k_, v_, a_, b_, g_ = _front(
        x, mod3, norm_mix_w, w_in_p, cosf, sinf, q_a_norm_w.reshape(1, -1), kv_a_norm_w.reshape(1, -1),
        wq, wkv, qg, qgs, kg, kgs, rwkv_mu.reshape(1, -1), wws, rwkv_w0.reshape(1, -1),
        rwkv_a0.reshape(1, -1), rwkv_g2.astype(BF16), rwkv_k_k.reshape(1, -1), rwkv_k_a.reshape(1, -1), bd,
        tm, min(ATTN_KEY_BLOCK, s))
    o = _attention(q, k, vt, min(ATTN_BLOCK, s))
    y = _wkv(r_, lw_, k_, v_, a_, b_, WKV_BATCH if bsz % WKV_BATCH == 0 else 1)

    woa = w_o_mla.astype(BF16)
    wr =_pad_cols(w_router, LANES)
    wrh = wr.astype(BF16)
    wrl = (wr - wrh.astype(F32)).astype(BF16)
    br = jnp.pad(b_router, (0, LANES - N_EXPERTS)).reshape(1, LANES)
    x1, h2, route, counts = _post(
        x, mod3, y, r_, k_, v_, g_, o, gates, rwkv_ln_w.reshape(1, -1), rwkv_ln_b.reshape(1, -1),
        rwkv_r_k.reshape(1, -1), bd, rwkv_w_o.astype(BF16), woa, w_out.astype(BF16),
        norm_ffn_w.reshape(1, -1), wrh, wrl, br, POST_TILE if s % POST_TILE == 0 else ROUTE_TILE)

    bm = MOE_BLOCK
    n_rows = t * TOP_K + N_EXPERTS * bm
    n_blocks = n_rows // bm
    cnt_tab = counts[:, :, 0, :N_EXPERTS].reshape(t // ROUTE_TILE, N_EXPERTS).astype(jnp.int32)
    total = jnp.sum(cnt_tab, axis=0)
    padded = (total + bm - 1) // bm * bm
    pad_end = jnp.cumsum(padded)
    pad_start = pad_end - padded
    off_tab = jnp.cumsum(cnt_tab, axis=1) - cnt_tab
    dst_tab = pad_start[None, :] + jnp.cumsum(cnt_tab, axis=0) - cnt_tab
    blk_start = jnp.arange(n_blocks, dtype=jnp.int32) * bm
    blk_e = jnp.minimum(jnp.sum((pad_end[None, :] <= blk_start[:, None]).astype(jnp.int32), axis=1),
                        N_EXPERTS - 1)
    blk_valid = jnp.clip((pad_start + total)[blk_e] - blk_start, 0, bm).astype(jnp.int32)
    tabs = _piece_tables(cnt_tab, off_tab, dst_tab, ROUTE_TILE)
    gap_tab = jnp.concatenate([pad_start + total, padded - total, pad_end[-1:],
                               (n_rows - pad_end[-1:]) // bm]).astype(jnp.int32)

    xs = _dispatch(*tabs, gap_tab, h2.reshape(t, d), route.reshape(t, LANES), n_rows, ROUTE_TILE)
    wg_t, wu_t = _split_gate_up(w_gate_up)
    bgu = b_gate_up.reshape(N_EXPERTS, 1, D_FF, 2)
    ys = _moe(blk_e, blk_valid, xs, wg_t, wu_t,
              bgu[..., 0], bgu[..., 1], w_down.astype(BF16), b_down.reshape(N_EXPERTS, 1, d))
    return _combine(*tabs, x1, route, mod3, ys, ROUTE_TILE)


def kernel(x, c, positions, ada_w, ada_b, norm_mix_w, norm_ffn_w, w_in, q_a_norm_w, w_q_up, kv_a_norm_w, w_kv_up, q_norm_w, k_norm_w, w_o_mla, rwkv_mu, rwkv_w0, rwkv_w2, rwkv_a0, rwkv_a2, rwkv_g2, rwkv_k_k, rwkv_k_a, rwkv_r_k, rwkv_ln_w, rwkv_ln_b, rwkv_w_o, w_out, w_router, b_router, w_gate_up, b_gate_up, w_down, b_down):
    depth = ada_w.shape[0]
    for l in range(depth):
        mod = _ada(c, ada_w[l], ada_b[l])
        x = _layer(x, mod, positions, w_in[l], q_a_norm_w[l], w_q_up[l], kv_a_norm_w[l], w_kv_up[l],
                   q_norm_w[l], k_norm_w[l], w_o_mla[l], rwkv_mu[l], rwkv_w0[l], rwkv_w2[l], rwkv_a0[l],
                   rwkv_a2[l], rwkv_g2[l], rwkv_k_k[l], rwkv_k_a[l], rwkv_r_k[l], rwkv_ln_w[l],
                   rwkv_ln_b[l], rwkv_w_o[l], w_out[l], norm_mix_w[l], norm_ffn_w[l], w_router[l],
                   b_router[l], w_gate_up[l], b_gate_up[l], w_down[l], b_down[l])
    return x
```

```python
import functools

import numpy as np
import jax
import jax.numpy as jnp
from jax import lax
from jax.experimental import pallas as pl
from jax.experimental.pallas import tpu as pltpu

F32 = jnp.float32
BF16 = jnp.bfloat16

D_MODEL = 1024
MLA_HEADS = 8
MLA_NOPE = 64
MLA_ROPE = 32
MLA_QK = MLA_NOPE + MLA_ROPE
MLA_V = 64
VT_ROWS = 80
Q_LORA = 256
KV_LORA = 128
ROPE_THETA = 10000.0
RWKV_HEADS = 8
RWKV_HEAD = 64
RWKV_WIDTH = RWKV_HEADS * RWKV_HEAD
DECAY_LORA = 64
AAA_LORA = 64
GATE_LORA = 128
RWKV_SLAB = 3 * RWKV_WIDTH + DECAY_LORA + AAA_LORA + GATE_LORA
GN_EPS = 64e-5
N_EXPERTS = 32
TOP_K = 4
D_FF = D_MODEL
SWIGLU_LIMIT = 7.0
SWIGLU_ALPHA = 1.702
NORM_EPS = 1e-6
LOG2_E = 1.4426950408889634

LANES = 128
HEAD_PAD = 128
MLA_COLS = 640
WKV_CHUNK = 64
WKV_BATCH = 8
ATTN_BLOCK = 512
ATTN_KEY_BLOCK = 512
ATTN_HEADS = 8
ROUTE_TILE = 256
POST_TILE = 512
RARE_PIECE_BIT = 6
MOE_BLOCK = 512
VMEM_LIMIT = 56 * 1024 * 1024


def _dot(a, b):
    return jnp.dot(a, b, preferred_element_type=F32)


def _dot_nt(a, b):
    return lax.dot_general(a, b, (((1,), (1,)), ((), ())), preferred_element_type=F32)


def _split(x):
    hi = x.astype(BF16)
    lo = (x - hi.astype(F32)).astype(BF16)
    return hi, lo


def _mm3(a, b):
    ah, al = _split(a)
    bh, bl = _split(b)
    return _dot(ah, bh) + (_dot(ah, bl) + _dot(al, bh))


def _trace_in_turn(stages):
    live = list(stages)
    while live:
        for gen in list(live):
            try:
                next(gen)
            except StopIteration:
                live.remove(gen)


def _params(*sem):
    return pltpu.CompilerParams(dimension_semantics=sem, vmem_limit_bytes=VMEM_LIMIT)


def _ada_kernel(c_ref, w_ref, b_ref, o_ref):
    c = c_ref[...]
    cond = c * jax.nn.sigmoid(c)
    o_ref[...] = _mm3(cond, w_ref[...]) + b_ref[...]


def _ada(c, w, b):
    bsz, d = c.shape
    n = w.shape[1]
    tn = 1024
    return pl.pallas_call(
        _ada_kernel,
        grid=(n // tn,),
        in_specs=[pl.BlockSpec((bsz, d), lambda j: (0, 0)),
                  pl.BlockSpec((d, tn), lambda j: (0, j)),
                  pl.BlockSpec((1, tn), lambda j: (0, j))],
        out_specs=pl.BlockSpec((bsz, tn), lambda j: (0, j)),
        out_shape=jax.ShapeDtypeStruct((bsz, n), F32),
        compiler_params=_params("parallel"),
        name="ada",
    )(c, w, b.reshape(1, n))


def _front_kernel(x_ref, mod_ref, nw_ref, w_ref,
                  cos_ref, sin_ref, qan_ref, kvn_ref, wq_ref, wkv_ref, qg_ref, qgs_ref, kg_ref, kgs_ref,
                  mu_ref, wws_ref, w0_ref, a0_ref, g2_ref, kk_ref, ka_ref, bd_ref,
                  gates_ref, q_out, k_out, vt_out, r_out, lw_out, k2_out, v_out, a_out, b_out, g_out,
                  last_ref):
    x = x_ref[0]
    ms = jnp.mean(x * x, axis=-1, keepdims=True)
    y = x * lax.rsqrt(ms + NORM_EPS) * nw_ref[...]
    h = y * (1.0 + mod_ref[0, 1:2, :]) + mod_ref[0, 0:1, :]
    hb = h.astype(BF16)
    mla = _dot(hb, w_ref[:, 0:MLA_COLS])
    slab = _dot(hb, w_ref[:, MLA_COLS:MLA_COLS + RWKV_SLAB])
    gates_ref[0] = _dot(hb, w_ref[:, MLA_COLS + RWKV_SLAB:]).astype(BF16)
    mla_prep = _mla_prep_body(mla, cos_ref, sin_ref, qan_ref, kvn_ref, wq_ref, wkv_ref,
                              qg_ref, qgs_ref, kg_ref, kgs_ref, q_out, k_out, vt_out)
    rwkv_prep = _rwkv_prep_body(slab, pl.program_id(1) == 0, mu_ref, wws_ref, w0_ref, a0_ref, g2_ref,
                                kk_ref, ka_ref, bd_ref, r_out, lw_out, k2_out, v_out, a_out, b_out, g_out,
                                last_ref)
    _trace_in_turn([mla_prep, rwkv_prep])


def _front(x, mod3, norm_w, w_in_p, cosf, sinf, qan, kvn, wq, wkv, qg, qgs, kg, kgs,
           mu, wws, w0, a0, g2, k_k, k_a, bd, tm, tk):
    bsz, s, d = x.shape
    w = RWKV_WIDTH
    tok = lambda n: pl.BlockSpec((1, tm, n), lambda b, i: (b, i, 0))
    row = lambda n: pl.BlockSpec((1, n), lambda b, i: (0, 0))
    full = lambda a: pl.BlockSpec(a.shape, lambda b, i: (0,) * a.ndim)
    head_spec = pl.BlockSpec((1, MLA_HEADS, tm, HEAD_PAD), lambda b, i: (b, 0, i, 0))
    head_sds = jax.ShapeDtypeStruct((bsz, MLA_HEADS, s, HEAD_PAD), BF16)
    if tk >= tm:
        per = tk // tm
        vt_spec = pl.BlockSpec((1, MLA_HEADS, 1, VT_ROWS, tm), lambda b, i: (b, 0, i // per, 0, i % per))
    else:
        vt_spec = pl.BlockSpec((1, MLA_HEADS, tm // tk, VT_ROWS, tk), lambda b, i: (b, 0, i, 0, 0))
    vt_sds = jax.ShapeDtypeStruct((bsz, MLA_HEADS, s // tk, VT_ROWS, tk), BF16)
    stream_sds = jax.ShapeDtypeStruct((bsz, s, w), F32)
    return pl.pallas_call(
        _front_kernel,
        grid=(bsz, s // tm),
        in_specs=[tok(d), pl.BlockSpec((1, 6, d), lambda b, i: (b, 0, 0)), row(d), full(w_in_p),
                  tok(LANES), tok(LANES), row(Q_LORA), row(KV_LORA), full(wq), full(wkv),
                  row(LANES), row(LANES), row(LANES), row(LANES),
                  row(RWKV_SLAB), full(wws), row(w), row(w), full(g2), row(w), row(w), full(bd)],
        out_specs=[tok(2 * d), head_spec, head_spec, vt_spec] + [tok(w)] * 7,
        out_shape=[jax.ShapeDtypeStruct((bsz, s, 2 * d), BF16), head_sds, head_sds, vt_sds] + [stream_sds] * 7,
        scratch_shapes=[pltpu.VMEM((8, RWKV_SLAB), F32)],
        compiler_params=_params("parallel", "arbitrary"),
        name="front",
    )(x, mod3, norm_w.reshape(1, d), w_in_p, cosf, sinf, qan, kvn, wq, wkv, qg, qgs, kg, kgs,
      mu, wws, w0, a0, g2, k_k, k_a, bd)


def _rope_kernel(pos_ref, invf_ref, cos_ref, sin_ref):
    ang = pos_ref[...] * invf_ref[...]
    cos_ref[...] = jnp.cos(ang)
    sin_ref[...] = jnp.sin(ang)


def _rope_table(pos_rep, invf_row):
    n = pos_rep.shape[0]
    tr = min(512, n)
    spec = pl.BlockSpec((tr, LANES), lambda i: (i, 0))
    sds = jax.ShapeDtypeStruct((n, LANES), F32)
    return pl.pallas_call(
        _rope_kernel,
        grid=(n // tr,),
        in_specs=[spec, pl.BlockSpec((1, LANES), lambda i: (0, 0))],
        out_specs=[spec, spec],
        out_shape=[sds, sds],
        compiler_params=_params("parallel"),
        name="rope",
    )(pos_rep, invf_row)


def _mla_prep_body(m, cos_ref, sin_ref, qan_ref, kvn_ref, wq_ref, wkv_ref,
                   qg_ref, qgs_ref, kg_ref, kgs_ref, q_out, k_out, vt_out):
    cq = m[:, 0:Q_LORA]
    ckv = m[:, Q_LORA:Q_LORA + KV_LORA]
    kr = m[:, 384:512]
    krs = m[:, 512:640]
    ql = cq * lax.rsqrt(jnp.mean(cq * cq, axis=-1, keepdims=True) + NORM_EPS) * qan_ref[...]
    kvl = ckv * lax.rsqrt(jnp.mean(ckv * ckv, axis=-1, keepdims=True) + NORM_EPS) * kvn_ref[...]
    qall = _dot(ql.astype(BF16), wq_ref[...])
    kvall = _dot(kvl.astype(BF16), wkv_ref[...])
    cosf = cos_ref[0]
    sinf = sin_ref[0]
    scale = MLA_QK ** -0.5 * LOG2_E
    hw = MLA_HEADS * HEAD_PAD
    for h in range(MLA_HEADS):
        lo, hi = h * HEAD_PAD, (h + 1) * HEAD_PAD
        qh = qall[:, lo:hi]
        qs = qall[:, hw + lo:hw + hi]
        rs = lax.rsqrt(jnp.sum(qh * qh, axis=-1, keepdims=True) * (1.0 / MLA_QK) + NORM_EPS)
        qo = (qh * rs * qg_ref[...]) * cosf + (qs * rs * qgs_ref[...]) * sinf
        q_out[0, h] = (qo * scale).astype(BF16)
        kh = kvall[:, lo:hi] + kr
        rs = lax.rsqrt(jnp.sum(kh * kh, axis=-1, keepdims=True) * (1.0 / MLA_QK) + NORM_EPS)
        ko = (kh * rs * kg_ref[...]) * cosf + (krs * rs * kgs_ref[...]) * sinf
        k_out[0, h] = ko.astype(BF16)
        extra = lax.broadcasted_iota(jnp.int32, (VT_ROWS - MLA_V, m.shape[0]), 0)
        v_t = jnp.concatenate([kvall[:, hw + lo:hw + hi].T[0:MLA_V, :],
                               jnp.where(extra == 0, 1.0, 0.0)], axis=0).astype(BF16)
        width = vt_out.shape[4]
        for c in range(vt_out.shape[2]):
            vt_out[0, h, c] = v_t[:, c * width:(c + 1) * width]
        yield


def _attn_kernel(q_ref, k_ref, vt_ref, o_ref, m_ref, acc_ref, *, tq):
    qi = pl.program_id(2)
    heads = range(q_ref.shape[1])
    qs = [q_ref[0, h] for h in heads]
    m_ref[...] = jnp.full_like(m_ref, -jnp.inf)
    acc_ref[...] = jnp.zeros_like(acc_ref)

    tk = vt_ref.shape[4]
    per = tq // tk

    def step(j, causal=False, k0=0, nk=tk, q0=0, nq=tq, kbase=0):
        off = pl.multiple_of(j * tk, tk)
        sts = [_dot_nt(k_ref[0, h, pl.ds(off + k0, nk), :], qs[h][q0:q0 + nq]) for h in heads]
        if causal:
            keys = lax.broadcasted_iota(jnp.int32, (nk, nq), 0) + (k0 + kbase)
            queries = lax.broadcasted_iota(jnp.int32, (nk, nq), 1) + q0
        for h in heads:
            st = jnp.where(keys <= queries, sts[h], -jnp.inf) if causal else sts[h]
            m = m_ref[h, :, q0:q0 + nq]
            m_new = jnp.maximum(m, jnp.max(st, axis=0, keepdims=True))
            p = jnp.exp2(st - m_new)
            alpha = jnp.exp2(m - m_new)
            m_ref[h, :, q0:q0 + nq] = m_new
            acc_ref[h, :, q0:q0 + nq] = (alpha * acc_ref[h, :, q0:q0 + nq]
                                         + _dot(vt_ref[0, h, j, :, k0:k0 + nk], p.astype(BF16)))

    def body(j, carry):
        step(j)
        return carry

    lax.fori_loop(0, qi * per, body, 0)
    if per == 1:
        step(qi, True, 0, tk // 2)
        step(qi, True, tk // 2, tk // 2, tq // 2, tq // 2)
    else:
        for part in range(per):
            step(qi * per + part, True, 0, tk, part * tk, tq - part * tk, part * tk)
    out = lambda h: acc_ref[h, 0:MLA_V, :] / acc_ref[h, MLA_V:MLA_V + 1, :]
    for h in heads[::2]:
        o_ref[0, :, h * MLA_V:(h + 2) * MLA_V] = jnp.concatenate([out(h), out(h + 1)], axis=0).T.astype(BF16)


def _attention(q, k, vt, tq):
    bsz, nh, s, dh = q.shape
    dv, tk = vt.shape[3], vt.shape[4]
    hp = ATTN_HEADS
    return pl.pallas_call(
        functools.partial(_attn_kernel, tq=tq),
        grid=(bsz, nh // hp, s // tq),
        in_specs=[pl.BlockSpec((1, hp, tq, dh), lambda b, h, i: (b, h, i, 0)),
                  pl.BlockSpec((1, hp, s, dh), lambda b, h, i: (b, h, 0, 0)),
                  pl.BlockSpec((1, hp, s // tk, dv, tk), lambda b, h, i: (b, h, 0, 0, 0))],
        out_specs=pl.BlockSpec((1, tq, hp * MLA_V), lambda b, h, i: (b, i, h)),
        out_shape=jax.ShapeDtypeStruct((bsz, s, nh * MLA_V), BF16),
        scratch_shapes=[pltpu.VMEM((hp, 1, tq), F32), pltpu.VMEM((hp, dv, tq), F32)],
        compiler_params=_params("parallel", "parallel", "parallel"),
        name="attn",
    )(q, k, vt)


def _softplus(x):
    return jnp.maximum(x, 0.0) + jnp.log(1.0 + jnp.exp(-jnp.abs(x)))


def _rwkv_prep_body(slab, first_tile, mu_ref, wws_ref, w0_ref, a0_ref, g2_ref, kk_ref, ka_ref, bd_ref,
                    r_out, lw_out, k_out, v_out, a_out, b_out, g_out, last_ref):
    tm = slab.shape[0]
    w = RWKV_WIDTH
    rolled = pltpu.roll(slab, 1, 0)
    rowi = lax.broadcasted_iota(jnp.int32, slab.shape, 0)
    carried = jnp.where(first_tile, 0.0, last_ref[0:1, :])
    prev = jnp.where(rowi == 0, carried, rolled)
    last_ref[0:1, :] = slab[tm - 1:tm, :]
    p = slab + (prev - slab) * mu_ref[...]
    r = p[:, 0:w]
    k = p[:, w:2 * w]
    v = p[:, 2 * w:3 * w]
    r_out[0] = r
    v_out[0] = v
    yield
    wa = p[:, 3 * w:3 * w + LANES]
    g_lo = p[:, 3 * w + LANES:3 * w + 2 * LANES]
    lane = lax.broadcasted_iota(jnp.int32, wa.shape, 1)
    wa = jnp.where(lane < DECAY_LORA, jnp.tanh(wa), wa)
    wah, wal = _split(wa)
    wa_o = _dot(jnp.concatenate([wah, wah, wal], axis=1), wws_ref[...])
    g_out[0] = _dot(jax.nn.sigmoid(g_lo).astype(BF16), g2_ref[...])
    yield
    log_w = -_softplus(-(w0_ref[...] + wa_o[:, 0:w])) - 0.5
    lw_out[0] = -jnp.exp(log_w)
    yield
    a = jax.nn.sigmoid(a0_ref[...] + wa_o[:, w:2 * w])
    k_out[0] = k * (1.0 + (a - 1.0) * ka_ref[...])
    yield
    kk = k * kk_ref[...]
    ss = _dot((kk * kk).astype(BF16), bd_ref[...])
    kk = kk / jnp.maximum(jnp.sqrt(ss), 1e-12)
    a_out[0] = -kk
    b_out[0] = kk * a
    yield


def _wkv_kernel(r_ref, lw_ref, k_ref, v_ref, a_ref, b_ref, y_ref, s_ref):
    c = WKV_CHUNK
    n = pl.program_id(1)

    @pl.when(n == 0)
    def _():
        s_ref[...] = jnp.zeros_like(s_ref)

    row = lax.broadcasted_iota(jnp.int32, (LANES, LANES), 0)
    col = lax.broadcasted_iota(jnp.int32, (LANES, LANES), 1)
    lower_strict = row > col
    lower_incl = row >= col
    eye = row == col
    same16 = (row >> 4) == (col >> 4)
    same32 = (row >> 5) == (col >> 5)
    ident = jnp.where(eye, 1.0, 0.0).astype(F32)
    tr = lax.broadcasted_iota(jnp.int32, (c, c), 0)
    tc = lax.broadcasted_iota(jnp.int32, (c, c), 1)
    tri = jnp.where(tr >= tc, 1.0, 0.0).astype(BF16)
    first = lax.broadcasted_iota(jnp.int32, (c, LANES), 1) < RWKV_HEAD

    def stack(x):
        return jnp.concatenate([jnp.where(first, x, 0.0), jnp.where(first, 0.0, x)], axis=0)

    def mm(x, y):
        return _dot(x.astype(BF16), y.astype(BF16))

    nb = r_ref.shape[0]
    probs = [(bi, p) for bi in range(nb) for p in range(RWKV_HEADS // 2)]
    each = lambda f, *xs: [f(*args) for args in zip(*xs)]
    load = lambda ref: [ref[bi, :, p * LANES:(p + 1) * LANES] for bi, p in probs]
    r, lw, k, v, a, b = (load(ref) for ref in (r_ref, lw_ref, k_ref, v_ref, a_ref, b_ref))

    def cumsum(x):
        hi = x.astype(BF16)
        rem = x - hi.astype(F32)
        mid = rem.astype(BF16)
        lo = (rem - mid.astype(F32)).astype(BF16)
        return _dot(tri, hi) + (_dot(tri, mid) + _dot(tri, lo))

    cum = each(cumsum, lw)
    cum_c = [x[c - 1:c, :] for x in cum]
    e_in = each(jnp.exp, cum)
    e_neg = each(lambda x: jnp.exp(-x), cum)
    e_end = each(lambda x, xc: jnp.exp(xc - x), cum, cum_c)
    a_s = each(lambda x, cu, l: stack(x * jnp.exp(cu - l)).astype(BF16), a, cum, lw)
    r_s = each(lambda x, e: stack(x * e), r, e_in)
    b_s = each(lambda x, e: stack(x * e).astype(BF16), b, e_neg)
    k_s = each(lambda x, e: stack(x * e).astype(BF16), k, e_neg)
    b_e = each(lambda x, e: stack(x * e).T.astype(BF16), b, e_end)
    k_e = each(lambda x, e: stack(x * e).T.astype(BF16), k, e_end)
    v_s = each(lambda x: stack(x).astype(BF16), v)

    tt = each(lambda x1, x2, y1, y2: _dot_nt(jnp.concatenate([x1, x2.astype(BF16)], axis=0),
                                             jnp.concatenate([y1, y2], axis=0)), a_s, r_s, b_s, k_s)
    d_ab = [jnp.where(lower_strict, x[0:LANES, 0:LANES], 0.0) for x in tt]
    e_ak = [jnp.where(lower_strict, x[0:LANES, LANES:], 0.0).astype(BF16) for x in tt]
    f_rb = [jnp.where(lower_incl, x[LANES:, 0:LANES], 0.0).astype(BF16) for x in tt]
    f_rk = [jnp.where(lower_incl, x[LANES:, LANES:], 0.0).astype(BF16) for x in tt]

    d16 = [jnp.where(same16, x, 0.0) for x in d_ab]
    d32 = [jnp.where(same32, x, 0.0) for x in d_ab]
    z = each(_dot, e_ak, v_s)
    x2 = each(mm, d16, d16)
    x4 = each(mm, x2, x2)
    x8 = each(mm, x4, x4)
    t = [ident + x for x in d16]
    t = each(lambda t_, x: t_ + mm(t_, x), t, x2)
    t = each(lambda t_, x: t_ + mm(t_, x), t, x4)
    t = each(lambda t_, x: t_ + mm(t_, x), t, x8)
    t = each(lambda t_, hi, lo: t_ + mm(mm(t_, hi - lo), t_), t, d32, d16)
    t = each(lambda t_, hi, lo: t_ + mm(mm(t_, hi - lo), t_), t, d_ab, d32)

    au = each(lambda t_, x, zz: mm(t_, jnp.concatenate([x, zz.astype(BF16)], axis=1)).astype(BF16),
              t, a_s, z)
    g = each(_dot, f_rb, au)
    y0 = each(lambda gg, f, vv: gg[:, LANES:] + _dot(f, vv), g, f_rk, v_s)
    r1 = each(lambda x, gg: x + gg[:, 0:LANES], r_s, g)
    hmat = each(_dot, b_e, au)
    m_mat = each(lambda h, xc: jnp.where(eye, jnp.exp(xc), 0.0) + h[:, 0:LANES], hmat, cum_c)
    n_mat = each(lambda h, ke, vv: h[:, LANES:] + _dot(ke, vv), hmat, k_e, v_s)

    for i, (bi, p) in enumerate(probs):
        s0 = s_ref[i]
        s0b = s0.astype(BF16)
        ys = _dot(r1[i].astype(BF16), s0b) + y0[i]
        s_ref[i] = _dot(m_mat[i].astype(BF16), s0b) + n_mat[i]
        y_ref[bi, :, p * LANES:(p + 1) * LANES] = ys[0:c, :] + ys[c:, :]


def _wkv(r, lw, k, v, a, b, nb):
    bsz, s, w = r.shape
    spec = pl.BlockSpec((nb, WKV_CHUNK, w), lambda bi, n: (bi, n, 0))
    return pl.pallas_call(
        _wkv_kernel,
        grid=(bsz // nb, s // WKV_CHUNK),
        in_specs=[spec] * 6,
        out_specs=spec,
        out_shape=jax.ShapeDtypeStruct((bsz, s, w), F32),
        scratch_shapes=[pltpu.VMEM((nb * RWKV_HEADS // 2, LANES, LANES), F32)],
        compiler_params=_params("parallel", "arbitrary"),
        name="wkv",
    )(r, lw, k, v, a, b)


def _post_kernel(x_ref, mod_ref, y_ref, r_ref, k_ref, v_ref, g_ref, o_ref, gates_ref,
                 lnw_ref, lnb_ref, rk_ref, bd_ref, wor_ref, woa_ref, wout_ref, nfw_ref,
                 wrh_ref, wrl_ref, br_ref,
                 x1_ref, h2_ref, route_ref, cnt_ref):
    tiles = [_post_tile(part, x_ref, mod_ref, y_ref, r_ref, k_ref, v_ref, g_ref, o_ref, gates_ref,
                        lnw_ref, lnb_ref, rk_ref, bd_ref, wor_ref, woa_ref, wout_ref, nfw_ref,
                        wrh_ref, wrl_ref, br_ref, x1_ref, h2_ref, route_ref, cnt_ref)
             for part in range(x_ref.shape[1] // ROUTE_TILE)]
    _trace_in_turn(tiles)


def _post_tile(part, x_ref, mod_ref, y_ref, r_ref, k_ref, v_ref, g_ref, o_ref, gates_ref,
               lnw_ref, lnb_ref, rk_ref, bd_ref, wor_ref, woa_ref, wout_ref, nfw_ref,
               wrh_ref, wrl_ref, br_ref, x1_ref, h2_ref, route_ref, cnt_ref):
    tm = ROUTE_TILE
    rows = slice(part * tm, (part + 1) * tm)
    d = x_ref.shape[2]
    bd = bd_ref[...]
    inv_n = 1.0 / RWKV_HEAD
    y = y_ref[0, rows, :]
    seg = lambda t: _dot(t.astype(BF16), bd)
    mu = seg(y) * inv_n
    dlt = y - mu
    var = seg(dlt * dlt) * inv_n
    yn = dlt * lax.rsqrt(var + GN_EPS) * lnw_ref[...] + lnb_ref[...]
    yield
    v = v_ref[0, rows, :]
    bonus = seg(r_ref[0, rows, :] * k_ref[0, rows, :] * rk_ref[...]) * v
    z = (yn + bonus) * g_ref[0, rows, :]
    o_b = _dot(z.astype(BF16), wor_ref[...])
    o_a = _dot(o_ref[0, rows, :], woa_ref[...])
    yield
    ga = jax.nn.sigmoid(gates_ref[0, rows, 0:d].astype(F32))
    gb = jax.nn.sigmoid(gates_ref[0, rows, d:2 * d].astype(F32))
    merged = ga * o_a + gb * o_b
    mix = _dot(merged.astype(BF16), wout_ref[...])
    yield
    x1 = x_ref[0, rows, :] + mod_ref[0, 2:3, :] * mix
    x1_ref[0, rows, :] = x1
    ms = jnp.mean(x1 * x1, axis=-1, keepdims=True)
    h2 = x1 * lax.rsqrt(ms + NORM_EPS) * nfw_ref[...]
    h2 = h2 * (1.0 + mod_ref[0, 4:5, :]) + mod_ref[0, 3:4, :]
    h2_ref[0, rows, :] = h2
    yield

    hh, hl = _split(h2)
    logits = _dot(hh, wrh_ref[...]) + (_dot(hh, wrl_ref[...]) + _dot(hl, wrh_ref[...])) + br_ref[...]
    lane = lax.broadcasted_iota(jnp.int32, (tm, LANES), 1)
    lanef = lane.astype(F32)
    cur = jnp.where(lane < N_EXPERTS, logits, -jnp.inf)
    yield
    vals, idxs, hots = [], [], []
    for _ in range(TOP_K):
        mx = jnp.max(cur, axis=-1, keepdims=True)
        idx = jnp.min(jnp.where(cur == mx, lanef, float(LANES)), axis=-1, keepdims=True)
        hot = lanef == idx
        cur = jnp.where(hot, -jnp.inf, cur)
        vals.append(mx)
        idxs.append(idx)
        hots.append(hot)
    exps = [jnp.exp(vv - vals[0]) for vv in vals]
    den = exps[0] + exps[1] + exps[2] + exps[3]
    yield
    sel = jnp.zeros((tm, LANES), F32)
    for hot in hots:
        sel = sel + jnp.where(hot, 1.0, 0.0)
    ri = lax.broadcasted_iota(jnp.int32, (tm, tm), 0)
    ci = lax.broadcasted_iota(jnp.int32, (tm, tm), 1)
    below = jnp.where(ri > ci, 1.0, 0.0).astype(BF16)
    before = _dot(below, sel.astype(BF16))
    counts = jnp.broadcast_to(jnp.sum(sel, axis=0, keepdims=True), (8, LANES))
    er = lax.broadcasted_iota(jnp.int32, (LANES, LANES), 0)
    ec = lax.broadcasted_iota(jnp.int32, (LANES, LANES), 1)
    lower = jnp.where(er < ec, 1.0, 0.0).astype(BF16)
    start = _dot(counts.astype(BF16), lower)[0:1, :]
    route = jnp.zeros((tm, LANES), F32)
    for j in range(TOP_K):
        pos = jnp.sum(jnp.where(hots[j], before + start, 0.0), axis=-1, keepdims=True)
        route = jnp.where(lane == j, pos, route)
        route = jnp.where(lane == TOP_K + j, exps[j] / den, route)
    route_ref[0, rows, :] = route
    cnt_ref[0, part] = counts
    yield


def _post(x, mod3, y, r, k, v, g, o, gates, lnw, lnb, rk, bd, wor, woa, wout, nfw, wrh, wrl, br, tm):
    bsz, s, d = x.shape
    w = RWKV_WIDTH
    parts = tm // ROUTE_TILE
    tok = lambda n: pl.BlockSpec((1, tm, n), lambda b, i: (b, i, 0))
    row = lambda n: pl.BlockSpec((1, n), lambda b, i: (0, 0))
    full = lambda a: pl.BlockSpec(a.shape, lambda b, i: (0,) * a.ndim)
    return pl.pallas_call(
        _post_kernel,
        grid=(bsz, s // tm),
        in_specs=[tok(d), pl.BlockSpec((1, 6, d), lambda b, i: (b, 0, 0)),
                  tok(w), tok(w), tok(w), tok(w), tok(w),
                  tok(MLA_HEADS * MLA_V), tok(2 * d),
                  row(w), row(w), row(w), full(bd), full(wor), full(woa), full(wout), row(d),
                  full(wrh), full(wrl), row(LANES)],
        out_specs=[tok(d), tok(d), tok(LANES), pl.BlockSpec((1, parts, 8, LANES), lambda b, i: (b, i, 0, 0))],
        out_shape=[jax.ShapeDtypeStruct((bsz, s, d), F32),
                   jax.ShapeDtypeStruct((bsz, s, d), F32),
                   jax.ShapeDtypeStruct((bsz, s, LANES), F32),
                   jax.ShapeDtypeStruct((bsz, s // ROUTE_TILE, 8, LANES), F32)],
        compiler_params=_params("parallel", "parallel"),
        name="post",
    )(x, mod3, y, r, k, v, g, o, gates, lnw, lnb, rk, bd, wor, woa, wout, nfw, wrh, wrl, br)


def _load_rows(ref, n):
    nc = ref.shape[0] // n
    return jnp.concatenate([ref[pl.ds(c, n, stride=nc), :] for c in range(nc)], axis=1)


def _store_rows(ref, val):
    nc = val.shape[1] // LANES
    for c in range(nc):
        ref[pl.ds(c, val.shape[0], stride=nc), :] = val[:, c * LANES:(c + 1) * LANES]


def _run_copies(tile, n_ref, src_ref, dst_ref, tm, make_copy):
    nbits = tm.bit_length()
    for b in range(nbits):
        group = tile * nbits + b

        def piece(j, carry, group=group, size=1 << b):
            k = group * N_EXPERTS + j
            make_copy(src_ref[k], dst_ref[k], size).start()
            return carry

        lax.fori_loop(0, n_ref[group], piece, 0)


def _pieces(count, limit, fn):
    def emit(bits):
        for b in bits:
            size = 1 << b
            done = count & ~(2 * size - 1)

            @pl.when((count & size) != 0)
            def _():
                fn(done, size)

    bits = range(limit.bit_length() - 1, -1, -1)
    large = [b for b in bits if b >= RARE_PIECE_BIT]
    if large:
        @pl.when(count >= (1 << RARE_PIECE_BIT))
        def _():
            emit(large)
    emit([b for b in bits if b < RARE_PIECE_BIT])


def _zero_fill(gap_ref, xs_ref, zero_ref, sem, nc):
    bm = MOE_BLOCK
    zero_ref[...] = jnp.zeros_like(zero_ref)

    def piece(dst, size):
        return pltpu.make_async_copy(zero_ref.at[pl.ds(0, size * nc), :],
                                     xs_ref.at[pl.ds(pl.multiple_of(dst * nc, nc), size * nc), :], sem)

    def sweep(act):
        def per_expert(e, carry):
            g0 = gap_ref[e]
            _pieces(gap_ref[N_EXPERTS + e], bm, lambda done, size: act(piece(g0 + done, size)))
            return carry

        def tail(blk, carry):
            act(piece(gap_ref[2 * N_EXPERTS] + blk * bm, bm))
            return carry

        lax.fori_loop(0, N_EXPERTS, per_expert, 0)
        lax.fori_loop(0, gap_ref[2 * N_EXPERTS + 1], tail, 0)

    sweep(lambda cp: cp.start())
    sweep(lambda cp: cp.wait())


def _dispatch_kernel(cnt_ref, off_ref, dst_ref, gap_ref, h_ref, route_ref, xs_ref, sorted_ref, zero_ref,
                     sem, zero_sem):
    tm = h_ref.shape[0]

    @pl.when(pl.program_id(0) == 0)
    def _():
        _zero_fill(gap_ref, xs_ref, zero_ref, zero_sem, h_ref.shape[1] // LANES)

    pos_t = route_ref[...].T
    slot = lax.broadcasted_iota(jnp.int32, (TOP_K * tm, tm), 0).astype(F32)
    perm = jnp.where(slot == pos_t[0:1, :], 1.0, 0.0)
    for j in range(1, TOP_K):
        perm = perm + jnp.where(slot == pos_t[j:j + 1, :], 1.0, 0.0)
    i = pl.program_id(0)
    cur = i % 2
    buf = sorted_ref.at[cur]
    _store_rows(buf, _dot(perm.astype(BF16), h_ref[...].astype(BF16)))
    nc = h_ref.shape[1] // LANES

    def make_copy(src, dst, size):
        return pltpu.make_async_copy(buf.at[pl.ds(pl.multiple_of(src * nc, nc), size * nc), :],
                                     xs_ref.at[pl.ds(pl.multiple_of(dst * nc, nc), size * nc), :], sem.at[cur])

    _run_copies(i, cnt_ref, off_ref, dst_ref, tm, make_copy)

    def drain(which):
        pltpu.make_async_copy(sorted_ref.at[which], xs_ref.at[pl.ds(0, TOP_K * tm * nc), :], sem.at[which]).wait()

    @pl.when(i > 0)
    def _():
        drain(1 - cur)

    @pl.when(i == pl.num_programs(0) - 1)
    def _():
        drain(cur)


def _dispatch(cnt_tab, off_tab, dst_tab, gap_tab, h2, route, n_rows, tm):
    t, d = h2.shape
    nc = d // LANES
    return pl.pallas_call(
        _dispatch_kernel,
        grid_spec=pltpu.PrefetchScalarGridSpec(
            num_scalar_prefetch=4,
            grid=(t // tm,),
            in_specs=[pl.BlockSpec((tm, d), lambda i, *_: (i, 0)),
                      pl.BlockSpec((tm, LANES), lambda i, *_: (i, 0))],
            out_specs=pl.BlockSpec(memory_space=pl.ANY),
            scratch_shapes=[pltpu.VMEM((2, TOP_K * tm * nc, LANES), F32), pltpu.VMEM((MOE_BLOCK * nc, LANES), F32),
                            pltpu.SemaphoreType.DMA((2,)), pltpu.SemaphoreType.DMA(())]),
        out_shape=jax.ShapeDtypeStruct((n_rows * nc, LANES), F32),
        compiler_params=_params("arbitrary"),
        name="dispatch",
    )(cnt_tab, off_tab, dst_tab, gap_tab, h2, route)


def _split_gate_up(w_ref, g_ref, u_ref, t_ref):
    half = LANES // 2
    nblk = w_ref.shape[1] // LANES
    for c in range(w_ref.shape[2] // LANES):
        w_t = w_ref[0, :, c * LANES:(c + 1) * LANES].T
        for b in range(nblk):
            t_ref[b] = w_t[:, b * LANES:(b + 1) * LANES]
        for out_ref, first in ((g_ref, 0), (u_ref, 1)):
            rows = [t_ref[b, pl.ds(first, half, stride=2), :] for b in range(nblk)]
            out_ref[c * half:(c + 1) * half, :] = jnp.concatenate(rows, axis=1).astype(BF16)


def _moe_kernel(blk_e_ref, valid_ref, xs_ref, wgu_ref, bg_ref, bu_ref, wdn_ref, bd_ref, ys_ref,
                wg_ref, wu_ref, wd_ref, t_ref):
    i = pl.program_id(0)
    valid = valid_ref[i]

    @pl.when(jnp.logical_or(i == 0, blk_e_ref[i] != blk_e_ref[jnp.maximum(i - 1, 0)]))
    def _():
        _split_gate_up(wgu_ref, wg_ref, wu_ref, t_ref)
        wd_ref[...] = wdn_ref[0].astype(BF16)

    @pl.when(valid > 0)
    def _():
        x = _load_rows(xs_ref, MOE_BLOCK).astype(BF16)
        gate = _dot_nt(x, wg_ref[...]) + bg_ref[0]
        up = _dot_nt(x, wu_ref[...]) + bu_ref[0]
        gate = jnp.minimum(gate, SWIGLU_LIMIT)
        up = jnp.clip(up, -SWIGLU_LIMIT, SWIGLU_LIMIT)
        act = (up + 1.0) * gate * jax.nn.sigmoid(SWIGLU_ALPHA * gate)
        _store_rows(ys_ref, _dot(act.astype(BF16), wd_ref[...]) + bd_ref[0])

    @pl.when(valid == 0)
    def _():
        ys_ref[...] = jnp.zeros_like(ys_ref)


def _moe(blk_e, blk_valid, xs, wgu, bg, bu, wd, bd):
    ff, d = wd.shape[1], wd.shape[2]
    nc = d // LANES
    n_rows = xs.shape[0] // nc
    bm = MOE_BLOCK
    wspec = lambda k, n: pl.BlockSpec((1, k, n), lambda i, be, nu: (be[i], 0, 0))
    return pl.pallas_call(
        _moe_kernel,
        grid_spec=pltpu.PrefetchScalarGridSpec(
            num_scalar_prefetch=2,
            grid=(n_rows // bm,),
            in_specs=[pl.BlockSpec((bm * nc, LANES), lambda i, be, nu: (i, 0)),
                      wspec(d, 2 * ff), wspec(1, ff), wspec(1, ff), wspec(ff, d), wspec(1, d)],
            out_specs=pl.BlockSpec((bm * nc, LANES), lambda i, be, nu: (i, 0)),
            scratch_shapes=[pltpu.VMEM((ff, d), BF16), pltpu.VMEM((ff, d), BF16), pltpu.VMEM((ff, d), BF16),
                            pltpu.VMEM((d // LANES, LANES, LANES), F32)]),
        out_shape=jax.ShapeDtypeStruct((n_rows * nc, LANES), F32),
        compiler_params=_params("arbitrary"),
        name="moe",
    )(blk_e, blk_valid, xs, wgu, bg, bu, wd, bd)


def _combine_kernel(cnt_ref, off_ref, dst_ref, x1_ref, route_ref, mod_ref, ys_ref, o_ref, rows_ref, sem):
    tm = x1_ref.shape[1]
    tile = pl.program_id(0) * pl.num_programs(1) + pl.program_id(1)
    n_tiles = pl.num_programs(0) * pl.num_programs(1)
    nc = x1_ref.shape[2] // LANES
    cur = tile % 2

    def fetch(which_tile):
        which = which_tile % 2

        def make_copy(dst, src, size):
            return pltpu.make_async_copy(
                ys_ref.at[pl.ds(pl.multiple_of(src * nc, nc), size * nc), :],
                rows_ref.at[which, pl.ds(pl.multiple_of(dst * nc, nc), size * nc), :], sem.at[which])

        _run_copies(which_tile, cnt_ref, off_ref, dst_ref, tm, make_copy)

    @pl.when(tile == 0)
    def _():
        fetch(tile)

    @pl.when(tile + 1 < n_tiles)
    def _():
        fetch(tile + 1)

    route = route_ref[0]
    slot = lax.broadcasted_iota(jnp.int32, (tm, TOP_K * tm), 1).astype(F32)
    mix = jnp.where(slot == route[:, 0:1], route[:, TOP_K:TOP_K + 1], 0.0)
    for j in range(1, TOP_K):
        mix = mix + jnp.where(slot == route[:, j:j + 1], route[:, TOP_K + j:TOP_K + j + 1], 0.0)
    pltpu.make_async_copy(ys_ref.at[pl.ds(0, TOP_K * tm * nc), :], rows_ref.at[cur], sem.at[cur]).wait()
    acc = _dot(mix.astype(BF16), _load_rows(rows_ref.at[cur], TOP_K * tm).astype(BF16))
    o_ref[0] = x1_ref[0] + mod_ref[0, 5:6, :] * acc


def _combine(cnt_tab, off_tab, dst_tab, x1, route, mod3, ys, tm):
    bsz, s, d = x1.shape
    return pl.pallas_call(
        _combine_kernel,
        grid_spec=pltpu.PrefetchScalarGridSpec(
            num_scalar_prefetch=3,
            grid=(bsz, s // tm),
            in_specs=[pl.BlockSpec((1, tm, d), lambda b, i, *_: (b, i, 0)),
                      pl.BlockSpec((1, tm, LANES), lambda b, i, *_: (b, i, 0)),
                      pl.BlockSpec((1, 6, d), lambda b, i, *_: (b, 0, 0)),
                      pl.BlockSpec(memory_space=pl.ANY)],
            out_specs=pl.BlockSpec((1, tm, d), lambda b, i, *_: (b, i, 0)),
            scratch_shapes=[pltpu.VMEM((2, TOP_K * tm * d // LANES, LANES), F32), pltpu.SemaphoreType.DMA((2,))]),
        out_shape=jax.ShapeDtypeStruct((bsz, s, d), F32),
        compiler_params=_params("arbitrary", "arbitrary"),
        name="combine",
    )(cnt_tab, off_tab, dst_tab, x1, route, mod3, ys)


def _piece_tables(cnt_tab, off_tab, dst_tab, tm):
    nbits = tm.bit_length()
    bits = jnp.arange(nbits, dtype=jnp.int32)[None, :, None]
    cnt = cnt_tab[:, None, :]
    active = (cnt >> bits) & 1
    done = cnt & ~((2 << bits) - 1)
    rank = jnp.cumsum(active, axis=-1) - active
    slot = jnp.arange(N_EXPERTS, dtype=jnp.int32)[None, None, :, None]
    place = (active[:, :, None, :] == 1) & (rank[:, :, None, :] == slot)
    compact = lambda v: jnp.sum(jnp.where(place, (v[:, None, :] + done)[:, :, None, :], 0), axis=-1)
    flat = lambda v: v.reshape(-1).astype(jnp.int32)
    return flat(jnp.sum(active, axis=-1)), flat(compact(off_tab)), flat(compact(dst_tab))


def _pad_cols(a, n):
    return jnp.pad(a, ((0, 0), (0, n - a.shape[1])))


def _head_blocks(cols_main, cols_rot=None):
    k = cols_main.shape[0]
    out = jnp.zeros((k, MLA_HEADS, HEAD_PAD), F32)
    out = out.at[:, :, :cols_main.shape[2]].set(cols_main)
    return out.reshape(k, MLA_HEADS * HEAD_PAD)


def _layer(x, cond_mod, positions, w_in, q_a_norm_w, w_q_up, kv_a_norm_w, w_kv_up, q_norm_w, k_norm_w,
           w_o_mla, rwkv_mu, rwkv_w0, rwkv_w2, rwkv_a0, rwkv_a2, rwkv_g2, rwkv_k_k, rwkv_k_a, rwkv_r_k,
           rwkv_ln_w, rwkv_ln_b, rwkv_w_o, w_out, norm_mix_w, norm_ffn_w, w_router, b_router,
           w_gate_up, b_gate_up, w_down, b_down):
    bsz, s, d = x.shape
    t = bsz * s
    half = MLA_ROPE // 2
    nope, qk = MLA_NOPE, MLA_QK
    mod3 = cond_mod.reshape(bsz, 6, d)

    o_q, o_kv, o_kr = 0, Q_LORA, Q_LORA + KV_LORA
    o_slab = o_kr + MLA_ROPE
    o_gate = o_slab + RWKV_SLAB
    kr_w = w_in[:, o_kr:o_slab]
    zeros = lambda n: jnp.zeros((d, n), F32)
    kr_blk = jnp.concatenate([zeros(nope), kr_w, zeros(HEAD_PAD - qk)], axis=1)
    kr_rot = jnp.concatenate([zeros(nope), -kr_w[:, half:], kr_w[:, :half], zeros(HEAD_PAD - qk)], axis=1)
    w_in_p = jnp.concatenate([w_in[:, o_q:o_kr], kr_blk, kr_rot, w_in[:, o_slab:]], axis=1).astype(BF16)

    tm = min(256, s)

    wq3 = w_q_up.reshape(Q_LORA, MLA_HEADS, qk)
    wq_rot = jnp.concatenate([jnp.zeros((Q_LORA, MLA_HEADS, nope), F32), -wq3[:, :, nope + half:],
                              wq3[:, :, nope:nope + half]], axis=2)
    wq = jnp.concatenate([_head_blocks(wq3), _head_blocks(wq_rot)], axis=1).astype(BF16)
    wkv3 = w_kv_up.reshape(KV_LORA, MLA_HEADS, nope + MLA_V)
    wkv = jnp.concatenate([_head_blocks(wkv3[:, :, :nope]), _head_blocks(wkv3[:, :, nope:])], axis=1).astype(BF16)

    def gains(wn):
        main = jnp.pad(wn, (0, HEAD_PAD - qk)).reshape(1, HEAD_PAD)
        rot = jnp.concatenate([jnp.zeros((nope,), F32), wn[nope + half:], wn[nope:nope + half],
                               jnp.zeros((HEAD_PAD - qk,), F32)]).reshape(1, HEAD_PAD)
        return main, rot

    qg, qgs = gains(q_norm_w)
    kg, kgs = gains(k_norm_w)
    inv_freq = ROPE_THETA ** (-jnp.arange(half, dtype=F32) / half)
    per_row = LANES // half
    pos_rep = jnp.repeat(positions.astype(F32).reshape(t // per_row, per_row), half, axis=1)
    cos16, sin16 = _rope_table(pos_rep, jnp.tile(inv_freq, per_row).reshape(1, LANES))
    cos16 = cos16.reshape(bsz, s, half)
    sin16 = sin16.reshape(bsz, s, half)
    cosf = jnp.concatenate([jnp.ones((bsz, s, nope), F32), cos16, cos16,
                            jnp.ones((bsz, s, HEAD_PAD - qk), F32)], axis=-1)
    sinf = jnp.concatenate([jnp.zeros((bsz, s, nope), F32), sin16, sin16,
                            jnp.zeros((bsz, s, HEAD_PAD - qk), F32)], axis=-1)

    w = RWKV_WIDTH
    wwa = jnp.zeros((LANES, 2 * w), F32)
    wwa = wwa.at[:DECAY_LORA, :w].set(rwkv_w2).at[DECAY_LORA:, w:].set(rwkv_a2)
    wwh = wwa.astype(BF16)
    wwl = (wwa - wwh.astype(F32)).astype(BF16)
    wws = jnp.concatenate([wwh, wwl, wwh], axis=0)
    hid = np.arange(w) // RWKV_HEAD
    bd = jnp.asarray(hid[:, None] == hid[None, :], BF16)
    gates, q, k, vt, r_, lw_, k_, v_, a_, b_, g_ = _front(
        x, mod3, norm_mix_w, w_in_p, cosf, sinf, q_a_norm_w.reshape(1, -1), kv_a_norm_w.reshape(1, -1),
        wq, wkv, qg, qgs, kg, kgs, rwkv_mu.reshape(1, -1), wws, rwkv_w0.reshape(1, -1),
        rwkv_a0.reshape(1, -1), rwkv_g2.astype(BF16), rwkv_k_k.reshape(1, -1), rwkv_k_a.reshape(1, -1), bd,
        tm, min(ATTN_KEY_BLOCK, s))
    o = _attention(q, k, vt, min(ATTN_BLOCK, s))
    y = _wkv(r_, lw_, k_, v_, a_, b_, WKV_BATCH if bsz % WKV_BATCH == 0 else 1)

    woa = w_o_mla.astype(BF16)
    wr =_pad_cols(w_router, LANES)
    wrh = wr.astype(BF16)
    wrl = (wr - wrh.astype(F32)).astype(BF16)
    br = jnp.pad(b_router, (0, LANES - N_EXPERTS)).reshape(1, LANES)
    x1, h2, route, counts = _post(
        x, mod3, y, r_, k_, v_, g_, o, gates, rwkv_ln_w.reshape(1, -1), rwkv_ln_b.reshape(1, -1),
        rwkv_r_k.reshape(1, -1), bd, rwkv_w_o.astype(BF16), woa, w_out.astype(BF16),
        norm_ffn_w.reshape(1, -1), wrh, wrl, br, POST_TILE if s % POST_TILE == 0 else ROUTE_TILE)

    bm = MOE_BLOCK
    n_rows = t * TOP_K + N_EXPERTS * bm
    n_blocks = n_rows // bm
    cnt_tab = counts[:, :, 0, :N_EXPERTS].reshape(t // ROUTE_TILE, N_EXPERTS).astype(jnp.int32)
    total = jnp.sum(cnt_tab, axis=0)
    padded = (total + bm - 1) // bm * bm
    pad_end = jnp.cumsum(padded)
    pad_start = pad_end - padded
    off_tab = jnp.cumsum(cnt_tab, axis=1) - cnt_tab
    dst_tab = pad_start[None, :] + jnp.cumsum(cnt_tab, axis=0) - cnt_tab
    blk_start = jnp.arange(n_blocks, dtype=jnp.int32) * bm
    blk_e = jnp.minimum(jnp.sum((pad_end[None, :] <= blk_start[:, None]).astype(jnp.int32), axis=1),
                        N_EXPERTS - 1)
    blk_valid = jnp.clip((pad_start + total)[blk_e] - blk_start, 0, bm).astype(jnp.int32)
    tabs = _piece_tables(cnt_tab, off_tab, dst_tab, ROUTE_TILE)
    gap_tab = jnp.concatenate([pad_start + total, padded - total, pad_end[-1:],
                               (n_rows - pad_end[-1:]) // bm]).astype(jnp.int32)

    xs = _dispatch(*tabs, gap_tab, h2.reshape(t, d), route.reshape(t, LANES), n_rows, ROUTE_TILE)
    bgu = b_gate_up.reshape(N_EXPERTS, 1, D_FF, 2)
    ys = _moe(blk_e, blk_valid, xs, w_gate_up, bgu[..., 0], bgu[..., 1], w_down,
              b_down.reshape(N_EXPERTS, 1, d))
    return _combine(*tabs, x1, route, mod3, ys, ROUTE_TILE)


def kernel(x, c, positions, ada_w, ada_b, norm_mix_w, norm_ffn_w, w_in, q_a_norm_w, w_q_up, kv_a_norm_w, w_kv_up, q_norm_w, k_norm_w, w_o_mla, rwkv_mu, rwkv_w0, rwkv_w2, rwkv_a0, rwkv_a2, rwkv_g2, rwkv_k_k, rwkv_k_a, rwkv_r_k, rwkv_ln_w, rwkv_ln_b, rwkv_w_o, w_out, w_router, b_router, w_gate_up, b_gate_up, w_down, b_down):
    depth = ada_w.shape[0]
    for l in range(depth):
        mod = _ada(c, ada_w[l], ada_b[l])
        x = _layer(x, mod, positions, w_in[l], q_a_norm_w[l], w_q_up[l], kv_a_norm_w[l], w_kv_up[l],
                   q_norm_w[l], k_norm_w[l], w_o_mla[l], rwkv_mu[l], rwkv_w0[l], rwkv_w2[l], rwkv_a0[l],
                   rwkv_a2[l], rwkv_g2[l], rwkv_k_k[l], rwkv_k_a[l], rwkv_r_k[l], rwkv_ln_w[l],
                   rwkv_ln_b[l], rwkv_w_o[l], w_out[l], norm_mix_w[l], norm_ffn_w[l], w_router[l],
                   b_router[l], w_gate_up[l], b_gate_up[l], w_down[l], b_down[l])
    return x
```

```python
import functools

import numpy as np
import jax
import jax.numpy as jnp
from jax import lax
from jax.experimental import pallas as pl
from jax.experimental.pallas import tpu as pltpu

F32 = jnp.float32
BF16 = jnp.bfloat16

D_MODEL = 1024
MLA_HEADS = 8
MLA_NOPE = 64
MLA_ROPE = 32
MLA_QK = MLA_NOPE + MLA_ROPE
MLA_V = 64
VT_ROWS = 80
Q_LORA = 256
KV_LORA = 128
ROPE_THETA = 10000.0
RWKV_HEADS = 8
RWKV_HEAD = 64
RWKV_WIDTH = RWKV_HEADS * RWKV_HEAD
DECAY_LORA = 64
AAA_LORA = 64
GATE_LORA = 128
RWKV_SLAB = 3 * RWKV_WIDTH + DECAY_LORA + AAA_LORA + GATE_LORA
GN_EPS = 64e-5
N_EXPERTS = 32
TOP_K = 4
D_FF = D_MODEL
SWIGLU_LIMIT = 7.0
SWIGLU_ALPHA = 1.702
NORM_EPS = 1e-6
LOG2_E = 1.4426950408889634

LANES = 128
HEAD_PAD = 128
MLA_COLS = 640
WKV_CHUNK = 64
WKV_BATCH = 8
ATTN_BLOCK = 512
ATTN_KEY_BLOCK = 512
ATTN_HEADS = 8
ROUTE_TILE = 256
POST_TILE = 512
RARE_PIECE_BIT = 6
MOE_BLOCK = 512
VMEM_LIMIT = 56 * 1024 * 1024


def _dot(a, b):
    return jnp.dot(a, b, preferred_element_type=F32)


def _dot_nt(a, b):
    return lax.dot_general(a, b, (((1,), (1,)), ((), ())), preferred_element_type=F32)


def _split(x):
    hi = x.astype(BF16)
    lo = (x - hi.astype(F32)).astype(BF16)
    return hi, lo


def _mm3(a, b):
    ah, al = _split(a)
    bh, bl = _split(b)
    return _dot(ah, bh) + (_dot(ah, bl) + _dot(al, bh))


def _trace_in_turn(stages):
    live = list(stages)
    while live:
        for gen in list(live):
            try:
                next(gen)
            except StopIteration:
                live.remove(gen)


def _params(*sem):
    return pltpu.CompilerParams(dimension_semantics=sem, vmem_limit_bytes=VMEM_LIMIT)


def _ada_kernel(c_ref, w_ref, b_ref, o_ref):
    c = c_ref[...]
    cond = c * jax.nn.sigmoid(c)
    o_ref[...] = _mm3(cond, w_ref[...]) + b_ref[...]


def _ada(c, w, b):
    bsz, d = c.shape
    n = w.shape[1]
    tn = 1024
    return pl.pallas_call(
        _ada_kernel,
        grid=(n // tn,),
        in_specs=[pl.BlockSpec((bsz, d), lambda j: (0, 0)),
                  pl.BlockSpec((d, tn), lambda j: (0, j)),
                  pl.BlockSpec((1, tn), lambda j: (0, j))],
        out_specs=pl.BlockSpec((bsz, tn), lambda j: (0, j)),
        out_shape=jax.ShapeDtypeStruct((bsz, n), F32),
        compiler_params=_params("parallel"),
        name="ada",
    )(c, w, b.reshape(1, n))


def _front_kernel(x_ref, mod_ref, nw_ref, w_ref,
                  cos_ref, sin_ref, qan_ref, kvn_ref, wq_ref, wkv_ref, qg_ref, qgs_ref, kg_ref, kgs_ref,
                  mu_ref, wws_ref, w0_ref, a0_ref, g2_ref, kk_ref, ka_ref, bd_ref,
                  gates_ref, q_out, k_out, vt_out, r_out, lw_out, k2_out, v_out, a_out, b_out, g_out,
                  last_ref):
    x = x_ref[0]
    ms = jnp.mean(x * x, axis=-1, keepdims=True)
    y = x * lax.rsqrt(ms + NORM_EPS) * nw_ref[...]
    h = y * (1.0 + mod_ref[0, 1:2, :]) + mod_ref[0, 0:1, :]
    hb = h.astype(BF16)
    mla = _dot(hb, w_ref[:, 0:MLA_COLS])
    slab = _dot(hb, w_ref[:, MLA_COLS:MLA_COLS + RWKV_SLAB])
    gates_ref[0] = _dot(hb, w_ref[:, MLA_COLS + RWKV_SLAB:]).astype(BF16)
    mla_prep = _mla_prep_body(mla, cos_ref, sin_ref, qan_ref, kvn_ref, wq_ref, wkv_ref,
                              qg_ref, qgs_ref, kg_ref, kgs_ref, q_out, k_out, vt_out)
    rwkv_prep = _rwkv_prep_body(slab, pl.program_id(1) == 0, mu_ref, wws_ref, w0_ref, a0_ref, g2_ref,
                                kk_ref, ka_ref, bd_ref, r_out, lw_out, k2_out, v_out, a_out, b_out, g_out,
                                last_ref)
    _trace_in_turn([mla_prep, rwkv_prep])


def _front(x, mod3, norm_w, w_in_p, cosf, sinf, qan, kvn, wq, wkv, qg, qgs, kg, kgs,
           mu, wws, w0, a0, g2, k_k, k_a, bd, tm, tk):
    bsz, s, d = x.shape
    w = RWKV_WIDTH
    tok = lambda n: pl.BlockSpec((1, tm, n), lambda b, i: (b, i, 0))
    row = lambda n: pl.BlockSpec((1, n), lambda b, i: (0, 0))
    full = lambda a: pl.BlockSpec(a.shape, lambda b, i: (0,) * a.ndim)
    head_spec = pl.BlockSpec((1, MLA_HEADS, tm, HEAD_PAD), lambda b, i: (b, 0, i, 0))
    head_sds = jax.ShapeDtypeStruct((bsz, MLA_HEADS, s, HEAD_PAD), BF16)
    if tk >= tm:
        per = tk // tm
        vt_spec = pl.BlockSpec((1, MLA_HEADS, 1, VT_ROWS, tm), lambda b, i: (b, 0, i // per, 0, i % per))
    else:
        vt_spec = pl.BlockSpec((1, MLA_HEADS, tm // tk, VT_ROWS, tk), lambda b, i: (b, 0, i, 0, 0))
    vt_sds = jax.ShapeDtypeStruct((bsz, MLA_HEADS, s // tk, VT_ROWS, tk), BF16)
    stream_sds = jax.ShapeDtypeStruct((bsz, s, w), F32)
    return pl.pallas_call(
        _front_kernel,
        grid=(bsz, s // tm),
        in_specs=[tok(d), pl.BlockSpec((1, 6, d), lambda b, i: (b, 0, 0)), row(d), full(w_in_p),
                  tok(LANES), tok(LANES), row(Q_LORA), row(KV_LORA), full(wq), full(wkv),
                  row(LANES), row(LANES), row(LANES), row(LANES),
                  row(RWKV_SLAB), full(wws), row(w), row(w), full(g2), row(w), row(w), full(bd)],
        out_specs=[tok(2 * d), head_spec, head_spec, vt_spec] + [tok(w)] * 7,
        out_shape=[jax.ShapeDtypeStruct((bsz, s, 2 * d), BF16), head_sds, head_sds, vt_sds] + [stream_sds] * 7,
        scratch_shapes=[pltpu.VMEM((8, RWKV_SLAB), F32)],
        compiler_params=_params("parallel", "arbitrary"),
        name="front",
    )(x, mod3, norm_w.reshape(1, d), w_in_p, cosf, sinf, qan, kvn, wq, wkv, qg, qgs, kg, kgs,
      mu, wws, w0, a0, g2, k_k, k_a, bd)


def _rope_kernel(pos_ref, invf_ref, cos_ref, sin_ref):
    ang = pos_ref[...] * invf_ref[...]
    cos_ref[...] = jnp.cos(ang)
    sin_ref[...] = jnp.sin(ang)


def _rope_table(pos_rep, invf_row):
    n = pos_rep.shape[0]
    tr = min(512, n)
    spec = pl.BlockSpec((tr, LANES), lambda i: (i, 0))
    sds = jax.ShapeDtypeStruct((n, LANES), F32)
    return pl.pallas_call(
        _rope_kernel,
        grid=(n // tr,),
        in_specs=[spec, pl.BlockSpec((1, LANES), lambda i: (0, 0))],
        out_specs=[spec, spec],
        out_shape=[sds, sds],
        compiler_params=_params("parallel"),
        name="rope",
    )(pos_rep, invf_row)


def _mla_prep_body(m, cos_ref, sin_ref, qan_ref, kvn_ref, wq_ref, wkv_ref,
                   qg_ref, qgs_ref, kg_ref, kgs_ref, q_out, k_out, vt_out):
    cq = m[:, 0:Q_LORA]
    ckv = m[:, Q_LORA:Q_LORA + KV_LORA]
    kr = m[:, 384:512]
    krs = m[:, 512:640]
    ql = cq * lax.rsqrt(jnp.mean(cq * cq, axis=-1, keepdims=True) + NORM_EPS) * qan_ref[...]
    kvl = ckv * lax.rsqrt(jnp.mean(ckv * ckv, axis=-1, keepdims=True) + NORM_EPS) * kvn_ref[...]
    qall = _dot(ql.astype(BF16), wq_ref[...])
    kvall = _dot(kvl.astype(BF16), wkv_ref[...])
    cosf = cos_ref[0]
    sinf = sin_ref[0]
    scale = MLA_QK ** -0.5 * LOG2_E
    hw = MLA_HEADS * HEAD_PAD
    for h in range(MLA_HEADS):
        lo, hi = h * HEAD_PAD, (h + 1) * HEAD_PAD
        qh = qall[:, lo:hi]
        qs = qall[:, hw + lo:hw + hi]
        rs = lax.rsqrt(jnp.sum(qh * qh, axis=-1, keepdims=True) * (1.0 / MLA_QK) + NORM_EPS)
        qo = (qh * rs * qg_ref[...]) * cosf + (qs * rs * qgs_ref[...]) * sinf
        q_out[0, h] = (qo * scale).astype(BF16)
        kh = kvall[:, lo:hi] + kr
        rs = lax.rsqrt(jnp.sum(kh * kh, axis=-1, keepdims=True) * (1.0 / MLA_QK) + NORM_EPS)
        ko = (kh * rs * kg_ref[...]) * cosf + (krs * rs * kgs_ref[...]) * sinf
        k_out[0, h] = ko.astype(BF16)
        extra = lax.broadcasted_iota(jnp.int32, (VT_ROWS - MLA_V, m.shape[0]), 0)
        v_t = jnp.concatenate([kvall[:, hw + lo:hw + hi].T[0:MLA_V, :],
                               jnp.where(extra == 0, 1.0, 0.0)], axis=0).astype(BF16)
        width = vt_out.shape[4]
        for c in range(vt_out.shape[2]):
            vt_out[0, h, c] = v_t[:, c * width:(c + 1) * width]
        yield


def _attn_kernel(q_ref, k_ref, vt_ref, o_ref, m_ref, acc_ref, *, tq):
    qi = pl.program_id(2)
    heads = range(q_ref.shape[1])
    qs = [q_ref[0, h] for h in heads]
    m_ref[...] = jnp.full_like(m_ref, -jnp.inf)
    acc_ref[...] = jnp.zeros_like(acc_ref)

    tk = vt_ref.shape[4]
    per = tq // tk

    def step(j, causal=False, k0=0, nk=tk, q0=0, nq=tq, kbase=0):
        off = pl.multiple_of(j * tk, tk)
        sts = [_dot_nt(k_ref[0, h, pl.ds(off + k0, nk), :], qs[h][q0:q0 + nq]) for h in heads]
        if causal:
            keys = lax.broadcasted_iota(jnp.int32, (nk, nq), 0) + (k0 + kbase)
            queries = lax.broadcasted_iota(jnp.int32, (nk, nq), 1) + q0
        for h in heads:
            st = jnp.where(keys <= queries, sts[h], -jnp.inf) if causal else sts[h]
            m = m_ref[h, :, q0:q0 + nq]
            m_new = jnp.maximum(m, jnp.max(st, axis=0, keepdims=True))
            p = jnp.exp2(st - m_new)
            alpha = jnp.exp2(m - m_new)
            m_ref[h, :, q0:q0 + nq] = m_new
            acc_ref[h, :, q0:q0 + nq] = (alpha * acc_ref[h, :, q0:q0 + nq]
                                         + _dot(vt_ref[0, h, j, :, k0:k0 + nk], p.astype(BF16)))

    def body(j, carry):
        step(j)
        return carry

    lax.fori_loop(0, qi * per, body, 0)
    if per == 1:
        step(qi, True, 0, tk // 2)
        step(qi, True, tk // 2, tk // 2, tq // 2, tq // 2)
    else:
        for part in range(per):
            step(qi * per + part, True, 0, tk, part * tk, tq - part * tk, part * tk)
    out = lambda h: acc_ref[h, 0:MLA_V, :] / acc_ref[h, MLA_V:MLA_V + 1, :]
    for h in heads[::2]:
        o_ref[0, :, h * MLA_V:(h + 2) * MLA_V] = jnp.concatenate([out(h), out(h + 1)], axis=0).T.astype(BF16)


def _attention(q, k, vt, tq):
    bsz, nh, s, dh = q.shape
    dv, tk = vt.shape[3], vt.shape[4]
    hp = ATTN_HEADS
    return pl.pallas_call(
        functools.partial(_attn_kernel, tq=tq),
        grid=(bsz, nh // hp, s // tq),
        in_specs=[pl.BlockSpec((1, hp, tq, dh), lambda b, h, i: (b, h, i, 0)),
                  pl.BlockSpec((1, hp, s, dh), lambda b, h, i: (b, h, 0, 0)),
                  pl.BlockSpec((1, hp, s // tk, dv, tk), lambda b, h, i: (b, h, 0, 0, 0))],
        out_specs=pl.BlockSpec((1, tq, hp * MLA_V), lambda b, h, i: (b, i, h)),
        out_shape=jax.ShapeDtypeStruct((bsz, s, nh * MLA_V), BF16),
        scratch_shapes=[pltpu.VMEM((hp, 1, tq), F32), pltpu.VMEM((hp, dv, tq), F32)],
        compiler_params=_params("parallel", "parallel", "parallel"),
        name="attn",
    )(q, k, vt)


def _softplus(x):
    return jnp.maximum(x, 0.0) + jnp.log(1.0 + jnp.exp(-jnp.abs(x)))


def _rwkv_prep_body(slab, first_tile, mu_ref, wws_ref, w0_ref, a0_ref, g2_ref, kk_ref, ka_ref, bd_ref,
                    r_out, lw_out, k_out, v_out, a_out, b_out, g_out, last_ref):
    tm = slab.shape[0]
    w = RWKV_WIDTH
    rolled = pltpu.roll(slab, 1, 0)
    rowi = lax.broadcasted_iota(jnp.int32, slab.shape, 0)
    carried = jnp.where(first_tile, 0.0, last_ref[0:1, :])
    prev = jnp.where(rowi == 0, carried, rolled)
    last_ref[0:1, :] = slab[tm - 1:tm, :]
    p = slab + (prev - slab) * mu_ref[...]
    r = p[:, 0:w]
    k = p[:, w:2 * w]
    v = p[:, 2 * w:3 * w]
    r_out[0] = r
    v_out[0] = v
    yield
    wa = p[:, 3 * w:3 * w + LANES]
    g_lo = p[:, 3 * w + LANES:3 * w + 2 * LANES]
    lane = lax.broadcasted_iota(jnp.int32, wa.shape, 1)
    wa = jnp.where(lane < DECAY_LORA, jnp.tanh(wa), wa)
    wah, wal = _split(wa)
    wa_o = _dot(jnp.concatenate([wah, wah, wal], axis=1), wws_ref[...])
    g_out[0] = _dot(jax.nn.sigmoid(g_lo).astype(BF16), g2_ref[...])
    yield
    log_w = -_softplus(-(w0_ref[...] + wa_o[:, 0:w])) - 0.5
    lw_out[0] = -jnp.exp(log_w)
    yield
    a = jax.nn.sigmoid(a0_ref[...] + wa_o[:, w:2 * w])
    k_out[0] = k * (1.0 + (a - 1.0) * ka_ref[...])
    yield
    kk = k * kk_ref[...]
    ss = _dot((kk * kk).astype(BF16), bd_ref[...])
    kk = kk / jnp.maximum(jnp.sqrt(ss), 1e-12)
    a_out[0] = -kk
    b_out[0] = kk * a
    yield


def _wkv_kernel(r_ref, lw_ref, k_ref, v_ref, a_ref, b_ref, y_ref, s_ref):
    c = WKV_CHUNK
    n = pl.program_id(1)

    @pl.when(n == 0)
    def _():
        s_ref[...] = jnp.zeros_like(s_ref)

    row = lax.broadcasted_iota(jnp.int32, (LANES, LANES), 0)
    col = lax.broadcasted_iota(jnp.int32, (LANES, LANES), 1)
    lower_strict = row > col
    lower_incl = row >= col
    eye = row == col
    same16 = (row >> 4) == (col >> 4)
    same32 = (row >> 5) == (col >> 5)
    ident = jnp.where(eye, 1.0, 0.0).astype(F32)
    tr = lax.broadcasted_iota(jnp.int32, (c, c), 0)
    tc = lax.broadcasted_iota(jnp.int32, (c, c), 1)
    tri = jnp.where(tr >= tc, 1.0, 0.0).astype(BF16)
    first = lax.broadcasted_iota(jnp.int32, (c, LANES), 1) < RWKV_HEAD

    def stack(x):
        return jnp.concatenate([jnp.where(first, x, 0.0), jnp.where(first, 0.0, x)], axis=0)

    def mm(x, y):
        return _dot(x.astype(BF16), y.astype(BF16))

    nb = r_ref.shape[0]
    probs = [(bi, p) for bi in range(nb) for p in range(RWKV_HEADS // 2)]
    each = lambda f, *xs: [f(*args) for args in zip(*xs)]
    load = lambda ref: [ref[bi, :, p * LANES:(p + 1) * LANES] for bi, p in probs]
    r, lw, k, v, a, b = (load(ref) for ref in (r_ref, lw_ref, k_ref, v_ref, a_ref, b_ref))

    def cumsum(x):
        hi = x.astype(BF16)
        rem = x - hi.astype(F32)
        mid = rem.astype(BF16)
        lo = (rem - mid.astype(F32)).astype(BF16)
        return _dot(tri, hi) + (_dot(tri, mid) + _dot(tri, lo))

    cum = each(cumsum, lw)
    cum_c = [x[c - 1:c, :] for x in cum]
    e_in = each(jnp.exp, cum)
    e_neg = each(lambda x: jnp.exp(-x), cum)
    e_end = each(lambda x, xc: jnp.exp(xc - x), cum, cum_c)
    a_s = each(lambda x, cu, l: stack(x * jnp.exp(cu - l)).astype(BF16), a, cum, lw)
    r_s = each(lambda x, e: stack(x * e), r, e_in)
    b_s = each(lambda x, e: stack(x * e).astype(BF16), b, e_neg)
    k_s = each(lambda x, e: stack(x * e).astype(BF16), k, e_neg)
    b_e = each(lambda x, e: stack(x * e).T.astype(BF16), b, e_end)
    k_e = each(lambda x, e: stack(x * e).T.astype(BF16), k, e_end)
    v_s = each(lambda x: stack(x).astype(BF16), v)

    tt = each(lambda x1, x2, y1, y2: _dot_nt(jnp.concatenate([x1, x2.astype(BF16)], axis=0),
                                             jnp.concatenate([y1, y2], axis=0)), a_s, r_s, b_s, k_s)
    d_ab = [jnp.where(lower_strict, x[0:LANES, 0:LANES], 0.0) for x in tt]
    e_ak = [jnp.where(lower_strict, x[0:LANES, LANES:], 0.0).astype(BF16) for x in tt]
    f_rb = [jnp.where(lower_incl, x[LANES:, 0:LANES], 0.0).astype(BF16) for x in tt]
    f_rk = [jnp.where(lower_incl, x[LANES:, LANES:], 0.0).astype(BF16) for x in tt]

    d16 = [jnp.where(same16, x, 0.0) for x in d_ab]
    d32 = [jnp.where(same32, x, 0.0) for x in d_ab]
    z = each(_dot, e_ak, v_s)
    x2 = each(mm, d16, d16)
    x4 = each(mm, x2, x2)
    x8 = each(mm, x4, x4)
    t = [ident + x for x in d16]
    t = each(lambda t_, x: t_ + mm(t_, x), t, x2)
    t = each(lambda t_, x: t_ + mm(t_, x), t, x4)
    t = each(lambda t_, x: t_ + mm(t_, x), t, x8)
    t = each(lambda t_, hi, lo: t_ + mm(mm(t_, hi - lo), t_), t, d32, d16)
    t = each(lambda t_, hi, lo: t_ + mm(mm(t_, hi - lo), t_), t, d_ab, d32)

    au = each(lambda t_, x, zz: mm(t_, jnp.concatenate([x, zz.astype(BF16)], axis=1)).astype(BF16),
              t, a_s, z)
    g = each(_dot, f_rb, au)
    y0 = each(lambda gg, f, vv: gg[:, LANES:] + _dot(f, vv), g, f_rk, v_s)
    r1 = each(lambda x, gg: x + gg[:, 0:LANES], r_s, g)
    hmat = each(_dot, b_e, au)
    m_mat = each(lambda h, xc: jnp.where(eye, jnp.exp(xc), 0.0) + h[:, 0:LANES], hmat, cum_c)
    n_mat = each(lambda h, ke, vv: h[:, LANES:] + _dot(ke, vv), hmat, k_e, v_s)

    for i, (bi, p) in enumerate(probs):
        s0 = s_ref[i]
        s0b = s0.astype(BF16)
        ys = _dot(r1[i].astype(BF16), s0b) + y0[i]
        s_ref[i] = _dot(m_mat[i].astype(BF16), s0b) + n_mat[i]
        y_ref[bi, :, p * LANES:(p + 1) * LANES] = ys[0:c, :] + ys[c:, :]


def _wkv(r, lw, k, v, a, b, nb):
    bsz, s, w = r.shape
    spec = pl.BlockSpec((nb, WKV_CHUNK, w), lambda bi, n: (bi, n, 0))
    return pl.pallas_call(
        _wkv_kernel,
        grid=(bsz // nb, s // WKV_CHUNK),
        in_specs=[spec] * 6,
        out_specs=spec,
        out_shape=jax.ShapeDtypeStruct((bsz, s, w), F32),
        scratch_shapes=[pltpu.VMEM((nb * RWKV_HEADS // 2, LANES, LANES), F32)],
        compiler_params=_params("parallel", "arbitrary"),
        name="wkv",
    )(r, lw, k, v, a, b)


def _post_kernel(x_ref, mod_ref, y_ref, r_ref, k_ref, v_ref, g_ref, o_ref, gates_ref,
                 lnw_ref, lnb_ref, rk_ref, bd_ref, wor_ref, woa_ref, wout_ref, nfw_ref,
                 wrh_ref, wrl_ref, br_ref,
                 x1_ref, h2_ref, route_ref, cnt_ref):
    tiles = [_post_tile(part, x_ref, mod_ref, y_ref, r_ref, k_ref, v_ref, g_ref, o_ref, gates_ref,
                        lnw_ref, lnb_ref, rk_ref, bd_ref, wor_ref, woa_ref, wout_ref, nfw_ref,
                        wrh_ref, wrl_ref, br_ref, x1_ref, h2_ref, route_ref, cnt_ref)
             for part in range(x_ref.shape[1] // ROUTE_TILE)]
    _trace_in_turn(tiles)


def _post_tile(part, x_ref, mod_ref, y_ref, r_ref, k_ref, v_ref, g_ref, o_ref, gates_ref,
               lnw_ref, lnb_ref, rk_ref, bd_ref, wor_ref, woa_ref, wout_ref, nfw_ref,
               wrh_ref, wrl_ref, br_ref, x1_ref, h2_ref, route_ref, cnt_ref):
    tm = ROUTE_TILE
    rows = slice(part * tm, (part + 1) * tm)
    d = x_ref.shape[2]
    bd = bd_ref[...]
    inv_n = 1.0 / RWKV_HEAD
    y = y_ref[0, rows, :]
    seg = lambda t: _dot(t.astype(BF16), bd)
    mu = seg(y) * inv_n
    dlt = y - mu
    var = seg(dlt * dlt) * inv_n
    yn = dlt * lax.rsqrt(var + GN_EPS) * lnw_ref[...] + lnb_ref[...]
    yield
    v = v_ref[0, rows, :]
    bonus = seg(r_ref[0, rows, :] * k_ref[0, rows, :] * rk_ref[...]) * v
    z = (yn + bonus) * g_ref[0, rows, :]
    o_b = _dot(z.astype(BF16), wor_ref[...])
    o_a = _dot(o_ref[0, rows, :], woa_ref[...])
    yield
    ga = jax.nn.sigmoid(gates_ref[0, rows, 0:d].astype(F32))
    gb = jax.nn.sigmoid(gates_ref[0, rows, d:2 * d].astype(F32))
    merged = ga * o_a + gb * o_b
    mix = _dot(merged.astype(BF16), wout_ref[...])
    yield
    x1 = x_ref[0, rows, :] + mod_ref[0, 2:3, :] * mix
    x1_ref[0, rows, :] = x1
    ms = jnp.mean(x1 * x1, axis=-1, keepdims=True)
    h2 = x1 * lax.rsqrt(ms + NORM_EPS) * nfw_ref[...]
    h2 = h2 * (1.0 + mod_ref[0, 4:5, :]) + mod_ref[0, 3:4, :]
    h2_ref[0, rows, :] = h2
    yield

    hh, hl = _split(h2)
    logits = _dot(hh, wrh_ref[...]) + (_dot(hh, wrl_ref[...]) + _dot(hl, wrh_ref[...])) + br_ref[...]
    lane = lax.broadcasted_iota(jnp.int32, (tm, LANES), 1)
    lanef = lane.astype(F32)
    cur = jnp.where(lane < N_EXPERTS, logits, -jnp.inf)
    yield
    vals, idxs, hots = [], [], []
    for _ in range(TOP_K):
        mx = jnp.max(cur, axis=-1, keepdims=True)
        idx = jnp.min(jnp.where(cur == mx, lanef, float(LANES)), axis=-1, keepdims=True)
        hot = lanef == idx
        cur = jnp.where(hot, -jnp.inf, cur)
        vals.append(mx)
        idxs.append(idx)
        hots.append(hot)
    exps = [jnp.exp(vv - vals[0]) for vv in vals]
    den = exps[0] + exps[1] + exps[2] + exps[3]
    yield
    sel = jnp.zeros((tm, LANES), F32)
    for hot in hots:
        sel = sel + jnp.where(hot, 1.0, 0.0)
    ri = lax.broadcasted_iota(jnp.int32, (tm, tm), 0)
    ci = lax.broadcasted_iota(jnp.int32, (tm, tm), 1)
    below = jnp.where(ri > ci, 1.0, 0.0).astype(BF16)
    before = _dot(below, sel.astype(BF16))
    counts = jnp.broadcast_to(jnp.sum(sel, axis=0, keepdims=True), (8, LANES))
    er = lax.broadcasted_iota(jnp.int32, (LANES, LANES), 0)
    ec = lax.broadcasted_iota(jnp.int32, (LANES, LANES), 1)
    lower = jnp.where(er < ec, 1.0, 0.0).astype(BF16)
    start = _dot(counts.astype(BF16), lower)[0:1, :]
    route = jnp.zeros((tm, LANES), F32)
    for j in range(TOP_K):
        pos = jnp.sum(jnp.where(hots[j], before + start, 0.0), axis=-1, keepdims=True)
        route = jnp.where(lane == j, pos, route)
        route = jnp.where(lane == TOP_K + j, exps[j] / den, route)
    route_ref[0, rows, :] = route
    cnt_ref[0, part] = counts
    yield


def _post(x, mod3, y, r, k, v, g, o, gates, lnw, lnb, rk, bd, wor, woa, wout, nfw, wrh, wrl, br, tm):
    bsz, s, d = x.shape
    w = RWKV_WIDTH
    parts = tm // ROUTE_TILE
    tok = lambda n: pl.BlockSpec((1, tm, n), lambda b, i: (b, i, 0))
    row = lambda n: pl.BlockSpec((1, n), lambda b, i: (0, 0))
    full = lambda a: pl.BlockSpec(a.shape, lambda b, i: (0,) * a.ndim)
    return pl.pallas_call(
        _post_kernel,
        grid=(bsz, s // tm),
        in_specs=[tok(d), pl.BlockSpec((1, 6, d), lambda b, i: (b, 0, 0)),
                  tok(w), tok(w), tok(w), tok(w), tok(w),
                  tok(MLA_HEADS * MLA_V), tok(2 * d),
                  row(w), row(w), row(w), full(bd), full(wor), full(woa), full(wout), row(d),
                  full(wrh), full(wrl), row(LANES)],
        out_specs=[tok(d), tok(d), tok(LANES), pl.BlockSpec((1, parts, 8, LANES), lambda b, i: (b, i, 0, 0))],
        out_shape=[jax.ShapeDtypeStruct((bsz, s, d), F32),
                   jax.ShapeDtypeStruct((bsz, s, d), F32),
                   jax.ShapeDtypeStruct((bsz, s, LANES), F32),
                   jax.ShapeDtypeStruct((bsz, s // ROUTE_TILE, 8, LANES), F32)],
        compiler_params=_params("parallel", "parallel"),
        name="post",
    )(x, mod3, y, r, k, v, g, o, gates, lnw, lnb, rk, bd, wor, woa, wout, nfw, wrh, wrl, br)


def _load_rows(ref, n):
    nc = ref.shape[0] // n
    return jnp.concatenate([ref[pl.ds(c, n, stride=nc), :] for c in range(nc)], axis=1)


def _store_rows(ref, val):
    nc = val.shape[1] // LANES
    for c in range(nc):
        ref[pl.ds(c, val.shape[0], stride=nc), :] = val[:, c * LANES:(c + 1) * LANES]


def _run_copies(tile, n_ref, src_ref, dst_ref, tm, make_copy):
    nbits = tm.bit_length()
    for b in range(nbits):
        group = tile * nbits + b

        def piece(j, carry, group=group, size=1 << b):
            k = group * N_EXPERTS + j
            make_copy(src_ref[k], dst_ref[k], size).start()
            return carry

        lax.fori_loop(0, n_ref[group], piece, 0)


def _pieces(count, limit, fn):
    def emit(bits):
        for b in bits:
            size = 1 << b
            done = count & ~(2 * size - 1)

            @pl.when((count & size) != 0)
            def _():
                fn(done, size)

    bits = range(limit.bit_length() - 1, -1, -1)
    large = [b for b in bits if b >= RARE_PIECE_BIT]
    if large:
        @pl.when(count >= (1 << RARE_PIECE_BIT))
        def _():
            emit(large)
    emit([b for b in bits if b < RARE_PIECE_BIT])


def _zero_fill(gap_ref, xs_ref, zero_ref, sem, nc):
    bm = MOE_BLOCK
    zero_ref[...] = jnp.zeros_like(zero_ref)

    def piece(dst, size):
        return pltpu.make_async_copy(zero_ref.at[pl.ds(0, size * nc), :],
                                     xs_ref.at[pl.ds(pl.multiple_of(dst * nc, nc), size * nc), :], sem)

    def sweep(act):
        def per_expert(e, carry):
            g0 = gap_ref[e]
            _pieces(gap_ref[N_EXPERTS + e], bm, lambda done, size: act(piece(g0 + done, size)))
            return carry

        def tail(blk, carry):
            act(piece(gap_ref[2 * N_EXPERTS] + blk * bm, bm))
            return carry

        lax.fori_loop(0, N_EXPERTS, per_expert, 0)
        lax.fori_loop(0, gap_ref[2 * N_EXPERTS + 1], tail, 0)

    sweep(lambda cp: cp.start())
    sweep(lambda cp: cp.wait())


def _dispatch_kernel(cnt_ref, off_ref, dst_ref, gap_ref, h_ref, route_ref, xs_ref, sorted_ref, zero_ref,
                     sem, zero_sem):
    tm = h_ref.shape[0]

    @pl.when(pl.program_id(0) == 0)
    def _():
        _zero_fill(gap_ref, xs_ref, zero_ref, zero_sem, h_ref.shape[1] // LANES)

    pos_t = route_ref[...].T
    slot = lax.broadcasted_iota(jnp.int32, (TOP_K * tm, tm), 0).astype(F32)
    perm = jnp.where(slot == pos_t[0:1, :], 1.0, 0.0)
    for j in range(1, TOP_K):
        perm = perm + jnp.where(slot == pos_t[j:j + 1, :], 1.0, 0.0)
    i = pl.program_id(0)
    cur = i % 2
    buf = sorted_ref.at[cur]
    _store_rows(buf, _dot(perm.astype(BF16), h_ref[...].astype(BF16)))
    nc = h_ref.shape[1] // LANES

    def make_copy(src, dst, size):
        return pltpu.make_async_copy(buf.at[pl.ds(pl.multiple_of(src * nc, nc), size * nc), :],
                                     xs_ref.at[pl.ds(pl.multiple_of(dst * nc, nc), size * nc), :], sem.at[cur])

    _run_copies(i, cnt_ref, off_ref, dst_ref, tm, make_copy)

    def drain(which):
        pltpu.make_async_copy(sorted_ref.at[which], xs_ref.at[pl.ds(0, TOP_K * tm * nc), :], sem.at[which]).wait()

    @pl.when(i > 0)
    def _():
        drain(1 - cur)

    @pl.when(i == pl.num_programs(0) - 1)
    def _():
        drain(cur)


def _dispatch(cnt_tab, off_tab, dst_tab, gap_tab, h2, route, n_rows, tm):
    t, d = h2.shape
    nc = d // LANES
    return pl.pallas_call(
        _dispatch_kernel,
        grid_spec=pltpu.PrefetchScalarGridSpec(
            num_scalar_prefetch=4,
            grid=(t // tm,),
            in_specs=[pl.BlockSpec((tm, d), lambda i, *_: (i, 0)),
                      pl.BlockSpec((tm, LANES), lambda i, *_: (i, 0))],
            out_specs=pl.BlockSpec(memory_space=pl.ANY),
            scratch_shapes=[pltpu.VMEM((2, TOP_K * tm * nc, LANES), F32), pltpu.VMEM((MOE_BLOCK * nc, LANES), F32),
                            pltpu.SemaphoreType.DMA((2,)), pltpu.SemaphoreType.DMA(())]),
        out_shape=jax.ShapeDtypeStruct((n_rows * nc, LANES), F32),
        compiler_params=_params("arbitrary"),
        name="dispatch",
    )(cnt_tab, off_tab, dst_tab, gap_tab, h2, route)


def _split_gate_up(w_ref, g_ref, u_ref, t_ref):
    half = LANES // 2
    nblk = w_ref.shape[1] // LANES
    for c in range(w_ref.shape[2] // LANES):
        w_t = w_ref[0, :, c * LANES:(c + 1) * LANES].T
        for b in range(nblk):
            t_ref[b] = w_t[:, b * LANES:(b + 1) * LANES]
        for out_ref, first in ((g_ref, 0), (u_ref, 1)):
            rows = [t_ref[b, pl.ds(first, half, stride=2), :] for b in range(nblk)]
            out_ref[c * half:(c + 1) * half, :] = jnp.concatenate(rows, axis=1).astype(BF16)


def _moe_kernel(blk_e_ref, valid_ref, xs_ref, wgu_ref, bg_ref, bu_ref, wdn_ref, bd_ref, ys_ref,
                wg_ref, wu_ref, wd_ref, t_ref):
    i = pl.program_id(0)
    valid = valid_ref[i]

    @pl.when(jnp.logical_or(i == 0, blk_e_ref[i] != blk_e_ref[jnp.maximum(i - 1, 0)]))
    def _():
        _split_gate_up(wgu_ref, wg_ref, wu_ref, t_ref)
        wd_ref[...] = wdn_ref[0].astype(BF16)

    @pl.when(valid > 0)
    def _():
        x = _load_rows(xs_ref, MOE_BLOCK).astype(BF16)
        gate = _dot_nt(x, wg_ref[...]) + bg_ref[0]
        up = _dot_nt(x, wu_ref[...]) + bu_ref[0]
        gate = jnp.minimum(gate, SWIGLU_LIMIT)
        up = jnp.clip(up, -SWIGLU_LIMIT, SWIGLU_LIMIT)
        act = (up + 1.0) * gate * jax.nn.sigmoid(SWIGLU_ALPHA * gate)
        _store_rows(ys_ref, _dot(act.astype(BF16), wd_ref[...]) + bd_ref[0])

    @pl.when(valid == 0)
    def _():
        ys_ref[...] = jnp.zeros_like(ys_ref)


def _moe(blk_e, blk_valid, xs, wgu, bg, bu, wd, bd):
    ff, d = wd.shape[1], wd.shape[2]
    nc = d // LANES
    n_rows = xs.shape[0] // nc
    bm = MOE_BLOCK
    wspec = lambda k, n: pl.BlockSpec((1, k, n), lambda i, be, nu: (be[i], 0, 0))
    return pl.pallas_call(
        _moe_kernel,
        grid_spec=pltpu.PrefetchScalarGridSpec(
            num_scalar_prefetch=2,
            grid=(n_rows // bm,),
            in_specs=[pl.BlockSpec((bm * nc, LANES), lambda i, be, nu: (i, 0)),
                      wspec(d, 2 * ff), wspec(1, ff), wspec(1, ff), wspec(ff, d), wspec(1, d)],
            out_specs=pl.BlockSpec((bm * nc, LANES), lambda i, be, nu: (i, 0)),
            scratch_shapes=[pltpu.VMEM((ff, d), BF16), pltpu.VMEM((ff, d), BF16), pltpu.VMEM((ff, d), BF16),
                            pltpu.VMEM((d // LANES, LANES, LANES), F32)]),
        out_shape=jax.ShapeDtypeStruct((n_rows * nc, LANES), F32),
        compiler_params=_params("arbitrary"),
        name="moe",
    )(blk_e, blk_valid, xs, wgu, bg, bu, wd, bd)


def _combine_kernel(cnt_ref, off_ref, dst_ref, x1_ref, route_ref, mod_ref, ys_ref, o_ref, rows_ref, sem):
    tm = x1_ref.shape[1]
    tile = pl.program_id(0) * pl.num_programs(1) + pl.program_id(1)
    n_tiles = pl.num_programs(0) * pl.num_programs(1)
    nc = x1_ref.shape[2] // LANES
    cur = tile % 2

    def fetch(which_tile):
        which = which_tile % 2

        def make_copy(dst, src, size):
            return pltpu.make_async_copy(
                ys_ref.at[pl.ds(pl.multiple_of(src * nc, nc), size * nc), :],
                rows_ref.at[which, pl.ds(pl.multiple_of(dst * nc, nc), size * nc), :], sem.at[which])

        _run_copies(which_tile, cnt_ref, off_ref, dst_ref, tm, make_copy)

    @pl.when(tile == 0)
    def _():
        fetch(tile)

    @pl.when(tile + 1 < n_tiles)
    def _():
        fetch(tile + 1)

    route = route_ref[0]
    slot = lax.broadcasted_iota(jnp.int32, (tm, TOP_K * tm), 1).astype(F32)
    mix = jnp.where(slot == route[:, 0:1], route[:, TOP_K:TOP_K + 1], 0.0)
    for j in range(1, TOP_K):
        mix = mix + jnp.where(slot == route[:, j:j + 1], route[:, TOP_K + j:TOP_K + j + 1], 0.0)
    pltpu.make_async_copy(ys_ref.at[pl.ds(0, TOP_K * tm * nc), :], rows_ref.at[cur], sem.at[cur]).wait()
    acc = _dot(mix.astype(BF16), _load_rows(rows_ref.at[cur], TOP_K * tm).astype(BF16))
    o_ref[0] = x1_ref[0] + mod_ref[0, 5:6, :] * acc


def _combine(cnt_tab, off_tab, dst_tab, x1, route, mod3, ys, tm):
    bsz, s, d = x1.shape
    return pl.pallas_call(
        _combine_kernel,
        grid_spec=pltpu.PrefetchScalarGridSpec(
            num_scalar_prefetch=3,
            grid=(bsz, s // tm),
            in_specs=[pl.BlockSpec((1, tm, d), lambda b, i, *_: (b, i, 0)),
                      pl.BlockSpec((1, tm, LANES), lambda b, i, *_: (b, i, 0)),
                      pl.BlockSpec((1, 6, d), lambda b, i, *_: (b, 0, 0)),
                      pl.BlockSpec(memory_space=pl.ANY)],
            out_specs=pl.BlockSpec((1, tm, d), lambda b, i, *_: (b, i, 0)),
            scratch_shapes=[pltpu.VMEM((2, TOP_K * tm * d // LANES, LANES), F32), pltpu.SemaphoreType.DMA((2,))]),
        out_shape=jax.ShapeDtypeStruct((bsz, s, d), F32),
        compiler_params=_params("arbitrary", "arbitrary"),
        name="combine",
    )(cnt_tab, off_tab, dst_tab, x1, route, mod3, ys)


def _piece_tables(cnt_tab, off_tab, dst_tab, tm):
    nbits = tm.bit_length()
    bits = jnp.arange(nbits, dtype=jnp.int32)[None, :, None]
    cnt = cnt_tab[:, None, :]
    active = (cnt >> bits) & 1
    done = cnt & ~((2 << bits) - 1)
    rank = jnp.cumsum(active, axis=-1) - active
    slot = jnp.arange(N_EXPERTS, dtype=jnp.int32)[None, None, :, None]
    place = (active[:, :, None, :] == 1) & (rank[:, :, None, :] == slot)
    compact = lambda v: jnp.sum(jnp.where(place, (v[:, None, :] + done)[:, :, None, :], 0), axis=-1)
    flat = lambda v: v.reshape(-1).astype(jnp.int32)
    return flat(jnp.sum(active, axis=-1)), flat(compact(off_tab)), flat(compact(dst_tab))


def _pad_cols(a, n):
    return jnp.pad(a, ((0, 0), (0, n - a.shape[1])))


def _head_blocks(cols):
    k = cols.shape[0]
    out = jnp.zeros((k, MLA_HEADS, HEAD_PAD), F32)
    out = out.at[:, :, :cols.shape[2]].set(cols)
    return out.reshape(k, MLA_HEADS * HEAD_PAD)


def _layer(x, cond_mod, positions, w_in, q_a_norm_w, w_q_up, kv_a_norm_w, w_kv_up, q_norm_w, k_norm_w,
           w_o_mla, rwkv_mu, rwkv_w0, rwkv_w2, rwkv_a0, rwkv_a2, rwkv_g2, rwkv_k_k, rwkv_k_a, rwkv_r_k,
           rwkv_ln_w, rwkv_ln_b, rwkv_w_o, w_out, norm_mix_w, norm_ffn_w, w_router, b_router,
           w_gate_up, b_gate_up, w_down, b_down):
    bsz, s, d = x.shape
    t = bsz * s
    half = MLA_ROPE // 2
    nope, qk = MLA_NOPE, MLA_QK
    mod3 = cond_mod.reshape(bsz, 6, d)

    o_q, o_kv, o_kr = 0, Q_LORA, Q_LORA + KV_LORA
    o_slab = o_kr + MLA_ROPE
    o_gate = o_slab + RWKV_SLAB
    kr_w = w_in[:, o_kr:o_slab]
    zeros = lambda n: jnp.zeros((d, n), F32)
    kr_blk = jnp.concatenate([zeros(nope), kr_w, zeros(HEAD_PAD - qk)], axis=1)
    kr_rot = jnp.concatenate([zeros(nope), -kr_w[:, half:], kr_w[:, :half], zeros(HEAD_PAD - qk)], axis=1)
    w_in_p = jnp.concatenate([w_in[:, o_q:o_kr], kr_blk, kr_rot, w_in[:, o_slab:]], axis=1).astype(BF16)

    tm = min(256, s)

    wq3 = w_q_up.reshape(Q_LORA, MLA_HEADS, qk)
    wq_rot = jnp.concatenate([jnp.zeros((Q_LORA, MLA_HEADS, nope), F32), -wq3[:, :, nope + half:],
                              wq3[:, :, nope:nope + half]], axis=2)
    wq = jnp.concatenate([_head_blocks(wq3), _head_blocks(wq_rot)], axis=1).astype(BF16)
    wkv3 = w_kv_up.reshape(KV_LORA, MLA_HEADS, nope + MLA_V)
    wkv = jnp.concatenate([_head_blocks(wkv3[:, :, :nope]), _head_blocks(wkv3[:, :, nope:])], axis=1).astype(BF16)

    def gains(wn):
        main = jnp.pad(wn, (0, HEAD_PAD - qk)).reshape(1, HEAD_PAD)
        rot = jnp.concatenate([jnp.zeros((nope,), F32), wn[nope + half:], wn[nope:nope + half],
                               jnp.zeros((HEAD_PAD - qk,), F32)]).reshape(1, HEAD_PAD)
        return main, rot

    qg, qgs = gains(q_norm_w)
    kg, kgs = gains(k_norm_w)
    inv_freq = ROPE_THETA ** (-jnp.arange(half, dtype=F32) / half)
    per_row = LANES // half
    pos_rep = jnp.repeat(positions.astype(F32).reshape(t // per_row, per_row), half, axis=1)
    cos16, sin16 = _rope_table(pos_rep, jnp.tile(inv_freq, per_row).reshape(1, LANES))
    cos16 = cos16.reshape(bsz, s, half)
    sin16 = sin16.reshape(bsz, s, half)
    cosf = jnp.concatenate([jnp.ones((bsz, s, nope), F32), cos16, cos16,
                            jnp.ones((bsz, s, HEAD_PAD - qk), F32)], axis=-1)
    sinf = jnp.concatenate([jnp.zeros((bsz, s, nope), F32), sin16, sin16,
                            jnp.zeros((bsz, s, HEAD_PAD - qk), F32)], axis=-1)

    w = RWKV_WIDTH
    wwa = jnp.zeros((LANES, 2 * w), F32)
    wwa = wwa.at[:DECAY_LORA, :w].set(rwkv_w2).at[DECAY_LORA:, w:].set(rwkv_a2)
    wwh = wwa.astype(BF16)
    wwl = (wwa - wwh.astype(F32)).astype(BF16)
    wws = jnp.concatenate([wwh, wwl, wwh], axis=0)
    hid = np.arange(w) // RWKV_HEAD
    bd = jnp.asarray(hid[:, None] == hid[None, :], BF16)
    gates, q, k, vt, r_, lw_, k_, v_, a_, b_, g_ = _front(
        x, mod3, norm_mix_w, w_in_p, cosf, sinf, q_a_norm_w.reshape(1, -1), kv_a_norm_w.reshape(1, -1),
        wq, wkv, qg, qgs, kg, kgs, rwkv_mu.reshape(1, -1), wws, rwkv_w0.reshape(1, -1),
        rwkv_a0.reshape(1, -1), rwkv_g2.astype(BF16), rwkv_k_k.reshape(1, -1), rwkv_k_a.reshape(1, -1), bd,
        tm, min(ATTN_KEY_BLOCK, s))
    o = _attention(q, k, vt, min(ATTN_BLOCK, s))
    y = _wkv(r_, lw_, k_, v_, a_, b_, WKV_BATCH if bsz % WKV_BATCH == 0 else 1)

    woa = w_o_mla.astype(BF16)
    wr =_pad_cols(w_router, LANES)
    wrh = wr.astype(BF16)
    wrl = (wr - wrh.astype(F32)).astype(BF16)
    br = jnp.pad(b_router, (0, LANES - N_EXPERTS)).reshape(1, LANES)
    x1, h2, route, counts = _post(
        x, mod3, y, r_, k_, v_, g_, o, gates, rwkv_ln_w.reshape(1, -1), rwkv_ln_b.reshape(1, -1),
        rwkv_r_k.reshape(1, -1), bd, rwkv_w_o.astype(BF16), woa, w_out.astype(BF16),
        norm_ffn_w.reshape(1, -1), wrh, wrl, br, POST_TILE if s % POST_TILE == 0 else ROUTE_TILE)

    bm = MOE_BLOCK
    n_rows = t * TOP_K + N_EXPERTS * bm
    n_blocks = n_rows // bm
    cnt_tab = counts[:, :, 0, :N_EXPERTS].reshape(t // ROUTE_TILE, N_EXPERTS).astype(jnp.int32)
    total = jnp.sum(cnt_tab, axis=0)
    padded = (total + bm - 1) // bm * bm
    pad_end = jnp.cumsum(padded)
    pad_start = pad_end - padded
    off_tab = jnp.cumsum(cnt_tab, axis=1) - cnt_tab
    dst_tab = pad_start[None, :] + jnp.cumsum(cnt_tab, axis=0) - cnt_tab
    blk_start = jnp.arange(n_blocks, dtype=jnp.int32) * bm
    blk_e = jnp.minimum(jnp.sum((pad_end[None, :] <= blk_start[:, None]).astype(jnp.int32), axis=1),
                        N_EXPERTS - 1)
    blk_valid = jnp.clip((pad_start + total)[blk_e] - blk_start, 0, bm).astype(jnp.int32)
    tabs = _piece_tables(cnt_tab, off_tab, dst_tab, ROUTE_TILE)
    gap_tab = jnp.concatenate([pad_start + total, padded - total, pad_end[-1:],
                               (n_rows - pad_end[-1:]) // bm]).astype(jnp.int32)

    xs = _dispatch(*tabs, gap_tab, h2.reshape(t, d), route.reshape(t, LANES), n_rows, ROUTE_TILE)
    bgu = b_gate_up.reshape(N_EXPERTS, 1, D_FF, 2)
    ys = _moe(blk_e, blk_valid, xs, w_gate_up, bgu[..., 0], bgu[..., 1], w_down,
              b_down.reshape(N_EXPERTS, 1, d))
    return _combine(*tabs, x1, route, mod3, ys, ROUTE_TILE)


def kernel(x, c, positions, ada_w, ada_b, norm_mix_w, norm_ffn_w, w_in, q_a_norm_w, w_q_up, kv_a_norm_w, w_kv_up, q_norm_w, k_norm_w, w_o_mla, rwkv_mu, rwkv_w0, rwkv_w2, rwkv_a0, rwkv_a2, rwkv_g2, rwkv_k_k, rwkv_k_a, rwkv_r_k, rwkv_ln_w, rwkv_ln_b, rwkv_w_o, w_out, w_router, b_router, w_gate_up, b_gate_up, w_down, b_down):
    depth = ada_w.shape[0]
    for l in range(depth):
        mod = _ada(c, ada_w[l], ada_b[l])
        x = _layer(x, mod, positions, w_in[l], q_a_norm_w[l], w_q_up[l], kv_a_norm_w[l], w_kv_up[l],
                   q_norm_w[l], k_norm_w[l], w_o_mla[l], rwkv_mu[l], rwkv_w0[l], rwkv_w2[l], rwkv_a0[l],
                   rwkv_a2[l], rwkv_g2[l], rwkv_k_k[l], rwkv_k_a[l], rwkv_r_k[l], rwkv_ln_w[l],
                   rwkv_ln_b[l], rwkv_w_o[l], w_out[l], norm_mix_w[l], norm_ffn_w[l], w_router[l],
                   b_router[l], w_gate_up[l], b_gate_up[l], w_down[l], b_down[l])
    return x
```

```python
import functools

import numpy as np
import jax
import jax.numpy as jnp
from jax import lax
from jax.experimental import pallas as pl
from jax.experimental.pallas import tpu as pltpu

F32 = jnp.float32
BF16 = jnp.bfloat16

D_MODEL = 1024
MLA_HEADS = 8
MLA_NOPE = 64
MLA_ROPE = 32
MLA_QK = MLA_NOPE + MLA_ROPE
MLA_V = 64
VT_ROWS = 80
Q_LORA = 256
KV_LORA = 128
ROPE_THETA = 10000.0
RWKV_HEADS = 8
RWKV_HEAD = 64
RWKV_WIDTH = RWKV_HEADS * RWKV_HEAD
DECAY_LORA = 64
AAA_LORA = 64
GATE_LORA = 128
RWKV_SLAB = 3 * RWKV_WIDTH + DECAY_LORA + AAA_LORA + GATE_LORA
GN_EPS = 64e-5
N_EXPERTS = 32
TOP_K = 4
D_FF = D_MODEL
SWIGLU_LIMIT = 7.0
SWIGLU_ALPHA = 1.702
NORM_EPS = 1e-6
LOG2_E = 1.4426950408889634

LANES = 128
HEAD_PAD = 128
MLA_COLS = 640
WKV_CHUNK = 64
WKV_BATCH = 8
ATTN_BLOCK = 512
ATTN_KEY_BLOCK = 512
ATTN_HEADS = 8
ROUTE_TILE = 256
POST_TILE = 512
RARE_PIECE_BIT = 6
MOE_BLOCK = 512
VMEM_LIMIT = 56 * 1024 * 1024


def _dot(a, b):
    return jnp.dot(a, b, preferred_element_type=F32)


def _dot_nt(a, b):
    return lax.dot_general(a, b, (((1,), (1,)), ((), ())), preferred_element_type=F32)


def _split(x):
    hi = x.astype(BF16)
    lo = (x - hi.astype(F32)).astype(BF16)
    return hi, lo


def _mm3(a, b):
    ah, al = _split(a)
    bh, bl = _split(b)
    return _dot(ah, bh) + (_dot(ah, bl) + _dot(al, bh))


def _trace_in_turn(stages):
    live = list(stages)
    while live:
        for gen in list(live):
            try:
                next(gen)
            except StopIteration:
                live.remove(gen)


def _params(*sem):
    return pltpu.CompilerParams(dimension_semantics=sem, vmem_limit_bytes=VMEM_LIMIT)


def _ada_kernel(c_ref, w_ref, b_ref, o_ref):
    c = c_ref[...]
    cond = c * jax.nn.sigmoid(c)
    o_ref[...] = _mm3(cond, w_ref[...]) + b_ref[...]


def _ada(c, w, b):
    bsz, d = c.shape
    n = w.shape[1]
    tn = 1024
    return pl.pallas_call(
        _ada_kernel,
        grid=(n // tn,),
        in_specs=[pl.BlockSpec((bsz, d), lambda j: (0, 0)),
                  pl.BlockSpec((d, tn), lambda j: (0, j)),
                  pl.BlockSpec((1, tn), lambda j: (0, j))],
        out_specs=pl.BlockSpec((bsz, tn), lambda j: (0, j)),
        out_shape=jax.ShapeDtypeStruct((bsz, n), F32),
        compiler_params=_params("parallel"),
        name="ada",
    )(c, w, b.reshape(1, n))


def _front_kernel(x_ref, mod_ref, nw_ref, w_ref,
                  cos_ref, sin_ref, qan_ref, kvn_ref, wq_ref, wkv_ref, qg_ref, qgs_ref, kg_ref, kgs_ref,
                  mu_ref, wws_ref, w0_ref, a0_ref, g2_ref, kk_ref, ka_ref, bd_ref,
                  gates_ref, q_out, k_out, vt_out, r_out, lw_out, k2_out, v_out, a_out, b_out, g_out,
                  last_ref):
    x = x_ref[0]
    ms = jnp.mean(x * x, axis=-1, keepdims=True)
    y = x * lax.rsqrt(ms + NORM_EPS) * nw_ref[...]
    h = y * (1.0 + mod_ref[0, 1:2, :]) + mod_ref[0, 0:1, :]
    hb = h.astype(BF16)
    mla = _dot(hb, w_ref[:, 0:MLA_COLS])
    slab = _dot(hb, w_ref[:, MLA_COLS:MLA_COLS + RWKV_SLAB])
    gates_ref[0] = _dot(hb, w_ref[:, MLA_COLS + RWKV_SLAB:]).astype(BF16)
    mla_prep = _mla_prep_body(mla, cos_ref, sin_ref, qan_ref, kvn_ref, wq_ref, wkv_ref,
                              qg_ref, qgs_ref, kg_ref, kgs_ref, q_out, k_out, vt_out)
    rwkv_prep = _rwkv_prep_body(slab, pl.program_id(1) == 0, mu_ref, wws_ref, w0_ref, a0_ref, g2_ref,
                                kk_ref, ka_ref, bd_ref, r_out, lw_out, k2_out, v_out, a_out, b_out, g_out,
                                last_ref)
    _trace_in_turn([mla_prep, rwkv_prep])


def _front(x, mod3, norm_w, w_in_p, cosf, sinf, qan, kvn, wq, wkv, qg, qgs, kg, kgs,
           mu, wws, w0, a0, g2, k_k, k_a, bd, tm, tk):
    bsz, s, d = x.shape
    w = RWKV_WIDTH
    tok = lambda n: pl.BlockSpec((1, tm, n), lambda b, i: (b, i, 0))
    row = lambda n: pl.BlockSpec((1, n), lambda b, i: (0, 0))
    full = lambda a: pl.BlockSpec(a.shape, lambda b, i: (0,) * a.ndim)
    head_spec = pl.BlockSpec((1, MLA_HEADS, tm, HEAD_PAD), lambda b, i: (b, 0, i, 0))
    head_sds = jax.ShapeDtypeStruct((bsz, MLA_HEADS, s, HEAD_PAD), BF16)
    if tk >= tm:
        per = tk // tm
        vt_spec = pl.BlockSpec((1, MLA_HEADS, 1, VT_ROWS, tm), lambda b, i: (b, 0, i // per, 0, i % per))
    else:
        vt_spec = pl.BlockSpec((1, MLA_HEADS, tm // tk, VT_ROWS, tk), lambda b, i: (b, 0, i, 0, 0))
    vt_sds = jax.ShapeDtypeStruct((bsz, MLA_HEADS, s // tk, VT_ROWS, tk), BF16)
    stream_sds = jax.ShapeDtypeStruct((bsz, s, w), F32)
    return pl.pallas_call(
        _front_kernel,
        grid=(bsz, s // tm),
        in_specs=[tok(d), pl.BlockSpec((1, 6, d), lambda b, i: (b, 0, 0)), row(d), full(w_in_p),
                  tok(LANES), tok(LANES), row(Q_LORA), row(KV_LORA), full(wq), full(wkv),
                  row(LANES), row(LANES), row(LANES), row(LANES),
                  row(RWKV_SLAB), full(wws), row(w), row(w), full(g2), row(w), row(w), full(bd)],
        out_specs=[tok(2 * d), head_spec, head_spec, vt_spec] + [tok(w)] * 7,
        out_shape=[jax.ShapeDtypeStruct((bsz, s, 2 * d), BF16), head_sds, head_sds, vt_sds] + [stream_sds] * 7,
        scratch_shapes=[pltpu.VMEM((8, RWKV_SLAB), F32)],
        compiler_params=_params("parallel", "arbitrary"),
        name="front",
    )(x, mod3, norm_w.reshape(1, d), w_in_p, cosf, sinf, qan, kvn, wq, wkv, qg, qgs, kg, kgs,
      mu, wws, w0, a0, g2, k_k, k_a, bd)


def _rope_kernel(pos_ref, invf_ref, cos_ref, sin_ref):
    ang = pos_ref[...] * invf_ref[...]
    cos_ref[...] = jnp.cos(ang)
    sin_ref[...] = jnp.sin(ang)


def _rope_table(pos_rep, invf_row):
    n = pos_rep.shape[0]
    tr = min(512, n)
    spec = pl.BlockSpec((tr, LANES), lambda i: (i, 0))
    sds = jax.ShapeDtypeStruct((n, LANES), F32)
    return pl.pallas_call(
        _rope_kernel,
        grid=(n // tr,),
        in_specs=[spec, pl.BlockSpec((1, LANES), lambda i: (0, 0))],
        out_specs=[spec, spec],
        out_shape=[sds, sds],
        compiler_params=_params("parallel"),
        name="rope",
    )(pos_rep, invf_row)


def _mla_prep_body(m, cos_ref, sin_ref, qan_ref, kvn_ref, wq_ref, wkv_ref,
                   qg_ref, qgs_ref, kg_ref, kgs_ref, q_out, k_out, vt_out):
    cq = m[:, 0:Q_LORA]
    ckv = m[:, Q_LORA:Q_LORA + KV_LORA]
    kr = m[:, 384:512]
    krs = m[:, 512:640]
    ql = cq * lax.rsqrt(jnp.mean(cq * cq, axis=-1, keepdims=True) + NORM_EPS) * qan_ref[...]
    kvl = ckv * lax.rsqrt(jnp.mean(ckv * ckv, axis=-1, keepdims=True) + NORM_EPS) * kvn_ref[...]
    qall = _dot(ql.astype(BF16), wq_ref[...])
    kvall = _dot(kvl.astype(BF16), wkv_ref[...])
    cosf = cos_ref[0]
    sinf = sin_ref[0]
    scale = MLA_QK ** -0.5 * LOG2_E
    hw = MLA_HEADS * HEAD_PAD
    for h in range(MLA_HEADS):
        lo, hi = h * HEAD_PAD, (h + 1) * HEAD_PAD
        qh = qall[:, lo:hi]
        qs = qall[:, hw + lo:hw + hi]
        rs = lax.rsqrt(jnp.sum(qh * qh, axis=-1, keepdims=True) * (1.0 / MLA_QK) + NORM_EPS)
        qo = (qh * rs * qg_ref[...]) * cosf + (qs * rs * qgs_ref[...]) * sinf
        q_out[0, h] = (qo * scale).astype(BF16)
        kh = kvall[:, lo:hi] + kr
        rs = lax.rsqrt(jnp.sum(kh * kh, axis=-1, keepdims=True) * (1.0 / MLA_QK) + NORM_EPS)
        ko = (kh * rs * kg_ref[...]) * cosf + (krs * rs * kgs_ref[...]) * sinf
        k_out[0, h] = ko.astype(BF16)
        extra = lax.broadcasted_iota(jnp.int32, (VT_ROWS - MLA_V, m.shape[0]), 0)
        v_t = jnp.concatenate([kvall[:, hw + lo:hw + hi].T[0:MLA_V, :],
                               jnp.where(extra == 0, 1.0, 0.0)], axis=0).astype(BF16)
        width = vt_out.shape[4]
        for c in range(vt_out.shape[2]):
            vt_out[0, h, c] = v_t[:, c * width:(c + 1) * width]
        yield


def _attn_kernel(q_ref, k_ref, vt_ref, o_ref, m_ref, acc_ref, *, tq):
    qi = pl.program_id(2)
    heads = range(q_ref.shape[1])
    qs = [q_ref[0, h] for h in heads]
    m_ref[...] = jnp.full_like(m_ref, -jnp.inf)
    acc_ref[...] = jnp.zeros_like(acc_ref)

    tk = vt_ref.shape[4]
    per = tq // tk

    def step(j, causal=False, k0=0, nk=tk, q0=0, nq=tq, kbase=0):
        off = pl.multiple_of(j * tk, tk)
        sts = [_dot_nt(k_ref[0, h, pl.ds(off + k0, nk), :], qs[h][q0:q0 + nq]) for h in heads]
        if causal:
            keys = lax.broadcasted_iota(jnp.int32, (nk, nq), 0) + (k0 + kbase)
            queries = lax.broadcasted_iota(jnp.int32, (nk, nq), 1) + q0
        for h in heads:
            st = jnp.where(keys <= queries, sts[h], -jnp.inf) if causal else sts[h]
            m = m_ref[h, :, q0:q0 + nq]
            m_new = jnp.maximum(m, jnp.max(st, axis=0, keepdims=True))
            p = jnp.exp2(st - m_new)
            alpha = jnp.exp2(m - m_new)
            m_ref[h, :, q0:q0 + nq] = m_new
            acc_ref[h, :, q0:q0 + nq] = (alpha * acc_ref[h, :, q0:q0 + nq]
                                         + _dot(vt_ref[0, h, j, :, k0:k0 + nk], p.astype(BF16)))

    def body(j, carry):
        step(j)
        return carry

    lax.fori_loop(0, qi * per, body, 0)
    if per == 1:
        step(qi, True, 0, tk // 2)
        step(qi, True, tk // 2, tk // 2, tq // 2, tq // 2)
    else:
        for part in range(per):
            step(qi * per + part, True, 0, tk, part * tk, tq - part * tk, part * tk)
    out = lambda h: acc_ref[h, 0:MLA_V, :] / acc_ref[h, MLA_V:MLA_V + 1, :]
    for h in heads[::2]:
        o_ref[0, :, h * MLA_V:(h + 2) * MLA_V] = jnp.concatenate([out(h), out(h + 1)], axis=0).T.astype(BF16)


def _attention(q, k, vt, tq):
    bsz, nh, s, dh = q.shape
    dv, tk = vt.shape[3], vt.shape[4]
    hp = ATTN_HEADS
    return pl.pallas_call(
        functools.partial(_attn_kernel, tq=tq),
        grid=(bsz, nh // hp, s // tq),
        in_specs=[pl.BlockSpec((1, hp, tq, dh), lambda b, h, i: (b, h, i, 0)),
                  pl.BlockSpec((1, hp, s, dh), lambda b, h, i: (b, h, 0, 0)),
                  pl.BlockSpec((1, hp, s // tk, dv, tk), lambda b, h, i: (b, h, 0, 0, 0))],
        out_specs=pl.BlockSpec((1, tq, hp * MLA_V), lambda b, h, i: (b, i, h)),
        out_shape=jax.ShapeDtypeStruct((bsz, s, nh * MLA_V), BF16),
        scratch_shapes=[pltpu.VMEM((hp, 1, tq), F32), pltpu.VMEM((hp, dv, tq), F32)],
        compiler_params=_params("parallel", "parallel", "parallel"),
        name="attn",
    )(q, k, vt)


def _softplus(x):
    return jnp.maximum(x, 0.0) + jnp.log(1.0 + jnp.exp(-jnp.abs(x)))


def _rwkv_prep_body(slab, first_tile, mu_ref, wws_ref, w0_ref, a0_ref, g2_ref, kk_ref, ka_ref, bd_ref,
                    r_out, lw_out, k_out, v_out, a_out, b_out, g_out, last_ref):
    tm = slab.shape[0]
    w = RWKV_WIDTH
    rolled = pltpu.roll(slab, 1, 0)
    rowi = lax.broadcasted_iota(jnp.int32, slab.shape, 0)
    carried = jnp.where(first_tile, 0.0, last_ref[0:1, :])
    prev = jnp.where(rowi == 0, carried, rolled)
    last_ref[0:1, :] = slab[tm - 1:tm, :]
    p = slab + (prev - slab) * mu_ref[...]
    r = p[:, 0:w]
    k = p[:, w:2 * w]
    v = p[:, 2 * w:3 * w]
    r_out[0] = r
    v_out[0] = v
    yield
    wa = p[:, 3 * w:3 * w + LANES]
    g_lo = p[:, 3 * w + LANES:3 * w + 2 * LANES]
    lane = lax.broadcasted_iota(jnp.int32, wa.shape, 1)
    wa = jnp.where(lane < DECAY_LORA, jnp.tanh(wa), wa)
    wah, wal = _split(wa)
    wa_o = _dot(jnp.concatenate([wah, wah, wal], axis=1), wws_ref[...])
    g_out[0] = _dot(jax.nn.sigmoid(g_lo).astype(BF16), g2_ref[...])
    yield
    log_w = -_softplus(-(w0_ref[...] + wa_o[:, 0:w])) - 0.5
    lw_out[0] = -jnp.exp(log_w)
    yield
    a = jax.nn.sigmoid(a0_ref[...] + wa_o[:, w:2 * w])
    k_out[0] = k * (1.0 + (a - 1.0) * ka_ref[...])
    yield
    kk = k * kk_ref[...]
    ss = _dot((kk * kk).astype(BF16), bd_ref[...])
    kk = kk / jnp.maximum(jnp.sqrt(ss), 1e-12)
    a_out[0] = -kk
    b_out[0] = kk * a
    yield


def _wkv_kernel(r_ref, lw_ref, k_ref, v_ref, a_ref, b_ref, y_ref, s_ref):
    c = WKV_CHUNK
    n = pl.program_id(1)

    @pl.when(n == 0)
    def _():
        s_ref[...] = jnp.zeros_like(s_ref)

    row = lax.broadcasted_iota(jnp.int32, (LANES, LANES), 0)
    col = lax.broadcasted_iota(jnp.int32, (LANES, LANES), 1)
    lower_strict = row > col
    lower_incl = row >= col
    eye = row == col
    same16 = (row >> 4) == (col >> 4)
    same32 = (row >> 5) == (col >> 5)
    ident = jnp.where(eye, 1.0, 0.0).astype(F32)
    tr = lax.broadcasted_iota(jnp.int32, (c, c), 0)
    tc = lax.broadcasted_iota(jnp.int32, (c, c), 1)
    tri = jnp.where(tr >= tc, 1.0, 0.0).astype(BF16)
    first = lax.broadcasted_iota(jnp.int32, (c, LANES), 1) < RWKV_HEAD

    def stack(x):
        return jnp.concatenate([jnp.where(first, x, 0.0), jnp.where(first, 0.0, x)], axis=0)

    def mm(x, y):
        return _dot(x.astype(BF16), y.astype(BF16))

    nb = r_ref.shape[0]
    probs = [(bi, p) for bi in range(nb) for p in range(RWKV_HEADS // 2)]
    each = lambda f, *xs: [f(*args) for args in zip(*xs)]
    load = lambda ref: [ref[bi, :, p * LANES:(p + 1) * LANES] for bi, p in probs]
    r, lw, k, v, a, b = (load(ref) for ref in (r_ref, lw_ref, k_ref, v_ref, a_ref, b_ref))

    def cumsum(x):
        hi = x.astype(BF16)
        rem = x - hi.astype(F32)
        mid = rem.astype(BF16)
        lo = (rem - mid.astype(F32)).astype(BF16)
        return _dot(tri, hi) + (_dot(tri, mid) + _dot(tri, lo))

    cum = each(cumsum, lw)
    cum_c = [x[c - 1:c, :] for x in cum]
    e_in = each(jnp.exp, cum)
    e_neg = each(lambda x: jnp.exp(-x), cum)
    e_end = each(lambda x, xc: jnp.exp(xc - x), cum, cum_c)
    a_s = each(lambda x, cu, l: stack(x * jnp.exp(cu - l)).astype(BF16), a, cum, lw)
    r_s = each(lambda x, e: stack(x * e), r, e_in)
    b_s = each(lambda x, e: stack(x * e).astype(BF16), b, e_neg)
    k_s = each(lambda x, e: stack(x * e).astype(BF16), k, e_neg)
    b_e = each(lambda x, e: stack(x * e).T.astype(BF16), b, e_end)
    k_e = each(lambda x, e: stack(x * e).T.astype(BF16), k, e_end)
    v_s = each(lambda x: stack(x).astype(BF16), v)

    tt = each(lambda x1, x2, y1, y2: _dot_nt(jnp.concatenate([x1, x2.astype(BF16)], axis=0),
                                             jnp.concatenate([y1, y2], axis=0)), a_s, r_s, b_s, k_s)
    d_ab = [jnp.where(lower_strict, x[0:LANES, 0:LANES], 0.0) for x in tt]
    e_ak = [jnp.where(lower_strict, x[0:LANES, LANES:], 0.0).astype(BF16) for x in tt]
    f_rb = [jnp.where(lower_incl, x[LANES:, 0:LANES], 0.0).astype(BF16) for x in tt]
    f_rk = [jnp.where(lower_incl, x[LANES:, LANES:], 0.0).astype(BF16) for x in tt]

    d16 = [jnp.where(same16, x, 0.0) for x in d_ab]
    d32 = [jnp.where(same32, x, 0.0) for x in d_ab]
    z = each(_dot, e_ak, v_s)
    x2 = each(mm, d16, d16)
    x4 = each(mm, x2, x2)
    x8 = each(mm, x4, x4)
    t = [ident + x for x in d16]
    t = each(lambda t_, x: t_ + mm(t_, x), t, x2)
    t = each(lambda t_, x: t_ + mm(t_, x), t, x4)
    t = each(lambda t_, x: t_ + mm(t_, x), t, x8)
    t = each(lambda t_, hi, lo: t_ + mm(mm(t_, hi - lo), t_), t, d32, d16)
    t = each(lambda t_, hi, lo: t_ + mm(mm(t_, hi - lo), t_), t, d_ab, d32)

    au = each(lambda t_, x, zz: mm(t_, jnp.concatenate([x, zz.astype(BF16)], axis=1)).astype(BF16),
              t, a_s, z)
    g = each(_dot, f_rb, au)
    y0 = each(lambda gg, f, vv: gg[:, LANES:] + _dot(f, vv), g, f_rk, v_s)
    r1 = each(lambda x, gg: x + gg[:, 0:LANES], r_s, g)
    hmat = each(_dot, b_e, au)
    m_mat = each(lambda h, xc: jnp.where(eye, jnp.exp(xc), 0.0) + h[:, 0:LANES], hmat, cum_c)
    n_mat = each(lambda h, ke, vv: h[:, LANES:] + _dot(ke, vv), hmat, k_e, v_s)

    for i, (bi, p) in enumerate(probs):
        s0 = s_ref[i]
        s0b = s0.astype(BF16)
        ys = _dot(r1[i].astype(BF16), s0b) + y0[i]
        s_ref[i] = _dot(m_mat[i].astype(BF16), s0b) + n_mat[i]
        y_ref[bi, :, p * LANES:(p + 1) * LANES] = ys[0:c, :] + ys[c:, :]


def _wkv(r, lw, k, v, a, b, nb):
    bsz, s, w = r.shape
    spec = pl.BlockSpec((nb, WKV_CHUNK, w), lambda bi, n: (bi, n, 0))
    return pl.pallas_call(
        _wkv_kernel,
        grid=(bsz // nb, s // WKV_CHUNK),
        in_specs=[spec] * 6,
        out_specs=spec,
        out_shape=jax.ShapeDtypeStruct((bsz, s, w), F32),
        scratch_shapes=[pltpu.VMEM((nb * RWKV_HEADS // 2, LANES, LANES), F32)],
        compiler_params=_params("parallel", "arbitrary"),
        name="wkv",
    )(r, lw, k, v, a, b)


def _post_kernel(x_ref, mod_ref, y_ref, r_ref, k_ref, v_ref, g_ref, o_ref, gates_ref,
                 lnw_ref, lnb_ref, rk_ref, bd_ref, wor_ref, woa_ref, wout_ref, nfw_ref,
                 wrh_ref, wrl_ref, br_ref,
                 x1_ref, h2_ref, route_ref, cnt_ref):
    tiles = [_post_tile(part, x_ref, mod_ref, y_ref, r_ref, k_ref, v_ref, g_ref, o_ref, gates_ref,
                        lnw_ref, lnb_ref, rk_ref, bd_ref, wor_ref, woa_ref, wout_ref, nfw_ref,
                        wrh_ref, wrl_ref, br_ref, x1_ref, h2_ref, route_ref, cnt_ref)
             for part in range(x_ref.shape[1] // ROUTE_TILE)]
    _trace_in_turn(tiles)


def _post_tile(part, x_ref, mod_ref, y_ref, r_ref, k_ref, v_ref, g_ref, o_ref, gates_ref,
               lnw_ref, lnb_ref, rk_ref, bd_ref, wor_ref, woa_ref, wout_ref, nfw_ref,
               wrh_ref, wrl_ref, br_ref, x1_ref, h2_ref, route_ref, cnt_ref):
    tm = ROUTE_TILE
    rows = slice(part * tm, (part + 1) * tm)
    d = x_ref.shape[2]
    bd = bd_ref[...]
    inv_n = 1.0 / RWKV_HEAD
    y = y_ref[0, rows, :]
    seg = lambda t: _dot(t.astype(BF16), bd)
    mu = seg(y) * inv_n
    dlt = y - mu
    var = seg(dlt * dlt) * inv_n
    yn = dlt * lax.rsqrt(var + GN_EPS) * lnw_ref[...] + lnb_ref[...]
    yield
    v = v_ref[0, rows, :]
    bonus = seg(r_ref[0, rows, :] * k_ref[0, rows, :] * rk_ref[...]) * v
    z = (yn + bonus) * g_ref[0, rows, :]
    o_b = _dot(z.astype(BF16), wor_ref[...])
    o_a = _dot(o_ref[0, rows, :], woa_ref[...])
    yield
    ga = jax.nn.sigmoid(gates_ref[0, rows, 0:d].astype(F32))
    gb = jax.nn.sigmoid(gates_ref[0, rows, d:2 * d].astype(F32))
    merged = ga * o_a + gb * o_b
    mix = _dot(merged.astype(BF16), wout_ref[...])
    yield
    x1 = x_ref[0, rows, :] + mod_ref[0, 2:3, :] * mix
    x1_ref[0, rows, :] = x1
    ms = jnp.mean(x1 * x1, axis=-1, keepdims=True)
    h2 = x1 * lax.rsqrt(ms + NORM_EPS) * nfw_ref[...]
    h2 = h2 * (1.0 + mod_ref[0, 4:5, :]) + mod_ref[0, 3:4, :]
    h2_ref[0, rows, :] = h2
    yield

    hh, hl = _split(h2)
    logits = _dot(hh, wrh_ref[...]) + (_dot(hh, wrl_ref[...]) + _dot(hl, wrh_ref[...])) + br_ref[...]
    lane = lax.broadcasted_iota(jnp.int32, (tm, LANES), 1)
    lanef = lane.astype(F32)
    cur = jnp.where(lane < N_EXPERTS, logits, -jnp.inf)
    yield
    vals, idxs, hots = [], [], []
    for _ in range(TOP_K):
        mx = jnp.max(cur, axis=-1, keepdims=True)
        idx = jnp.min(jnp.where(cur == mx, lanef, float(LANES)), axis=-1, keepdims=True)
        hot = lanef == idx
        cur = jnp.where(hot, -jnp.inf, cur)
        vals.append(mx)
        idxs.append(idx)
        hots.append(hot)
    exps = [jnp.exp(vv - vals[0]) for vv in vals]
    den = exps[0] + exps[1] + exps[2] + exps[3]
    yield
    sel = jnp.zeros((tm, LANES), F32)
    for hot in hots:
        sel = sel + jnp.where(hot, 1.0, 0.0)
    ri = lax.broadcasted_iota(jnp.int32, (tm, tm), 0)
    ci = lax.broadcasted_iota(jnp.int32, (tm, tm), 1)
    below = jnp.where(ri > ci, 1.0, 0.0).astype(BF16)
    before = _dot(below, sel.astype(BF16))
    counts = jnp.broadcast_to(jnp.sum(sel, axis=0, keepdims=True), (8, LANES))
    er = lax.broadcasted_iota(jnp.int32, (LANES, LANES), 0)
    ec = lax.broadcasted_iota(jnp.int32, (LANES, LANES), 1)
    lower = jnp.where(er < ec, 1.0, 0.0).astype(BF16)
    start = _dot(counts.astype(BF16), lower)[0:1, :]
    route = jnp.zeros((tm, LANES), F32)
    for j in range(TOP_K):
        pos = jnp.sum(jnp.where(hots[j], before + start, 0.0), axis=-1, keepdims=True)
        route = jnp.where(lane == j, pos, route)
        route = jnp.where(lane == TOP_K + j, exps[j] / den, route)
    route_ref[0, rows, :] = route
    cnt_ref[0, part] = counts
    yield


def _post(x, mod3, y, r, k, v, g, o, gates, lnw, lnb, rk, bd, wor, woa, wout, nfw, wrh, wrl, br, tm):
    bsz, s, d = x.shape
    w = RWKV_WIDTH
    parts = tm // ROUTE_TILE
    tok = lambda n: pl.BlockSpec((1, tm, n), lambda b, i: (b, i, 0))
    row = lambda n: pl.BlockSpec((1, n), lambda b, i: (0, 0))
    full = lambda a: pl.BlockSpec(a.shape, lambda b, i: (0,) * a.ndim)
    return pl.pallas_call(
        _post_kernel,
        grid=(bsz, s // tm),
        in_specs=[tok(d), pl.BlockSpec((1, 6, d), lambda b, i: (b, 0, 0)),
                  tok(w), tok(w), tok(w), tok(w), tok(w),
                  tok(MLA_HEADS * MLA_V), tok(2 * d),
                  row(w), row(w), row(w), full(bd), full(wor), full(woa), full(wout), row(d),
                  full(wrh), full(wrl), row(LANES)],
        out_specs=[tok(d), tok(d), tok(LANES), pl.BlockSpec((1, parts, 8, LANES), lambda b, i: (b, i, 0, 0))],
        out_shape=[jax.ShapeDtypeStruct((bsz, s, d), F32),
                   jax.ShapeDtypeStruct((bsz, s, d), F32),
                   jax.ShapeDtypeStruct((bsz, s, LANES), F32),
                   jax.ShapeDtypeStruct((bsz, s // ROUTE_TILE, 8, LANES), F32)],
        compiler_params=_params("parallel", "parallel"),
        name="post",
    )(x, mod3, y, r, k, v, g, o, gates, lnw, lnb, rk, bd, wor, woa, wout, nfw, wrh, wrl, br)


def _load_rows(ref, n):
    nc = ref.shape[0] // n
    return jnp.concatenate([ref[pl.ds(c, n, stride=nc), :] for c in range(nc)], axis=1)


def _store_rows(ref, val):
    nc = val.shape[1] // LANES
    for c in range(nc):
        ref[pl.ds(c, val.shape[0], stride=nc), :] = val[:, c * LANES:(c + 1) * LANES]


def _run_copies(tile, n_ref, src_ref, dst_ref, tm, make_copy):
    nbits = tm.bit_length()
    for b in range(nbits):
        group = tile * nbits + b

        def piece(j, carry, group=group, size=1 << b, priority=b % 2):
            k = group * N_EXPERTS + j
            make_copy(src_ref[k], dst_ref[k], size).start(priority=priority)
            return carry

        lax.fori_loop(0, n_ref[group], piece, 0)


def _pieces(count, limit, fn):
    def emit(bits):
        for b in bits:
            size = 1 << b
            done = count & ~(2 * size - 1)

            @pl.when((count & size) != 0)
            def _():
                fn(done, size)

    bits = range(limit.bit_length() - 1, -1, -1)
    large = [b for b in bits if b >= RARE_PIECE_BIT]
    if large:
        @pl.when(count >= (1 << RARE_PIECE_BIT))
        def _():
            emit(large)
    emit([b for b in bits if b < RARE_PIECE_BIT])


def _zero_fill(gap_ref, xs_ref, zero_ref, sem, nc):
    bm = MOE_BLOCK
    zero_ref[...] = jnp.zeros_like(zero_ref)

    def piece(dst, size):
        return pltpu.make_async_copy(zero_ref.at[pl.ds(0, size * nc), :],
                                     xs_ref.at[pl.ds(pl.multiple_of(dst * nc, nc), size * nc), :], sem)

    def sweep(act):
        def per_expert(e, carry):
            g0 = gap_ref[e]
            _pieces(gap_ref[N_EXPERTS + e], bm, lambda done, size: act(piece(g0 + done, size)))
            return carry

        def tail(blk, carry):
            act(piece(gap_ref[2 * N_EXPERTS] + blk * bm, bm))
            return carry

        lax.fori_loop(0, N_EXPERTS, per_expert, 0)
        lax.fori_loop(0, gap_ref[2 * N_EXPERTS + 1], tail, 0)

    sweep(lambda cp: cp.start())
    sweep(lambda cp: cp.wait())


def _dispatch_kernel(cnt_ref, off_ref, dst_ref, gap_ref, h_ref, route_ref, xs_ref, sorted_ref, zero_ref,
                     sem, zero_sem):
    tm = h_ref.shape[0]

    @pl.when(pl.program_id(0) == 0)
    def _():
        _zero_fill(gap_ref, xs_ref, zero_ref, zero_sem, h_ref.shape[1] // LANES)

    pos_t = route_ref[...].T
    slot = lax.broadcasted_iota(jnp.int32, (TOP_K * tm, tm), 0).astype(F32)
    perm = jnp.where(slot == pos_t[0:1, :], 1.0, 0.0)
    for j in range(1, TOP_K):
        perm = perm + jnp.where(slot == pos_t[j:j + 1, :], 1.0, 0.0)
    i = pl.program_id(0)
    cur = i % 2
    buf = sorted_ref.at[cur]
    _store_rows(buf, _dot(perm.astype(BF16), h_ref[...].astype(BF16)))
    nc = h_ref.shape[1] // LANES

    def make_copy(src, dst, size):
        return pltpu.make_async_copy(buf.at[pl.ds(pl.multiple_of(src * nc, nc), size * nc), :],
                                     xs_ref.at[pl.ds(pl.multiple_of(dst * nc, nc), size * nc), :], sem.at[cur])

    _run_copies(i, cnt_ref, off_ref, dst_ref, tm, make_copy)

    def drain(which):
        pltpu.make_async_copy(sorted_ref.at[which], xs_ref.at[pl.ds(0, TOP_K * tm * nc), :], sem.at[which]).wait()

    @pl.when(i > 0)
    def _():
        drain(1 - cur)

    @pl.when(i == pl.num_programs(0) - 1)
    def _():
        drain(cur)


def _dispatch(cnt_tab, off_tab, dst_tab, gap_tab, h2, route, n_rows, tm):
    t, d = h2.shape
    nc = d // LANES
    return pl.pallas_call(
        _dispatch_kernel,
        grid_spec=pltpu.PrefetchScalarGridSpec(
            num_scalar_prefetch=4,
            grid=(t // tm,),
            in_specs=[pl.BlockSpec((tm, d), lambda i, *_: (i, 0)),
                      pl.BlockSpec((tm, LANES), lambda i, *_: (i, 0))],
            out_specs=pl.BlockSpec(memory_space=pl.ANY),
            scratch_shapes=[pltpu.VMEM((2, TOP_K * tm * nc, LANES), F32), pltpu.VMEM((MOE_BLOCK * nc, LANES), F32),
                            pltpu.SemaphoreType.DMA((2,)), pltpu.SemaphoreType.DMA(())]),
        out_shape=jax.ShapeDtypeStruct((n_rows * nc, LANES), F32),
        compiler_params=_params("arbitrary"),
        name="dispatch",
    )(cnt_tab, off_tab, dst_tab, gap_tab, h2, route)


def _split_gate_up(w_ref, g_ref, u_ref, t_ref):
    half = LANES // 2
    nblk = w_ref.shape[1] // LANES
    for c in range(w_ref.shape[2] // LANES):
        w_t = w_ref[0, :, c * LANES:(c + 1) * LANES].T
        for b in range(nblk):
            t_ref[b] = w_t[:, b * LANES:(b + 1) * LANES]
        for out_ref, first in ((g_ref, 0), (u_ref, 1)):
            rows = [t_ref[b, pl.ds(first, half, stride=2), :] for b in range(nblk)]
            out_ref[c * half:(c + 1) * half, :] = jnp.concatenate(rows, axis=1).astype(BF16)


def _moe_kernel(blk_e_ref, valid_ref, xs_ref, wgu_ref, bg_ref, bu_ref, wdn_ref, bd_ref, ys_ref,
                wg_ref, wu_ref, wd_ref, t_ref):
    i = pl.program_id(0)
    valid = valid_ref[i]

    @pl.when(jnp.logical_or(i == 0, blk_e_ref[i] != blk_e_ref[jnp.maximum(i - 1, 0)]))
    def _():
        _split_gate_up(wgu_ref, wg_ref, wu_ref, t_ref)
        wd_ref[...] = wdn_ref[0].astype(BF16)

    @pl.when(valid > 0)
    def _():
        x = _load_rows(xs_ref, MOE_BLOCK).astype(BF16)
        gate = _dot_nt(x, wg_ref[...]) + bg_ref[0]
        up = _dot_nt(x, wu_ref[...]) + bu_ref[0]
        gate = jnp.minimum(gate, SWIGLU_LIMIT)
        up = jnp.clip(up, -SWIGLU_LIMIT, SWIGLU_LIMIT)
        act = (up + 1.0) * gate * jax.nn.sigmoid(SWIGLU_ALPHA * gate)
        _store_rows(ys_ref, _dot(act.astype(BF16), wd_ref[...]) + bd_ref[0])

    @pl.when(valid == 0)
    def _():
        ys_ref[...] = jnp.zeros_like(ys_ref)


def _moe(blk_e, blk_valid, xs, wgu, bg, bu, wd, bd):
    ff, d = wd.shape[1], wd.shape[2]
    nc = d // LANES
    n_rows = xs.shape[0] // nc
    bm = MOE_BLOCK
    wspec = lambda k, n: pl.BlockSpec((1, k, n), lambda i, be, nu: (be[i], 0, 0))
    return pl.pallas_call(
        _moe_kernel,
        grid_spec=pltpu.PrefetchScalarGridSpec(
            num_scalar_prefetch=2,
            grid=(n_rows // bm,),
            in_specs=[pl.BlockSpec((bm * nc, LANES), lambda i, be, nu: (i, 0)),
                      wspec(d, 2 * ff), wspec(1, ff), wspec(1, ff), wspec(ff, d), wspec(1, d)],
            out_specs=pl.BlockSpec((bm * nc, LANES), lambda i, be, nu: (i, 0)),
            scratch_shapes=[pltpu.VMEM((ff, d), BF16), pltpu.VMEM((ff, d), BF16), pltpu.VMEM((ff, d), BF16),
                            pltpu.VMEM((d // LANES, LANES, LANES), F32)]),
        out_shape=jax.ShapeDtypeStruct((n_rows * nc, LANES), F32),
        compiler_params=_params("arbitrary"),
        name="moe",
    )(blk_e, blk_valid, xs, wgu, bg, bu, wd, bd)


def _combine_kernel(cnt_ref, off_ref, dst_ref, x1_ref, route_ref, mod_ref, ys_ref, o_ref, rows_ref, sem):
    tm = x1_ref.shape[1]
    tile = pl.program_id(0) * pl.num_programs(1) + pl.program_id(1)
    n_tiles = pl.num_programs(0) * pl.num_programs(1)
    nc = x1_ref.shape[2] // LANES
    cur = tile % 2

    def fetch(which_tile):
        which = which_tile % 2

        def make_copy(dst, src, size):
            return pltpu.make_async_copy(
                ys_ref.at[pl.ds(pl.multiple_of(src * nc, nc), size * nc), :],
                rows_ref.at[which, pl.ds(pl.multiple_of(dst * nc, nc), size * nc), :], sem.at[which])

        _run_copies(which_tile, cnt_ref, off_ref, dst_ref, tm, make_copy)

    @pl.when(tile == 0)
    def _():
        fetch(tile)

    @pl.when(tile + 1 < n_tiles)
    def _():
        fetch(tile + 1)

    route = route_ref[0]
    slot = lax.broadcasted_iota(jnp.int32, (tm, TOP_K * tm), 1).astype(F32)
    mix = jnp.where(slot == route[:, 0:1], route[:, TOP_K:TOP_K + 1], 0.0)
    for j in range(1, TOP_K):
        mix = mix + jnp.where(slot == route[:, j:j + 1], route[:, TOP_K + j:TOP_K + j + 1], 0.0)
    pltpu.make_async_copy(ys_ref.at[pl.ds(0, TOP_K * tm * nc), :], rows_ref.at[cur], sem.at[cur]).wait()
    acc = _dot(mix.astype(BF16), _load_rows(rows_ref.at[cur], TOP_K * tm).astype(BF16))
    o_ref[0] = x1_ref[0] + mod_ref[0, 5:6, :] * acc


def _combine(cnt_tab, off_tab, dst_tab, x1, route, mod3, ys, tm):
    bsz, s, d = x1.shape
    return pl.pallas_call(
        _combine_kernel,
        grid_spec=pltpu.PrefetchScalarGridSpec(
            num_scalar_prefetch=3,
            grid=(bsz, s // tm),
            in_specs=[pl.BlockSpec((1, tm, d), lambda b, i, *_: (b, i, 0)),
                      pl.BlockSpec((1, tm, LANES), lambda b, i, *_: (b, i, 0)),
                      pl.BlockSpec((1, 6, d), lambda b, i, *_: (b, 0, 0)),
                      pl.BlockSpec(memory_space=pl.ANY)],
            out_specs=pl.BlockSpec((1, tm, d), lambda b, i, *_: (b, i, 0)),
            scratch_shapes=[pltpu.VMEM((2, TOP_K * tm * d // LANES, LANES), F32), pltpu.SemaphoreType.DMA((2,))]),
        out_shape=jax.ShapeDtypeStruct((bsz, s, d), F32),
        compiler_params=_params("arbitrary", "arbitrary"),
        name="combine",
    )(cnt_tab, off_tab, dst_tab, x1, route, mod3, ys)


def _piece_tables(cnt_tab, off_tab, dst_tab, tm):
    nbits = tm.bit_length()
    bits = jnp.arange(nbits, dtype=jnp.int32)[None, :, None]
    cnt = cnt_tab[:, None, :]
    active = (cnt >> bits) & 1
    done = cnt & ~((2 << bits) - 1)
    rank = jnp.cumsum(active, axis=-1) - active
    slot = jnp.arange(N_EXPERTS, dtype=jnp.int32)[None, None, :, None]
    place = (active[:, :, None, :] == 1) & (rank[:, :, None, :] == slot)
    compact = lambda v: jnp.sum(jnp.where(place, (v[:, None, :] + done)[:, :, None, :], 0), axis=-1)
    flat = lambda v: v.reshape(-1).astype(jnp.int32)
    return flat(jnp.sum(active, axis=-1)), flat(compact(off_tab)), flat(compact(dst_tab))


def _pad_cols(a, n):
    return jnp.pad(a, ((0, 0), (0, n - a.shape[1])))


def _head_blocks(cols):
    k = cols.shape[0]
    out = jnp.zeros((k, MLA_HEADS, HEAD_PAD), F32)
    out = out.at[:, :, :cols.shape[2]].set(cols)
    return out.reshape(k, MLA_HEADS * HEAD_PAD)


def _layer(x, cond_mod, positions, w_in, q_a_norm_w, w_q_up, kv_a_norm_w, w_kv_up, q_norm_w, k_norm_w,
           w_o_mla, rwkv_mu, rwkv_w0, rwkv_w2, rwkv_a0, rwkv_a2, rwkv_g2, rwkv_k_k, rwkv_k_a, rwkv_r_k,
           rwkv_ln_w, rwkv_ln_b, rwkv_w_o, w_out, norm_mix_w, norm_ffn_w, w_router, b_router,
           w_gate_up, b_gate_up, w_down, b_down):
    bsz, s, d = x.shape
    t = bsz * s
    half = MLA_ROPE // 2
    nope, qk = MLA_NOPE, MLA_QK
    mod3 = cond_mod.reshape(bsz, 6, d)

    o_q, o_kv, o_kr = 0, Q_LORA, Q_LORA + KV_LORA
    o_slab = o_kr + MLA_ROPE
    o_gate = o_slab + RWKV_SLAB
    kr_w = w_in[:, o_kr:o_slab]
    zeros = lambda n: jnp.zeros((d, n), F32)
    kr_blk = jnp.concatenate([zeros(nope), kr_w, zeros(HEAD_PAD - qk)], axis=1)
    kr_rot = jnp.concatenate([zeros(nope), -kr_w[:, half:], kr_w[:, :half], zeros(HEAD_PAD - qk)], axis=1)
    w_in_p = jnp.concatenate([w_in[:, o_q:o_kr], kr_blk, kr_rot, w_in[:, o_slab:]], axis=1).astype(BF16)

    tm = min(256, s)

    wq3 = w_q_up.reshape(Q_LORA, MLA_HEADS, qk)
    wq_rot = jnp.concatenate([jnp.zeros((Q_LORA, MLA_HEADS, nope), F32), -wq3[:, :, nope + half:],
                              wq3[:, :, nope:nope + half]], axis=2)
    wq = jnp.concatenate([_head_blocks(wq3), _head_blocks(wq_rot)], axis=1).astype(BF16)
    wkv3 = w_kv_up.reshape(KV_LORA, MLA_HEADS, nope + MLA_V)
    wkv = jnp.concatenate([_head_blocks(wkv3[:, :, :nope]), _head_blocks(wkv3[:, :, nope:])], axis=1).astype(BF16)

    def gains(wn):
        main = jnp.pad(wn, (0, HEAD_PAD - qk)).reshape(1, HEAD_PAD)
        rot = jnp.concatenate([jnp.zeros((nope,), F32), wn[nope + half:], wn[nope:nope + half],
                               jnp.zeros((HEAD_PAD - qk,), F32)]).reshape(1, HEAD_PAD)
        return main, rot

    qg, qgs = gains(q_norm_w)
    kg, kgs = gains(k_norm_w)
    inv_freq = ROPE_THETA ** (-jnp.arange(half, dtype=F32) / half)
    per_row = LANES // half
    pos_rep = jnp.repeat(positions.astype(F32).reshape(t // per_row, per_row), half, axis=1)
    cos16, sin16 = _rope_table(pos_rep, jnp.tile(inv_freq, per_row).reshape(1, LANES))
    cos16 = cos16.reshape(bsz, s, half)
    sin16 = sin16.reshape(bsz, s, half)
    cosf = jnp.concatenate([jnp.ones((bsz, s, nope), F32), cos16, cos16,
                            jnp.ones((bsz, s, HEAD_PAD - qk), F32)], axis=-1)
    sinf = jnp.concatenate([jnp.zeros((bsz, s, nope), F32), sin16, sin16,
                            jnp.zeros((bsz, s, HEAD_PAD - qk), F32)], axis=-1)

    w = RWKV_WIDTH
    wwa = jnp.zeros((LANES, 2 * w), F32)
    wwa = wwa.at[:DECAY_LORA, :w].set(rwkv_w2).at[DECAY_LORA:, w:].set(rwkv_a2)
    wwh = wwa.astype(BF16)
    wwl = (wwa - wwh.astype(F32)).astype(BF16)
    wws = jnp.concatenate([wwh, wwl, wwh], axis=0)
    hid = np.arange(w) // RWKV_HEAD
    bd = jnp.asarray(hid[:, None] == hid[None, :], BF16)
    gates, q, k, vt, r_, lw_, k_, v_, a_, b_, g_ = _front(
        x, mod3, norm_mix_w, w_in_p, cosf, sinf, q_a_norm_w.reshape(1, -1), kv_a_norm_w.reshape(1, -1),
        wq, wkv, qg, qgs, kg, kgs, rwkv_mu.reshape(1, -1), wws, rwkv_w0.reshape(1, -1),
        rwkv_a0.reshape(1, -1), rwkv_g2.astype(BF16), rwkv_k_k.reshape(1, -1), rwkv_k_a.reshape(1, -1), bd,
        tm, min(ATTN_KEY_BLOCK, s))
    o = _attention(q, k, vt, min(ATTN_BLOCK, s))
    y = _wkv(r_, lw_, k_, v_, a_, b_, WKV_BATCH if bsz % WKV_BATCH == 0 else 1)

    woa = w_o_mla.astype(BF16)
    wr =_pad_cols(w_router, LANES)
    wrh = wr.astype(BF16)
    wrl = (wr - wrh.astype(F32)).astype(BF16)
    br = jnp.pad(b_router, (0, LANES - N_EXPERTS)).reshape(1, LANES)
    x1, h2, route, counts = _post(
        x, mod3, y, r_, k_, v_, g_, o, gates, rwkv_ln_w.reshape(1, -1), rwkv_ln_b.reshape(1, -1),
        rwkv_r_k.reshape(1, -1), bd, rwkv_w_o.astype(BF16), woa, w_out.astype(BF16),
        norm_ffn_w.reshape(1, -1), wrh, wrl, br, POST_TILE if s % POST_TILE == 0 else ROUTE_TILE)

    bm = MOE_BLOCK
    n_rows = t * TOP_K + N_EXPERTS * bm
    n_blocks = n_rows // bm
    cnt_tab = counts[:, :, 0, :N_EXPERTS].reshape(t // ROUTE_TILE, N_EXPERTS).astype(jnp.int32)
    total = jnp.sum(cnt_tab, axis=0)
    padded = (total + bm - 1) // bm * bm
    pad_end = jnp.cumsum(padded)
    pad_start = pad_end - padded
    off_tab = jnp.cumsum(cnt_tab, axis=1) - cnt_tab
    dst_tab = pad_start[None, :] + jnp.cumsum(cnt_tab, axis=0) - cnt_tab
    blk_start = jnp.arange(n_blocks, dtype=jnp.int32) * bm
    blk_e = jnp.minimum(jnp.sum((pad_end[None, :] <= blk_start[:, None]).astype(jnp.int32), axis=1),
                        N_EXPERTS - 1)
    blk_valid = jnp.clip((pad_start + total)[blk_e] - blk_start, 0, bm).astype(jnp.int32)
    tabs = _piece_tables(cnt_tab, off_tab, dst_tab, ROUTE_TILE)
    gap_tab = jnp.concatenate([pad_start + total, padded - total, pad_end[-1:],
                               (n_rows - pad_end[-1:]) // bm]).astype(jnp.int32)

    xs = _dispatch(*tabs, gap_tab, h2.reshape(t, d), route.reshape(t, LANES), n_rows, ROUTE_TILE)
    bgu = b_gate_up.reshape(N_EXPERTS, 1, D_FF, 2)
    ys = _moe(blk_e, blk_valid, xs, w_gate_up, bgu[..., 0], bgu[..., 1], w_down,
              b_down.reshape(N_EXPERTS, 1, d))
    return _combine(*tabs, x1, route, mod3, ys, ROUTE_TILE)


def kernel(x, c, positions, ada_w, ada_b, norm_mix_w, norm_ffn_w, w_in, q_a_norm_w, w_q_up, kv_a_norm_w, w_kv_up, q_norm_w, k_norm_w, w_o_mla, rwkv_mu, rwkv_w0, rwkv_w2, rwkv_a0, rwkv_a2, rwkv_g2, rwkv_k_k, rwkv_k_a, rwkv_r_k, rwkv_ln_w, rwkv_ln_b, rwkv_w_o, w_out, w_router, b_router, w_gate_up, b_gate_up, w_down, b_down):
    depth = ada_w.shape[0]
    for l in range(depth):
        mod = _ada(c, ada_w[l], ada_b[l])
        x = _layer(x, mod, positions, w_in[l], q_a_norm_w[l], w_q_up[l], kv_a_norm_w[l], w_kv_up[l],
                   q_norm_w[l], k_norm_w[l], w_o_mla[l], rwkv_mu[l], rwkv_w0[l], rwkv_w2[l], rwkv_a0[l],
                   rwkv_a2[l], rwkv_g2[l], rwkv_k_k[l], rwkv_k_a[l], rwkv_r_k[l], rwkv_ln_w[l],
                   rwkv_ln_b[l], rwkv_w_o[l], w_out[l], norm_mix_w[l], norm_ffn_w[l], w_router[l],
                   b_router[l], w_gate_up[l], b_gate_up[l], w_down[l], b_down[l])
    return x
```
